```python
import math
import jax, jax.numpy as jnp
from jax import lax
import numpy as np

D_MODEL = 2048
BATCH = 8
SEQ = 2048
DEPTH = 2
DEC_BATCH = 128
DEC_SEQ = 1
PAST_LEN = 2048
PAGE_SIZE = 128

HEAD_DIM = 64
GROUP_WIDTH = D_MODEL // 4
MIX_WIDTH = 4 * GROUP_WIDTH
M_HEADS = GROUP_WIDTH // HEAD_DIM
M_CHUNK = 64
A_HEADS = GROUP_WIDTH // HEAD_DIM
A_KV_HEADS = 2
IDX_HEADS = 8
IDX_DIM = 64
TOPK_MAX = 256
Q_BLOCK = 128
ROPE_THETA = 10000.0
R_HEADS = GROUP_WIDTH // HEAD_DIM
R_DECAY_LORA = 64
R_AAA_LORA = 64
R_GATE_LORA = 128
R_IN = 3 * GROUP_WIDTH + R_DECAY_LORA + R_AAA_LORA + R_GATE_LORA
R_OFFSETS = (GROUP_WIDTH, 2 * GROUP_WIDTH, 3 * GROUP_WIDTH, 3 * GROUP_WIDTH + R_DECAY_LORA, 3 * GROUP_WIDTH + R_DECAY_LORA + R_AAA_LORA)
R_LN_EPS = 64e-5
S5_GROUP_CH = 16
S5_GROUPS = GROUP_WIDTH // S5_GROUP_CH
S5_STATE = 64
D_FF = 256 * ((8 * D_MODEL // 3 + 255) // 256)
CONV_W = 3
NORM_EPS = 1e-6
IN_SIZES = (GROUP_WIDTH, GROUP_WIDTH, GROUP_WIDTH, GROUP_WIDTH, M_HEADS, M_HEADS,
            A_HEADS * HEAD_DIM, A_KV_HEADS * HEAD_DIM, A_KV_HEADS * HEAD_DIM,
            IDX_HEADS * IDX_DIM, IDX_DIM, IDX_HEADS,
            R_IN, GROUP_WIDTH)
IN_OFFSETS = tuple(sum(IN_SIZES[:i + 1]) for i in range(len(IN_SIZES) - 1))
N_IN = sum(IN_SIZES)

kernel_name = 'hybrid_mlstm_dsa_rwkv7_s5_step'

F32 = jnp.float32


def rmsnorm(x, g):
    xf = x.astype(F32)
    y = xf * lax.rsqrt(jnp.mean(xf * xf, axis=-1, keepdims=True) + NORM_EPS)
    return (y * g.astype(F32)).astype(x.dtype)


def head_rmsnorm(x, g):
    y = x * lax.rsqrt(jnp.mean(x * x, axis=-1, keepdims=True) + NORM_EPS)
    return y.reshape(x.shape[:2] + (-1,)) * g.astype(F32)


def rope(x, pos):
    half = x.shape[-1] // 2
    inv = ROPE_THETA ** (-jnp.arange(half, dtype=F32) / half)
    ang = pos.astype(F32)[:, None] * inv[None, :]
    cos = jnp.cos(ang)[None, :, None, :]
    sin = jnp.sin(ang)[None, :, None, :]
    x1, x2 = x[..., :half], x[..., half:]
    return jnp.concatenate([x1 * cos - x2 * sin, x2 * cos + x1 * sin], axis=-1)


def mlstm(q, k, v, ig, fg, c0, n0, m0):
    B, T, H, d = q.shape
    k = k * d ** -0.5
    lf = jax.nn.log_sigmoid(fg)
    L = min(M_CHUNK, T)
    nc = -(-T // L)
    pad = nc * L - T
    if pad:
        p4 = ((0, 0), (0, pad), (0, 0), (0, 0))
        p3 = ((0, 0), (0, pad), (0, 0))
        q, k, v = jnp.pad(q, p4), jnp.pad(k, p4), jnp.pad(v, p4)
        ig = jnp.pad(ig, p3, constant_values=-jnp.inf)
        lf = jnp.pad(lf, p3)
    ch4 = lambda z: z.reshape(B, nc, L, H, d).transpose(1, 0, 3, 2, 4)
    ch3 = lambda z: z.reshape(B, nc, L, H).transpose(1, 0, 3, 2)
    causal = jnp.tril(jnp.ones((L, L), dtype=bool))

    def chunk(carry, xs):
        C, n, m = carry
        qc, kc, vc, ic, fc = xs
        b = jnp.cumsum(fc, axis=-1)
        dmat = jnp.where(causal, b[..., :, None] - b[..., None, :] + ic[..., None, :], -jnp.inf)
        inter = b + m[..., None]
        mj = jnp.maximum(inter, jnp.max(dmat, axis=-1))
        s = jnp.einsum('bhld,bhsd->bhls', qc, kc) * jnp.exp(dmat - mj[..., None])
        iw = jnp.exp(inter - mj)
        num = jnp.einsum('bhls,bhsd->bhld', s, vc) + iw[..., None] * jnp.einsum('bhvk,bhlk->bhlv', C, qc)
        den = jnp.sum(s, axis=-1) + iw * jnp.einsum('bhk,bhlk->bhl', n, qc)
        hc = num / jnp.maximum(jnp.abs(den), jnp.exp(-mj))[..., None]
        bl = b[..., -1]
        wl = bl[..., None] - b + ic
        m_new = jnp.maximum(bl + m, jnp.max(wl, axis=-1))
        dec = jnp.exp(bl + m - m_new)
        ws = jnp.exp(wl - m_new[..., None])
        C_new = dec[..., None, None] * C + jnp.einsum('bhs,bhsv,bhsk->bhvk', ws, vc, kc)
        n_new = dec[..., None] * n + jnp.einsum('bhs,bhsk->bhk', ws, kc)
        return (C_new, n_new, m_new), hc

    carry0 = (c0.astype(F32), n0.astype(F32), m0.astype(F32))
    (c1, n1, m1), h = lax.scan(chunk, carry0, (ch4(q), ch4(k), ch4(v), ch3(ig), ch3(lf)))
    h = h.transpose(1, 0, 3, 2, 4).reshape(B, nc * L, H, d)[:, :T]
    return h, c1, n1, m1


def indexer_scores(qi, wi, ki):
    qk = jnp.einsum('bthd,bsd->bths', qi, ki.astype(F32)) * IDX_DIM ** -0.5
    return jnp.einsum('bths,bth->bts', jax.nn.relu(qk), wi)


def sparse_attend(q, k_sel, v_sel, valid):
    B, T, H, d = q.shape
    qg = q.reshape(B, T, A_KV_HEADS, H // A_KV_HEADS, d)
    s = jnp.einsum('btgjd,btkgd->btgjk', qg, k_sel.astype(F32)) * d ** -0.5
    s = jnp.where(valid[:, :, None, None, :], s, -jnp.inf)
    p = jax.nn.softmax(s, axis=-1)
    o = jnp.einsum('btgjk,btkgd->btgjd', p, v_sel.astype(F32))
    return o.reshape(B, T, H * d)


def dsa_prompt(q, k, v, qi, ki, wi):
    B, T, H, d = q.shape
    K = min(TOPK_MAX, T // 4)
    nb = T // Q_BLOCK
    blk = lambda z: jnp.moveaxis(z.reshape((B, nb, Q_BLOCK) + z.shape[2:]), 1, 0)
    key_pos = jnp.arange(T)
    bi = jnp.arange(B)[:, None, None]

    def one_block(args):
        qb, qib, wib, t0 = args
        tq = t0 + jnp.arange(Q_BLOCK)
        sc = indexer_scores(qib, wib, ki)
        sc = jnp.where(key_pos[None, None, :] <= tq[None, :, None], sc, -jnp.inf)
        _, idx = lax.top_k(sc, K)
        k_sel = k[bi, idx]
        v_sel = v[bi, idx]
        return sparse_attend(qb, k_sel, v_sel, idx <= tq[None, :, None])

    out = lax.map(one_block, (blk(q), blk(qi), blk(wi), jnp.arange(nb) * Q_BLOCK))
    return jnp.moveaxis(out, 0, 1).reshape(B, T, H * d)


def dsa_sample(q, k, v, qi, ki, wi, ck, cv, cki, page_table):
    B, T, H, d = q.shape
    L = PAST_LEN + T
    K = min(TOPK_MAX, L // 4)
    ki_all = jnp.concatenate([cki[page_table].reshape(B, PAST_LEN, IDX_DIM).astype(F32), ki], axis=1)
    tq = PAST_LEN + jnp.arange(T)
    sc = indexer_scores(qi, wi, ki_all)
    sc = jnp.where(jnp.arange(L)[None, None, :] <= tq[None, :, None], sc, -jnp.inf)
    _, idx = lax.top_k(sc, K)
    bi = jnp.arange(B)[:, None, None]
    pidx = jnp.minimum(idx, PAST_LEN - 1)
    phys = page_table[bi, pidx // PAGE_SIZE]
    off = pidx % PAGE_SIZE
    nidx = jnp.clip(idx - PAST_LEN, 0, T - 1)
    is_new = (idx >= PAST_LEN)[..., None, None]
    k_sel = jnp.where(is_new, k[bi, nidx], ck[phys, off].astype(F32))
    v_sel = jnp.where(is_new, v[bi, nidx], cv[phys, off].astype(F32))
    return sparse_attend(q, k_sel, v_sel, idx <= tq[None, :, None])


def rwkv7(xin, prev, s0, mu, w0, w_w2, a0, w_a2, w_g2, k_k, k_a, r_k, ln_w, ln_b):
    B, T, _ = xin.shape
    xprev = jnp.concatenate([prev.astype(F32)[:, None], xin[:, :-1]], axis=1)
    xm = xin + (xprev - xin) * mu
    r, kx, v, xw, xa, xg = jnp.split(xm, R_OFFSETS, axis=-1)
    w = -jax.nn.softplus(-(w0 + jnp.tanh(xw) @ w_w2)) - 0.5
    decay = jnp.exp(-jnp.exp(w))
    a = jax.nn.sigmoid(a0 + xa @ w_a2)
    g = jax.nn.sigmoid(xg) @ w_g2
    hs = lambda z: z.reshape(B, T, R_HEADS, HEAD_DIM)
    kk = hs(kx * k_k)
    kk = kk / jnp.maximum(jnp.sqrt(jnp.sum(kk * kk, axis=-1, keepdims=True)), 1e-12)
    k = kx * (1.0 + (a - 1.0) * k_a)
    r_, w_, k_, v_, a_ = hs(r), hs(decay), hs(k), hs(v), hs(a)

    def step(S, z):
        r_t, w_t, k_t, v_t, kk_t, a_t = z
        Sk = jnp.einsum('bhvk,bhk->bhv', S, kk_t)
        S = S * w_t[:, :, None, :] - Sk[..., None] * (kk_t * a_t)[:, :, None, :] + v_t[..., None] * k_t[:, :, None, :]
        return S, jnp.einsum('bhvk,bhk->bhv', S, r_t)

    tm = lambda z: jnp.moveaxis(z, 1, 0)
    s1, o = lax.scan(step, s0.astype(F32), (tm(r_), tm(w_), tm(k_), tm(v_), tm(kk), tm(a_)))
    o = jnp.moveaxis(o, 0, 1)
    mean = jnp.mean(o, axis=-1, keepdims=True)
    var = jnp.mean(jnp.square(o - mean), axis=-1, keepdims=True)
    o = ((o - mean) * lax.rsqrt(var + R_LN_EPS)).reshape(B, T, GROUP_WIDTH) * ln_w + ln_b
    bonus = jnp.sum(r_ * k_ * r_k, axis=-1, keepdims=True) * v_
    o = (o + bonus.reshape(B, T, GROUP_WIDTH)) * g
    return o, s1, xin[:, -1]


def s5(u, a_re, a_im, b_re, b_im, c_re, c_im, d_skip, log_dt, w_glu, b_glu, h0_re, h0_im):
    B, T, W = u.shape
    a_re, a_im = a_re.astype(F32), a_im.astype(F32)
    uf = u.reshape(B, T, S5_GROUPS, S5_GROUP_CH)
    dt = jnp.exp(log_dt.astype(F32))
    mag = jnp.exp(a_re * dt)
    lb_re, lb_im = mag * jnp.cos(a_im * dt), mag * jnp.sin(a_im * dt)
    den = a_re * a_re + a_im * a_im
    f_re = ((lb_re - 1.0) * a_re + lb_im * a_im) / den
    f_im = (lb_im * a_re - (lb_re - 1.0) * a_im) / den
    bb_re = f_re[..., None] * b_re - f_im[..., None] * b_im
    bb_im = f_re[..., None] * b_im + f_im[..., None] * b_re
    bu_re = jnp.einsum('gpc,btgc->btgp', bb_re, uf)
    bu_im = jnp.einsum('gpc,btgc->btgp', bb_im, uf)
    al_re = jnp.broadcast_to(lb_re, bu_re.shape)
    al_im = jnp.broadcast_to(lb_im, bu_im.shape)

    def combine(e1, e2):
        a1r, a1i, b1r, b1i = e1
        a2r, a2i, b2r, b2i = e2
        return (a2r * a1r - a2i * a1i, a2r * a1i + a2i * a1r,
                a2r * b1r - a2i * b1i + b2r, a2r * b1i + a2i * b1r + b2i)

    ar, ai, br, bi = lax.associative_scan(combine, (al_re, al_im, bu_re, bu_im), axis=1)
    h0r, h0i = h0_re.astype(F32)[:, None], h0_im.astype(F32)[:, None]
    h_re = ar * h0r - ai * h0i + br
    h_im = ar * h0i + ai * h0r + bi
    y = jnp.einsum('gcp,btgp->btgc', c_re, h_re) - jnp.einsum('gcp,btgp->btgc', c_im, h_im)
    y = y.reshape(B, T, W) + d_skip * u
    y = jax.nn.gelu(y)
    y = y * jax.nn.sigmoid(y @ w_glu + b_glu)
    return y, h_re[:, -1], h_im[:, -1]


def conv_ffn(h, w_up, conv_w, conv_b, w_down, buf):
    T = h.shape[1]
    a, b = jnp.split(h @ w_up, 2, axis=-1)
    a_ext = jnp.concatenate([buf.astype(a.dtype), a], axis=1)
    c = conv_b + sum(a_ext[:, j:j + T] * conv_w[j] for j in range(CONV_W))
    y = jax.nn.silu(c) * b
    return y @ w_down, a_ext[:, a_ext.shape[1] - (CONV_W - 1):]


def layer(x, pos, l, P, st, cache):
    B, T, _ = x.shape
    c0, n0, m0, rs0, rsh0, sre0, sim0, conv0 = st
    h = rmsnorm(x, P['norm_mix'][l])
    proj = (h @ P['w_in'][l]).astype(F32)
    mq, mk, mv, mo, mi, mf, aq, ak, av, iq, ik, iw, rin, su = jnp.split(proj, IN_OFFSETS, axis=-1)
    heads = lambda z, n: z.reshape(B, T, n, -1)
    hm, c1, n1, m1 = mlstm(heads(mq, M_HEADS), heads(mk, M_HEADS), heads(mv, M_HEADS),
                           mi + P['m_b_i'][l], mf + P['m_b_f'][l], c0, n0, m0)
    ym = head_rmsnorm(hm, P['m_norm'][l]) * jax.nn.sigmoid(mo)
    aq = rope(heads(aq, A_HEADS), pos)
    ak = rope(heads(ak, A_KV_HEADS), pos)
    av = heads(av, A_KV_HEADS)
    iq = rope(heads(iq, IDX_HEADS), pos)
    ik = rope(ik[:, :, None, :], pos)[:, :, 0]
    iw = iw * IDX_HEADS ** -0.5
    if cache is None:
        ya = dsa_prompt(aq, ak, av, iq, ik, iw)
    else:
        ya = dsa_sample(aq, ak, av, iq, ik, iw, *cache)
    yr, rs1, rsh1 = rwkv7(rin, rsh0, rs0, P['r_mu'][l], P['r_w0'][l], P['r_w_w2'][l], P['r_a0'][l],
                          P['r_w_a2'][l], P['r_w_g2'][l], P['r_k_k'][l], P['r_k_a'][l], P['r_r_k'][l],
                          P['r_ln_w'][l], P['r_ln_b'][l])
    ys, sre1, sim1 = s5(su, P['s5_a_re'][l], P['s5_a_im'][l], P['s5_b_re'][l], P['s5_b_im'][l],
                        P['s5_c_re'][l], P['s5_c_im'][l], P['s5_d'][l], P['s5_log_dt'][l],
                        P['s5_w_glu'][l], P['s5_b_glu'][l], sre0, sim0)
    mixed = jnp.concatenate([ym, ya, yr, ys], axis=-1).astype(x.dtype)
    x = x + mixed @ P['w_out'][l]
    h2 = rmsnorm(x, P['norm_ffn'][l])
    yf, conv1 = conv_ffn(h2, P['ffn_w_up'][l], P['ffn_conv_w'][l], P['ffn_conv_b'][l], P['ffn_w_down'][l], conv0)
    x = x + yf
    return x, (ak.astype(x.dtype), av.astype(x.dtype), ik.astype(x.dtype), c1, n1, m1, rs1, rsh1, sre1, sim1, conv1)


def setup_inputs(seed: int = 0) -> dict:
    key = jax.random.key(seed)
    keys = jax.random.split(key, 64)
    cnt = [0]

    def nk():
        cnt[0] += 1
        return keys[cnt[0] - 1]

    def nrm(shape, scale):
        return jax.random.normal(nk(), shape, F32) * scale

    GW = GROUP_WIDTH
    n_pages = PAST_LEN // PAGE_SIZE
    n_used = DEC_BATCH * n_pages
    n_pool = (n_used * 5) // 4
    page_table = jax.random.permutation(nk(), n_pool)[:n_used].reshape(DEC_BATCH, n_pages).astype(jnp.int32)
    w0_base = -6.0 + 5.0 * (jnp.arange(GW, dtype=F32) / (GW - 1)) ** 0.85
    a_im_base = jnp.pi * jnp.arange(S5_STATE, dtype=F32)
    return {
        'x_prompt': nrm((BATCH, SEQ, D_MODEL), 1.0),
        'x_sample': nrm((DEC_BATCH, DEC_SEQ, D_MODEL), 1.0),
        'cache_k': nrm((DEPTH, n_pool, PAGE_SIZE, A_KV_HEADS, HEAD_DIM), 1.0),
        'cache_v': nrm((DEPTH, n_pool, PAGE_SIZE, A_KV_HEADS, HEAD_DIM), 1.0),
        'cache_kidx': nrm((DEPTH, n_pool, PAGE_SIZE, IDX_DIM), 1.0),
        'page_table': page_table,
        'state_mlstm_c': nrm((DEPTH, DEC_BATCH, M_HEADS, HEAD_DIM, HEAD_DIM), 0.5),
        'state_mlstm_n': jnp.abs(nrm((DEPTH, DEC_BATCH, M_HEADS, HEAD_DIM), 1.0)),
        'state_mlstm_m': nrm((DEPTH, DEC_BATCH, M_HEADS), 0.5),
        'state_rwkv_s': nrm((DEPTH, DEC_BATCH, R_HEADS, HEAD_DIM, HEAD_DIM), 0.5),
        'state_rwkv_shift': nrm((DEPTH, DEC_BATCH, R_IN), 1.0),
        'state_s5_re': nrm((DEPTH, DEC_BATCH, S5_GROUPS, S5_STATE), 0.5),
        'state_s5_im': nrm((DEPTH, DEC_BATCH, S5_GROUPS, S5_STATE), 0.5),
        'state_ffn_conv': nrm((DEPTH, DEC_BATCH, CONV_W - 1, D_FF), 1.0),
        'norm_mix': 1.0 + nrm((DEPTH, D_MODEL), 0.02),
        'w_in': nrm((DEPTH, D_MODEL, N_IN), D_MODEL ** -0.5),
        'w_out': nrm((DEPTH, MIX_WIDTH, D_MODEL), MIX_WIDTH ** -0.5),
        'm_b_i': nrm((DEPTH, M_HEADS), 0.1),
        'm_b_f': jnp.linspace(3.0, 6.0, M_HEADS, dtype=F32)[None] + nrm((DEPTH, M_HEADS), 0.1),
        'm_norm': 1.0 + nrm((DEPTH, GW), 0.02),
        'r_mu': jax.random.uniform(nk(), (DEPTH, R_IN), F32),
        'r_w0': w0_base[None] + nrm((DEPTH, GW), 0.01),
        'r_w_w2': nrm((DEPTH, R_DECAY_LORA, GW), 0.1),
        'r_a0': nrm((DEPTH, GW), 0.1),
        'r_w_a2': nrm((DEPTH, R_AAA_LORA, GW), 0.5 * R_AAA_LORA ** -0.5),
        'r_w_g2': nrm((DEPTH, R_GATE_LORA, GW), R_GATE_LORA ** -0.5),
        'r_k_k': 0.85 + nrm((DEPTH, GW), 0.02),
        'r_k_a': 1.0 + nrm((DEPTH, GW), 0.02),
        'r_r_k': nrm((DEPTH, R_HEADS, HEAD_DIM), 0.1),
        'r_ln_w': 1.0 + nrm((DEPTH, GW), 0.02),
        'r_ln_b': nrm((DEPTH, GW), 0.01),
        's5_a_re': -0.5 + nrm((DEPTH, S5_GROUPS, S5_STATE), 0.01),
        's5_a_im': a_im_base[None, None] + nrm((DEPTH, S5_GROUPS, S5_STATE), 0.01),
        's5_b_re': nrm((DEPTH, S5_GROUPS, S5_STATE, S5_GROUP_CH), (2 * S5_GROUP_CH) ** -0.5),
        's5_b_im': nrm((DEPTH, S5_GROUPS, S5_STATE, S5_GROUP_CH), (2 * S5_GROUP_CH) ** -0.5),
        's5_c_re': nrm((DEPTH, S5_GROUPS, S5_GROUP_CH, S5_STATE), (2 * S5_STATE) ** -0.5),
        's5_c_im': nrm((DEPTH, S5_GROUPS, S5_GROUP_CH, S5_STATE), (2 * S5_STATE) ** -0.5),
        's5_d': nrm((DEPTH, GW), 1.0),
        's5_log_dt': jax.random.uniform(nk(), (DEPTH, S5_GROUPS, S5_STATE), F32, math.log(1e-3), math.log(1e-1)),
        's5_w_glu': nrm((DEPTH, GW, GW), GW ** -0.5),
        's5_b_glu': nrm((DEPTH, GW), 0.01),
        'norm_ffn': 1.0 + nrm((DEPTH, D_MODEL), 0.02),
        'ffn_w_up': nrm((DEPTH, D_MODEL, 2 * D_FF), D_MODEL ** -0.5),
        'ffn_conv_w': nrm((DEPTH, CONV_W, D_FF), CONV_W ** -0.5),
        'ffn_conv_b': nrm((DEPTH, D_FF), 0.01),
        'ffn_w_down': nrm((DEPTH, D_FF, D_MODEL), D_FF ** -0.5),
        'norm_final': 1.0 + nrm((D_MODEL,), 0.02),
    }


def reference(x_prompt, x_sample, cache_k, cache_v, cache_kidx, page_table, state_mlstm_c, state_mlstm_n,
              state_mlstm_m, state_rwkv_s, state_rwkv_shift, state_s5_re, state_s5_im, state_ffn_conv,
              norm_mix, w_in, w_out, m_b_i, m_b_f, m_norm, r_mu, r_w0, r_w_w2, r_a0, r_w_a2, r_w_g2,
              r_k_k, r_k_a, r_r_k, r_ln_w, r_ln_b, s5_a_re, s5_a_im, s5_b_re, s5_b_im, s5_c_re, s5_c_im,
              s5_d, s5_log_dt, s5_w_glu, s5_b_glu, norm_ffn, ffn_w_up, ffn_conv_w, ffn_conv_b, ffn_w_down,
              norm_final):
    P = dict(norm_mix=norm_mix, w_in=w_in, w_out=w_out, m_b_i=m_b_i, m_b_f=m_b_f, m_norm=m_norm,
             r_mu=r_mu, r_w0=r_w0, r_w_w2=r_w_w2, r_a0=r_a0, r_w_a2=r_w_a2, r_w_g2=r_w_g2, r_k_k=r_k_k,
             r_k_a=r_k_a, r_r_k=r_r_k, r_ln_w=r_ln_w, r_ln_b=r_ln_b, s5_a_re=s5_a_re, s5_a_im=s5_a_im,
             s5_b_re=s5_b_re, s5_b_im=s5_b_im, s5_c_re=s5_c_re, s5_c_im=s5_c_im, s5_d=s5_d,
             s5_log_dt=s5_log_dt, s5_w_glu=s5_w_glu, s5_b_glu=s5_b_glu, norm_ffn=norm_ffn,
             ffn_w_up=ffn_w_up, ffn_conv_w=ffn_conv_w, ffn_conv_b=ffn_conv_b, ffn_w_down=ffn_w_down)
    bp = x_prompt.shape[0]
    pos_p = jnp.arange(x_prompt.shape[1])
    pos_s = PAST_LEN + jnp.arange(x_sample.shape[1])
    st_p = (jnp.zeros((bp, M_HEADS, HEAD_DIM, HEAD_DIM), F32), jnp.zeros((bp, M_HEADS, HEAD_DIM), F32),
            jnp.zeros((bp, M_HEADS), F32), jnp.zeros((bp, R_HEADS, HEAD_DIM, HEAD_DIM), F32),
            jnp.zeros((bp, R_IN), F32), jnp.zeros((bp, S5_GROUPS, S5_STATE), F32),
            jnp.zeros((bp, S5_GROUPS, S5_STATE), F32), jnp.zeros((bp, CONV_W - 1, D_FF), x_prompt.dtype))
    xp, xs = x_prompt, x_sample
    new_p, new_s = [], []
    for l in range(DEPTH):
        xp, sp = layer(xp, pos_p, l, P, st_p, None)
        st_s = (state_mlstm_c[l], state_mlstm_n[l], state_mlstm_m[l], state_rwkv_s[l], state_rwkv_shift[l],
                state_s5_re[l], state_s5_im[l], state_ffn_conv[l])
        xs, ss = layer(xs, pos_s, l, P, st_s, (cache_k[l], cache_v[l], cache_kidx[l], page_table))
        new_p.append(sp)
        new_s.append(ss)
    (k_p, v_p, ki_p, mc_p, mn_p, mm_p, rs_p, rsh_p, sre_p, sim_p, fc_p) = [jnp.stack(z) for z in zip(*new_p)]
    (k_s, v_s, ki_s, mc_s, mn_s, mm_s, rs_s, rsh_s, sre_s, sim_s, fc_s) = [jnp.stack(z) for z in zip(*new_s)]
    y_prompt = rmsnorm(xp, norm_final)
    y_sample = rmsnorm(xs, norm_final)
    return (y_prompt, y_sample, k_p, k_s, v_p, v_s, ki_p, ki_s, mc_p, mc_s, mn_p, mn_s, mm_p, mm_s,
            rs_p, rs_s, rsh_p, rsh_s, sre_p, sre_s, sim_p, sim_s, fc_p, fc_s)
```

```python
import functools
import math

import jax
import jax.numpy as jnp
from jax import lax
from jax.experimental import pallas as pl
from jax.experimental.pallas import tpu as pltpu

F32 = jnp.float32
BF16 = jnp.bfloat16
HI = lax.Precision.HIGHEST

HD = 64
NH = 8
GW = NH * HD
A_KV = 2
PAGE = 128
TOPK_MAX = 256
ROPE_THETA = 10000.0
R_IN = 3 * GW + 64 + 64 + 128
R_LN_EPS = 64e-5
S5_G, S5_CH, S5_P = 32, 16, 64
S5_W = S5_G * S5_P
NORM_EPS = 1e-6
CHUNK = 64
IDX_SCALE = HD ** -0.5 * NH ** -0.5

C_M, C_AQ, C_IQ, C_SU, C_RIN, C_AK, C_AV, C_SM = 0, 2048, 2560, 3072, 3584, 5376, 5504, 5632
NP = 5760
SM_IK, SM_MI, SM_MF, SM_IW = 0, 64, 72, 80

VMEM_LIMIT = 56 * 1024 * 1024


def _cp(*sem):
    return pltpu.CompilerParams(dimension_semantics=sem, vmem_limit_bytes=VMEM_LIMIT)


def _dot(a, b, prec=None):
    return jnp.dot(a, b, preferred_element_type=F32, precision=prec)


def _dot_nt(a, b, prec=None):
    return lax.dot_general(a, b, (((1,), (1,)), ((), ())), preferred_element_type=F32, precision=prec)


def _dot_tn(a, b, prec=None):
    return lax.dot_general(a, b, (((0,), (0,)), ((), ())), preferred_element_type=F32, precision=prec)


def _sigmoid(x):
    return 1.0 / (1.0 + jnp.exp(-x))


def _softplus(x):
    return jnp.maximum(x, 0.0) + jnp.log(1.0 + jnp.exp(-jnp.abs(x)))


def _iota(shape, dim):
    return lax.broadcasted_iota(jnp.int32, shape, dim)


def _inproj_kernel(x_ref, g_ref, w_ref, o_ref, h_scr):
    @pl.when(pl.program_id(1) == 0)
    def _():
        x = x_ref[...]
        ms = jnp.mean(x * x, axis=-1, keepdims=True)
        h_scr[...] = (x * lax.rsqrt(ms + NORM_EPS) * g_ref[...]).astype(BF16)

    o_ref[...] = _dot(h_scr[...], w_ref[...])


def _in_proj(x2, g, w):
    n, d = x2.shape
    npad = w.shape[1]
    tm = min(512, n)
    tn = 640
    return pl.pallas_call(
        _inproj_kernel,
        out_shape=jax.ShapeDtypeStruct((n, npad), F32),
        grid=(n // tm, npad // tn),
        in_specs=[pl.BlockSpec((tm, d), lambda i, j: (i, 0)),
                  pl.BlockSpec((1, d), lambda i, j: (0, 0)),
                  pl.BlockSpec((d, tn), lambda i, j: (0, j))],
        out_specs=pl.BlockSpec((tm, tn), lambda i, j: (i, j)),
        scratch_shapes=[pltpu.VMEM((tm, d), BF16)],
        compiler_params=_cp("parallel", "arbitrary"),
    )(x2, g, w)


def _rmsnorm_kernel(x_ref, g_ref, o_ref):
    x = x_ref[...]
    ms = jnp.mean(x * x, axis=-1, keepdims=True)
    o_ref[...] = (x * lax.rsqrt(ms + NORM_EPS) * g_ref[...]).astype(o_ref.dtype)


def _rmsnorm(x2, g, dtype):
    n, d = x2.shape
    tm = min(512, n)
    return pl.pallas_call(
        _rmsnorm_kernel,
        out_shape=jax.ShapeDtypeStruct((n, d), dtype),
        grid=(n // tm,),
        in_specs=[pl.BlockSpec((tm, d), lambda i: (i, 0)), pl.BlockSpec((1, d), lambda i: (0, 0))],
        out_specs=pl.BlockSpec((tm, d), lambda i: (i, 0)),
        compiler_params=_cp("parallel"),
    )(x2, g)


def _resmm_kernel(r_ref, y_ref, w_ref, o_ref):
    o_ref[...] = r_ref[...] + _dot(y_ref[...], w_ref[...])


def _res_matmul(res, y, w):
    n, k = y.shape
    d = w.shape[1]
    tm = min(512, n)
    tn = 512
    return pl.pallas_call(
        _resmm_kernel,
        out_shape=jax.ShapeDtypeStruct((n, d), F32),
        grid=(n // tm, d // tn),
        in_specs=[pl.BlockSpec((tm, tn), lambda i, j: (i, j)),
                  pl.BlockSpec((tm, k), lambda i, j: (i, 0)),
                  pl.BlockSpec((k, tn), lambda i, j: (0, j))],
        out_specs=pl.BlockSpec((tm, tn), lambda i, j: (i, j)),
        compiler_params=_cp("parallel", "arbitrary"),
    )(res, y, w)


def _rope(x, cos, sin):
    w = x.shape[1]
    first = (_iota(x.shape, 1) & (HD - 1)) < HD // 2
    sw = jnp.where(first, pltpu.roll(x, w - HD // 2, 1), pltpu.roll(x, HD // 2, 1))
    return x * cos + sw * sin


def _rope_kernel(aq_ref, iq_ref, ak_ref, av_ref, sm_ref, cos_ref, sin_ref, aqo, iqo, ko, vo, iko):
    cos = cos_ref[...]
    sin = sin_ref[...]
    aqo[...] = _rope(aq_ref[...], cos, sin)
    iqo[...] = _rope(iq_ref[...], cos, sin)
    ko[...] = _rope(ak_ref[...], cos[:, :128], sin[:, :128])
    vo[...] = av_ref[...]
    iko[...] = _rope(sm_ref[...], cos[:, :128], sin[:, :128])[:, :HD]


def _rope_call(proj, cos, sin, nb, nt_rows):
    n = proj.shape[0]
    tm = min(512, nt_rows)
    nt = nt_rows // tm
    row = lambda b, i: b * nt + i
    return pl.pallas_call(
        _rope_kernel,
        out_shape=(jax.ShapeDtypeStruct((n, GW), F32), jax.ShapeDtypeStruct((n, GW), F32),
                   jax.ShapeDtypeStruct((n, 128), F32), jax.ShapeDtypeStruct((n, 128), F32),
                   jax.ShapeDtypeStruct((n, HD), F32)),
        grid=(nb, nt),
        in_specs=[pl.BlockSpec((tm, GW), lambda b, i: (row(b, i), C_AQ // GW)),
                  pl.BlockSpec((tm, GW), lambda b, i: (row(b, i), C_IQ // GW)),
                  pl.BlockSpec((tm, 128), lambda b, i: (row(b, i), C_AK // 128)),
                  pl.BlockSpec((tm, 128), lambda b, i: (row(b, i), C_AV // 128)),
                  pl.BlockSpec((tm, 128), lambda b, i: (row(b, i), C_SM // 128)),
                  pl.BlockSpec((tm, GW), lambda b, i: (i, 0)),
                  pl.BlockSpec((tm, GW), lambda b, i: (i, 0))],
        out_specs=(pl.BlockSpec((tm, GW), lambda b, i: (row(b, i), 0)),
                   pl.BlockSpec((tm, GW), lambda b, i: (row(b, i), 0)),
                   pl.BlockSpec((tm, 128), lambda b, i: (row(b, i), 0)),
                   pl.BlockSpec((tm, 128), lambda b, i: (row(b, i), 0)),
                   pl.BlockSpec((tm, HD), lambda b, i: (row(b, i), 0))),
        compiler_params=_cp("parallel", "parallel"),
    )(proj, proj, proj, proj, proj, cos, sin)


def _rope_tables(pos):
    half = HD // 2
    inv = ROPE_THETA ** (-jnp.arange(half, dtype=F32) / half)
    ang = pos.astype(F32)[:, None] * inv[None, :]
    cos, sin = jnp.cos(ang), jnp.sin(ang)
    cos64 = jnp.concatenate([cos, cos], axis=-1)
    sin64 = jnp.concatenate([-sin, sin], axis=-1)
    return jnp.tile(cos64, (1, NH)), jnp.tile(sin64, (1, NH))


def _kth_largest(sc, extra, kk):
    kf = jnp.float32(kk)

    def count_ge(c):
        n = jnp.sum(jnp.where(sc >= c, 1.0, 0.0), axis=-1, keepdims=True)
        if extra is not None:
            n = n + jnp.where(extra >= c, 1.0, 0.0)
        return n

    def key_to_f(key):
        bits = key ^ ((key >> 31) & jnp.int32(0x7FFFFFFF))
        return lax.bitcast_convert_type(bits, F32)

    r = sc.shape[0]
    int_min = jnp.int32(-2 ** 31)
    lo = jnp.where(count_ge(jnp.zeros((r, 1), F32)) >= kf, jnp.int32(0), int_min)

    def body(j, lo):
        cand = lo + jnp.left_shift(jnp.int32(1), jnp.int32(30) - j)
        ok = count_ge(key_to_f(cand)) >= kf
        return jnp.where(ok, cand, lo)

    lo = lax.fori_loop(0, 31, body, lo)
    key_neg_inf = jnp.int32(-2 ** 31 + 0x7FFFFF)
    return jnp.where(lo <= key_neg_inf, -jnp.inf, key_to_f(lo))


def _strict_upper_bf16(n):
    return jnp.where(_iota((n, n), 0) < _iota((n, n), 1), 1.0, 0.0).astype(BF16)


def _dsa_prompt_kernel(iq_ref, sm_ref, ik_ref, aq_ref, k_ref, v_ref, o_ref, sel_scr, *, topk, qb):
    t_keys = ik_ref.shape[0]
    i = pl.program_id(1)
    ik = ik_ref[...]
    iq = iq_ref[...]
    wts = sm_ref[:, SM_IW:SM_IW + NH] * IDX_SCALE
    sc = jnp.zeros((qb, t_keys), F32)
    for h in range(NH):
        qk = _dot_nt(iq[:, h * HD:(h + 1) * HD], ik, HI)
        sc = sc + jnp.maximum(qk, 0.0) * wts[:, h:h + 1]
    tq = i * qb + _iota((qb, 1), 0)
    causal = _iota((1, t_keys), 1) <= tq
    sc = jnp.where(causal, sc, -jnp.inf)

    thr = _kth_largest(sc, None, topk)
    gt = sc > thr
    eq = sc == thr
    n_gt = jnp.sum(jnp.where(gt, 1.0, 0.0), axis=-1, keepdims=True)
    n_eq = jnp.sum(jnp.where(eq, 1.0, 0.0), axis=-1, keepdims=True)
    need = jnp.float32(topk) - n_gt
    sel_scr[...] = jnp.where(jnp.logical_and(sc >= thr, causal), 1.0, 0.0)
    tie = jnp.logical_and(n_eq > need, thr > -jnp.inf)

    @pl.when(jnp.max(jnp.where(tie, 1.0, 0.0)) > 0.5)
    def _():
        ut = _strict_upper_bf16(128)
        run = jnp.zeros((qb, 1), F32)
        for c in range(t_keys // 128):
            sl = slice(c * 128, (c + 1) * 128)
            eqc = jnp.where(eq[:, sl], 1.0, 0.0)
            pref = _dot(eqc.astype(BF16), ut) + run
            keep = jnp.logical_or(gt[:, sl], jnp.logical_and(eq[:, sl], pref < need))
            sel_scr[:, sl] = jnp.where(jnp.logical_and(keep, causal[:, sl]), 1.0, 0.0)
            run = run + jnp.sum(eqc, axis=-1, keepdims=True)

    sel = sel_scr[...] > 0.5
    aq = aq_ref[...]
    for g in range(A_KV):
        kg = k_ref[:, g * HD:(g + 1) * HD]
        vg = v_ref[:, g * HD:(g + 1) * HD]
        for j in range(NH // A_KV):
            h = g * (NH // A_KV) + j
            s = _dot_nt(aq[:, h * HD:(h + 1) * HD], kg) * HD ** -0.5
            s = jnp.where(sel, s, -jnp.inf)
            m = jnp.max(s, axis=-1, keepdims=True)
            p = jnp.exp(s - m)
            l = jnp.sum(p, axis=-1, keepdims=True)
            o_ref[:, h * HD:(h + 1) * HD] = _dot(p, vg) / l


def _dsa_prompt(iq_r, proj, ik_r, aq_r, k_r, v_r, nb, t):
    qb = min(128, t)
    nq = t // qb
    topk = min(TOPK_MAX, t // 4)
    n = nb * t
    row = lambda b, i: b * nq + i
    return pl.pallas_call(
        functools.partial(_dsa_prompt_kernel, topk=topk, qb=qb),
        out_shape=jax.ShapeDtypeStruct((n, GW), F32),
        grid=(nb, nq),
        in_specs=[pl.BlockSpec((qb, GW), lambda b, i: (row(b, i), 0)),
                  pl.BlockSpec((qb, 128), lambda b, i: (row(b, i), C_SM // 128)),
                  pl.BlockSpec((t, HD), lambda b, i: (b, 0)),
                  pl.BlockSpec((qb, GW), lambda b, i: (row(b, i), 0)),
                  pl.BlockSpec((t, 128), lambda b, i: (b, 0)),
                  pl.BlockSpec((t, 128), lambda b, i: (b, 0))],
        out_specs=pl.BlockSpec((qb, GW), lambda b, i: (row(b, i), 0)),
        scratch_shapes=[pltpu.VMEM((qb, t), F32)],
        compiler_params=_cp("parallel", "arbitrary"),
    )(iq_r, proj, ik_r, aq_r, k_r, v_r)


def _dsa_sample_kernel(pt_ref, iq_ref, w_ref, ikn_ref, aq_ref, kn_ref, vn_ref, cki_ref, ck_ref, cv_ref,
                       o_ref, sc_scr, k_scr, v_scr, *, topk, n_pages):
    del pt_ref
    p = pl.program_id(1)
    s_keys = n_pages * PAGE
    iq = iq_ref[0]
    w = w_ref[0] * IDX_SCALE
    qk = _dot_nt(iq, cki_ref[0], HI)
    scp = jnp.sum(jnp.maximum(qk, 0.0) * w, axis=0, keepdims=True)
    page_of_lane = _iota((1, s_keys), 1) // PAGE
    sc_scr[...] = jnp.where(page_of_lane == p, jnp.tile(scp, (1, n_pages)), sc_scr[...])
    rows = pl.ds(pl.multiple_of(p * PAGE, PAGE), PAGE)
    k_scr[rows, :] = ck_ref[0]
    v_scr[rows, :] = cv_ref[0]

    @pl.when(p == n_pages - 1)
    def _():
        sc = sc_scr[...]
        qkn = jnp.sum(iq * ikn_ref[0], axis=-1, keepdims=True)
        sn = jnp.sum(jnp.maximum(qkn, 0.0) * w, axis=0, keepdims=True)
        thr = _kth_largest(sc, sn, topk)
        gt = sc > thr
        eq = sc == thr
        n_gt = jnp.sum(jnp.where(gt, 1.0, 0.0), axis=-1, keepdims=True) + jnp.where(sn > thr, 1.0, 0.0)
        need = jnp.float32(topk) - n_gt
        ut = _strict_upper_bf16(PAGE)
        run = jnp.zeros((1, 1), F32)
        pieces = []
        for c in range(n_pages):
            sl = slice(c * PAGE, (c + 1) * PAGE)
            eqc = jnp.where(eq[:, sl], 1.0, 0.0)
            pref = _dot(eqc.astype(BF16), ut) + run
            pieces.append(jnp.logical_or(gt[:, sl], jnp.logical_and(eq[:, sl], pref < need)))
            run = run + jnp.sum(eqc, axis=-1, keepdims=True)
        sel = jnp.concatenate(pieces, axis=1)
        sel_new = jnp.logical_or(sn > thr, jnp.logical_and(sn == thr, run < need))

        aq = aq_ref[0]
        kn = kn_ref[0]
        vn = vn_ref[0]
        hpg = NH // A_KV
        for g in range(A_KV):
            qg = aq[g * hpg:(g + 1) * hpg, :]
            s = _dot_nt(qg, k_scr[:, g * HD:(g + 1) * HD]) * HD ** -0.5
            s = jnp.where(sel, s, -jnp.inf)
            s_new = jnp.sum(qg * kn[:, g * HD:(g + 1) * HD], axis=-1, keepdims=True) * HD ** -0.5
            s_new = jnp.where(sel_new, s_new, -jnp.inf)
            m = jnp.maximum(jnp.max(s, axis=-1, keepdims=True), s_new)
            pr = jnp.exp(s - m)
            pn = jnp.exp(s_new - m)
            l = jnp.sum(pr, axis=-1, keepdims=True) + pn
            o = _dot(pr, v_scr[:, g * HD:(g + 1) * HD]) + pn * vn[:, g * HD:(g + 1) * HD]
            o_ref[0, g * hpg:(g + 1) * hpg, :] = o / l


def _dsa_sample(layer, page_table, iq_r, iw, ik_r, aq_r, k_r, v_r, cki, ck, cv, n_pool):
    bd, n_pages = page_table.shape
    past = n_pages * PAGE
    topk = min(TOPK_MAX, (past + 1) // 4)
    base = layer * n_pool
    page = lambda b, p, pt: (base + pt[b, p], 0, 0)
    per_b = lambda b, p, pt: (b, 0, 0)
    grid_spec = pltpu.PrefetchScalarGridSpec(
        num_scalar_prefetch=1,
        grid=(bd, n_pages),
        in_specs=[pl.BlockSpec((1, NH, HD), per_b),
                  pl.BlockSpec((1, NH, 1), per_b),
                  pl.BlockSpec((1, 1, HD), per_b),
                  pl.BlockSpec((1, NH, HD), per_b),
                  pl.BlockSpec((1, 1, 128), per_b),
                  pl.BlockSpec((1, 1, 128), per_b),
                  pl.BlockSpec((1, PAGE, HD), page),
                  pl.BlockSpec((1, PAGE, 128), page),
                  pl.BlockSpec((1, PAGE, 128), page)],
        out_specs=pl.BlockSpec((1, NH, HD), per_b),
        scratch_shapes=[pltpu.VMEM((1, past), F32), pltpu.VMEM((past, 128), F32), pltpu.VMEM((past, 128), F32)],
    )
    out = pl.pallas_call(
        functools.partial(_dsa_sample_kernel, topk=topk, n_pages=n_pages),
        out_shape=jax.ShapeDtypeStruct((bd, NH, HD), F32),
        grid_spec=grid_spec,
        compiler_params=_cp("arbitrary", "arbitrary"),
    )(page_table, iq_r.reshape(bd, NH, HD), iw.reshape(bd, NH, 1), ik_r.reshape(bd, 1, HD),
      aq_r.reshape(bd, NH, HD), k_r.reshape(bd, 1, 128), v_r.reshape(bd, 1, 128), cki, ck, cv)
    return out.reshape(bd, GW)


def _log_sigmoid(x):
    return jnp.minimum(x, 0.0) - jnp.log(1.0 + jnp.exp(-jnp.abs(x)))


def _mlstm_prompt_kernel(m_ref, sm_ref, bi_ref, bf_ref, nw_ref, y_ref, c_ref, n_ref, mm_ref):
    L = CHUNK

    @pl.when(pl.program_id(1) == 0)
    def _():
        c_ref[...] = jnp.zeros(c_ref.shape, F32)
        n_ref[...] = jnp.zeros(n_ref.shape, F32)
        mm_ref[...] = jnp.zeros(mm_ref.shape, F32)

    sm = sm_ref[...]
    ig = sm[:, SM_MI:SM_MI + NH] + bi_ref[...]
    lf = _log_sigmoid(sm[:, SM_MF:SM_MF + NH] + bf_ref[...])
    row, col = _iota((L, L), 0), _iota((L, L), 1)
    causal = col <= row
    bcs = _dot(jnp.where(causal, 1.0, 0.0), lf, HI)
    lane8 = _iota((L, NH), 1)
    for h in range(NH):
        q = m_ref[:, h * HD:(h + 1) * HD]
        k = m_ref[:, GW + h * HD:GW + (h + 1) * HD] * HD ** -0.5
        v = m_ref[:, 2 * GW + h * HD:2 * GW + (h + 1) * HD]
        og = m_ref[:, 3 * GW + h * HD:3 * GW + (h + 1) * HD]
        e = jnp.where(lane8 == h, 1.0, 0.0)
        brow = _dot_nt(e, bcs, HI)
        irow = _dot_nt(e, ig, HI)
        bcol = bcs[:, h:h + 1]
        icol = ig[:, h:h + 1]
        mprev = mm_ref[0, :, h:h + 1]
        dmat = jnp.where(causal, bcol - brow + irow, -jnp.inf)
        inter = bcol + mprev
        mj = jnp.maximum(inter, jnp.max(dmat, axis=-1, keepdims=True))
        s = _dot_nt(q, k) * jnp.exp(dmat - mj)
        iw = jnp.exp(inter - mj)
        cmat = c_ref[0, h]
        nrow = n_ref[0, h:h + 1, :]
        num = _dot(s, v) + iw * _dot_nt(q, cmat)
        den = jnp.sum(s, axis=-1, keepdims=True) + iw * jnp.sum(q * nrow, axis=-1, keepdims=True)
        hc = num / jnp.maximum(jnp.abs(den), jnp.exp(-mj))
        bl = bcol[L - 1:L, :]
        wl = bl - bcol + icol
        m_new = jnp.maximum(bl + mprev, jnp.max(wl, axis=0, keepdims=True))
        dec = jnp.exp(bl + mprev - m_new)
        ws = jnp.exp(wl - m_new)
        c_ref[0, h] = dec * cmat + _dot_tn(ws * v, k)
        n_ref[0, h:h + 1, :] = dec * nrow + jnp.sum(ws * k, axis=0, keepdims=True)
        mm_ref[0, :, h:h + 1] = m_new
        hn = hc * lax.rsqrt(jnp.mean(hc * hc, axis=-1, keepdims=True) + NORM_EPS)
        y_ref[:, h * HD:(h + 1) * HD] = hn * nw_ref[:, h * HD:(h + 1) * HD] * _sigmoid(og)


def _mlstm_prompt(proj, b_i, b_f, nw, nb, t):
    nc = t // CHUNK
    n = nb * t
    row = lambda b, c: b * nc + c
    return pl.pallas_call(
        _mlstm_prompt_kernel,
        out_shape=(jax.ShapeDtypeStruct((n, GW), F32), jax.ShapeDtypeStruct((nb, NH, HD, HD), F32),
                   jax.ShapeDtypeStruct((nb, NH, HD), F32), jax.ShapeDtypeStruct((nb, 1, NH), F32)),
        grid=(nb, nc),
        in_specs=[pl.BlockSpec((CHUNK, 4 * GW), lambda b, c: (row(b, c), 0)),
                  pl.BlockSpec((CHUNK, 128), lambda b, c: (row(b, c), C_SM // 128)),
                  pl.BlockSpec((1, NH), lambda b, c: (0, 0)),
                  pl.BlockSpec((1, NH), lambda b, c: (0, 0)),
                  pl.BlockSpec((1, GW), lambda b, c: (0, 0))],
        out_specs=(pl.BlockSpec((CHUNK, GW), lambda b, c: (row(b, c), 0)),
                   pl.BlockSpec((1, NH, HD, HD), lambda b, c: (b, 0, 0, 0)),
                   pl.BlockSpec((1, NH, HD), lambda b, c: (b, 0, 0)),
                   pl.BlockSpec((1, 1, NH), lambda b, c: (b, 0, 0))),
        compiler_params=_cp("parallel", "arbitrary"),
    )(proj, proj, b_i, b_f, nw)


def _col_of_row(x_row, eye):
    return jnp.sum(jnp.where(eye, x_row, 0.0), axis=1, keepdims=True)


def _row_of_col(x_col, eye):
    return jnp.sum(jnp.where(eye, x_col, 0.0), axis=0, keepdims=True)


def _mlstm_step_kernel(m_ref, sm_ref, bi_ref, bf_ref, nw_ref, c_ref, n_ref, mm_ref, y_ref, co_ref, no_ref, mo_ref):
    xr = m_ref[0]
    sm = sm_ref[0]
    ig = sm[:, SM_MI:SM_MI + NH] + bi_ref[...]
    lf = _log_sigmoid(sm[:, SM_MF:SM_MF + NH] + bf_ref[...])
    eye = _iota((HD, HD), 0) == _iota((HD, HD), 1)
    for h in range(NH):
        q = xr[:, h * HD:(h + 1) * HD]
        k = xr[:, GW + h * HD:GW + (h + 1) * HD] * HD ** -0.5
        v = xr[:, 2 * GW + h * HD:2 * GW + (h + 1) * HD]
        og = xr[:, 3 * GW + h * HD:3 * GW + (h + 1) * HD]
        cmat = c_ref[0, h]
        nrow = n_ref[0, h:h + 1, :]
        i_ = ig[:, h:h + 1]
        inter = lf[:, h:h + 1] + mm_ref[0, :, h:h + 1]
        mj = jnp.maximum(inter, i_)
        vcol = _col_of_row(v, eye)
        s = jnp.sum(q * k, axis=-1, keepdims=True) * jnp.exp(i_ - mj)
        iw = jnp.exp(inter - mj)
        cq = jnp.sum(cmat * q, axis=1, keepdims=True)
        num = s * vcol + iw * cq
        den = s + iw * jnp.sum(nrow * q, axis=-1, keepdims=True)
        hcol = num / jnp.maximum(jnp.abs(den), jnp.exp(-mj))
        dec = jnp.exp(inter - mj)
        ws = jnp.exp(i_ - mj)
        co_ref[0, h] = dec * cmat + (ws * vcol) * k
        no_ref[0, h:h + 1, :] = dec * nrow + ws * k
        mo_ref[0, :, h:h + 1] = mj
        hn = hcol * lax.rsqrt(jnp.mean(hcol * hcol, axis=0, keepdims=True) + NORM_EPS)
        y_ref[0, :, h * HD:(h + 1) * HD] = _row_of_col(hn, eye) * nw_ref[:, h * HD:(h + 1) * HD] * _sigmoid(og)


def _mlstm_step(proj, b_i, b_f, nw, c0, n0, m0):
    bd = proj.shape[0]
    per_b3 = lambda b: (b, 0, 0)
    return pl.pallas_call(
        _mlstm_step_kernel,
        out_shape=(jax.ShapeDtypeStruct((bd, 1, GW), F32), jax.ShapeDtypeStruct((bd, NH, HD, HD), F32),
                   jax.ShapeDtypeStruct((bd, NH, HD), F32), jax.ShapeDtypeStruct((bd, 1, NH), F32)),
        grid=(bd,),
        in_specs=[pl.BlockSpec((1, 1, 4 * GW), lambda b: (b, 0, 0)),
                  pl.BlockSpec((1, 1, 128), lambda b: (b, 0, C_SM // 128)),
                  pl.BlockSpec((1, NH), lambda b: (0, 0)),
                  pl.BlockSpec((1, NH), lambda b: (0, 0)),
                  pl.BlockSpec((1, GW), lambda b: (0, 0)),
                  pl.BlockSpec((1, NH, HD, HD), lambda b: (b, 0, 0, 0)),
                  pl.BlockSpec((1, NH, HD), per_b3),
                  pl.BlockSpec((1, 1, NH), per_b3)],
        out_specs=(pl.BlockSpec((1, 1, GW), per_b3),
                   pl.BlockSpec((1, NH, HD, HD), lambda b: (b, 0, 0, 0)),
                   pl.BlockSpec((1, NH, HD), per_b3),
                   pl.BlockSpec((1, 1, NH), per_b3)),
        compiler_params=_cp("parallel"),
    )(proj.reshape(bd, 1, NP), proj.reshape(bd, 1, NP), b_i, b_f, nw, c0, n0, m0.reshape(bd, 1, NH))


def _rwkv_prep_kernel(x_ref, prev_ref, mu_ref, w0_ref, ww2_ref, a0_ref, wa2_ref, wg2_ref, kk_ref, ka_ref,
                      r_o, lw_o, k_o, v_o, kk_o, a_o, g_o, carry_scr, *, seq):
    x = x_ref[...]
    tm = x.shape[0]
    if seq:
        first = jnp.where(pl.program_id(1) == 0, prev_ref[0], carry_scr[...])
        xprev = jnp.where(_iota((tm, 1), 0) == 0, first, pltpu.roll(x, 1, 0))
        carry_scr[...] = x[tm - 1:tm, :]
    else:
        xprev = prev_ref[...]
    xm = x + (xprev - x) * mu_ref[...]
    r = xm[:, 0:GW]
    kx = xm[:, GW:2 * GW]
    v = xm[:, 2 * GW:3 * GW]
    xw = xm[:, 3 * GW:3 * GW + 64]
    xa = xm[:, 3 * GW + 64:3 * GW + 128]
    xg = xm[:, 3 * GW + 128:R_IN]
    w = -_softplus(-(w0_ref[...] + _dot(jnp.tanh(xw), ww2_ref[...]))) - 0.5
    a = _sigmoid(a0_ref[...] + _dot(xa, wa2_ref[...]))
    r_o[...] = r
    lw_o[...] = -jnp.exp(w)
    v_o[...] = v
    a_o[...] = a
    g_o[...] = _dot(_sigmoid(xg), wg2_ref[...])
    k_o[...] = kx * (1.0 + (a - 1.0) * ka_ref[...])
    kk = kx * kk_ref[...]
    for h in range(NH):
        kh = kk[:, h * HD:(h + 1) * HD]
        nrm = jnp.sqrt(jnp.sum(kh * kh, axis=-1, keepdims=True))
        kk_o[:, h * HD:(h + 1) * HD] = kh / jnp.maximum(nrm, 1e-12)


def _rwkv_prep(proj, prev, mu, w0, ww2, a0, wa2, wg2, k_k, k_a, nb, t):
    n = nb * t
    seq = t > 1
    full = lambda *_: (0, 0)
    if seq:
        tm = min(256, t)
        nt = t // tm
        grid = (nb, nt)
        xmap = lambda b, i: (b * nt + i, C_RIN // R_IN)
        pspec = pl.BlockSpec((1, 1, R_IN), lambda b, i: (b, 0, 0))
        omap = lambda b, i: (b * nt + i, 0)
        sem = ("parallel", "arbitrary")
    else:
        tm = n
        grid = (1,)
        xmap = lambda i: (0, C_RIN // R_IN)
        pspec = pl.BlockSpec((tm, R_IN), lambda i: (0, 0))
        omap = lambda i: (0, 0)
        sem = ("arbitrary",)
    wspecs = [pl.BlockSpec((1, R_IN), full), pl.BlockSpec((1, GW), full), pl.BlockSpec((64, GW), full),
              pl.BlockSpec((1, GW), full), pl.BlockSpec((64, GW), full), pl.BlockSpec((128, GW), full),
              pl.BlockSpec((1, GW), full), pl.BlockSpec((1, GW), full)]
    return pl.pallas_call(
        functools.partial(_rwkv_prep_kernel, seq=seq),
        out_shape=tuple(jax.ShapeDtypeStruct((n, GW), F32) for _ in range(7)),
        grid=grid,
        in_specs=[pl.BlockSpec((tm, R_IN), xmap), pspec] + wspecs,
        out_specs=tuple(pl.BlockSpec((tm, GW), omap) for _ in range(7)),
        scratch_shapes=[pltpu.VMEM((1, R_IN), F32)],
        compiler_params=_cp(*sem),
    )(proj, prev, mu, w0, ww2, a0, wa2, wg2, k_k, k_a)


def _rwkv_post(o, r, k, v, g, rk, lnw, lnb, axis):
    mean = jnp.mean(o, axis=axis, keepdims=True)
    var = jnp.mean(jnp.square(o - mean), axis=axis, keepdims=True)
    return (o - mean) * lax.rsqrt(var + R_LN_EPS)


RWKV_PREC = HI


def _rwkv_scan_kernel(r_ref, lw_ref, k_ref, v_ref, kk_ref, a_ref, g_ref, rk_ref, lnw_ref, lnb_ref, y_ref, s_ref):
    C = CHUNK
    P = RWKV_PREC

    @pl.when(pl.program_id(1) == 0)
    def _():
        s_ref[...] = jnp.zeros(s_ref.shape, F32)

    row, col = _iota((C, C), 0), _iota((C, C), 1)
    incl = col <= row
    strict = col < row
    eye = jnp.where(row == col, 1.0, 0.0)
    lw = lw_ref[...]
    cs = _dot(jnp.where(incl, 1.0, 0.0), lw, HI)
    gam = jnp.exp(cs)
    ginv = jnp.exp(-cs)
    r = r_ref[...]
    k = k_ref[...]
    v = v_ref[...]
    kk = kk_ref[...]
    at = -kk * jnp.exp(cs - lw)
    bt = kk * a_ref[...] * ginv
    kt = k * ginv
    rt = r * gam
    glast = gam[C - 1:C, :]
    bonus_in = r * k * rk_ref[...]
    for h in range(NH):
        sl = slice(h * HD, (h + 1) * HD)
        s0 = s_ref[0, h]
        ah, bh, kh, rh, vh = at[:, sl], bt[:, sl], kt[:, sl], rt[:, sl], v[:, sl]
        a_ab = jnp.where(strict, _dot_nt(ah, bh, P), 0.0)
        a_ak = jnp.where(strict, _dot_nt(ah, kh, P), 0.0)
        x = eye + a_ab
        pm = a_ab
        for _ in range(5):
            pm = _dot(pm, pm, P)
            x = x + _dot(x, pm, P)
        u = _dot(x, _dot_nt(ah, s0, P) + _dot(a_ak, vh, P), P)
        a_rb = jnp.where(incl, _dot_nt(rh, bh, P), 0.0)
        a_rk = jnp.where(incl, _dot_nt(rh, kh, P), 0.0)
        o = _dot_nt(rh, s0, P) + _dot(a_rb, u, P) + _dot(a_rk, vh, P)
        gl = glast[:, sl]
        s_ref[0, h] = s0 * gl + _dot_tn(u, bh * gl, P) + _dot_tn(vh, kh * gl, P)
        mean = jnp.mean(o, axis=-1, keepdims=True)
        var = jnp.mean(jnp.square(o - mean), axis=-1, keepdims=True)
        on = (o - mean) * lax.rsqrt(var + R_LN_EPS) * lnw_ref[:, sl] + lnb_ref[:, sl]
        bonus = jnp.sum(bonus_in[:, sl], axis=-1, keepdims=True) * vh
        y_ref[:, sl] = (on + bonus) * g_ref[:, sl]


def _rwkv_scan(rs, rk, lnw, lnb, nb, t):
    nc = t // CHUNK
    n = nb * t
    rowmap = lambda b, c: (b * nc + c, 0)
    full = lambda b, c: (0, 0)
    return pl.pallas_call(
        _rwkv_scan_kernel,
        out_shape=(jax.ShapeDtypeStruct((n, GW), F32), jax.ShapeDtypeStruct((nb, NH, HD, HD), F32)),
        grid=(nb, nc),
        in_specs=[pl.BlockSpec((CHUNK, GW), rowmap)] * 7 + [pl.BlockSpec((1, GW), full)] * 3,
        out_specs=(pl.BlockSpec((CHUNK, GW), rowmap), pl.BlockSpec((1, NH, HD, HD), lambda b, c: (b, 0, 0, 0))),
        compiler_params=_cp("parallel", "arbitrary"),
    )(*rs, rk, lnw, lnb)


def _rwkv_step_kernel(r_ref, lw_ref, k_ref, v_ref, kk_ref, a_ref, g_ref, rk_ref, lnw_ref, lnb_ref, s_ref,
                      y_ref, so_ref):
    eye = _iota((HD, HD), 0) == _iota((HD, HD), 1)
    r, lw, k, v, kk, a, g = (ref[0] for ref in (r_ref, lw_ref, k_ref, v_ref, kk_ref, a_ref, g_ref))
    bonus_in = r * k * rk_ref[...]
    for h in range(NH):
        sl = slice(h * HD, (h + 1) * HD)
        s0 = s_ref[0, h]
        kkr = kk[:, sl]
        sk = jnp.sum(s0 * kkr, axis=1, keepdims=True)
        vcol = _col_of_row(v[:, sl], eye)
        s1 = s0 * jnp.exp(lw[:, sl]) - sk * (kkr * a[:, sl]) + vcol * k[:, sl]
        so_ref[0, h] = s1
        ocol = jnp.sum(s1 * r[:, sl], axis=1, keepdims=True)
        mean = jnp.mean(ocol, axis=0, keepdims=True)
        var = jnp.mean(jnp.square(ocol - mean), axis=0, keepdims=True)
        on = _row_of_col((ocol - mean) * lax.rsqrt(var + R_LN_EPS), eye) * lnw_ref[:, sl] + lnb_ref[:, sl]
        bonus = jnp.sum(bonus_in[:, sl], axis=-1, keepdims=True) * v[:, sl]
        y_ref[0, :, sl] = (on + bonus) * g[:, sl]


def _rwkv_step(rs, rk, lnw, lnb, s0):
    bd = s0.shape[0]
    per_b = lambda b: (b, 0, 0)
    full = lambda b: (0, 0)
    smap = lambda b: (b, 0, 0, 0)
    return pl.pallas_call(
        _rwkv_step_kernel,
        out_shape=(jax.ShapeDtypeStruct((bd, 1, GW), F32), jax.ShapeDtypeStruct((bd, NH, HD, HD), F32)),
        grid=(bd,),
        in_specs=[pl.BlockSpec((1, 1, GW), per_b)] * 7 + [pl.BlockSpec((1, GW), full)] * 3
                 + [pl.BlockSpec((1, NH, HD, HD), smap)],
        out_specs=(pl.BlockSpec((1, 1, GW), per_b), pl.BlockSpec((1, NH, HD, HD), smap)),
        compiler_params=_cp("parallel"),
    )(*(z.reshape(bd, 1, GW) for z in rs), rk, lnw, lnb, s0)


def _gelu_tanh(x):
    return 0.5 * x * (1.0 + jnp.tanh(math.sqrt(2.0 / math.pi) * (x + 0.044715 * (x * x * x))))


def _s5_kernel(u_ref, bre_ref, bim_ref, lre_ref, lim_ref, cre_ref, cim_ref, d_ref, wg_ref, bg_ref, h0r_ref, h0i_ref,
               y_ref, hr_ref, hi_ref, hre_scr, him_scr, *, nb, tb):
    @pl.when(pl.program_id(0) == 0)
    def _():
        hr_ref[...] = h0r_ref[...]
        hi_ref[...] = h0i_ref[...]

    u = u_ref[...]
    hre_scr[...] = _dot(u, bre_ref[...])
    him_scr[...] = _dot(u, bim_ref[...])
    lr = lre_ref[...]
    li = lim_ref[...]

    def body(t, carry):
        hr, hi = carry
        rows = pl.ds(pl.multiple_of(t * nb, nb), nb)
        nr = lr * hr - li * hi + hre_scr[rows, :]
        ni = lr * hi + li * hr + him_scr[rows, :]
        hre_scr[rows, :] = nr
        him_scr[rows, :] = ni
        return nr, ni

    hr, hi = lax.fori_loop(0, tb, body, (hr_ref[...], hi_ref[...]))
    hr_ref[...] = hr
    hi_ref[...] = hi
    y = _dot(hre_scr[...], cre_ref[...]) - _dot(him_scr[...], cim_ref[...]) + d_ref[...] * u
    y = _gelu_tanh(y)
    y_ref[...] = y * _sigmoid(_dot(y, wg_ref[...]) + bg_ref[...])


def _s5(u_tm, mats, h0r, h0i, nb, t):
    bre, bim, lre, lim, cre, cim, d, wg, bg = mats
    tb = min(64, t)
    full = lambda i: (0, 0)
    return pl.pallas_call(
        functools.partial(_s5_kernel, nb=nb, tb=tb),
        out_shape=(jax.ShapeDtypeStruct((t * nb, GW), F32), jax.ShapeDtypeStruct((nb, S5_W), F32),
                   jax.ShapeDtypeStruct((nb, S5_W), F32)),
        grid=(t // tb,),
        in_specs=[pl.BlockSpec((tb * nb, GW), lambda i: (i, 0)),
                  pl.BlockSpec((GW, S5_W), full), pl.BlockSpec((GW, S5_W), full),
                  pl.BlockSpec((1, S5_W), full), pl.BlockSpec((1, S5_W), full),
                  pl.BlockSpec((S5_W, GW), full), pl.BlockSpec((S5_W, GW), full),
                  pl.BlockSpec((1, GW), full), pl.BlockSpec((GW, GW), full), pl.BlockSpec((1, GW), full),
                  pl.BlockSpec((nb, S5_W), full), pl.BlockSpec((nb, S5_W), full)],
        out_specs=(pl.BlockSpec((tb * nb, GW), lambda i: (i, 0)),
                   pl.BlockSpec((nb, S5_W), full), pl.BlockSpec((nb, S5_W), full)),
        scratch_shapes=[pltpu.VMEM((tb * nb, S5_W), F32), pltpu.VMEM((tb * nb, S5_W), F32)],
        compiler_params=_cp("arbitrary"),
    )(u_tm, bre, bim, lre, lim, cre, cim, d, wg, bg, h0r, h0i)


def _s5_mats(a_re, a_im, b_re, b_im, c_re, c_im, d_skip, log_dt, w_glu, b_glu):
    dt = jnp.exp(log_dt)
    mag = jnp.exp(a_re * dt)
    lb_re, lb_im = mag * jnp.cos(a_im * dt), mag * jnp.sin(a_im * dt)
    den = a_re * a_re + a_im * a_im
    f_re = ((lb_re - 1.0) * a_re + lb_im * a_im) / den
    f_im = (lb_im * a_re - (lb_re - 1.0) * a_im) / den
    bb_re = f_re[..., None] * b_re - f_im[..., None] * b_im
    bb_im = f_re[..., None] * b_im + f_im[..., None] * b_re
    eye = jnp.eye(S5_G, dtype=F32)
    bd = lambda bb: jnp.einsum('gpc,gh->gchp', bb, eye).reshape(GW, S5_W)
    cd = lambda cc: jnp.einsum('gcp,gh->gphc', cc, eye).reshape(S5_W, GW)
    return (bd(bb_re), bd(bb_im), lb_re.reshape(1, S5_W), lb_im.reshape(1, S5_W), cd(c_re), cd(c_im),
            d_skip.reshape(1, GW), w_glu, b_glu.reshape(1, GW))


def _ffn_up_kernel(h_ref, wa_ref, wb_ref, cw_ref, cb_ref, s0_ref, s1_ref, y_ref, a_ref, carry_scr, *, seq):
    h = h_ref[...]
    a = _dot(h, wa_ref[...])
    b = _dot(h, wb_ref[...])
    tm = a.shape[0]
    if seq:
        @pl.when(pl.program_id(2) == 0)
        def _():
            carry_scr[...] = s0_ref[0]

        rowid = _iota((tm, 1), 0)
        c0 = carry_scr[0:1, :]
        c1 = carry_scr[1:2, :]
        a1 = jnp.where(rowid == 0, c1, pltpu.roll(a, 1, 0))
        a2 = jnp.where(rowid == 0, c0, jnp.where(rowid == 1, c1, pltpu.roll(a, 2, 0)))
        carry_scr[...] = a[tm - 2:tm, :]
        a_ref[0] = a[tm - 2:tm, :]
    else:
        a2 = s0_ref[...]
        a1 = s1_ref[...]
        a_ref[...] = a
    c = cb_ref[...] + a2 * cw_ref[0:1, :] + a1 * cw_ref[1:2, :] + a * cw_ref[2:3, :]
    y_ref[...] = (c * _sigmoid(c) * b).astype(BF16)


def _ffn_up(h2, w_up, cw, cb, st, nb, t):
    n, d = h2.shape
    dff = w_up.shape[1] // 2
    tn = 512
    nj = dff // tn
    if t > 1:
        tm = min(512, t)
        nt = t // tm
        y, fc = pl.pallas_call(
            functools.partial(_ffn_up_kernel, seq=True),
            out_shape=(jax.ShapeDtypeStruct((n, dff), BF16), jax.ShapeDtypeStruct((nb, 2, dff), F32)),
            grid=(nb, nj, nt),
            in_specs=[pl.BlockSpec((tm, d), lambda b, j, i: (b * nt + i, 0)),
                      pl.BlockSpec((d, tn), lambda b, j, i: (0, j)),
                      pl.BlockSpec((d, tn), lambda b, j, i: (0, nj + j)),
                      pl.BlockSpec((3, tn), lambda b, j, i: (0, j)),
                      pl.BlockSpec((1, tn), lambda b, j, i: (0, j)),
                      pl.BlockSpec((1, 2, tn), lambda b, j, i: (b, 0, j)),
                      pl.BlockSpec((1, 2, tn), lambda b, j, i: (b, 0, j))],
            out_specs=(pl.BlockSpec((tm, tn), lambda b, j, i: (b * nt + i, j)),
                       pl.BlockSpec((1, 2, tn), lambda b, j, i: (b, 0, j))),
            scratch_shapes=[pltpu.VMEM((2, tn), F32)],
            compiler_params=_cp("parallel", "parallel", "arbitrary"),
        )(h2, w_up, w_up, cw, cb, st, st)
        return y, fc
    s0, s1 = st[:, 0, :], st[:, 1, :]
    y, a = pl.pallas_call(
        functools.partial(_ffn_up_kernel, seq=False),
        out_shape=(jax.ShapeDtypeStruct((n, dff), BF16), jax.ShapeDtypeStruct((n, dff), F32)),
        grid=(nj,),
        in_specs=[pl.BlockSpec((n, d), lambda j: (0, 0)),
                  pl.BlockSpec((d, tn), lambda j: (0, j)),
                  pl.BlockSpec((d, tn), lambda j: (0, nj + j)),
                  pl.BlockSpec((3, tn), lambda j: (0, j)),
                  pl.BlockSpec((1, tn), lambda j: (0, j)),
                  pl.BlockSpec((n, tn), lambda j: (0, j)),
                  pl.BlockSpec((n, tn), lambda j: (0, j))],
        out_specs=(pl.BlockSpec((n, tn), lambda j: (0, j)), pl.BlockSpec((n, tn), lambda j: (0, j))),
        scratch_shapes=[pltpu.VMEM((2, tn), F32)],
        compiler_params=_cp("parallel"),
    )(h2, w_up, w_up, cw, cb, s0, s1)
    return y, jnp.stack([s1, a], axis=1)


def _permute_w_in(w):
    cols = lambda s, n: w[:, s:s + n]
    parts = [cols(0, 4 * GW), cols(2064, GW), cols(2832, GW), cols(5208, GW), cols(3416, R_IN),
             cols(2576, 128), cols(2704, 128), cols(3344, HD), cols(2048, NH), cols(2056, NH), cols(3408, NH),
             jnp.zeros((w.shape[0], NP - C_SM - HD - 3 * NH), w.dtype)]
    return jnp.concatenate(parts, axis=1).astype(BF16)


def _layer(x2, nb, t, l, W, st, tables, cache):
    n = nb * t
    c0, n0, m0, rs0, rsh0, sre0, sim0, conv0 = st
    proj = _in_proj(x2, W['norm_mix'], W['w_in'])
    aq_r, iq_r, k_r, v_r, ik_r = _rope_call(proj, tables[0], tables[1], *tables[2])

    if cache is None:
        ym, c1, n1, m1 = _mlstm_prompt(proj, W['m_b_i'], W['m_b_f'], W['m_norm'], nb, t)
        ya = _dsa_prompt(iq_r, proj, ik_r, aq_r, k_r, v_r, nb, t)
    else:
        ym, c1, n1, m1 = _mlstm_step(proj, W['m_b_i'], W['m_b_f'], W['m_norm'], c0, n0, m0)
        ym = ym.reshape(n, GW)
        cki, ck, cv, page_table, n_pool = cache
        ya = _dsa_sample(l, page_table, iq_r, proj[:, C_SM + SM_IW:C_SM + SM_IW + NH], ik_r, aq_r, k_r, v_r,
                         cki, ck, cv, n_pool)

    prev = rsh0.reshape(nb, 1, R_IN) if t > 1 else rsh0
    rs = _rwkv_prep(proj, prev, W['r_mu'], W['r_w0'], W['r_w_w2'], W['r_a0'], W['r_w_a2'], W['r_w_g2'],
                    W['r_k_k'], W['r_k_a'], nb, t)
    if t > 1:
        yr, rs1 = _rwkv_scan(rs, W['r_r_k'], W['r_ln_w'], W['r_ln_b'], nb, t)
    else:
        yr, rs1 = _rwkv_step(rs, W['r_r_k'], W['r_ln_w'], W['r_ln_b'], rs0)
        yr = yr.reshape(n, GW)
    rsh1 = proj.reshape(nb, t, NP)[:, t - 1, C_RIN:C_RIN + R_IN]

    su = proj[:, C_SU:C_SU + GW]
    u_tm = su.reshape(nb, t, GW).transpose(1, 0, 2).reshape(t * nb, GW)
    ys_tm, sre1, sim1 = _s5(u_tm, W['s5'], sre0.reshape(nb, S5_W), sim0.reshape(nb, S5_W), nb, t)
    ys = ys_tm.reshape(t, nb, GW).transpose(1, 0, 2).reshape(n, GW)

    mixed = jnp.concatenate([ym, ya, yr, ys], axis=-1).astype(BF16)
    x2 = _res_matmul(x2, mixed, W['w_out'])
    h2 = _rmsnorm(x2, W['norm_ffn'], BF16)
    y, conv1 = _ffn_up(h2, W['ffn_w_up'], W['ffn_conv_w'], W['ffn_conv_b'], conv0, nb, t)
    x2 = _res_matmul(x2, y, W['ffn_w_down'])
    outs = (k_r.reshape(nb, t, A_KV, HD), v_r.reshape(nb, t, A_KV, HD), ik_r.reshape(nb, t, HD),
            c1, n1, m1.reshape(nb, NH), rs1, rsh1, sre1.reshape(nb, S5_G, S5_P), sim1.reshape(nb, S5_G, S5_P), conv1)
    return x2, outs


def kernel(x_prompt, x_sample, cache_k, cache_v, cache_kidx, page_table, state_mlstm_c, state_mlstm_n,
           state_mlstm_m, state_rwkv_s, state_rwkv_shift, state_s5_re, state_s5_im, state_ffn_conv,
           norm_mix, w_in, w_out, m_b_i, m_b_f, m_norm, r_mu, r_w0, r_w_w2, r_a0, r_w_a2, r_w_g2,
           r_k_k, r_k_a, r_r_k, r_ln_w, r_ln_b, s5_a_re, s5_a_im, s5_b_re, s5_b_im, s5_c_re, s5_c_im,
           s5_d, s5_log_dt, s5_w_glu, s5_b_glu, norm_ffn, ffn_w_up, ffn_conv_w, ffn_conv_b, ffn_w_down,
           norm_final):
    bp, tp, d = x_prompt.shape
    bs, ts, _ = x_sample.shape
    assert ts == 1 and tp % CHUNK == 0
    depth = w_in.shape[0]
    n_pool = cache_k.shape[1]
    past = page_table.shape[1] * PAGE
    dff = ffn_conv_b.shape[-1]

    row = lambda z: z.reshape(1, -1)
    layers = []
    for l in range(depth):
        layers.append(dict(
            norm_mix=row(norm_mix[l]), w_in=_permute_w_in(w_in[l]), w_out=w_out[l].astype(BF16),
            m_b_i=row(m_b_i[l]), m_b_f=row(m_b_f[l]), m_norm=row(m_norm[l]),
            r_mu=row(r_mu[l]), r_w0=row(r_w0[l]), r_w_w2=r_w_w2[l], r_a0=row(r_a0[l]), r_w_a2=r_w_a2[l],
            r_w_g2=r_w_g2[l], r_k_k=row(r_k_k[l]), r_k_a=row(r_k_a[l]), r_r_k=row(r_r_k[l]),
            r_ln_w=row(r_ln_w[l]), r_ln_b=row(r_ln_b[l]),
            s5=_s5_mats(s5_a_re[l], s5_a_im[l], s5_b_re[l], s5_b_im[l], s5_c_re[l], s5_c_im[l], s5_d[l],
                        s5_log_dt[l], s5_w_glu[l], s5_b_glu[l]),
            norm_ffn=row(norm_ffn[l]), ffn_w_up=ffn_w_up[l].astype(BF16), ffn_conv_w=ffn_conv_w[l],
            ffn_conv_b=row(ffn_conv_b[l]), ffn_w_down=ffn_w_down[l].astype(BF16)))

    cos_p, sin_p = _rope_tables(jnp.arange(tp))
    cos_s, sin_s = _rope_tables(jnp.full((bs,), past))
    tab_p = (cos_p, sin_p, (bp, tp))
    tab_s = (cos_s, sin_s, (1, bs))

    zeros = lambda *s: jnp.zeros(s, F32)
    st_p = (zeros(bp, NH, HD, HD), zeros(bp, NH, HD), zeros(bp, NH), zeros(bp, NH, HD, HD), zeros(bp, R_IN),
            zeros(bp, S5_G, S5_P), zeros(bp, S5_G, S5_P), zeros(bp, 2, dff))
    cki = cache_kidx.reshape(depth * n_pool, PAGE, HD)
    ck = cache_k.reshape(depth * n_pool, PAGE, A_KV * HD)
    cv = cache_v.reshape(depth * n_pool, PAGE, A_KV * HD)

    xp = x_prompt.reshape(bp * tp, d)
    xs = x_sample.reshape(bs, d)
    new_p, new_s = [], []
    for l in range(depth):
        xp, sp = _layer(xp, bp, tp, l, layers[l], st_p, tab_p, None)
        st_s = (state_mlstm_c[l], state_mlstm_n[l], state_mlstm_m[l], state_rwkv_s[l], state_rwkv_shift[l],
                state_s5_re[l], state_s5_im[l], state_ffn_conv[l])
        xs, ss = _layer(xs, bs, 1, l, layers[l], st_s, tab_s, (cki, ck, cv, page_table, n_pool))
        new_p.append(sp)
        new_s.append(ss)
    (k_p, v_p, ki_p, mc_p, mn_p, mm_p, rs_p, rsh_p, sre_p, sim_p, fc_p) = [jnp.stack(z) for z in zip(*new_p)]
    (k_s, v_s, ki_s, mc_s, mn_s, mm_s, rs_s, rsh_s, sre_s, sim_s, fc_s) = [jnp.stack(z) for z in zip(*new_s)]
    y_prompt = _rmsnorm(xp, row(norm_final), F32).reshape(bp, tp, d)
    y_sample = _rmsnorm(xs, row(norm_final), F32).reshape(bs, ts, d)
    return (y_prompt, y_sample, k_p, k_s, v_p, v_s, ki_p, ki_s, mc_p, mc_s, mn_p, mn_s, mm_p, mm_s,
            rs_p, rs_s, rsh_p, rsh_s, sre_p, sre_s, sim_p, sim_s, fc_p, fc_s)
```

```python
import functools
import math

import jax
import jax.numpy as jnp
from jax import lax
from jax.experimental import pallas as pl
from jax.experimental.pallas import tpu as pltpu

F32 = jnp.float32
BF16 = jnp.bfloat16
HI = lax.Precision.HIGHEST

HD = 64
NH = 8
GW = NH * HD
A_KV = 2
PAGE = 128
TOPK_MAX = 256
ROPE_THETA = 10000.0
R_IN = 3 * GW + 64 + 64 + 128
R_LN_EPS = 64e-5
S5_G, S5_CH, S5_P = 32, 16, 64
S5_W = S5_G * S5_P
NORM_EPS = 1e-6
CHUNK = 64
IDX_SCALE = HD ** -0.5 * NH ** -0.5

C_M, C_AQ, C_IQ, C_SU, C_RIN, C_AK, C_AV, C_SM = 0, 2048, 2560, 3072, 3584, 5376, 5504, 5632
NP = 5760
SM_IK, SM_MI, SM_MF, SM_IW = 0, 64, 72, 80

VMEM_LIMIT = 56 * 1024 * 1024


def _cp(*sem):
    return pltpu.CompilerParams(dimension_semantics=sem, vmem_limit_bytes=VMEM_LIMIT)


def _dot(a, b, prec=None):
    return jnp.dot(a, b, preferred_element_type=F32, precision=prec)


def _dot_nt(a, b, prec=None):
    return lax.dot_general(a, b, (((1,), (1,)), ((), ())), preferred_element_type=F32, precision=prec)


def _dot_tn(a, b, prec=None):
    return lax.dot_general(a, b, (((0,), (0,)), ((), ())), preferred_element_type=F32, precision=prec)


def _sigmoid(x):
    return 1.0 / (1.0 + jnp.exp(-x))


def _softplus(x):
    return jnp.maximum(x, 0.0) + jnp.log(1.0 + jnp.exp(-jnp.abs(x)))


def _iota(shape, dim):
    return lax.broadcasted_iota(jnp.int32, shape, dim)


def _inproj_kernel(x_ref, g_ref, w_ref, o_ref, h_scr):
    @pl.when(pl.program_id(1) == 0)
    def _():
        x = x_ref[...]
        ms = jnp.mean(x * x, axis=-1, keepdims=True)
        h_scr[...] = (x * lax.rsqrt(ms + NORM_EPS) * g_ref[...]).astype(BF16)

    o_ref[...] = _dot(h_scr[...], w_ref[...])


def _in_proj(x2, g, w):
    n, d = x2.shape
    npad = w.shape[1]
    tm = min(512, n)
    tn = 640
    return pl.pallas_call(
        _inproj_kernel,
        out_shape=jax.ShapeDtypeStruct((n, npad), F32),
        grid=(n // tm, npad // tn),
        in_specs=[pl.BlockSpec((tm, d), lambda i, j: (i, 0)),
                  pl.BlockSpec((1, d), lambda i, j: (0, 0)),
                  pl.BlockSpec((d, tn), lambda i, j: (0, j))],
        out_specs=pl.BlockSpec((tm, tn), lambda i, j: (i, j)),
        scratch_shapes=[pltpu.VMEM((tm, d), BF16)],
        compiler_params=_cp("parallel", "arbitrary"),
    )(x2, g, w)


def _rmsnorm_kernel(x_ref, g_ref, o_ref):
    x = x_ref[...]
    ms = jnp.mean(x * x, axis=-1, keepdims=True)
    o_ref[...] = (x * lax.rsqrt(ms + NORM_EPS) * g_ref[...]).astype(o_ref.dtype)


def _rmsnorm(x2, g, dtype):
    n, d = x2.shape
    tm = min(512, n)
    return pl.pallas_call(
        _rmsnorm_kernel,
        out_shape=jax.ShapeDtypeStruct((n, d), dtype),
        grid=(n // tm,),
        in_specs=[pl.BlockSpec((tm, d), lambda i: (i, 0)), pl.BlockSpec((1, d), lambda i: (0, 0))],
        out_specs=pl.BlockSpec((tm, d), lambda i: (i, 0)),
        compiler_params=_cp("parallel"),
    )(x2, g)


def _resmm_kernel(r_ref, y_ref, w_ref, o_ref):
    o_ref[...] = r_ref[...] + _dot(y_ref[...], w_ref[...])


def _res_matmul(res, y, w):
    n, k = y.shape
    d = w.shape[1]
    tm = min(512, n)
    tn = 512
    return pl.pallas_call(
        _resmm_kernel,
        out_shape=jax.ShapeDtypeStruct((n, d), F32),
        grid=(n // tm, d // tn),
        in_specs=[pl.BlockSpec((tm, tn), lambda i, j: (i, j)),
                  pl.BlockSpec((tm, k), lambda i, j: (i, 0)),
                  pl.BlockSpec((k, tn), lambda i, j: (0, j))],
        out_specs=pl.BlockSpec((tm, tn), lambda i, j: (i, j)),
        compiler_params=_cp("parallel", "arbitrary"),
    )(res, y, w)


def _rope(x, cos, sin):
    w = x.shape[1]
    first = (_iota(x.shape, 1) & (HD - 1)) < HD // 2
    sw = jnp.where(first, pltpu.roll(x, w - HD // 2, 1), pltpu.roll(x, HD // 2, 1))
    return x * cos + sw * sin


def _rope_kernel(aq_ref, iq_ref, ak_ref, av_ref, sm_ref, cos_ref, sin_ref, aqo, iqo, ko, vo, iko):
    cos = cos_ref[...]
    sin = sin_ref[...]
    aqo[...] = _rope(aq_ref[...], cos, sin)
    iqo[...] = _rope(iq_ref[...], cos, sin)
    ko[...] = _rope(ak_ref[...], cos[:, :128], sin[:, :128])
    vo[...] = av_ref[...]
    iko[...] = _rope(sm_ref[...], cos[:, :128], sin[:, :128])[:, :HD]


def _rope_call(proj, cos, sin, nb, nt_rows):
    n = proj.shape[0]
    tm = min(512, nt_rows)
    nt = nt_rows // tm
    row = lambda b, i: b * nt + i
    return pl.pallas_call(
        _rope_kernel,
        out_shape=(jax.ShapeDtypeStruct((n, GW), F32), jax.ShapeDtypeStruct((n, GW), F32),
                   jax.ShapeDtypeStruct((n, 128), F32), jax.ShapeDtypeStruct((n, 128), F32),
                   jax.ShapeDtypeStruct((n, HD), F32)),
        grid=(nb, nt),
        in_specs=[pl.BlockSpec((tm, GW), lambda b, i: (row(b, i), C_AQ // GW)),
                  pl.BlockSpec((tm, GW), lambda b, i: (row(b, i), C_IQ // GW)),
                  pl.BlockSpec((tm, 128), lambda b, i: (row(b, i), C_AK // 128)),
                  pl.BlockSpec((tm, 128), lambda b, i: (row(b, i), C_AV // 128)),
                  pl.BlockSpec((tm, 128), lambda b, i: (row(b, i), C_SM // 128)),
                  pl.BlockSpec((tm, GW), lambda b, i: (i, 0)),
                  pl.BlockSpec((tm, GW), lambda b, i: (i, 0))],
        out_specs=(pl.BlockSpec((tm, GW), lambda b, i: (row(b, i), 0)),
                   pl.BlockSpec((tm, GW), lambda b, i: (row(b, i), 0)),
                   pl.BlockSpec((tm, 128), lambda b, i: (row(b, i), 0)),
                   pl.BlockSpec((tm, 128), lambda b, i: (row(b, i), 0)),
                   pl.BlockSpec((tm, HD), lambda b, i: (row(b, i), 0))),
        compiler_params=_cp("parallel", "parallel"),
    )(proj, proj, proj, proj, proj, cos, sin)


def _rope_tables(pos):
    half = HD // 2
    inv = ROPE_THETA ** (-jnp.arange(half, dtype=F32) / half)
    ang = pos.astype(F32)[:, None] * inv[None, :]
    cos, sin = jnp.cos(ang), jnp.sin(ang)
    cos64 = jnp.concatenate([cos, cos], axis=-1)
    sin64 = jnp.concatenate([-sin, sin], axis=-1)
    return jnp.tile(cos64, (1, NH)), jnp.tile(sin64, (1, NH))


def _kth_largest(sc, extra, kk):
    kf = jnp.float32(kk)

    def count_ge(c):
        n = jnp.sum(jnp.where(sc >= c, 1.0, 0.0), axis=-1, keepdims=True)
        if extra is not None:
            n = n + jnp.where(extra >= c, 1.0, 0.0)
        return n

    def key_to_f(key):
        bits = key ^ ((key >> 31) & jnp.int32(0x7FFFFFFF))
        return lax.bitcast_convert_type(bits, F32)

    r = sc.shape[0]
    int_min = jnp.int32(-2 ** 31)
    lo = jnp.where(count_ge(jnp.zeros((r, 1), F32)) >= kf, jnp.int32(0), int_min)

    def body(j, lo):
        cand = lo + jnp.left_shift(jnp.int32(1), jnp.int32(30) - j)
        ok = count_ge(key_to_f(cand)) >= kf
        return jnp.where(ok, cand, lo)

    lo = lax.fori_loop(0, 31, body, lo)
    key_neg_inf = jnp.int32(-2 ** 31 + 0x7FFFFF)
    return jnp.where(lo <= key_neg_inf, -jnp.inf, key_to_f(lo))


def _strict_upper_bf16(n):
    return jnp.where(_iota((n, n), 0) < _iota((n, n), 1), 1.0, 0.0).astype(BF16)


def _split_bf16(x):
    hi = x.astype(BF16).astype(F32)
    return hi, x - hi


def _dsa_prompt_kernel(iq_ref, sm_ref, ik_ref, aq_ref, k_ref, v_ref, o_ref, sel_scr, kcat_scr, *, topk, qb, n_ext):
    t_keys = ik_ref.shape[0]
    i = pl.program_id(1)
    per_ext = (t_keys // qb) // n_ext

    @pl.when(i == 0)
    def _():
        hi, lo = _split_bf16(ik_ref[...])
        kcat_scr[...] = jnp.concatenate([hi, lo, hi], axis=1)

    iq = iq_ref[...]
    aq = aq_ref[...]
    wts = sm_ref[:, SM_IW:SM_IW + NH] * IDX_SCALE
    tq = i * qb + _iota((qb, 1), 0)

    def body(ext):
        kcat = kcat_scr[0:ext, :]
        sc = jnp.zeros((qb, ext), F32)
        for h in range(NH):
            hi, lo = _split_bf16(iq[:, h * HD:(h + 1) * HD])
            qk = _dot_nt(jnp.concatenate([hi, hi, lo], axis=1), kcat)
            sc = sc + jnp.maximum(qk, 0.0) * wts[:, h:h + 1]
        causal = _iota((1, ext), 1) <= tq
        sc = jnp.where(causal, sc, -jnp.inf)

        thr = _kth_largest(sc, None, topk)
        gt = sc > thr
        eq = sc == thr
        n_gt = jnp.sum(jnp.where(gt, 1.0, 0.0), axis=-1, keepdims=True)
        n_eq = jnp.sum(jnp.where(eq, 1.0, 0.0), axis=-1, keepdims=True)
        need = jnp.float32(topk) - n_gt
        sel_scr[:, 0:ext] = jnp.where(jnp.logical_and(sc >= thr, causal), 1.0, 0.0)
        tie = jnp.logical_and(n_eq > need, thr > -jnp.inf)

        @pl.when(jnp.max(jnp.where(tie, 1.0, 0.0)) > 0.5)
        def _():
            ut = _strict_upper_bf16(128)
            run = jnp.zeros((qb, 1), F32)
            for c in range(ext // 128):
                sl = slice(c * 128, (c + 1) * 128)
                eqc = jnp.where(eq[:, sl], 1.0, 0.0)
                pref = _dot(eqc.astype(BF16), ut) + run
                keep = jnp.logical_or(gt[:, sl], jnp.logical_and(eq[:, sl], pref < need))
                sel_scr[:, sl] = jnp.where(jnp.logical_and(keep, causal[:, sl]), 1.0, 0.0)
                run = run + jnp.sum(eqc, axis=-1, keepdims=True)

        sel = sel_scr[:, 0:ext] > 0.5
        for g in range(A_KV):
            kg = k_ref[0:ext, g * HD:(g + 1) * HD]
            vg = v_ref[0:ext, g * HD:(g + 1) * HD]
            for j in range(NH // A_KV):
                h = g * (NH // A_KV) + j
                s = _dot_nt(aq[:, h * HD:(h + 1) * HD], kg) * HD ** -0.5
                s = jnp.where(sel, s, -jnp.inf)
                m = jnp.max(s, axis=-1, keepdims=True)
                p = jnp.exp(s - m)
                l = jnp.sum(p, axis=-1, keepdims=True)
                o_ref[:, h * HD:(h + 1) * HD] = _dot(p, vg) / l

    for j in range(n_ext):
        pl.when(i // per_ext == j)(functools.partial(body, (j + 1) * (t_keys // n_ext)))


def _dsa_prompt(iq_r, proj, ik_r, aq_r, k_r, v_r, nb, t):
    qb = min(128, t)
    nq = t // qb
    n_ext = min(4, nq)
    assert nq % n_ext == 0
    topk = min(TOPK_MAX, t // 4)
    n = nb * t
    row = lambda b, i: b * nq + i
    return pl.pallas_call(
        functools.partial(_dsa_prompt_kernel, topk=topk, qb=qb, n_ext=n_ext),
        out_shape=jax.ShapeDtypeStruct((n, GW), F32),
        grid=(nb, nq),
        in_specs=[pl.BlockSpec((qb, GW), lambda b, i: (row(b, i), 0)),
                  pl.BlockSpec((qb, 128), lambda b, i: (row(b, i), C_SM // 128)),
                  pl.BlockSpec((t, HD), lambda b, i: (b, 0)),
                  pl.BlockSpec((qb, GW), lambda b, i: (row(b, i), 0)),
                  pl.BlockSpec((t, 128), lambda b, i: (b, 0)),
                  pl.BlockSpec((t, 128), lambda b, i: (b, 0))],
        out_specs=pl.BlockSpec((qb, GW), lambda b, i: (row(b, i), 0)),
        scratch_shapes=[pltpu.VMEM((qb, t), F32), pltpu.VMEM((t, 3 * HD), F32)],
        compiler_params=_cp("parallel", "arbitrary"),
    )(iq_r, proj, ik_r, aq_r, k_r, v_r)


def _dsa_sample_score_kernel(*refs, n_pages):
    _, iq_ref, w_ref, ikn_ref = refs[:4]
    pages = refs[4:4 + n_pages]
    sc_ref, sn_ref = refs[4 + n_pages:]
    iq = iq_ref[0]
    w = w_ref[0] * IDX_SCALE
    for c in range(n_pages):
        qk = _dot_nt(iq, pages[c][0], HI)
        sc_ref[0, :, c * PAGE:(c + 1) * PAGE] = jnp.sum(jnp.maximum(qk, 0.0) * w, axis=0, keepdims=True)
    qkn = jnp.sum(iq * ikn_ref[0], axis=-1, keepdims=True)
    sn = jnp.sum(jnp.maximum(qkn, 0.0) * w, axis=0, keepdims=True)
    sn_ref[0] = jnp.broadcast_to(sn, (1, 128))


def _dsa_sample_select_kernel(sc_ref, sn_ref, sel_ref, seln_ref, *, topk):
    sc = sc_ref[...]
    sn = sn_ref[:, 0:1]
    bd, s_keys = sc.shape
    thr = _kth_largest(sc, sn, topk)
    gt = sc > thr
    eq = sc == thr
    n_gt = jnp.sum(jnp.where(gt, 1.0, 0.0), axis=-1, keepdims=True) + jnp.where(sn > thr, 1.0, 0.0)
    need = jnp.float32(topk) - n_gt
    ut = _strict_upper_bf16(PAGE)
    run = jnp.zeros((bd, 1), F32)
    for c in range(s_keys // PAGE):
        sl = slice(c * PAGE, (c + 1) * PAGE)
        eqc = jnp.where(eq[:, sl], 1.0, 0.0)
        pref = _dot(eqc.astype(BF16), ut) + run
        keep = jnp.logical_or(gt[:, sl], jnp.logical_and(eq[:, sl], pref < need))
        sel_ref[:, sl] = jnp.where(keep, 1.0, 0.0)
        run = run + jnp.sum(eqc, axis=-1, keepdims=True)
    sel_new = jnp.logical_or(sn > thr, jnp.logical_and(sn == thr, run < need))
    seln_ref[...] = jnp.broadcast_to(jnp.where(sel_new, 1.0, 0.0), seln_ref.shape)


def _dsa_sample_attn_kernel(*refs, n_pages):
    _, aq_ref, kn_ref, vn_ref, sel_ref, seln_ref = refs[:6]
    kpages = refs[6:6 + n_pages]
    vpages = refs[6 + n_pages:6 + 2 * n_pages]
    o_ref, k_scr, v_scr = refs[6 + 2 * n_pages:]
    for c in range(n_pages):
        k_scr[c * PAGE:(c + 1) * PAGE, :] = kpages[c][0]
        v_scr[c * PAGE:(c + 1) * PAGE, :] = vpages[c][0]
    sel = sel_ref[0] > 0.5
    sel_new = seln_ref[0][:, 0:1] > 0.5
    aq = aq_ref[0]
    kn = kn_ref[0]
    vn = vn_ref[0]
    hpg = NH // A_KV
    for g in range(A_KV):
        qg = aq[g * hpg:(g + 1) * hpg, :]
        s = _dot_nt(qg, k_scr[:, g * HD:(g + 1) * HD]) * HD ** -0.5
        s = jnp.where(sel, s, -jnp.inf)
        s_new = jnp.sum(qg * kn[:, g * HD:(g + 1) * HD], axis=-1, keepdims=True) * HD ** -0.5
        s_new = jnp.where(sel_new, s_new, -jnp.inf)
        m = jnp.maximum(jnp.max(s, axis=-1, keepdims=True), s_new)
        pr = jnp.exp(s - m)
        pn = jnp.exp(s_new - m)
        l = jnp.sum(pr, axis=-1, keepdims=True) + pn
        o = _dot(pr, v_scr[:, g * HD:(g + 1) * HD]) + pn * vn[:, g * HD:(g + 1) * HD]
        o_ref[0, g * hpg:(g + 1) * hpg, :] = o / l


def _dsa_sample(layer, page_table, iq_r, iw, ik_r, aq_r, k_r, v_r, cki, ck, cv, n_pool):
    bd, n_pages = page_table.shape
    past = n_pages * PAGE
    topk = min(TOPK_MAX, (past + 1) // 4)
    base = layer * n_pool
    per_b = lambda b, pt: (b, 0, 0)

    def page_specs(width):
        return [pl.BlockSpec((1, PAGE, width), lambda b, pt, c=c: (base + pt[b, c], 0, 0)) for c in range(n_pages)]

    sc, sn = pl.pallas_call(
        functools.partial(_dsa_sample_score_kernel, n_pages=n_pages),
        out_shape=(jax.ShapeDtypeStruct((bd, 1, past), F32), jax.ShapeDtypeStruct((bd, 1, 128), F32)),
        grid_spec=pltpu.PrefetchScalarGridSpec(
            num_scalar_prefetch=1,
            grid=(bd,),
            in_specs=[pl.BlockSpec((1, NH, HD), per_b), pl.BlockSpec((1, NH, 1), per_b),
                      pl.BlockSpec((1, 1, HD), per_b)] + page_specs(HD),
            out_specs=(pl.BlockSpec((1, 1, past), per_b), pl.BlockSpec((1, 1, 128), per_b))),
        compiler_params=_cp("arbitrary"),
    )(page_table, iq_r.reshape(bd, NH, HD), iw.reshape(bd, NH, 1), ik_r.reshape(bd, 1, HD), *([cki] * n_pages))

    sel, seln = pl.pallas_call(
        functools.partial(_dsa_sample_select_kernel, topk=topk),
        out_shape=(jax.ShapeDtypeStruct((bd, past), F32), jax.ShapeDtypeStruct((bd, 128), F32)),
    )(sc.reshape(bd, past), sn.reshape(bd, 128))

    out = pl.pallas_call(
        functools.partial(_dsa_sample_attn_kernel, n_pages=n_pages),
        out_shape=jax.ShapeDtypeStruct((bd, NH, HD), F32),
        grid_spec=pltpu.PrefetchScalarGridSpec(
            num_scalar_prefetch=1,
            grid=(bd,),
            in_specs=[pl.BlockSpec((1, NH, HD), per_b), pl.BlockSpec((1, 1, 128), per_b),
                      pl.BlockSpec((1, 1, 128), per_b), pl.BlockSpec((1, 1, past), per_b),
                      pl.BlockSpec((1, 1, 128), per_b)] + page_specs(128) + page_specs(128),
            out_specs=pl.BlockSpec((1, NH, HD), per_b),
            scratch_shapes=[pltpu.VMEM((past, 128), F32), pltpu.VMEM((past, 128), F32)]),
        compiler_params=_cp("arbitrary"),
    )(page_table, aq_r.reshape(bd, NH, HD), k_r.reshape(bd, 1, 128), v_r.reshape(bd, 1, 128),
      sel.reshape(bd, 1, past), seln.reshape(bd, 1, 128), *([ck] * n_pages), *([cv] * n_pages))
    return out.reshape(bd, GW)


def _log_sigmoid(x):
    return jnp.minimum(x, 0.0) - jnp.log(1.0 + jnp.exp(-jnp.abs(x)))


MLSTM_ROWS = 2 * CHUNK


def _mlstm_prompt_kernel(m_ref, sm_ref, gb_ref, nw_ref, y_ref, c_ref, n_ref, mm_ref):
    L = CHUNK

    @pl.when(pl.program_id(1) == 0)
    def _():
        c_ref[...] = jnp.zeros(c_ref.shape, F32)
        n_ref[...] = jnp.zeros(n_ref.shape, F32)
        mm_ref[...] = jnp.zeros(mm_ref.shape, F32)

    smb = sm_ref[...] + gb_ref[...]
    smt = smb.T
    ig_all = smb[:, SM_MI:SM_MI + NH]
    lf_all = _log_sigmoid(smb[:, SM_MF:SM_MF + NH])
    igt_all = smt[SM_MI:SM_MI + NH, :]
    lft_all = _log_sigmoid(smt[SM_MF:SM_MF + NH, :])
    row, col = _iota((L, L), 0), _iota((L, L), 1)
    causal = col <= row
    tril = jnp.where(causal, 1.0, 0.0)
    triu = jnp.where(col >= row, 1.0, 0.0)
    for cc in range(MLSTM_ROWS // L):
        rows = slice(cc * L, (cc + 1) * L)
        ig = ig_all[rows]
        bcs = _dot(tril, lf_all[rows], HI)
        bcst = _dot(lft_all[:, rows], triu, HI)
        igt = igt_all[:, rows]
        for h in range(NH):
            q = m_ref[rows, h * HD:(h + 1) * HD]
            k = m_ref[rows, GW + h * HD:GW + (h + 1) * HD] * HD ** -0.5
            v = m_ref[rows, 2 * GW + h * HD:2 * GW + (h + 1) * HD]
            og = m_ref[rows, 3 * GW + h * HD:3 * GW + (h + 1) * HD]
            bcol = bcs[:, h:h + 1]
            icol = ig[:, h:h + 1]
            mprev = mm_ref[0, :, h:h + 1]
            dmat = jnp.where(causal, bcol - bcst[h:h + 1, :] + igt[h:h + 1, :], -jnp.inf)
            inter = bcol + mprev
            mj = jnp.maximum(inter, jnp.max(dmat, axis=-1, keepdims=True))
            s = _dot_nt(q, k) * jnp.exp(dmat - mj)
            iw = jnp.exp(inter - mj)
            cmat = c_ref[0, h]
            nrow = n_ref[0, h:h + 1, :]
            num = _dot(s, v) + iw * _dot_nt(q, cmat)
            den = jnp.sum(s, axis=-1, keepdims=True) + iw * jnp.sum(q * nrow, axis=-1, keepdims=True)
            hc = num / jnp.maximum(jnp.abs(den), jnp.exp(-mj))
            bl = bcol[L - 1:L, :]
            wl = bl - bcol + icol
            m_new = jnp.maximum(bl + mprev, jnp.max(wl, axis=0, keepdims=True))
            dec = jnp.exp(bl + mprev - m_new)
            ws = jnp.exp(wl - m_new)
            c_ref[0, h] = dec * cmat + _dot_tn(ws * v, k)
            n_ref[0, h:h + 1, :] = dec * nrow + jnp.sum(ws * k, axis=0, keepdims=True)
            mm_ref[0, :, h:h + 1] = m_new
            hn = hc * lax.rsqrt(jnp.mean(hc * hc, axis=-1, keepdims=True) + NORM_EPS)
            y_ref[rows, h * HD:(h + 1) * HD] = hn * nw_ref[:, h * HD:(h + 1) * HD] * _sigmoid(og)


def _gate_bias_row(b_i, b_f):
    z = lambda n: jnp.zeros((1, n), F32)
    return jnp.concatenate([z(SM_MI), b_i, b_f, z(128 - SM_MF - NH)], axis=1)


def _mlstm_prompt(proj, b_i, b_f, nw, nb, t):
    tm = MLSTM_ROWS
    nc = t // tm
    n = nb * t
    row = lambda b, c: b * nc + c
    return pl.pallas_call(
        _mlstm_prompt_kernel,
        out_shape=(jax.ShapeDtypeStruct((n, GW), F32), jax.ShapeDtypeStruct((nb, NH, HD, HD), F32),
                   jax.ShapeDtypeStruct((nb, NH, HD), F32), jax.ShapeDtypeStruct((nb, 1, NH), F32)),
        grid=(nb, nc),
        in_specs=[pl.BlockSpec((tm, 4 * GW), lambda b, c: (row(b, c), 0)),
                  pl.BlockSpec((tm, 128), lambda b, c: (row(b, c), C_SM // 128)),
                  pl.BlockSpec((1, 128), lambda b, c: (0, 0)),
                  pl.BlockSpec((1, GW), lambda b, c: (0, 0))],
        out_specs=(pl.BlockSpec((tm, GW), lambda b, c: (row(b, c), 0)),
                   pl.BlockSpec((1, NH, HD, HD), lambda b, c: (b, 0, 0, 0)),
                   pl.BlockSpec((1, NH, HD), lambda b, c: (b, 0, 0)),
                   pl.BlockSpec((1, 1, NH), lambda b, c: (b, 0, 0))),
        compiler_params=_cp("parallel", "arbitrary"),
    )(proj, proj, _gate_bias_row(b_i, b_f), nw)


def _col_of_row(x_row, eye):
    return jnp.sum(jnp.where(eye, x_row, 0.0), axis=1, keepdims=True)


def _row_of_col(x_col, eye):
    return jnp.sum(jnp.where(eye, x_col, 0.0), axis=0, keepdims=True)


def _mlstm_step_kernel(m_ref, sm_ref, bi_ref, bf_ref, nw_ref, c_ref, n_ref, mm_ref, y_ref, co_ref, no_ref, mo_ref):
    xr = m_ref[0]
    sm = sm_ref[0]
    ig = sm[:, SM_MI:SM_MI + NH] + bi_ref[...]
    lf = _log_sigmoid(sm[:, SM_MF:SM_MF + NH] + bf_ref[...])
    eye = _iota((HD, HD), 0) == _iota((HD, HD), 1)
    for h in range(NH):
        q = xr[:, h * HD:(h + 1) * HD]
        k = xr[:, GW + h * HD:GW + (h + 1) * HD] * HD ** -0.5
        v = xr[:, 2 * GW + h * HD:2 * GW + (h + 1) * HD]
        og = xr[:, 3 * GW + h * HD:3 * GW + (h + 1) * HD]
        cmat = c_ref[0, h]
        nrow = n_ref[0, h:h + 1, :]
        i_ = ig[:, h:h + 1]
        inter = lf[:, h:h + 1] + mm_ref[0, :, h:h + 1]
        mj = jnp.maximum(inter, i_)
        vcol = _col_of_row(v, eye)
        s = jnp.sum(q * k, axis=-1, keepdims=True) * jnp.exp(i_ - mj)
        iw = jnp.exp(inter - mj)
        cq = jnp.sum(cmat * q, axis=1, keepdims=True)
        num = s * vcol + iw * cq
        den = s + iw * jnp.sum(nrow * q, axis=-1, keepdims=True)
        hcol = num / jnp.maximum(jnp.abs(den), jnp.exp(-mj))
        dec = jnp.exp(inter - mj)
        ws = jnp.exp(i_ - mj)
        co_ref[0, h] = dec * cmat + (ws * vcol) * k
        no_ref[0, h:h + 1, :] = dec * nrow + ws * k
        mo_ref[0, :, h:h + 1] = mj
        hn = hcol * lax.rsqrt(jnp.mean(hcol * hcol, axis=0, keepdims=True) + NORM_EPS)
        y_ref[0, :, h * HD:(h + 1) * HD] = _row_of_col(hn, eye) * nw_ref[:, h * HD:(h + 1) * HD] * _sigmoid(og)


def _mlstm_step(proj, b_i, b_f, nw, c0, n0, m0):
    bd = proj.shape[0]
    per_b3 = lambda b: (b, 0, 0)
    return pl.pallas_call(
        _mlstm_step_kernel,
        out_shape=(jax.ShapeDtypeStruct((bd, 1, GW), F32), jax.ShapeDtypeStruct((bd, NH, HD, HD), F32),
                   jax.ShapeDtypeStruct((bd, NH, HD), F32), jax.ShapeDtypeStruct((bd, 1, NH), F32)),
        grid=(bd,),
        in_specs=[pl.BlockSpec((1, 1, 4 * GW), lambda b: (b, 0, 0)),
                  pl.BlockSpec((1, 1, 128), lambda b: (b, 0, C_SM // 128)),
                  pl.BlockSpec((1, NH), lambda b: (0, 0)),
                  pl.BlockSpec((1, NH), lambda b: (0, 0)),
                  pl.BlockSpec((1, GW), lambda b: (0, 0)),
                  pl.BlockSpec((1, NH, HD, HD), lambda b: (b, 0, 0, 0)),
                  pl.BlockSpec((1, NH, HD), per_b3),
                  pl.BlockSpec((1, 1, NH), per_b3)],
        out_specs=(pl.BlockSpec((1, 1, GW), per_b3),
                   pl.BlockSpec((1, NH, HD, HD), lambda b: (b, 0, 0, 0)),
                   pl.BlockSpec((1, NH, HD), per_b3),
                   pl.BlockSpec((1, 1, NH), per_b3)),
        compiler_params=_cp("parallel"),
    )(proj.reshape(bd, 1, NP), proj.reshape(bd, 1, NP), b_i, b_f, nw, c0, n0, m0.reshape(bd, 1, NH))


def _rwkv_prep_kernel(x_ref, prev_ref, mu_ref, w0_ref, ww2_ref, a0_ref, wa2_ref, wg2_ref, kk_ref, ka_ref,
                      r_o, lw_o, k_o, v_o, kk_o, a_o, g_o, carry_scr, *, seq):
    x = x_ref[...]
    tm = x.shape[0]
    if seq:
        first = jnp.where(pl.program_id(1) == 0, prev_ref[0], carry_scr[...])
        xprev = jnp.where(_iota((tm, 1), 0) == 0, first, pltpu.roll(x, 1, 0))
        carry_scr[...] = x[tm - 1:tm, :]
    else:
        xprev = prev_ref[...]
    xm = x + (xprev - x) * mu_ref[...]
    r = xm[:, 0:GW]
    kx = xm[:, GW:2 * GW]
    v = xm[:, 2 * GW:3 * GW]
    xw = xm[:, 3 * GW:3 * GW + 64]
    xa = xm[:, 3 * GW + 64:3 * GW + 128]
    xg = xm[:, 3 * GW + 128:R_IN]
    w = -_softplus(-(w0_ref[...] + _dot(jnp.tanh(xw), ww2_ref[...]))) - 0.5
    a = _sigmoid(a0_ref[...] + _dot(xa, wa2_ref[...]))
    r_o[...] = r
    lw_o[...] = -jnp.exp(w)
    v_o[...] = v
    a_o[...] = a
    g_o[...] = _dot(_sigmoid(xg), wg2_ref[...])
    k_o[...] = kx * (1.0 + (a - 1.0) * ka_ref[...])
    kk = kx * kk_ref[...]
    for h in range(NH):
        kh = kk[:, h * HD:(h + 1) * HD]
        nrm = jnp.sqrt(jnp.sum(kh * kh, axis=-1, keepdims=True))
        kk_o[:, h * HD:(h + 1) * HD] = kh / jnp.maximum(nrm, 1e-12)


def _rwkv_prep(proj, prev, mu, w0, ww2, a0, wa2, wg2, k_k, k_a, nb, t):
    n = nb * t
    seq = t > 1
    full = lambda *_: (0, 0)
    if seq:
        tm = min(256, t)
        nt = t // tm
        grid = (nb, nt)
        xmap = lambda b, i: (b * nt + i, C_RIN // R_IN)
        pspec = pl.BlockSpec((1, 1, R_IN), lambda b, i: (b, 0, 0))
        omap = lambda b, i: (b * nt + i, 0)
        sem = ("parallel", "arbitrary")
    else:
        tm = n
        grid = (1,)
        xmap = lambda i: (0, C_RIN // R_IN)
        pspec = pl.BlockSpec((tm, R_IN), lambda i: (0, 0))
        omap = lambda i: (0, 0)
        sem = ("arbitrary",)
    wspecs = [pl.BlockSpec((1, R_IN), full), pl.BlockSpec((1, GW), full), pl.BlockSpec((64, GW), full),
              pl.BlockSpec((1, GW), full), pl.BlockSpec((64, GW), full), pl.BlockSpec((128, GW), full),
              pl.BlockSpec((1, GW), full), pl.BlockSpec((1, GW), full)]
    return pl.pallas_call(
        functools.partial(_rwkv_prep_kernel, seq=seq),
        out_shape=tuple(jax.ShapeDtypeStruct((n, GW), F32) for _ in range(7)),
        grid=grid,
        in_specs=[pl.BlockSpec((tm, R_IN), xmap), pspec] + wspecs,
        out_specs=tuple(pl.BlockSpec((tm, GW), omap) for _ in range(7)),
        scratch_shapes=[pltpu.VMEM((1, R_IN), F32)],
        compiler_params=_cp(*sem),
    )(proj, prev, mu, w0, ww2, a0, wa2, wg2, k_k, k_a)


def _rwkv_scan_kernel(r_ref, lw_ref, k_ref, v_ref, kk_ref, a_ref, g_ref, rk_ref, lnw_ref, lnb_ref, y_ref, s_ref):
    C = CHUNK

    @pl.when(pl.program_id(1) == 0)
    def _():
        s_ref[...] = jnp.zeros(s_ref.shape, F32)

    row, col = _iota((C, C), 0), _iota((C, C), 1)
    incl = col <= row
    strict = col < row
    eye = jnp.where(row == col, 1.0, 0.0)
    lw = lw_ref[...]
    cs = _dot(jnp.where(incl, 1.0, 0.0), lw, HI)
    gam = jnp.exp(cs)
    ginv = jnp.exp(-cs)
    r = r_ref[...]
    k = k_ref[...]
    v = v_ref[...]
    kk = kk_ref[...]
    at = -kk * jnp.exp(cs - lw)
    bt = kk * a_ref[...] * ginv
    kt = k * ginv
    rt = r * gam
    glast = gam[C - 1:C, :]
    bonus_in = r * k * rk_ref[...]
    for h in range(NH):
        sl = slice(h * HD, (h + 1) * HD)
        s0 = s_ref[0, h]
        bh, kh, vh = bt[:, sl], kt[:, sl], v[:, sl]
        ar = jnp.concatenate([at[:, sl], rt[:, sl]], axis=0)
        gb = _dot_nt(ar, bh)
        gk = _dot_nt(ar, kh)
        gs = _dot_nt(ar, s0)
        n_ab = jnp.where(strict, gb[:C], 0.0)
        a_ak = jnp.where(strict, gk[:C], 0.0)
        x = eye + n_ab
        pm = _dot(n_ab, n_ab)
        for j in range(5):
            xn = x + _dot(x, pm)
            if j < 4:
                pm = _dot(pm, pm)
            x = xn
        u = _dot(x, gs[:C] + _dot(a_ak, vh))
        o = gs[C:] + _dot(jnp.where(incl, gb[C:], 0.0), u) + _dot(jnp.where(incl, gk[C:], 0.0), vh)
        gl = glast[:, sl]
        uv = jnp.concatenate([u, vh], axis=0)
        bk = jnp.concatenate([bh * gl, kh * gl], axis=0)
        s_ref[0, h] = s0 * gl + _dot_tn(uv, bk)
        mean = jnp.mean(o, axis=-1, keepdims=True)
        var = jnp.mean(jnp.square(o - mean), axis=-1, keepdims=True)
        on = (o - mean) * lax.rsqrt(var + R_LN_EPS) * lnw_ref[:, sl] + lnb_ref[:, sl]
        bonus = jnp.sum(bonus_in[:, sl], axis=-1, keepdims=True) * vh
        y_ref[:, sl] = (on + bonus) * g_ref[:, sl]


def _rwkv_scan(rs, rk, lnw, lnb, nb, t):
    nc = t // CHUNK
    n = nb * t
    rowmap = lambda b, c: (b * nc + c, 0)
    full = lambda b, c: (0, 0)
    return pl.pallas_call(
        _rwkv_scan_kernel,
        out_shape=(jax.ShapeDtypeStruct((n, GW), F32), jax.ShapeDtypeStruct((nb, NH, HD, HD), F32)),
        grid=(nb, nc),
        in_specs=[pl.BlockSpec((CHUNK, GW), rowmap)] * 7 + [pl.BlockSpec((1, GW), full)] * 3,
        out_specs=(pl.BlockSpec((CHUNK, GW), rowmap), pl.BlockSpec((1, NH, HD, HD), lambda b, c: (b, 0, 0, 0))),
        compiler_params=_cp("parallel", "arbitrary"),
    )(*rs, rk, lnw, lnb)


def _rwkv_step_kernel(r_ref, lw_ref, k_ref, v_ref, kk_ref, a_ref, g_ref, rk_ref, lnw_ref, lnb_ref, s_ref,
                      y_ref, so_ref):
    eye = _iota((HD, HD), 0) == _iota((HD, HD), 1)
    r, lw, k, v, kk, a, g = (ref[0] for ref in (r_ref, lw_ref, k_ref, v_ref, kk_ref, a_ref, g_ref))
    bonus_in = r * k * rk_ref[...]
    for h in range(NH):
        sl = slice(h * HD, (h + 1) * HD)
        s0 = s_ref[0, h]
        kkr = kk[:, sl]
        sk = jnp.sum(s0 * kkr, axis=1, keepdims=True)
        vcol = _col_of_row(v[:, sl], eye)
        s1 = s0 * jnp.exp(lw[:, sl]) - sk * (kkr * a[:, sl]) + vcol * k[:, sl]
        so_ref[0, h] = s1
        ocol = jnp.sum(s1 * r[:, sl], axis=1, keepdims=True)
        mean = jnp.mean(ocol, axis=0, keepdims=True)
        var = jnp.mean(jnp.square(ocol - mean), axis=0, keepdims=True)
        on = _row_of_col((ocol - mean) * lax.rsqrt(var + R_LN_EPS), eye) * lnw_ref[:, sl] + lnb_ref[:, sl]
        bonus = jnp.sum(bonus_in[:, sl], axis=-1, keepdims=True) * v[:, sl]
        y_ref[0, :, sl] = (on + bonus) * g[:, sl]


def _rwkv_step(rs, rk, lnw, lnb, s0):
    bd = s0.shape[0]
    per_b = lambda b: (b, 0, 0)
    full = lambda b: (0, 0)
    smap = lambda b: (b, 0, 0, 0)
    return pl.pallas_call(
        _rwkv_step_kernel,
        out_shape=(jax.ShapeDtypeStruct((bd, 1, GW), F32), jax.ShapeDtypeStruct((bd, NH, HD, HD), F32)),
        grid=(bd,),
        in_specs=[pl.BlockSpec((1, 1, GW), per_b)] * 7 + [pl.BlockSpec((1, GW), full)] * 3
                 + [pl.BlockSpec((1, NH, HD, HD), smap)],
        out_specs=(pl.BlockSpec((1, 1, GW), per_b), pl.BlockSpec((1, NH, HD, HD), smap)),
        compiler_params=_cp("parallel"),
    )(*(z.reshape(bd, 1, GW) for z in rs), rk, lnw, lnb, s0)


def _gelu_tanh(x):
    return 0.5 * x * (1.0 + jnp.tanh(math.sqrt(2.0 / math.pi) * (x + 0.044715 * (x * x * x))))


def _s5_kernel(u_ref, bre_ref, bim_ref, lre_ref, lim_ref, cre_ref, cim_ref, d_ref, wg_ref, bg_ref, h0r_ref, h0i_ref,
               y_ref, hr_ref, hi_ref, hre_scr, him_scr, *, nb, tb):
    @pl.when(pl.program_id(0) == 0)
    def _():
        hr_ref[...] = h0r_ref[...]
        hi_ref[...] = h0i_ref[...]

    u = u_ref[...]
    hre_scr[...] = _dot(u, bre_ref[...])
    him_scr[...] = _dot(u, bim_ref[...])
    lr = lre_ref[...]
    li = lim_ref[...]

    def body(t, carry):
        hr, hi = carry
        rows = pl.ds(pl.multiple_of(t * nb, nb), nb)
        nr = lr * hr - li * hi + hre_scr[rows, :]
        ni = lr * hi + li * hr + him_scr[rows, :]
        hre_scr[rows, :] = nr
        him_scr[rows, :] = ni
        return nr, ni

    hr, hi = lax.fori_loop(0, tb, body, (hr_ref[...], hi_ref[...]))
    hr_ref[...] = hr
    hi_ref[...] = hi
    y = _dot(hre_scr[...], cre_ref[...]) - _dot(him_scr[...], cim_ref[...]) + d_ref[...] * u
    y = _gelu_tanh(y)
    y_ref[...] = y * _sigmoid(_dot(y, wg_ref[...]) + bg_ref[...])


def _s5(u_tm, mats, h0r, h0i, nb, t):
    bre, bim, lre, lim, cre, cim, d, wg, bg = mats
    tb = min(64, t)
    full = lambda i: (0, 0)
    return pl.pallas_call(
        functools.partial(_s5_kernel, nb=nb, tb=tb),
        out_shape=(jax.ShapeDtypeStruct((t * nb, GW), F32), jax.ShapeDtypeStruct((nb, S5_W), F32),
                   jax.ShapeDtypeStruct((nb, S5_W), F32)),
        grid=(t // tb,),
        in_specs=[pl.BlockSpec((tb * nb, GW), lambda i: (i, 0)),
                  pl.BlockSpec((GW, S5_W), full), pl.BlockSpec((GW, S5_W), full),
                  pl.BlockSpec((1, S5_W), full), pl.BlockSpec((1, S5_W), full),
                  pl.BlockSpec((S5_W, GW), full), pl.BlockSpec((S5_W, GW), full),
                  pl.BlockSpec((1, GW), full), pl.BlockSpec((GW, GW), full), pl.BlockSpec((1, GW), full),
                  pl.BlockSpec((nb, S5_W), full), pl.BlockSpec((nb, S5_W), full)],
        out_specs=(pl.BlockSpec((tb * nb, GW), lambda i: (i, 0)),
                   pl.BlockSpec((nb, S5_W), full), pl.BlockSpec((nb, S5_W), full)),
        scratch_shapes=[pltpu.VMEM((tb * nb, S5_W), F32), pltpu.VMEM((tb * nb, S5_W), F32)],
        compiler_params=_cp("arbitrary"),
    )(u_tm, bre, bim, lre, lim, cre, cim, d, wg, bg, h0r, h0i)


def _s5_mats(a_re, a_im, b_re, b_im, c_re, c_im, d_skip, log_dt, w_glu, b_glu):
    dt = jnp.exp(log_dt)
    mag = jnp.exp(a_re * dt)
    lb_re, lb_im = mag * jnp.cos(a_im * dt), mag * jnp.sin(a_im * dt)
    den = a_re * a_re + a_im * a_im
    f_re = ((lb_re - 1.0) * a_re + lb_im * a_im) / den
    f_im = (lb_im * a_re - (lb_re - 1.0) * a_im) / den
    bb_re = f_re[..., None] * b_re - f_im[..., None] * b_im
    bb_im = f_re[..., None] * b_im + f_im[..., None] * b_re
    eye = jnp.eye(S5_G, dtype=F32)
    bd = lambda bb: jnp.einsum('gpc,gh->gchp', bb, eye).reshape(GW, S5_W)
    cd = lambda cc: jnp.einsum('gcp,gh->gphc', cc, eye).reshape(S5_W, GW)
    return (bd(bb_re), bd(bb_im), lb_re.reshape(1, S5_W), lb_im.reshape(1, S5_W), cd(c_re), cd(c_im),
            d_skip.reshape(1, GW), w_glu, b_glu.reshape(1, GW))


def _ffn_up_kernel(h_ref, wa_ref, wb_ref, cw_ref, cb_ref, s0_ref, s1_ref, y_ref, a_ref, carry_scr, *, seq):
    h = h_ref[...]
    a = _dot(h, wa_ref[...])
    b = _dot(h, wb_ref[...])
    tm = a.shape[0]
    if seq:
        @pl.when(pl.program_id(2) == 0)
        def _():
            carry_scr[...] = s0_ref[0]

        rowid = _iota((tm, 1), 0)
        c0 = carry_scr[0:1, :]
        c1 = carry_scr[1:2, :]
        a1 = jnp.where(rowid == 0, c1, pltpu.roll(a, 1, 0))
        a2 = jnp.where(rowid == 0, c0, jnp.where(rowid == 1, c1, pltpu.roll(a, 2, 0)))
        carry_scr[...] = a[tm - 2:tm, :]
        a_ref[0] = a[tm - 2:tm, :]
    else:
        a2 = s0_ref[...]
        a1 = s1_ref[...]
        a_ref[...] = a
    c = cb_ref[...] + a2 * cw_ref[0:1, :] + a1 * cw_ref[1:2, :] + a * cw_ref[2:3, :]
    y_ref[...] = (c * _sigmoid(c) * b).astype(BF16)


def _ffn_up(h2, w_up, cw, cb, st, nb, t):
    n, d = h2.shape
    dff = w_up.shape[1] // 2
    tn = 512
    nj = dff // tn
    if t > 1:
        tm = min(512, t)
        nt = t // tm
        y, fc = pl.pallas_call(
            functools.partial(_ffn_up_kernel, seq=True),
            out_shape=(jax.ShapeDtypeStruct((n, dff), BF16), jax.ShapeDtypeStruct((nb, 2, dff), F32)),
            grid=(nb, nj, nt),
            in_specs=[pl.BlockSpec((tm, d), lambda b, j, i: (b * nt + i, 0)),
                      pl.BlockSpec((d, tn), lambda b, j, i: (0, j)),
                      pl.BlockSpec((d, tn), lambda b, j, i: (0, nj + j)),
                      pl.BlockSpec((3, tn), lambda b, j, i: (0, j)),
                      pl.BlockSpec((1, tn), lambda b, j, i: (0, j)),
                      pl.BlockSpec((1, 2, tn), lambda b, j, i: (b, 0, j)),
                      pl.BlockSpec((1, 2, tn), lambda b, j, i: (b, 0, j))],
            out_specs=(pl.BlockSpec((tm, tn), lambda b, j, i: (b * nt + i, j)),
                       pl.BlockSpec((1, 2, tn), lambda b, j, i: (b, 0, j))),
            scratch_shapes=[pltpu.VMEM((2, tn), F32)],
            compiler_params=_cp("parallel", "parallel", "arbitrary"),
        )(h2, w_up, w_up, cw, cb, st, st)
        return y, fc
    s0, s1 = st[:, 0, :], st[:, 1, :]
    y, a = pl.pallas_call(
        functools.partial(_ffn_up_kernel, seq=False),
        out_shape=(jax.ShapeDtypeStruct((n, dff), BF16), jax.ShapeDtypeStruct((n, dff), F32)),
        grid=(nj,),
        in_specs=[pl.BlockSpec((n, d), lambda j: (0, 0)),
                  pl.BlockSpec((d, tn), lambda j: (0, j)),
                  pl.BlockSpec((d, tn), lambda j: (0, nj + j)),
                  pl.BlockSpec((3, tn), lambda j: (0, j)),
                  pl.BlockSpec((1, tn), lambda j: (0, j)),
                  pl.BlockSpec((n, tn), lambda j: (0, j)),
                  pl.BlockSpec((n, tn), lambda j: (0, j))],
        out_specs=(pl.BlockSpec((n, tn), lambda j: (0, j)), pl.BlockSpec((n, tn), lambda j: (0, j))),
        scratch_shapes=[pltpu.VMEM((2, tn), F32)],
        compiler_params=_cp("parallel"),
    )(h2, w_up, w_up, cw, cb, s0, s1)
    return y, jnp.stack([s1, a], axis=1)


def _permute_w_in(w):
    cols = lambda s, n: w[:, s:s + n]
    parts = [cols(0, 4 * GW), cols(2064, GW), cols(2832, GW), cols(5208, GW), cols(3416, R_IN),
             cols(2576, 128), cols(2704, 128), cols(3344, HD), cols(2048, NH), cols(2056, NH), cols(3408, NH),
             jnp.zeros((w.shape[0], NP - C_SM - HD - 3 * NH), w.dtype)]
    return jnp.concatenate(parts, axis=1).astype(BF16)


def _layer(x2, nb, t, l, W, st, tables, cache):
    n = nb * t
    c0, n0, m0, rs0, rsh0, sre0, sim0, conv0 = st
    proj = _in_proj(x2, W['norm_mix'], W['w_in'])
    aq_r, iq_r, k_r, v_r, ik_r = _rope_call(proj, tables[0], tables[1], *tables[2])

    if cache is None:
        ym, c1, n1, m1 = _mlstm_prompt(proj, W['m_b_i'], W['m_b_f'], W['m_norm'], nb, t)
        ya = _dsa_prompt(iq_r, proj, ik_r, aq_r, k_r, v_r, nb, t)
    else:
        ym, c1, n1, m1 = _mlstm_step(proj, W['m_b_i'], W['m_b_f'], W['m_norm'], c0, n0, m0)
        ym = ym.reshape(n, GW)
        cki, ck, cv, page_table, n_pool = cache
        ya = _dsa_sample(l, page_table, iq_r, proj[:, C_SM + SM_IW:C_SM + SM_IW + NH], ik_r, aq_r, k_r, v_r,
                         cki, ck, cv, n_pool)

    prev = rsh0.reshape(nb, 1, R_IN) if t > 1 else rsh0
    rs = _rwkv_prep(proj, prev, W['r_mu'], W['r_w0'], W['r_w_w2'], W['r_a0'], W['r_w_a2'], W['r_w_g2'],
                    W['r_k_k'], W['r_k_a'], nb, t)
    if t > 1:
        yr, rs1 = _rwkv_scan(rs, W['r_r_k'], W['r_ln_w'], W['r_ln_b'], nb, t)
    else:
        yr, rs1 = _rwkv_step(rs, W['r_r_k'], W['r_ln_w'], W['r_ln_b'], rs0)
        yr = yr.reshape(n, GW)
    rsh1 = proj.reshape(nb, t, NP)[:, t - 1, C_RIN:C_RIN + R_IN]

    su = proj[:, C_SU:C_SU + GW]
    u_tm = su.reshape(nb, t, GW).transpose(1, 0, 2).reshape(t * nb, GW)
    ys_tm, sre1, sim1 = _s5(u_tm, W['s5'], sre0.reshape(nb, S5_W), sim0.reshape(nb, S5_W), nb, t)
    ys = ys_tm.reshape(t, nb, GW).transpose(1, 0, 2).reshape(n, GW)

    mixed = jnp.concatenate([ym, ya, yr, ys], axis=-1).astype(BF16)
    x2 = _res_matmul(x2, mixed, W['w_out'])
    h2 = _rmsnorm(x2, W['norm_ffn'], BF16)
    y, conv1 = _ffn_up(h2, W['ffn_w_up'], W['ffn_conv_w'], W['ffn_conv_b'], conv0, nb, t)
    x2 = _res_matmul(x2, y, W['ffn_w_down'])
    outs = (k_r.reshape(nb, t, A_KV, HD), v_r.reshape(nb, t, A_KV, HD), ik_r.reshape(nb, t, HD),
            c1, n1, m1.reshape(nb, NH), rs1, rsh1, sre1.reshape(nb, S5_G, S5_P), sim1.reshape(nb, S5_G, S5_P), conv1)
    return x2, outs


def kernel(x_prompt, x_sample, cache_k, cache_v, cache_kidx, page_table, state_mlstm_c, state_mlstm_n,
           state_mlstm_m, state_rwkv_s, state_rwkv_shift, state_s5_re, state_s5_im, state_ffn_conv,
           norm_mix, w_in, w_out, m_b_i, m_b_f, m_norm, r_mu, r_w0, r_w_w2, r_a0, r_w_a2, r_w_g2,
           r_k_k, r_k_a, r_r_k, r_ln_w, r_ln_b, s5_a_re, s5_a_im, s5_b_re, s5_b_im, s5_c_re, s5_c_im,
           s5_d, s5_log_dt, s5_w_glu, s5_b_glu, norm_ffn, ffn_w_up, ffn_conv_w, ffn_conv_b, ffn_w_down,
           norm_final):
    bp, tp, d = x_prompt.shape
    bs, ts, _ = x_sample.shape
    assert ts == 1 and tp % CHUNK == 0
    depth = w_in.shape[0]
    n_pool = cache_k.shape[1]
    past = page_table.shape[1] * PAGE
    dff = ffn_conv_b.shape[-1]

    row = lambda z: z.reshape(1, -1)
    layers = []
    for l in range(depth):
        layers.append(dict(
            norm_mix=row(norm_mix[l]), w_in=_permute_w_in(w_in[l]), w_out=w_out[l].astype(BF16),
            m_b_i=row(m_b_i[l]), m_b_f=row(m_b_f[l]), m_norm=row(m_norm[l]),
            r_mu=row(r_mu[l]), r_w0=row(r_w0[l]), r_w_w2=r_w_w2[l], r_a0=row(r_a0[l]), r_w_a2=r_w_a2[l],
            r_w_g2=r_w_g2[l], r_k_k=row(r_k_k[l]), r_k_a=row(r_k_a[l]), r_r_k=row(r_r_k[l]),
            r_ln_w=row(r_ln_w[l]), r_ln_b=row(r_ln_b[l]),
            s5=_s5_mats(s5_a_re[l], s5_a_im[l], s5_b_re[l], s5_b_im[l], s5_c_re[l], s5_c_im[l], s5_d[l],
                        s5_log_dt[l], s5_w_glu[l], s5_b_glu[l]),
            norm_ffn=row(norm_ffn[l]), ffn_w_up=ffn_w_up[l].astype(BF16), ffn_conv_w=ffn_conv_w[l],
            ffn_conv_b=row(ffn_conv_b[l]), ffn_w_down=ffn_w_down[l].astype(BF16)))

    cos_p, sin_p = _rope_tables(jnp.arange(tp))
    cos_s, sin_s = _rope_tables(jnp.full((bs,), past))
    tab_p = (cos_p, sin_p, (bp, tp))
    tab_s = (cos_s, sin_s, (1, bs))

    zeros = lambda *s: jnp.zeros(s, F32)
    st_p = (zeros(bp, NH, HD, HD), zeros(bp, NH, HD), zeros(bp, NH), zeros(bp, NH, HD, HD), zeros(bp, R_IN),
            zeros(bp, S5_G, S5_P), zeros(bp, S5_G, S5_P), zeros(bp, 2, dff))
    cki = cache_kidx.reshape(depth * n_pool, PAGE, HD)
    ck = cache_k.reshape(depth * n_pool, PAGE, A_KV * HD)
    cv = cache_v.reshape(depth * n_pool, PAGE, A_KV * HD)

    xp = x_prompt.reshape(bp * tp, d)
    xs = x_sample.reshape(bs, d)
    new_p, new_s = [], []
    for l in range(depth):
        xp, sp = _layer(xp, bp, tp, l, layers[l], st_p, tab_p, None)
        st_s = (state_mlstm_c[l], state_mlstm_n[l], state_mlstm_m[l], state_rwkv_s[l], state_rwkv_shift[l],
                state_s5_re[l], state_s5_im[l], state_ffn_conv[l])
        xs, ss = _layer(xs, bs, 1, l, layers[l], st_s, tab_s, (cki, ck, cv, page_table, n_pool))
        new_p.append(sp)
        new_s.append(ss)
    (k_p, v_p, ki_p, mc_p, mn_p, mm_p, rs_p, rsh_p, sre_p, sim_p, fc_p) = [jnp.stack(z) for z in zip(*new_p)]
    (k_s, v_s, ki_s, mc_s, mn_s, mm_s, rs_s, rsh_s, sre_s, sim_s, fc_s) = [jnp.stack(z) for z in zip(*new_s)]
    y_prompt = _rmsnorm(xp, row(norm_final), F32).reshape(bp, tp, d)
    y_sample = _rmsnorm(xs, row(norm_final), F32).reshape(bs, ts, d)
    return (y_prompt, y_sample, k_p, k_s, v_p, v_s, ki_p, ki_s, mc_p, mc_s, mn_p, mn_s, mm_p, mm_s,
            rs_p, rs_s, rsh_p, rsh_s, sre_p, sre_s, sim_p, sim_s, fc_p, fc_s)
```

```python
import functools
import math

import jax
import jax.numpy as jnp
from jax import lax
from jax.experimental import pallas as pl
from jax.experimental.pallas import tpu as pltpu

F32 = jnp.float32
BF16 = jnp.bfloat16
HI = lax.Precision.HIGHEST

HD = 64
NH = 8
GW = NH * HD
A_KV = 2
PAGE = 128
TOPK_MAX = 256
ROPE_THETA = 10000.0
R_IN = 3 * GW + 64 + 64 + 128
R_LN_EPS = 64e-5
S5_G, S5_CH, S5_P = 32, 16, 64
S5_W = S5_G * S5_P
NORM_EPS = 1e-6
CHUNK = 64
IDX_SCALE = HD ** -0.5 * NH ** -0.5

C_M, C_AQ, C_IQ, C_SU, C_RIN, C_AK, C_AV, C_SM = 0, 2048, 2560, 3072, 3584, 5376, 5504, 5632
NP = 5760
SM_IK, SM_MI, SM_MF, SM_IW = 0, 64, 72, 80

VMEM_LIMIT = 56 * 1024 * 1024
MM_ROWS = 1024


def _cp(*sem):
    return pltpu.CompilerParams(dimension_semantics=sem, vmem_limit_bytes=VMEM_LIMIT)


def _dot(a, b, prec=None):
    return jnp.dot(a, b, preferred_element_type=F32, precision=prec)


def _dot_nt(a, b, prec=None):
    return lax.dot_general(a, b, (((1,), (1,)), ((), ())), preferred_element_type=F32, precision=prec)


def _dot_tn(a, b, prec=None):
    return lax.dot_general(a, b, (((0,), (0,)), ((), ())), preferred_element_type=F32, precision=prec)


def _sigmoid(x):
    return 1.0 / (1.0 + jnp.exp(-x))


def _softplus(x):
    return jnp.maximum(x, 0.0) + jnp.log(1.0 + jnp.exp(-jnp.abs(x)))


def _iota(shape, dim):
    return lax.broadcasted_iota(jnp.int32, shape, dim)


HG = 4
BD = HG * HD
assert CHUNK == HD


def _bd_masks(c):
    row, col = _iota((HG * c, HG * c), 0), _iota((HG * c, HG * c), 1)
    same = (row // c) == (col // c)
    t, s = row % c, col % c
    return same, jnp.logical_and(same, s <= t), jnp.logical_and(same, s < t)


def _bd_diag(x):
    nb, ng = x.shape[:2]
    x6 = x.reshape(nb, ng, HG, HD, HG, HD)
    return jnp.stack([x6[:, :, j, :, j, :] for j in range(HG)], axis=2).reshape(nb, ng * HG, HD, HD)


def _inproj_kernel(x_ref, g_ref, w_ref, o_ref, h_scr):
    @pl.when(pl.program_id(1) == 0)
    def _():
        x = x_ref[...]
        ms = jnp.mean(x * x, axis=-1, keepdims=True)
        h_scr[...] = (x * lax.rsqrt(ms + NORM_EPS) * g_ref[...]).astype(BF16)

    o_ref[...] = _dot(h_scr[...], w_ref[...])


def _in_proj(x2, g, w):
    n, d = x2.shape
    npad = w.shape[1]
    tm = min(MM_ROWS, n)
    tn = 640
    return pl.pallas_call(
        _inproj_kernel,
        out_shape=jax.ShapeDtypeStruct((n, npad), F32),
        grid=(n // tm, npad // tn),
        in_specs=[pl.BlockSpec((tm, d), lambda i, j: (i, 0)),
                  pl.BlockSpec((1, d), lambda i, j: (0, 0)),
                  pl.BlockSpec((d, tn), lambda i, j: (0, j))],
        out_specs=pl.BlockSpec((tm, tn), lambda i, j: (i, j)),
        scratch_shapes=[pltpu.VMEM((tm, d), BF16)],
        compiler_params=_cp("parallel", "arbitrary"),
    )(x2, g, w)


def _rmsnorm_kernel(x_ref, g_ref, o_ref):
    x = x_ref[...]
    ms = jnp.mean(x * x, axis=-1, keepdims=True)
    o_ref[...] = (x * lax.rsqrt(ms + NORM_EPS) * g_ref[...]).astype(o_ref.dtype)


def _rmsnorm(x2, g, dtype):
    n, d = x2.shape
    tm = min(512, n)
    return pl.pallas_call(
        _rmsnorm_kernel,
        out_shape=jax.ShapeDtypeStruct((n, d), dtype),
        grid=(n // tm,),
        in_specs=[pl.BlockSpec((tm, d), lambda i: (i, 0)), pl.BlockSpec((1, d), lambda i: (0, 0))],
        out_specs=pl.BlockSpec((tm, d), lambda i: (i, 0)),
        compiler_params=_cp("parallel"),
    )(x2, g)


def _resmm_kernel(r_ref, y_ref, w_ref, o_ref):
    o_ref[...] = r_ref[...] + _dot(y_ref[...], w_ref[...])


def _res_matmul(res, y, w):
    n, k = y.shape
    d = w.shape[1]
    tm = min(MM_ROWS, n)
    tn = 512
    return pl.pallas_call(
        _resmm_kernel,
        out_shape=jax.ShapeDtypeStruct((n, d), F32),
        grid=(n // tm, d // tn),
        in_specs=[pl.BlockSpec((tm, tn), lambda i, j: (i, j)),
                  pl.BlockSpec((tm, k), lambda i, j: (i, 0)),
                  pl.BlockSpec((k, tn), lambda i, j: (0, j))],
        out_specs=pl.BlockSpec((tm, tn), lambda i, j: (i, j)),
        compiler_params=_cp("parallel", "arbitrary"),
    )(res, y, w)


def _rope(x, cos, sin):
    w = x.shape[1]
    first = (_iota(x.shape, 1) & (HD - 1)) < HD // 2
    sw = jnp.where(first, pltpu.roll(x, w - HD // 2, 1), pltpu.roll(x, HD // 2, 1))
    return x * cos + sw * sin


def _rope_kernel(aq_ref, iq_ref, ak_ref, av_ref, sm_ref, cos_ref, sin_ref, aqo, iqo, ko, vo, iko):
    cos = cos_ref[...]
    sin = sin_ref[...]
    aqo[...] = _rope(aq_ref[...], cos, sin)
    iqo[...] = _rope(iq_ref[...], cos, sin)
    ko[...] = _rope(ak_ref[...], cos[:, :128], sin[:, :128])
    vo[...] = av_ref[...]
    iko[...] = _rope(sm_ref[...], cos[:, :128], sin[:, :128])[:, :HD]


def _rope_call(proj, cos, sin, nb, nt_rows):
    n = proj.shape[0]
    tm = min(512, nt_rows)
    nt = nt_rows // tm
    row = lambda b, i: b * nt + i
    return pl.pallas_call(
        _rope_kernel,
        out_shape=(jax.ShapeDtypeStruct((n, GW), F32), jax.ShapeDtypeStruct((n, GW), F32),
                   jax.ShapeDtypeStruct((n, 128), F32), jax.ShapeDtypeStruct((n, 128), F32),
                   jax.ShapeDtypeStruct((n, HD), F32)),
        grid=(nb, nt),
        in_specs=[pl.BlockSpec((tm, GW), lambda b, i: (row(b, i), C_AQ // GW)),
                  pl.BlockSpec((tm, GW), lambda b, i: (row(b, i), C_IQ // GW)),
                  pl.BlockSpec((tm, 128), lambda b, i: (row(b, i), C_AK // 128)),
                  pl.BlockSpec((tm, 128), lambda b, i: (row(b, i), C_AV // 128)),
                  pl.BlockSpec((tm, 128), lambda b, i: (row(b, i), C_SM // 128)),
                  pl.BlockSpec((tm, GW), lambda b, i: (i, 0)),
                  pl.BlockSpec((tm, GW), lambda b, i: (i, 0))],
        out_specs=(pl.BlockSpec((tm, GW), lambda b, i: (row(b, i), 0)),
                   pl.BlockSpec((tm, GW), lambda b, i: (row(b, i), 0)),
                   pl.BlockSpec((tm, 128), lambda b, i: (row(b, i), 0)),
                   pl.BlockSpec((tm, 128), lambda b, i: (row(b, i), 0)),
                   pl.BlockSpec((tm, HD), lambda b, i: (row(b, i), 0))),
        compiler_params=_cp("parallel", "parallel"),
    )(proj, proj, proj, proj, proj, cos, sin)


def _rope_tables(pos):
    half = HD // 2
    inv = ROPE_THETA ** (-jnp.arange(half, dtype=F32) / half)
    ang = pos.astype(F32)[:, None] * inv[None, :]
    cos, sin = jnp.cos(ang), jnp.sin(ang)
    cos64 = jnp.concatenate([cos, cos], axis=-1)
    sin64 = jnp.concatenate([-sin, sin], axis=-1)
    return jnp.tile(cos64, (1, NH)), jnp.tile(sin64, (1, NH))


def _kth_largest(sc, extra, kk):
    kf = jnp.float32(kk)

    def count_ge(c):
        n = jnp.sum(jnp.where(sc >= c, 1.0, 0.0), axis=-1, keepdims=True)
        if extra is not None:
            n = n + jnp.where(extra >= c, 1.0, 0.0)
        return n

    def key_to_f(key):
        bits = key ^ ((key >> 31) & jnp.int32(0x7FFFFFFF))
        return lax.bitcast_convert_type(bits, F32)

    r = sc.shape[0]
    int_min = jnp.int32(-2 ** 31)
    lo = jnp.where(count_ge(jnp.zeros((r, 1), F32)) >= kf, jnp.int32(0), int_min)

    def body(j, lo):
        cand = lo + jnp.left_shift(jnp.int32(1), jnp.int32(30) - j)
        ok = count_ge(key_to_f(cand)) >= kf
        return jnp.where(ok, cand, lo)

    lo = lax.fori_loop(0, 31, body, lo)
    key_neg_inf = jnp.int32(-2 ** 31 + 0x7FFFFF)
    return jnp.where(lo <= key_neg_inf, -jnp.inf, key_to_f(lo))


def _strict_upper_bf16(n):
    return jnp.where(_iota((n, n), 0) < _iota((n, n), 1), 1.0, 0.0).astype(BF16)


def _split_bf16(x):
    hi = x.astype(BF16).astype(F32)
    return hi, x - hi


def _dsa_prompt_kernel(iq_ref, sm_ref, ik_ref, aq_ref, k_ref, v_ref, o_ref, sel_scr, kcat_scr, *, topk, qb, n_ext):
    t_keys = ik_ref.shape[0]
    i = pl.program_id(1)
    per_ext = (t_keys // qb) // n_ext

    @pl.when(i == 0)
    def _():
        hi, lo = _split_bf16(ik_ref[...])
        kcat_scr[...] = jnp.concatenate([hi, lo, hi], axis=1)

    iq = iq_ref[...]
    aq = aq_ref[...]
    wts = sm_ref[:, SM_IW:SM_IW + NH] * IDX_SCALE
    tq = i * qb + _iota((qb, 1), 0)

    def body(ext):
        kcat = kcat_scr[0:ext, :]
        sc = jnp.zeros((qb, ext), F32)
        for h in range(NH):
            hi, lo = _split_bf16(iq[:, h * HD:(h + 1) * HD])
            qk = _dot_nt(jnp.concatenate([hi, hi, lo], axis=1), kcat)
            sc = sc + jnp.maximum(qk, 0.0) * wts[:, h:h + 1]
        causal = _iota((1, ext), 1) <= tq
        sc = jnp.where(causal, sc, -jnp.inf)

        thr = _kth_largest(sc, None, topk)
        gt = sc > thr
        eq = sc == thr
        n_gt = jnp.sum(jnp.where(gt, 1.0, 0.0), axis=-1, keepdims=True)
        n_eq = jnp.sum(jnp.where(eq, 1.0, 0.0), axis=-1, keepdims=True)
        need = jnp.float32(topk) - n_gt
        sel_scr[:, 0:ext] = jnp.where(jnp.logical_and(sc >= thr, causal), 1.0, 0.0)
        tie = jnp.logical_and(n_eq > need, thr > -jnp.inf)

        @pl.when(jnp.max(jnp.where(tie, 1.0, 0.0)) > 0.5)
        def _():
            ut = _strict_upper_bf16(128)
            run = jnp.zeros((qb, 1), F32)
            for c in range(ext // 128):
                sl = slice(c * 128, (c + 1) * 128)
                eqc = jnp.where(eq[:, sl], 1.0, 0.0)
                pref = _dot(eqc.astype(BF16), ut) + run
                keep = jnp.logical_or(gt[:, sl], jnp.logical_and(eq[:, sl], pref < need))
                sel_scr[:, sl] = jnp.where(jnp.logical_and(keep, causal[:, sl]), 1.0, 0.0)
                run = run + jnp.sum(eqc, axis=-1, keepdims=True)

        sel = sel_scr[:, 0:ext] > 0.5
        for g in range(A_KV):
            kg = k_ref[0:ext, g * HD:(g + 1) * HD]
            vg = v_ref[0:ext, g * HD:(g + 1) * HD]
            for j in range(NH // A_KV):
                h = g * (NH // A_KV) + j
                s = _dot_nt(aq[:, h * HD:(h + 1) * HD], kg) * HD ** -0.5
                s = jnp.where(sel, s, -jnp.inf)
                m = jnp.max(s, axis=-1, keepdims=True)
                p = jnp.exp(s - m)
                l = jnp.sum(p, axis=-1, keepdims=True)
                o_ref[:, h * HD:(h + 1) * HD] = _dot(p, vg) / l

    for j in range(n_ext):
        pl.when(i // per_ext == j)(functools.partial(body, (j + 1) * (t_keys // n_ext)))


def _dsa_prompt(iq_r, proj, ik_r, aq_r, k_r, v_r, nb, t):
    qb = min(128, t)
    nq = t // qb
    n_ext = min(4, nq)
    assert nq % n_ext == 0
    topk = min(TOPK_MAX, t // 4)
    n = nb * t
    row = lambda b, i: b * nq + i
    return pl.pallas_call(
        functools.partial(_dsa_prompt_kernel, topk=topk, qb=qb, n_ext=n_ext),
        out_shape=jax.ShapeDtypeStruct((n, GW), F32),
        grid=(nb, nq),
        in_specs=[pl.BlockSpec((qb, GW), lambda b, i: (row(b, i), 0)),
                  pl.BlockSpec((qb, 128), lambda b, i: (row(b, i), C_SM // 128)),
                  pl.BlockSpec((t, HD), lambda b, i: (b, 0)),
                  pl.BlockSpec((qb, GW), lambda b, i: (row(b, i), 0)),
                  pl.BlockSpec((t, 128), lambda b, i: (b, 0)),
                  pl.BlockSpec((t, 128), lambda b, i: (b, 0))],
        out_specs=pl.BlockSpec((qb, GW), lambda b, i: (row(b, i), 0)),
        scratch_shapes=[pltpu.VMEM((qb, t), F32), pltpu.VMEM((t, 3 * HD), F32)],
        compiler_params=_cp("parallel", "arbitrary"),
    )(iq_r, proj, ik_r, aq_r, k_r, v_r)


def _dsa_sample_score_kernel(*refs, n_pages):
    _, iq_ref, w_ref, ikn_ref = refs[:4]
    pages = refs[4:4 + n_pages]
    sc_ref, sn_ref = refs[4 + n_pages:]
    iq = iq_ref[0]
    w = w_ref[0] * IDX_SCALE
    for c in range(n_pages):
        qk = _dot_nt(iq, pages[c][0], HI)
        sc_ref[0, :, c * PAGE:(c + 1) * PAGE] = jnp.sum(jnp.maximum(qk, 0.0) * w, axis=0, keepdims=True)
    qkn = jnp.sum(iq * ikn_ref[0], axis=-1, keepdims=True)
    sn = jnp.sum(jnp.maximum(qkn, 0.0) * w, axis=0, keepdims=True)
    sn_ref[0] = jnp.broadcast_to(sn, (1, 128))


def _dsa_sample_select_kernel(sc_ref, sn_ref, sel_ref, seln_ref, *, topk):
    sc = sc_ref[...]
    sn = sn_ref[:, 0:1]
    bd, s_keys = sc.shape
    thr = _kth_largest(sc, sn, topk)
    gt = sc > thr
    eq = sc == thr
    n_gt = jnp.sum(jnp.where(gt, 1.0, 0.0), axis=-1, keepdims=True) + jnp.where(sn > thr, 1.0, 0.0)
    need = jnp.float32(topk) - n_gt
    ut = _strict_upper_bf16(PAGE)
    run = jnp.zeros((bd, 1), F32)
    for c in range(s_keys // PAGE):
        sl = slice(c * PAGE, (c + 1) * PAGE)
        eqc = jnp.where(eq[:, sl], 1.0, 0.0)
        pref = _dot(eqc.astype(BF16), ut) + run
        keep = jnp.logical_or(gt[:, sl], jnp.logical_and(eq[:, sl], pref < need))
        sel_ref[:, sl] = jnp.where(keep, 1.0, 0.0)
        run = run + jnp.sum(eqc, axis=-1, keepdims=True)
    sel_new = jnp.logical_or(sn > thr, jnp.logical_and(sn == thr, run < need))
    seln_ref[...] = jnp.broadcast_to(jnp.where(sel_new, 1.0, 0.0), seln_ref.shape)


def _dsa_sample_attn_kernel(*refs, n_pages):
    _, aq_ref, kn_ref, vn_ref, sel_ref, seln_ref = refs[:6]
    kpages = refs[6:6 + n_pages]
    vpages = refs[6 + n_pages:6 + 2 * n_pages]
    o_ref, k_scr, v_scr = refs[6 + 2 * n_pages:]
    for c in range(n_pages):
        k_scr[c * PAGE:(c + 1) * PAGE, :] = kpages[c][0]
        v_scr[c * PAGE:(c + 1) * PAGE, :] = vpages[c][0]
    sel = sel_ref[0] > 0.5
    sel_new = seln_ref[0][:, 0:1] > 0.5
    aq = aq_ref[0]
    kn = kn_ref[0]
    vn = vn_ref[0]
    hpg = NH // A_KV
    for g in range(A_KV):
        qg = aq[g * hpg:(g + 1) * hpg, :]
        s = _dot_nt(qg, k_scr[:, g * HD:(g + 1) * HD]) * HD ** -0.5
        s = jnp.where(sel, s, -jnp.inf)
        s_new = jnp.sum(qg * kn[:, g * HD:(g + 1) * HD], axis=-1, keepdims=True) * HD ** -0.5
        s_new = jnp.where(sel_new, s_new, -jnp.inf)
        m = jnp.maximum(jnp.max(s, axis=-1, keepdims=True), s_new)
        pr = jnp.exp(s - m)
        pn = jnp.exp(s_new - m)
        l = jnp.sum(pr, axis=-1, keepdims=True) + pn
        o = _dot(pr, v_scr[:, g * HD:(g + 1) * HD]) + pn * vn[:, g * HD:(g + 1) * HD]
        o_ref[0, g * hpg:(g + 1) * hpg, :] = o / l


def _dsa_sample(layer, page_table, iq_r, iw, ik_r, aq_r, k_r, v_r, cki, ck, cv, n_pool):
    bd, n_pages = page_table.shape
    past = n_pages * PAGE
    topk = min(TOPK_MAX, (past + 1) // 4)
    base = layer * n_pool
    per_b = lambda b, pt: (b, 0, 0)

    def page_specs(width):
        return [pl.BlockSpec((1, PAGE, width), lambda b, pt, c=c: (base + pt[b, c], 0, 0)) for c in range(n_pages)]

    sc, sn = pl.pallas_call(
        functools.partial(_dsa_sample_score_kernel, n_pages=n_pages),
        out_shape=(jax.ShapeDtypeStruct((bd, 1, past), F32), jax.ShapeDtypeStruct((bd, 1, 128), F32)),
        grid_spec=pltpu.PrefetchScalarGridSpec(
            num_scalar_prefetch=1,
            grid=(bd,),
            in_specs=[pl.BlockSpec((1, NH, HD), per_b), pl.BlockSpec((1, NH, 1), per_b),
                      pl.BlockSpec((1, 1, HD), per_b)] + page_specs(HD),
            out_specs=(pl.BlockSpec((1, 1, past), per_b), pl.BlockSpec((1, 1, 128), per_b))),
        compiler_params=_cp("arbitrary"),
    )(page_table, iq_r.reshape(bd, NH, HD), iw.reshape(bd, NH, 1), ik_r.reshape(bd, 1, HD), *([cki] * n_pages))

    sel, seln = pl.pallas_call(
        functools.partial(_dsa_sample_select_kernel, topk=topk),
        out_shape=(jax.ShapeDtypeStruct((bd, past), F32), jax.ShapeDtypeStruct((bd, 128), F32)),
    )(sc.reshape(bd, past), sn.reshape(bd, 128))

    out = pl.pallas_call(
        functools.partial(_dsa_sample_attn_kernel, n_pages=n_pages),
        out_shape=jax.ShapeDtypeStruct((bd, NH, HD), F32),
        grid_spec=pltpu.PrefetchScalarGridSpec(
            num_scalar_prefetch=1,
            grid=(bd,),
            in_specs=[pl.BlockSpec((1, NH, HD), per_b), pl.BlockSpec((1, 1, 128), per_b),
                      pl.BlockSpec((1, 1, 128), per_b), pl.BlockSpec((1, 1, past), per_b),
                      pl.BlockSpec((1, 1, 128), per_b)] + page_specs(128) + page_specs(128),
            out_specs=pl.BlockSpec((1, NH, HD), per_b),
            scratch_shapes=[pltpu.VMEM((past, 128), F32), pltpu.VMEM((past, 128), F32)]),
        compiler_params=_cp("arbitrary"),
    )(page_table, aq_r.reshape(bd, NH, HD), k_r.reshape(bd, 1, 128), v_r.reshape(bd, 1, 128),
      sel.reshape(bd, 1, past), seln.reshape(bd, 1, 128), *([ck] * n_pages), *([cv] * n_pages))
    return out.reshape(bd, GW)


def _log_sigmoid(x):
    return jnp.minimum(x, 0.0) - jnp.log(1.0 + jnp.exp(-jnp.abs(x)))


MLSTM_ROWS = 2 * CHUNK


def _mlstm_prompt_kernel(m_ref, sm_ref, gb_ref, nw_ref, y_ref, c_ref, n_ref, mm_ref):
    L = CHUNK

    @pl.when(pl.program_id(1) == 0)
    def _():
        c_ref[...] = jnp.zeros(c_ref.shape, F32)
        n_ref[...] = jnp.zeros(n_ref.shape, F32)
        mm_ref[...] = jnp.zeros(mm_ref.shape, F32)

    smb = sm_ref[...] + gb_ref[...]
    smt = smb.T
    ig_all = smb[:, SM_MI:SM_MI + NH]
    lf_all = _log_sigmoid(smb[:, SM_MF:SM_MF + NH])
    igt_all = smt[SM_MI:SM_MI + NH, :]
    lft_all = _log_sigmoid(smt[SM_MF:SM_MF + NH, :])
    ri, ci = _iota((L, L), 0), _iota((L, L), 1)
    tril = jnp.where(ci <= ri, 1.0, 0.0)
    triu = jnp.where(ci >= ri, 1.0, 0.0)
    same, causal, _ = _bd_masks(L)
    tile = lambda z: jnp.concatenate([z] * HG, axis=0)
    msk = lambda z: jnp.where(same, z, 0.0)
    stack_cols = lambda z, g: jnp.concatenate([z[:, g * HG + j:g * HG + j + 1] for j in range(HG)], axis=0)
    stack_rows = lambda z, g: jnp.concatenate([z[g * HG + j:g * HG + j + 1, :] for j in range(HG)], axis=1)
    for cc in range(MLSTM_ROWS // L):
        rows = slice(cc * L, (cc + 1) * L)
        ig = ig_all[rows]
        bcs = _dot(tril, lf_all[rows], HI)
        bcst = _dot(lft_all[:, rows], triu, HI)
        igt = igt_all[:, rows]
        for g in range(NH // HG):
            ln = slice(g * BD, (g + 1) * BD)
            heads = range(g * HG, (g + 1) * HG)
            qexp = msk(tile(m_ref[rows, ln]))
            kt = tile(m_ref[rows, GW + g * BD:GW + (g + 1) * BD] * HD ** -0.5)
            kexp = msk(kt)
            vexp = msk(tile(m_ref[rows, 2 * GW + g * BD:2 * GW + (g + 1) * BD]))
            og = m_ref[rows, 3 * GW + g * BD:3 * GW + (g + 1) * BD]
            bcol = stack_cols(bcs, g)
            icol = stack_cols(ig, g)
            brow = stack_rows(bcst, g)
            irow = stack_rows(igt, g)
            mprev = [mm_ref[0, :, h:h + 1] for h in heads]
            bl = [bcs[L - 1:L, h:h + 1] for h in heads]
            percol = lambda zs: jnp.concatenate([jnp.broadcast_to(z, (L, 1)) for z in zs], axis=0)
            perrow = lambda zs: jnp.concatenate([jnp.broadcast_to(z, (1, HD)) for z in zs], axis=1)
            mprev_col = percol(mprev)
            dmat = jnp.where(causal, bcol - brow + irow, -jnp.inf)
            inter = bcol + mprev_col
            mj = jnp.maximum(inter, jnp.max(dmat, axis=-1, keepdims=True))
            s = _dot_nt(qexp, kt) * jnp.exp(dmat - mj)
            iw = jnp.exp(inter - mj)
            cbd = c_ref[0, g]
            nrow = n_ref[0, :, ln]
            num = _dot(s, vexp) + iw * _dot_nt(qexp, cbd)
            den = jnp.sum(s, axis=-1, keepdims=True) + iw * jnp.sum(qexp * nrow, axis=-1, keepdims=True)
            hc = num / jnp.maximum(jnp.abs(den), jnp.exp(-mj))
            wl = percol(bl) - bcol + icol
            m_new = [jnp.maximum(bl[j] + mprev[j], jnp.max(wl[j * L:(j + 1) * L], axis=0, keepdims=True))
                     for j in range(HG)]
            dec = [jnp.exp(bl[j] + mprev[j] - m_new[j]) for j in range(HG)]
            ws = jnp.exp(wl - percol(m_new))
            c_ref[0, g] = percol(dec) * cbd + _dot_tn(vexp * ws, kexp)
            n_ref[0, :, ln] = perrow(dec) * nrow + jnp.sum(kexp * ws, axis=0, keepdims=True)
            for j, h in enumerate(heads):
                mm_ref[0, :, h:h + 1] = m_new[j]
            hn = hc * lax.rsqrt(jnp.sum(hc * hc, axis=-1, keepdims=True) * (1.0 / HD) + NORM_EPS)
            hn = sum(hn[j * L:(j + 1) * L] for j in range(HG))
            y_ref[rows, ln] = hn * nw_ref[:, ln] * _sigmoid(og)


def _gate_bias_row(b_i, b_f):
    z = lambda n: jnp.zeros((1, n), F32)
    return jnp.concatenate([z(SM_MI), b_i, b_f, z(128 - SM_MF - NH)], axis=1)


def _mlstm_prompt(proj, b_i, b_f, nw, nb, t):
    tm = MLSTM_ROWS
    nc = t // tm
    n = nb * t
    ng = NH // HG
    row = lambda b, c: b * nc + c
    y, cbd, nrow, m = pl.pallas_call(
        _mlstm_prompt_kernel,
        out_shape=(jax.ShapeDtypeStruct((n, GW), F32), jax.ShapeDtypeStruct((nb, ng, BD, BD), F32),
                   jax.ShapeDtypeStruct((nb, 1, GW), F32), jax.ShapeDtypeStruct((nb, 1, NH), F32)),
        grid=(nb, nc),
        in_specs=[pl.BlockSpec((tm, 4 * GW), lambda b, c: (row(b, c), 0)),
                  pl.BlockSpec((tm, 128), lambda b, c: (row(b, c), C_SM // 128)),
                  pl.BlockSpec((1, 128), lambda b, c: (0, 0)),
                  pl.BlockSpec((1, GW), lambda b, c: (0, 0))],
        out_specs=(pl.BlockSpec((tm, GW), lambda b, c: (row(b, c), 0)),
                   pl.BlockSpec((1, ng, BD, BD), lambda b, c: (b, 0, 0, 0)),
                   pl.BlockSpec((1, 1, GW), lambda b, c: (b, 0, 0)),
                   pl.BlockSpec((1, 1, NH), lambda b, c: (b, 0, 0))),
        compiler_params=_cp("parallel", "arbitrary"),
    )(proj, proj, _gate_bias_row(b_i, b_f), nw)
    return y, _bd_diag(cbd), nrow.reshape(nb, NH, HD), m


def _col_of_row(x_row, eye):
    return jnp.sum(jnp.where(eye, x_row, 0.0), axis=1, keepdims=True)


def _row_of_col(x_col, eye):
    return jnp.sum(jnp.where(eye, x_col, 0.0), axis=0, keepdims=True)


def _mlstm_step_kernel(m_ref, sm_ref, bi_ref, bf_ref, nw_ref, c_ref, n_ref, mm_ref, y_ref, co_ref, no_ref, mo_ref):
    xr = m_ref[0]
    sm = sm_ref[0]
    ig = sm[:, SM_MI:SM_MI + NH] + bi_ref[...]
    lf = _log_sigmoid(sm[:, SM_MF:SM_MF + NH] + bf_ref[...])
    eye = _iota((HD, HD), 0) == _iota((HD, HD), 1)
    for h in range(NH):
        q = xr[:, h * HD:(h + 1) * HD]
        k = xr[:, GW + h * HD:GW + (h + 1) * HD] * HD ** -0.5
        v = xr[:, 2 * GW + h * HD:2 * GW + (h + 1) * HD]
        og = xr[:, 3 * GW + h * HD:3 * GW + (h + 1) * HD]
        cmat = c_ref[0, h]
        nrow = n_ref[0, h:h + 1, :]
        i_ = ig[:, h:h + 1]
        inter = lf[:, h:h + 1] + mm_ref[0, :, h:h + 1]
        mj = jnp.maximum(inter, i_)
        vcol = _col_of_row(v, eye)
        s = jnp.sum(q * k, axis=-1, keepdims=True) * jnp.exp(i_ - mj)
        iw = jnp.exp(inter - mj)
        cq = jnp.sum(cmat * q, axis=1, keepdims=True)
        num = s * vcol + iw * cq
        den = s + iw * jnp.sum(nrow * q, axis=-1, keepdims=True)
        hcol = num / jnp.maximum(jnp.abs(den), jnp.exp(-mj))
        dec = jnp.exp(inter - mj)
        ws = jnp.exp(i_ - mj)
        co_ref[0, h] = dec * cmat + (ws * vcol) * k
        no_ref[0, h:h + 1, :] = dec * nrow + ws * k
        mo_ref[0, :, h:h + 1] = mj
        hn = hcol * lax.rsqrt(jnp.mean(hcol * hcol, axis=0, keepdims=True) + NORM_EPS)
        y_ref[0, :, h * HD:(h + 1) * HD] = _row_of_col(hn, eye) * nw_ref[:, h * HD:(h + 1) * HD] * _sigmoid(og)


def _mlstm_step(proj, b_i, b_f, nw, c0, n0, m0):
    bd = proj.shape[0]
    per_b3 = lambda b: (b, 0, 0)
    return pl.pallas_call(
        _mlstm_step_kernel,
        out_shape=(jax.ShapeDtypeStruct((bd, 1, GW), F32), jax.ShapeDtypeStruct((bd, NH, HD, HD), F32),
                   jax.ShapeDtypeStruct((bd, NH, HD), F32), jax.ShapeDtypeStruct((bd, 1, NH), F32)),
        grid=(bd,),
        in_specs=[pl.BlockSpec((1, 1, 4 * GW), lambda b: (b, 0, 0)),
                  pl.BlockSpec((1, 1, 128), lambda b: (b, 0, C_SM // 128)),
                  pl.BlockSpec((1, NH), lambda b: (0, 0)),
                  pl.BlockSpec((1, NH), lambda b: (0, 0)),
                  pl.BlockSpec((1, GW), lambda b: (0, 0)),
                  pl.BlockSpec((1, NH, HD, HD), lambda b: (b, 0, 0, 0)),
                  pl.BlockSpec((1, NH, HD), per_b3),
                  pl.BlockSpec((1, 1, NH), per_b3)],
        out_specs=(pl.BlockSpec((1, 1, GW), per_b3),
                   pl.BlockSpec((1, NH, HD, HD), lambda b: (b, 0, 0, 0)),
                   pl.BlockSpec((1, NH, HD), per_b3),
                   pl.BlockSpec((1, 1, NH), per_b3)),
        compiler_params=_cp("parallel"),
    )(proj.reshape(bd, 1, NP), proj.reshape(bd, 1, NP), b_i, b_f, nw, c0, n0, m0.reshape(bd, 1, NH))


def _rwkv_prep_kernel(x_ref, prev_ref, mu_ref, w0_ref, ww2_ref, a0_ref, wa2_ref, wg2_ref, kk_ref, ka_ref,
                      r_o, lw_o, k_o, v_o, kk_o, a_o, g_o, carry_scr, *, seq):
    x = x_ref[...]
    tm = x.shape[0]
    if seq:
        first = jnp.where(pl.program_id(1) == 0, prev_ref[0], carry_scr[...])
        xprev = jnp.where(_iota((tm, 1), 0) == 0, first, pltpu.roll(x, 1, 0))
        carry_scr[...] = x[tm - 1:tm, :]
    else:
        xprev = prev_ref[...]
    xm = x + (xprev - x) * mu_ref[...]
    r = xm[:, 0:GW]
    kx = xm[:, GW:2 * GW]
    v = xm[:, 2 * GW:3 * GW]
    xw = xm[:, 3 * GW:3 * GW + 64]
    xa = xm[:, 3 * GW + 64:3 * GW + 128]
    xg = xm[:, 3 * GW + 128:R_IN]
    w = -_softplus(-(w0_ref[...] + _dot(jnp.tanh(xw), ww2_ref[...]))) - 0.5
    a = _sigmoid(a0_ref[...] + _dot(xa, wa2_ref[...]))
    r_o[...] = r
    lw_o[...] = -jnp.exp(w)
    v_o[...] = v
    a_o[...] = a
    g_o[...] = _dot(_sigmoid(xg), wg2_ref[...])
    k_o[...] = kx * (1.0 + (a - 1.0) * ka_ref[...])
    kk = kx * kk_ref[...]
    for h in range(NH):
        kh = kk[:, h * HD:(h + 1) * HD]
        nrm = jnp.sqrt(jnp.sum(kh * kh, axis=-1, keepdims=True))
        kk_o[:, h * HD:(h + 1) * HD] = kh / jnp.maximum(nrm, 1e-12)


def _rwkv_prep(proj, prev, mu, w0, ww2, a0, wa2, wg2, k_k, k_a, nb, t):
    n = nb * t
    seq = t > 1
    full = lambda *_: (0, 0)
    if seq:
        tm = min(256, t)
        nt = t // tm
        grid = (nb, nt)
        xmap = lambda b, i: (b * nt + i, C_RIN // R_IN)
        pspec = pl.BlockSpec((1, 1, R_IN), lambda b, i: (b, 0, 0))
        omap = lambda b, i: (b * nt + i, 0)
        sem = ("parallel", "arbitrary")
    else:
        tm = n
        grid = (1,)
        xmap = lambda i: (0, C_RIN // R_IN)
        pspec = pl.BlockSpec((tm, R_IN), lambda i: (0, 0))
        omap = lambda i: (0, 0)
        sem = ("arbitrary",)
    wspecs = [pl.BlockSpec((1, R_IN), full), pl.BlockSpec((1, GW), full), pl.BlockSpec((64, GW), full),
              pl.BlockSpec((1, GW), full), pl.BlockSpec((64, GW), full), pl.BlockSpec((128, GW), full),
              pl.BlockSpec((1, GW), full), pl.BlockSpec((1, GW), full)]
    return pl.pallas_call(
        functools.partial(_rwkv_prep_kernel, seq=seq),
        out_shape=tuple(jax.ShapeDtypeStruct((n, GW), F32) for _ in range(7)),
        grid=grid,
        in_specs=[pl.BlockSpec((tm, R_IN), xmap), pspec] + wspecs,
        out_specs=tuple(pl.BlockSpec((tm, GW), omap) for _ in range(7)),
        scratch_shapes=[pltpu.VMEM((1, R_IN), F32)],
        compiler_params=_cp(*sem),
    )(proj, prev, mu, w0, ww2, a0, wa2, wg2, k_k, k_a)


def _rwkv_scan_kernel(r_ref, lw_ref, k_ref, v_ref, kk_ref, a_ref, g_ref, rk_ref, lnw_ref, lnb_ref, y_ref, s_ref):
    C = CHUNK

    @pl.when(pl.program_id(1) == 0)
    def _():
        s_ref[...] = jnp.zeros(s_ref.shape, F32)

    ri, ci = _iota((C, C), 0), _iota((C, C), 1)
    lw = lw_ref[...]
    cs = _dot(jnp.where(ci <= ri, 1.0, 0.0), lw, HI)
    gam = jnp.exp(cs)
    ginv = jnp.exp(-cs)
    r = r_ref[...]
    k = k_ref[...]
    v = v_ref[...]
    kk = kk_ref[...]
    at = -kk * jnp.exp(cs - lw)
    bt = kk * a_ref[...] * ginv
    kt = k * ginv
    rt = r * gam
    glast = gam[C - 1:C, :]
    bonus_in = r * k * rk_ref[...]

    same, incl, strict = _bd_masks(C)
    row, col = _iota((BD, BD), 0), _iota((BD, BD), 1)
    eye = jnp.where(row == col, 1.0, 0.0)
    tile = lambda z: jnp.concatenate([z] * HG, axis=0)
    msk = lambda z: jnp.where(same, z, 0.0)
    fold = lambda z: sum(z[j * C:(j + 1) * C] for j in range(HG))
    for g in range(NH // HG):
        ln = slice(g * BD, (g + 1) * BD)
        btl, ktl = tile(bt[:, ln]), tile(kt[:, ln])
        vexp = msk(tile(v[:, ln]))
        ar = jnp.concatenate([msk(tile(at[:, ln])), msk(tile(rt[:, ln]))], axis=0)
        sbd = s_ref[0, g]
        gb = _dot_nt(ar, btl)
        gk = _dot_nt(ar, ktl)
        gs = _dot_nt(ar, sbd)
        n_ab = jnp.where(strict, gb[:BD], 0.0)
        a_ak = jnp.where(strict, gk[:BD], 0.0)
        x = eye + n_ab
        pm = _dot(n_ab, n_ab)
        for j in range(5):
            xn = x + _dot(x, pm)
            if j < 4:
                pm = _dot(pm, pm)
            x = xn
        u = _dot(x, gs[:BD] + _dot(a_ak, vexp))
        o = gs[BD:] + _dot(jnp.where(incl, gb[BD:], 0.0), u) + _dot(jnp.where(incl, gk[BD:], 0.0), vexp)
        gl = glast[:, ln]
        uv = jnp.concatenate([u, vexp], axis=0)
        bk = jnp.concatenate([msk(btl * gl), msk(ktl * gl)], axis=0)
        s_ref[0, g] = sbd * gl + _dot_tn(uv, bk)
        mean = jnp.sum(o, axis=-1, keepdims=True) * (1.0 / HD)
        dev = msk(o - mean)
        var = jnp.sum(dev * dev, axis=-1, keepdims=True) * (1.0 / HD)
        on = fold(dev * lax.rsqrt(var + R_LN_EPS)) * lnw_ref[:, ln] + lnb_ref[:, ln]
        bonus = fold(jnp.sum(msk(tile(bonus_in[:, ln])), axis=-1, keepdims=True) * vexp)
        y_ref[:, ln] = (on + bonus) * g_ref[:, ln]


def _rwkv_scan(rs, rk, lnw, lnb, nb, t):
    nc = t // CHUNK
    n = nb * t
    ng = NH // HG
    rowmap = lambda b, c: (b * nc + c, 0)
    full = lambda b, c: (0, 0)
    y, sbd = pl.pallas_call(
        _rwkv_scan_kernel,
        out_shape=(jax.ShapeDtypeStruct((n, GW), F32), jax.ShapeDtypeStruct((nb, ng, BD, BD), F32)),
        grid=(nb, nc),
        in_specs=[pl.BlockSpec((CHUNK, GW), rowmap)] * 7 + [pl.BlockSpec((1, GW), full)] * 3,
        out_specs=(pl.BlockSpec((CHUNK, GW), rowmap), pl.BlockSpec((1, ng, BD, BD), lambda b, c: (b, 0, 0, 0))),
        compiler_params=_cp("parallel", "arbitrary"),
    )(*rs, rk, lnw, lnb)
    return y, _bd_diag(sbd)


def _rwkv_step_kernel(r_ref, lw_ref, k_ref, v_ref, kk_ref, a_ref, g_ref, rk_ref, lnw_ref, lnb_ref, s_ref,
                      y_ref, so_ref):
    eye = _iota((HD, HD), 0) == _iota((HD, HD), 1)
    r, lw, k, v, kk, a, g = (ref[0] for ref in (r_ref, lw_ref, k_ref, v_ref, kk_ref, a_ref, g_ref))
    bonus_in = r * k * rk_ref[...]
    for h in range(NH):
        sl = slice(h * HD, (h + 1) * HD)
        s0 = s_ref[0, h]
        kkr = kk[:, sl]
        sk = jnp.sum(s0 * kkr, axis=1, keepdims=True)
        vcol = _col_of_row(v[:, sl], eye)
        s1 = s0 * jnp.exp(lw[:, sl]) - sk * (kkr * a[:, sl]) + vcol * k[:, sl]
        so_ref[0, h] = s1
        ocol = jnp.sum(s1 * r[:, sl], axis=1, keepdims=True)
        mean = jnp.mean(ocol, axis=0, keepdims=True)
        var = jnp.mean(jnp.square(ocol - mean), axis=0, keepdims=True)
        on = _row_of_col((ocol - mean) * lax.rsqrt(var + R_LN_EPS), eye) * lnw_ref[:, sl] + lnb_ref[:, sl]
        bonus = jnp.sum(bonus_in[:, sl], axis=-1, keepdims=True) * v[:, sl]
        y_ref[0, :, sl] = (on + bonus) * g[:, sl]


def _rwkv_step(rs, rk, lnw, lnb, s0):
    bd = s0.shape[0]
    per_b = lambda b: (b, 0, 0)
    full = lambda b: (0, 0)
    smap = lambda b: (b, 0, 0, 0)
    return pl.pallas_call(
        _rwkv_step_kernel,
        out_shape=(jax.ShapeDtypeStruct((bd, 1, GW), F32), jax.ShapeDtypeStruct((bd, NH, HD, HD), F32)),
        grid=(bd,),
        in_specs=[pl.BlockSpec((1, 1, GW), per_b)] * 7 + [pl.BlockSpec((1, GW), full)] * 3
                 + [pl.BlockSpec((1, NH, HD, HD), smap)],
        out_specs=(pl.BlockSpec((1, 1, GW), per_b), pl.BlockSpec((1, NH, HD, HD), smap)),
        compiler_params=_cp("parallel"),
    )(*(z.reshape(bd, 1, GW) for z in rs), rk, lnw, lnb, s0)


def _gelu_tanh(x):
    return 0.5 * x * (1.0 + jnp.tanh(math.sqrt(2.0 / math.pi) * (x + 0.044715 * (x * x * x))))


def _s5_kernel(u_ref, bre_ref, bim_ref, lre_ref, lim_ref, cre_ref, cim_ref, d_ref, wg_ref, bg_ref, h0r_ref, h0i_ref,
               y_ref, hr_ref, hi_ref, hre_scr, him_scr, *, nb, tb):
    @pl.when(pl.program_id(0) == 0)
    def _():
        hr_ref[...] = h0r_ref[...]
        hi_ref[...] = h0i_ref[...]

    u = u_ref[...]
    hre_scr[...] = _dot(u, bre_ref[...])
    him_scr[...] = _dot(u, bim_ref[...])
    lr = lre_ref[...]
    li = lim_ref[...]

    def body(t, carry):
        hr, hi = carry
        rows = pl.ds(pl.multiple_of(t * nb, nb), nb)
        nr = lr * hr - li * hi + hre_scr[rows, :]
        ni = lr * hi + li * hr + him_scr[rows, :]
        hre_scr[rows, :] = nr
        him_scr[rows, :] = ni
        return nr, ni

    hr, hi = lax.fori_loop(0, tb, body, (hr_ref[...], hi_ref[...]))
    hr_ref[...] = hr
    hi_ref[...] = hi
    y = _dot(hre_scr[...], cre_ref[...]) - _dot(him_scr[...], cim_ref[...]) + d_ref[...] * u
    y = _gelu_tanh(y)
    y_ref[...] = y * _sigmoid(_dot(y, wg_ref[...]) + bg_ref[...])


def _s5(u_tm, mats, h0r, h0i, nb, t):
    bre, bim, lre, lim, cre, cim, d, wg, bg = mats
    tb = min(64, t)
    full = lambda i: (0, 0)
    return pl.pallas_call(
        functools.partial(_s5_kernel, nb=nb, tb=tb),
        out_shape=(jax.ShapeDtypeStruct((t * nb, GW), F32), jax.ShapeDtypeStruct((nb, S5_W), F32),
                   jax.ShapeDtypeStruct((nb, S5_W), F32)),
        grid=(t // tb,),
        in_specs=[pl.BlockSpec((tb * nb, GW), lambda i: (i, 0)),
                  pl.BlockSpec((GW, S5_W), full), pl.BlockSpec((GW, S5_W), full),
                  pl.BlockSpec((1, S5_W), full), pl.BlockSpec((1, S5_W), full),
                  pl.BlockSpec((S5_W, GW), full), pl.BlockSpec((S5_W, GW), full),
                  pl.BlockSpec((1, GW), full), pl.BlockSpec((GW, GW), full), pl.BlockSpec((1, GW), full),
                  pl.BlockSpec((nb, S5_W), full), pl.BlockSpec((nb, S5_W), full)],
        out_specs=(pl.BlockSpec((tb * nb, GW), lambda i: (i, 0)),
                   pl.BlockSpec((nb, S5_W), full), pl.BlockSpec((nb, S5_W), full)),
        scratch_shapes=[pltpu.VMEM((tb * nb, S5_W), F32), pltpu.VMEM((tb * nb, S5_W), F32)],
        compiler_params=_cp("arbitrary"),
    )(u_tm, bre, bim, lre, lim, cre, cim, d, wg, bg, h0r, h0i)


def _s5_mats(a_re, a_im, b_re, b_im, c_re, c_im, d_skip, log_dt, w_glu, b_glu):
    dt = jnp.exp(log_dt)
    mag = jnp.exp(a_re * dt)
    lb_re, lb_im = mag * jnp.cos(a_im * dt), mag * jnp.sin(a_im * dt)
    den = a_re * a_re + a_im * a_im
    f_re = ((lb_re - 1.0) * a_re + lb_im * a_im) / den
    f_im = (lb_im * a_re - (lb_re - 1.0) * a_im) / den
    bb_re = f_re[..., None] * b_re - f_im[..., None] * b_im
    bb_im = f_re[..., None] * b_im + f_im[..., None] * b_re
    eye = jnp.eye(S5_G, dtype=F32)
    bd = lambda bb: jnp.einsum('gpc,gh->gchp', bb, eye).reshape(GW, S5_W)
    cd = lambda cc: jnp.einsum('gcp,gh->gphc', cc, eye).reshape(S5_W, GW)
    return (bd(bb_re), bd(bb_im), lb_re.reshape(1, S5_W), lb_im.reshape(1, S5_W), cd(c_re), cd(c_im),
            d_skip.reshape(1, GW), w_glu, b_glu.reshape(1, GW))


def _ffn_up_kernel(h_ref, wa_ref, wb_ref, cw_ref, cb_ref, s0_ref, s1_ref, y_ref, a_ref, carry_scr, *, seq):
    h = h_ref[...]
    a = _dot(h, wa_ref[...])
    b = _dot(h, wb_ref[...])
    tm = a.shape[0]
    if seq:
        @pl.when(pl.program_id(2) == 0)
        def _():
            carry_scr[...] = s0_ref[0]

        rowid = _iota((tm, 1), 0)
        c0 = carry_scr[0:1, :]
        c1 = carry_scr[1:2, :]
        a1 = jnp.where(rowid == 0, c1, pltpu.roll(a, 1, 0))
        a2 = jnp.where(rowid == 0, c0, jnp.where(rowid == 1, c1, pltpu.roll(a, 2, 0)))
        carry_scr[...] = a[tm - 2:tm, :]
        a_ref[0] = a[tm - 2:tm, :]
    else:
        a2 = s0_ref[...]
        a1 = s1_ref[...]
        a_ref[...] = a
    c = cb_ref[...] + a2 * cw_ref[0:1, :] + a1 * cw_ref[1:2, :] + a * cw_ref[2:3, :]
    y_ref[...] = (c * _sigmoid(c) * b).astype(BF16)


def _ffn_up(h2, w_up, cw, cb, st, nb, t):
    n, d = h2.shape
    dff = w_up.shape[1] // 2
    tn = 512
    nj = dff // tn
    if t > 1:
        tm = min(MM_ROWS, t)
        nt = t // tm
        y, fc = pl.pallas_call(
            functools.partial(_ffn_up_kernel, seq=True),
            out_shape=(jax.ShapeDtypeStruct((n, dff), BF16), jax.ShapeDtypeStruct((nb, 2, dff), F32)),
            grid=(nb, nj, nt),
            in_specs=[pl.BlockSpec((tm, d), lambda b, j, i: (b * nt + i, 0)),
                      pl.BlockSpec((d, tn), lambda b, j, i: (0, j)),
                      pl.BlockSpec((d, tn), lambda b, j, i: (0, nj + j)),
                      pl.BlockSpec((3, tn), lambda b, j, i: (0, j)),
                      pl.BlockSpec((1, tn), lambda b, j, i: (0, j)),
                      pl.BlockSpec((1, 2, tn), lambda b, j, i: (b, 0, j)),
                      pl.BlockSpec((1, 2, tn), lambda b, j, i: (b, 0, j))],
            out_specs=(pl.BlockSpec((tm, tn), lambda b, j, i: (b * nt + i, j)),
                       pl.BlockSpec((1, 2, tn), lambda b, j, i: (b, 0, j))),
            scratch_shapes=[pltpu.VMEM((2, tn), F32)],
            compiler_params=_cp("parallel", "parallel", "arbitrary"),
        )(h2, w_up, w_up, cw, cb, st, st)
        return y, fc
    s0, s1 = st[:, 0, :], st[:, 1, :]
    y, a = pl.pallas_call(
        functools.partial(_ffn_up_kernel, seq=False),
        out_shape=(jax.ShapeDtypeStruct((n, dff), BF16), jax.ShapeDtypeStruct((n, dff), F32)),
        grid=(nj,),
        in_specs=[pl.BlockSpec((n, d), lambda j: (0, 0)),
                  pl.BlockSpec((d, tn), lambda j: (0, j)),
                  pl.BlockSpec((d, tn), lambda j: (0, nj + j)),
                  pl.BlockSpec((3, tn), lambda j: (0, j)),
                  pl.BlockSpec((1, tn), lambda j: (0, j)),
                  pl.BlockSpec((n, tn), lambda j: (0, j)),
                  pl.BlockSpec((n, tn), lambda j: (0, j))],
        out_specs=(pl.BlockSpec((n, tn), lambda j: (0, j)), pl.BlockSpec((n, tn), lambda j: (0, j))),
        scratch_shapes=[pltpu.VMEM((2, tn), F32)],
        compiler_params=_cp("parallel"),
    )(h2, w_up, w_up, cw, cb, s0, s1)
    return y, jnp.stack([s1, a], axis=1)


def _permute_w_in(w):
    cols = lambda s, n: w[:, s:s + n]
    parts = [cols(0, 4 * GW), cols(2064, GW), cols(2832, GW), cols(5208, GW), cols(3416, R_IN),
             cols(2576, 128), cols(2704, 128), cols(3344, HD), cols(2048, NH), cols(2056, NH), cols(3408, NH),
             jnp.zeros((w.shape[0], NP - C_SM - HD - 3 * NH), w.dtype)]
    return jnp.concatenate(parts, axis=1).astype(BF16)


def _layer(x2, nb, t, l, W, st, tables, cache):
    n = nb * t
    c0, n0, m0, rs0, rsh0, sre0, sim0, conv0 = st
    proj = _in_proj(x2, W['norm_mix'], W['w_in'])
    aq_r, iq_r, k_r, v_r, ik_r = _rope_call(proj, tables[0], tables[1], *tables[2])

    if cache is None:
        ym, c1, n1, m1 = _mlstm_prompt(proj, W['m_b_i'], W['m_b_f'], W['m_norm'], nb, t)
        ya = _dsa_prompt(iq_r, proj, ik_r, aq_r, k_r, v_r, nb, t)
    else:
        ym, c1, n1, m1 = _mlstm_step(proj, W['m_b_i'], W['m_b_f'], W['m_norm'], c0, n0, m0)
        ym = ym.reshape(n, GW)
        cki, ck, cv, page_table, n_pool = cache
        ya = _dsa_sample(l, page_table, iq_r, proj[:, C_SM + SM_IW:C_SM + SM_IW + NH], ik_r, aq_r, k_r, v_r,
                         cki, ck, cv, n_pool)

    prev = rsh0.reshape(nb, 1, R_IN) if t > 1 else rsh0
    rs = _rwkv_prep(proj, prev, W['r_mu'], W['r_w0'], W['r_w_w2'], W['r_a0'], W['r_w_a2'], W['r_w_g2'],
                    W['r_k_k'], W['r_k_a'], nb, t)
    if t > 1:
        yr, rs1 = _rwkv_scan(rs, W['r_r_k'], W['r_ln_w'], W['r_ln_b'], nb, t)
    else:
        yr, rs1 = _rwkv_step(rs, W['r_r_k'], W['r_ln_w'], W['r_ln_b'], rs0)
        yr = yr.reshape(n, GW)
    rsh1 = proj.reshape(nb, t, NP)[:, t - 1, C_RIN:C_RIN + R_IN]

    su = proj[:, C_SU:C_SU + GW]
    u_tm = su.reshape(nb, t, GW).transpose(1, 0, 2).reshape(t * nb, GW)
    ys_tm, sre1, sim1 = _s5(u_tm, W['s5'], sre0.reshape(nb, S5_W), sim0.reshape(nb, S5_W), nb, t)
    ys = ys_tm.reshape(t, nb, GW).transpose(1, 0, 2).reshape(n, GW)

    mixed = jnp.concatenate([ym, ya, yr, ys], axis=-1).astype(BF16)
    x2 = _res_matmul(x2, mixed, W['w_out'])
    h2 = _rmsnorm(x2, W['norm_ffn'], BF16)
    y, conv1 = _ffn_up(h2, W['ffn_w_up'], W['ffn_conv_w'], W['ffn_conv_b'], conv0, nb, t)
    x2 = _res_matmul(x2, y, W['ffn_w_down'])
    outs = (k_r.reshape(nb, t, A_KV, HD), v_r.reshape(nb, t, A_KV, HD), ik_r.reshape(nb, t, HD),
            c1, n1, m1.reshape(nb, NH), rs1, rsh1, sre1.reshape(nb, S5_G, S5_P), sim1.reshape(nb, S5_G, S5_P), conv1)
    return x2, outs


def kernel(x_prompt, x_sample, cache_k, cache_v, cache_kidx, page_table, state_mlstm_c, state_mlstm_n,
           state_mlstm_m, state_rwkv_s, state_rwkv_shift, state_s5_re, state_s5_im, state_ffn_conv,
           norm_mix, w_in, w_out, m_b_i, m_b_f, m_norm, r_mu, r_w0, r_w_w2, r_a0, r_w_a2, r_w_g2,
           r_k_k, r_k_a, r_r_k, r_ln_w, r_ln_b, s5_a_re, s5_a_im, s5_b_re, s5_b_im, s5_c_re, s5_c_im,
           s5_d, s5_log_dt, s5_w_glu, s5_b_glu, norm_ffn, ffn_w_up, ffn_conv_w, ffn_conv_b, ffn_w_down,
           norm_final):
    bp, tp, d = x_prompt.shape
    bs, ts, _ = x_sample.shape
    assert ts == 1 and tp % CHUNK == 0
    depth = w_in.shape[0]
    n_pool = cache_k.shape[1]
    past = page_table.shape[1] * PAGE
    dff = ffn_conv_b.shape[-1]

    row = lambda z: z.reshape(1, -1)
    layers = []
    for l in range(depth):
        layers.append(dict(
            norm_mix=row(norm_mix[l]), w_in=_permute_w_in(w_in[l]), w_out=w_out[l].astype(BF16),
            m_b_i=row(m_b_i[l]), m_b_f=row(m_b_f[l]), m_norm=row(m_norm[l]),
            r_mu=row(r_mu[l]), r_w0=row(r_w0[l]), r_w_w2=r_w_w2[l], r_a0=row(r_a0[l]), r_w_a2=r_w_a2[l],
            r_w_g2=r_w_g2[l], r_k_k=row(r_k_k[l]), r_k_a=row(r_k_a[l]), r_r_k=row(r_r_k[l]),
            r_ln_w=row(r_ln_w[l]), r_ln_b=row(r_ln_b[l]),
            s5=_s5_mats(s5_a_re[l], s5_a_im[l], s5_b_re[l], s5_b_im[l], s5_c_re[l], s5_c_im[l], s5_d[l],
                        s5_log_dt[l], s5_w_glu[l], s5_b_glu[l]),
            norm_ffn=row(norm_ffn[l]), ffn_w_up=ffn_w_up[l].astype(BF16), ffn_conv_w=ffn_conv_w[l],
            ffn_conv_b=row(ffn_conv_b[l]), ffn_w_down=ffn_w_down[l].astype(BF16)))

    cos_p, sin_p = _rope_tables(jnp.arange(tp))
    cos_s, sin_s = _rope_tables(jnp.full((bs,), past))
    tab_p = (cos_p, sin_p, (bp, tp))
    tab_s = (cos_s, sin_s, (1, bs))

    zeros = lambda *s: jnp.zeros(s, F32)
    st_p = (zeros(bp, NH, HD, HD), zeros(bp, NH, HD), zeros(bp, NH), zeros(bp, NH, HD, HD), zeros(bp, R_IN),
            zeros(bp, S5_G, S5_P), zeros(bp, S5_G, S5_P), zeros(bp, 2, dff))
    cki = cache_kidx.reshape(depth * n_pool, PAGE, HD)
    ck = cache_k.reshape(depth * n_pool, PAGE, A_KV * HD)
    cv = cache_v.reshape(depth * n_pool, PAGE, A_KV * HD)

    xp = x_prompt.reshape(bp * tp, d)
    xs = x_sample.reshape(bs, d)
    new_p, new_s = [], []
    for l in range(depth):
        xp, sp = _layer(xp, bp, tp, l, layers[l], st_p, tab_p, None)
        st_s = (state_mlstm_c[l], state_mlstm_n[l], state_mlstm_m[l], state_rwkv_s[l], state_rwkv_shift[l],
                state_s5_re[l], state_s5_im[l], state_ffn_conv[l])
        xs, ss = _layer(xs, bs, 1, l, layers[l], st_s, tab_s, (cki, ck, cv, page_table, n_pool))
        new_p.append(sp)
        new_s.append(ss)
    (k_p, v_p, ki_p, mc_p, mn_p, mm_p, rs_p, rsh_p, sre_p, sim_p, fc_p) = [jnp.stack(z) for z in zip(*new_p)]
    (k_s, v_s, ki_s, mc_s, mn_s, mm_s, rs_s, rsh_s, sre_s, sim_s, fc_s) = [jnp.stack(z) for z in zip(*new_s)]
    y_prompt = _rmsnorm(xp, row(norm_final), F32).reshape(bp, tp, d)
    y_sample = _rmsnorm(xs, row(norm_final), F32).reshape(bs, ts, d)
    return (y_prompt, y_sample, k_p, k_s, v_p, v_s, ki_p, ki_s, mc_p, mc_s, mn_p, mn_s, mm_p, mm_s,
            rs_p, rs_s, rsh_p, rsh_s, sre_p, sre_s, sim_p, sim_s, fc_p, fc_s)
```

```python
import functools
import math

import jax
import jax.numpy as jnp
from jax import lax
from jax.experimental import pallas as pl
from jax.experimental.pallas import tpu as pltpu

F32 = jnp.float32
BF16 = jnp.bfloat16
HI = lax.Precision.HIGHEST

HD = 64
NH = 8
GW = NH * HD
A_KV = 2
PAGE = 128
TOPK_MAX = 256
ROPE_THETA = 10000.0
R_IN = 3 * GW + 64 + 64 + 128
R_LN_EPS = 64e-5
S5_G, S5_CH, S5_P = 32, 16, 64
S5_W = S5_G * S5_P
NORM_EPS = 1e-6
CHUNK = 64
IDX_SCALE = HD ** -0.5 * NH ** -0.5

C_M, C_AQ, C_IQ, C_SU, C_RIN, C_AK, C_AV, C_SM = 0, 2048, 2560, 3072, 3584, 5376, 5504, 5632
NP = 5760
SM_IK, SM_MI, SM_MF, SM_IW = 0, 64, 72, 80

VMEM_LIMIT = 56 * 1024 * 1024
MM_ROWS = 1024


def _cp(*sem):
    return pltpu.CompilerParams(dimension_semantics=sem, vmem_limit_bytes=VMEM_LIMIT)


def _dot(a, b, prec=None):
    return jnp.dot(a, b, preferred_element_type=F32, precision=prec)


def _dot_nt(a, b, prec=None):
    return lax.dot_general(a, b, (((1,), (1,)), ((), ())), preferred_element_type=F32, precision=prec)


def _dot_tn(a, b, prec=None):
    return lax.dot_general(a, b, (((0,), (0,)), ((), ())), preferred_element_type=F32, precision=prec)


def _sigmoid(x):
    return 1.0 / (1.0 + jnp.exp(-x))


def _softplus(x):
    return jnp.maximum(x, 0.0) + jnp.log(1.0 + jnp.exp(-jnp.abs(x)))


def _iota(shape, dim):
    return lax.broadcasted_iota(jnp.int32, shape, dim)


HG = 4
BD = HG * HD
assert CHUNK == HD


def _bd_masks(c):
    row, col = _iota((HG * c, HG * c), 0), _iota((HG * c, HG * c), 1)
    same = (row // c) == (col // c)
    t, s = row % c, col % c
    return same, jnp.logical_and(same, s <= t), jnp.logical_and(same, s < t)


def _bd_diag(x):
    nb, ng = x.shape[:2]
    x6 = x.reshape(nb, ng, HG, HD, HG, HD)
    return jnp.stack([x6[:, :, j, :, j, :] for j in range(HG)], axis=2).reshape(nb, ng * HG, HD, HD)


def _inproj_kernel(x_ref, g_ref, w_ref, o_ref, h_scr):
    @pl.when(pl.program_id(1) == 0)
    def _():
        x = x_ref[...]
        ms = jnp.mean(x * x, axis=-1, keepdims=True)
        h_scr[...] = (x * lax.rsqrt(ms + NORM_EPS) * g_ref[...]).astype(BF16)

    o_ref[...] = _dot(h_scr[...], w_ref[...])


def _in_proj(x2, g, w, l):
    n, d = x2.shape
    npad = w.shape[2]
    tm = min(MM_ROWS, n)
    tn = 640
    return pl.pallas_call(
        _inproj_kernel,
        out_shape=jax.ShapeDtypeStruct((n, npad), F32),
        grid=(n // tm, npad // tn),
        in_specs=[pl.BlockSpec((tm, d), lambda i, j: (i, 0)),
                  pl.BlockSpec((1, d), lambda i, j: (0, 0)),
                  pl.BlockSpec((None, d, tn), lambda i, j: (l, 0, j))],
        out_specs=pl.BlockSpec((tm, tn), lambda i, j: (i, j)),
        scratch_shapes=[pltpu.VMEM((tm, d), BF16)],
        compiler_params=_cp("parallel", "arbitrary"),
    )(x2, g, w)


def _rmsnorm_kernel(x_ref, g_ref, o_ref):
    x = x_ref[...]
    ms = jnp.mean(x * x, axis=-1, keepdims=True)
    o_ref[...] = (x * lax.rsqrt(ms + NORM_EPS) * g_ref[...]).astype(o_ref.dtype)


def _rmsnorm(x2, g, dtype):
    n, d = x2.shape
    tm = min(512, n)
    return pl.pallas_call(
        _rmsnorm_kernel,
        out_shape=jax.ShapeDtypeStruct((n, d), dtype),
        grid=(n // tm,),
        in_specs=[pl.BlockSpec((tm, d), lambda i: (i, 0)), pl.BlockSpec((1, d), lambda i: (0, 0))],
        out_specs=pl.BlockSpec((tm, d), lambda i: (i, 0)),
        compiler_params=_cp("parallel"),
    )(x2, g)


def _resmm_kernel(r_ref, *refs):
    y_refs, w_ref, o_ref = refs[:-2], refs[-2], refs[-1]
    acc = r_ref[...]
    k0 = 0
    for y_ref in y_refs:
        kw = y_ref.shape[1]
        acc = acc + _dot(y_ref[...].astype(BF16), w_ref[k0:k0 + kw, :])
        k0 += kw
    o_ref[...] = acc


def _res_matmul(res, ys, w, l):
    n = res.shape[0]
    k, d = w.shape[1:]
    assert sum(y.shape[1] for y in ys) == k
    tm = min(MM_ROWS, n)
    tn = 512
    return pl.pallas_call(
        _resmm_kernel,
        out_shape=jax.ShapeDtypeStruct((n, d), F32),
        grid=(n // tm, d // tn),
        in_specs=[pl.BlockSpec((tm, tn), lambda i, j: (i, j))]
                 + [pl.BlockSpec((tm, y.shape[1]), lambda i, j: (i, 0)) for y in ys]
                 + [pl.BlockSpec((None, k, tn), lambda i, j: (l, 0, j))],
        out_specs=pl.BlockSpec((tm, tn), lambda i, j: (i, j)),
        compiler_params=_cp("parallel", "arbitrary"),
    )(res, *ys, w)


def _rope(x, cos, sin):
    w = x.shape[1]
    first = (_iota(x.shape, 1) & (HD - 1)) < HD // 2
    sw = jnp.where(first, pltpu.roll(x, w - HD // 2, 1), pltpu.roll(x, HD // 2, 1))
    return x * cos + sw * sin


def _rope_kernel(aq_ref, iq_ref, ak_ref, av_ref, sm_ref, cos_ref, sin_ref, aqo, iqo, ko, vo, iko):
    cos = cos_ref[...]
    sin = sin_ref[...]
    aqo[...] = _rope(aq_ref[...], cos, sin)
    iqo[...] = _rope(iq_ref[...], cos, sin)
    ko[...] = _rope(ak_ref[...], cos[:, :128], sin[:, :128])
    vo[...] = av_ref[...]
    iko[...] = _rope(sm_ref[...], cos[:, :128], sin[:, :128])[:, :HD]


def _rope_call(proj, cos, sin, nb, nt_rows):
    n = proj.shape[0]
    tm = min(512, nt_rows)
    nt = nt_rows // tm
    row = lambda b, i: b * nt + i
    return pl.pallas_call(
        _rope_kernel,
        out_shape=(jax.ShapeDtypeStruct((n, GW), F32), jax.ShapeDtypeStruct((n, GW), F32),
                   jax.ShapeDtypeStruct((n, 128), F32), jax.ShapeDtypeStruct((n, 128), F32),
                   jax.ShapeDtypeStruct((n, HD), F32)),
        grid=(nb, nt),
        in_specs=[pl.BlockSpec((tm, GW), lambda b, i: (row(b, i), C_AQ // GW)),
                  pl.BlockSpec((tm, GW), lambda b, i: (row(b, i), C_IQ // GW)),
                  pl.BlockSpec((tm, 128), lambda b, i: (row(b, i), C_AK // 128)),
                  pl.BlockSpec((tm, 128), lambda b, i: (row(b, i), C_AV // 128)),
                  pl.BlockSpec((tm, 128), lambda b, i: (row(b, i), C_SM // 128)),
                  pl.BlockSpec((tm, GW), lambda b, i: (i, 0)),
                  pl.BlockSpec((tm, GW), lambda b, i: (i, 0))],
        out_specs=(pl.BlockSpec((tm, GW), lambda b, i: (row(b, i), 0)),
                   pl.BlockSpec((tm, GW), lambda b, i: (row(b, i), 0)),
                   pl.BlockSpec((tm, 128), lambda b, i: (row(b, i), 0)),
                   pl.BlockSpec((tm, 128), lambda b, i: (row(b, i), 0)),
                   pl.BlockSpec((tm, HD), lambda b, i: (row(b, i), 0))),
        compiler_params=_cp("parallel", "parallel"),
    )(proj, proj, proj, proj, proj, cos, sin)


def _rope_tables(pos):
    half = HD // 2
    inv = ROPE_THETA ** (-jnp.arange(half, dtype=F32) / half)
    ang = pos.astype(F32)[:, None] * inv[None, :]
    cos, sin = jnp.cos(ang), jnp.sin(ang)
    cos64 = jnp.concatenate([cos, cos], axis=-1)
    sin64 = jnp.concatenate([-sin, sin], axis=-1)
    return jnp.tile(cos64, (1, NH)), jnp.tile(sin64, (1, NH))


def _kth_largest(sc, extra, kk):
    kf = jnp.float32(kk)

    def count_ge(c):
        n = jnp.sum(jnp.where(sc >= c, 1.0, 0.0), axis=-1, keepdims=True)
        if extra is not None:
            n = n + jnp.where(extra >= c, 1.0, 0.0)
        return n

    def key_to_f(key):
        bits = key ^ ((key >> 31) & jnp.int32(0x7FFFFFFF))
        return lax.bitcast_convert_type(bits, F32)

    r = sc.shape[0]
    int_min = jnp.int32(-2 ** 31)
    lo = jnp.where(count_ge(jnp.zeros((r, 1), F32)) >= kf, jnp.int32(0), int_min)

    def body(j, lo):
        cand = lo + jnp.left_shift(jnp.int32(1), jnp.int32(30) - j)
        ok = count_ge(key_to_f(cand)) >= kf
        return jnp.where(ok, cand, lo)

    lo = lax.fori_loop(0, 31, body, lo)
    key_neg_inf = jnp.int32(-2 ** 31 + 0x7FFFFF)
    return jnp.where(lo <= key_neg_inf, -jnp.inf, key_to_f(lo))


def _strict_upper_bf16(n):
    return jnp.where(_iota((n, n), 0) < _iota((n, n), 1), 1.0, 0.0).astype(BF16)


def _split_bf16(x):
    hi = x.astype(BF16).astype(F32)
    return hi, x - hi


def _dsa_prompt_kernel(iq_ref, sm_ref, ik_ref, aq_ref, k_ref, v_ref, o_ref, sel_scr, kcat_scr, *, topk, qb, n_ext):
    t_keys = ik_ref.shape[0]
    i = pl.program_id(1)
    per_ext = (t_keys // qb) // n_ext

    @pl.when(i == 0)
    def _():
        hi, lo = _split_bf16(ik_ref[...])
        kcat_scr[...] = jnp.concatenate([hi, lo, hi], axis=1)

    iq = iq_ref[...]
    aq = aq_ref[...] * HD ** -0.5
    wts = sm_ref[:, SM_IW:SM_IW + NH] * IDX_SCALE
    tq = i * qb + _iota((qb, 1), 0)

    def body(ext):
        kcat = kcat_scr[0:ext, :]
        sc = jnp.zeros((qb, ext), F32)
        for h in range(NH):
            hi, lo = _split_bf16(iq[:, h * HD:(h + 1) * HD])
            qk = _dot_nt(jnp.concatenate([hi, hi, lo], axis=1), kcat)
            sc = sc + jnp.maximum(qk, 0.0) * wts[:, h:h + 1]
        causal = _iota((1, ext), 1) <= tq
        sc = jnp.where(causal, sc, -jnp.inf)

        thr = _kth_largest(sc, None, topk)
        gt = sc > thr
        eq = sc == thr
        n_gt = jnp.sum(jnp.where(gt, 1.0, 0.0), axis=-1, keepdims=True)
        n_eq = jnp.sum(jnp.where(eq, 1.0, 0.0), axis=-1, keepdims=True)
        need = jnp.float32(topk) - n_gt
        sel_scr[:, 0:ext] = jnp.where(jnp.logical_and(sc >= thr, causal), 1.0, 0.0)
        tie = jnp.logical_and(n_eq > need, thr > -jnp.inf)

        @pl.when(jnp.max(jnp.where(tie, 1.0, 0.0)) > 0.5)
        def _():
            ut = _strict_upper_bf16(128)
            run = jnp.zeros((qb, 1), F32)
            for c in range(ext // 128):
                sl = slice(c * 128, (c + 1) * 128)
                eqc = jnp.where(eq[:, sl], 1.0, 0.0)
                pref = _dot(eqc.astype(BF16), ut) + run
                keep = jnp.logical_or(gt[:, sl], jnp.logical_and(eq[:, sl], pref < need))
                sel_scr[:, sl] = jnp.where(jnp.logical_and(keep, causal[:, sl]), 1.0, 0.0)
                run = run + jnp.sum(eqc, axis=-1, keepdims=True)

        sel = sel_scr[:, 0:ext] > 0.5
        for g in range(A_KV):
            kg = k_ref[0:ext, g * HD:(g + 1) * HD]
            vg = v_ref[0:ext, g * HD:(g + 1) * HD]
            for j in range(NH // A_KV):
                h = g * (NH // A_KV) + j
                s = _dot_nt(aq[:, h * HD:(h + 1) * HD], kg)
                s = jnp.where(sel, s, -jnp.inf)
                m = jnp.max(s, axis=-1, keepdims=True)
                p = jnp.exp(s - m)
                l = jnp.sum(p, axis=-1, keepdims=True)
                o_ref[:, h * HD:(h + 1) * HD] = _dot(p, vg) / l

    for j in range(n_ext):
        pl.when(i // per_ext == j)(functools.partial(body, (j + 1) * (t_keys // n_ext)))


def _dsa_prompt(iq_r, proj, ik_r, aq_r, k_r, v_r, nb, t):
    qb = min(128, t)
    nq = t // qb
    n_ext = min(8, nq)
    assert nq % n_ext == 0
    topk = min(TOPK_MAX, t // 4)
    n = nb * t
    row = lambda b, i: b * nq + i
    return pl.pallas_call(
        functools.partial(_dsa_prompt_kernel, topk=topk, qb=qb, n_ext=n_ext),
        out_shape=jax.ShapeDtypeStruct((n, GW), F32),
        grid=(nb, nq),
        in_specs=[pl.BlockSpec((qb, GW), lambda b, i: (row(b, i), 0)),
                  pl.BlockSpec((qb, 128), lambda b, i: (row(b, i), C_SM // 128)),
                  pl.BlockSpec((t, HD), lambda b, i: (b, 0)),
                  pl.BlockSpec((qb, GW), lambda b, i: (row(b, i), 0)),
                  pl.BlockSpec((t, 128), lambda b, i: (b, 0)),
                  pl.BlockSpec((t, 128), lambda b, i: (b, 0))],
        out_specs=pl.BlockSpec((qb, GW), lambda b, i: (row(b, i), 0)),
        scratch_shapes=[pltpu.VMEM((qb, t), F32), pltpu.VMEM((t, 3 * HD), F32)],
        compiler_params=_cp("parallel", "arbitrary"),
    )(iq_r, proj, ik_r, aq_r, k_r, v_r)


def _dsa_sample_score_kernel(*refs, n_pages):
    _, iq_ref, w_ref, ikn_ref = refs[:4]
    pages = refs[4:4 + n_pages]
    sc_ref, sn_ref = refs[4 + n_pages:]
    iq = iq_ref[0]
    w = w_ref[0] * IDX_SCALE
    for c in range(n_pages):
        qk = _dot(iq, pages[c][0], HI)
        sc_ref[0, :, c * PAGE:(c + 1) * PAGE] = jnp.sum(jnp.maximum(qk, 0.0) * w, axis=0, keepdims=True)
    qkn = jnp.sum(iq * ikn_ref[0], axis=-1, keepdims=True)
    sn = jnp.sum(jnp.maximum(qkn, 0.0) * w, axis=0, keepdims=True)
    sn_ref[0] = jnp.broadcast_to(sn, (1, 128))


def _dsa_sample_select_kernel(sc_ref, sn_ref, sel_ref, seln_ref, *, topk):
    sc = sc_ref[...]
    sn = sn_ref[:, 0:1]
    bd, s_keys = sc.shape
    thr = _kth_largest(sc, sn, topk)
    gt = sc > thr
    eq = sc == thr
    n_gt = jnp.sum(jnp.where(gt, 1.0, 0.0), axis=-1, keepdims=True) + jnp.where(sn > thr, 1.0, 0.0)
    need = jnp.float32(topk) - n_gt
    ut = _strict_upper_bf16(PAGE)
    run = jnp.zeros((bd, 1), F32)
    for c in range(s_keys // PAGE):
        sl = slice(c * PAGE, (c + 1) * PAGE)
        eqc = jnp.where(eq[:, sl], 1.0, 0.0)
        pref = _dot(eqc.astype(BF16), ut) + run
        keep = jnp.logical_or(gt[:, sl], jnp.logical_and(eq[:, sl], pref < need))
        sel_ref[:, sl] = jnp.where(keep, 1.0, 0.0)
        run = run + jnp.sum(eqc, axis=-1, keepdims=True)
    sel_new = jnp.logical_or(sn > thr, jnp.logical_and(sn == thr, run < need))
    seln_ref[...] = jnp.broadcast_to(jnp.where(sel_new, 1.0, 0.0), seln_ref.shape)


def _dsa_sample_attn_kernel(*refs, n_pages):
    _, aq_ref, kn_ref, vn_ref, sel_ref, seln_ref = refs[:6]
    kpages = refs[6:6 + n_pages]
    vpages = refs[6 + n_pages:6 + 2 * n_pages]
    o_ref, k_scr, v_scr = refs[6 + 2 * n_pages:]
    for c in range(n_pages):
        k_scr[:, c * PAGE:(c + 1) * PAGE] = kpages[c][0]
        v_scr[:, c * PAGE:(c + 1) * PAGE] = vpages[c][0]
    sel = sel_ref[0] > 0.5
    sel_new = seln_ref[0][:, 0:1] > 0.5
    aq = aq_ref[0]
    kn = kn_ref[0]
    vn = vn_ref[0]
    hpg = NH // A_KV
    for g in range(A_KV):
        qg = aq[g * hpg:(g + 1) * hpg, :]
        s = _dot(qg, k_scr[g * HD:(g + 1) * HD, :]) * HD ** -0.5
        s = jnp.where(sel, s, -jnp.inf)
        s_new = jnp.sum(qg * kn[:, g * HD:(g + 1) * HD], axis=-1, keepdims=True) * HD ** -0.5
        s_new = jnp.where(sel_new, s_new, -jnp.inf)
        m = jnp.maximum(jnp.max(s, axis=-1, keepdims=True), s_new)
        pr = jnp.exp(s - m)
        pn = jnp.exp(s_new - m)
        l = jnp.sum(pr, axis=-1, keepdims=True) + pn
        o = _dot_nt(pr, v_scr[g * HD:(g + 1) * HD, :]) + pn * vn[:, g * HD:(g + 1) * HD]
        o_ref[0, g * hpg:(g + 1) * hpg, :] = o / l


def _dsa_sample(layer, page_table, iq_r, iw, ik_r, aq_r, k_r, v_r, cki, ck, cv, n_pool):
    bd, n_pages = page_table.shape
    past = n_pages * PAGE
    topk = min(TOPK_MAX, (past + 1) // 4)
    base = layer * n_pool
    per_b = lambda b, pt: (b, 0, 0)

    def page_specs(width):
        return [pl.BlockSpec((1, width, PAGE), lambda b, pt, c=c: (base + pt[b, c], 0, 0)) for c in range(n_pages)]

    sc, sn = pl.pallas_call(
        functools.partial(_dsa_sample_score_kernel, n_pages=n_pages),
        out_shape=(jax.ShapeDtypeStruct((bd, 1, past), F32), jax.ShapeDtypeStruct((bd, 1, 128), F32)),
        grid_spec=pltpu.PrefetchScalarGridSpec(
            num_scalar_prefetch=1,
            grid=(bd,),
            in_specs=[pl.BlockSpec((1, NH, HD), per_b), pl.BlockSpec((1, NH, 1), per_b),
                      pl.BlockSpec((1, 1, HD), per_b)] + page_specs(HD),
            out_specs=(pl.BlockSpec((1, 1, past), per_b), pl.BlockSpec((1, 1, 128), per_b))),
        compiler_params=_cp("arbitrary"),
    )(page_table, iq_r.reshape(bd, NH, HD), iw.reshape(bd, NH, 1), ik_r.reshape(bd, 1, HD), *([cki] * n_pages))

    sel, seln = pl.pallas_call(
        functools.partial(_dsa_sample_select_kernel, topk=topk),
        out_shape=(jax.ShapeDtypeStruct((bd, past), F32), jax.ShapeDtypeStruct((bd, 128), F32)),
    )(sc.reshape(bd, past), sn.reshape(bd, 128))

    out = pl.pallas_call(
        functools.partial(_dsa_sample_attn_kernel, n_pages=n_pages),
        out_shape=jax.ShapeDtypeStruct((bd, NH, HD), F32),
        grid_spec=pltpu.PrefetchScalarGridSpec(
            num_scalar_prefetch=1,
            grid=(bd,),
            in_specs=[pl.BlockSpec((1, NH, HD), per_b), pl.BlockSpec((1, 1, 128), per_b),
                      pl.BlockSpec((1, 1, 128), per_b), pl.BlockSpec((1, 1, past), per_b),
                      pl.BlockSpec((1, 1, 128), per_b)] + page_specs(128) + page_specs(128),
            out_specs=pl.BlockSpec((1, NH, HD), per_b),
            scratch_shapes=[pltpu.VMEM((128, past), F32), pltpu.VMEM((128, past), F32)]),
        compiler_params=_cp("arbitrary"),
    )(page_table, aq_r.reshape(bd, NH, HD), k_r.reshape(bd, 1, 128), v_r.reshape(bd, 1, 128),
      sel.reshape(bd, 1, past), seln.reshape(bd, 1, 128), *([ck] * n_pages), *([cv] * n_pages))
    return out.reshape(bd, GW)


def _log_sigmoid(x):
    return jnp.minimum(x, 0.0) - jnp.log(1.0 + jnp.exp(-jnp.abs(x)))


MLSTM_ROWS = 2 * CHUNK


def _mlstm_prompt_kernel(m_ref, sm_ref, gb_ref, nw_ref, y_ref, c_ref, n_ref, mm_ref):
    L = CHUNK

    @pl.when(pl.program_id(1) == 0)
    def _():
        c_ref[...] = jnp.zeros(c_ref.shape, F32)
        n_ref[...] = jnp.zeros(n_ref.shape, F32)
        mm_ref[...] = jnp.zeros(mm_ref.shape, F32)

    smb = sm_ref[...] + gb_ref[...]
    smt = smb.T
    ig_all = smb[:, SM_MI:SM_MI + NH]
    lf_all = _log_sigmoid(smb[:, SM_MF:SM_MF + NH])
    igt_all = smt[SM_MI:SM_MI + NH, :]
    lft_all = _log_sigmoid(smt[SM_MF:SM_MF + NH, :])
    ri, ci = _iota((L, L), 0), _iota((L, L), 1)
    tril = jnp.where(ci <= ri, 1.0, 0.0)
    triu = jnp.where(ci >= ri, 1.0, 0.0)
    same, causal, _ = _bd_masks(L)
    tile = lambda z: jnp.concatenate([z] * HG, axis=0)
    msk = lambda z: jnp.where(same, z, 0.0)
    stack_cols = lambda z, g: jnp.concatenate([z[:, g * HG + j:g * HG + j + 1] for j in range(HG)], axis=0)
    stack_rows = lambda z, g: jnp.concatenate([z[g * HG + j:g * HG + j + 1, :] for j in range(HG)], axis=1)
    for cc in range(MLSTM_ROWS // L):
        rows = slice(cc * L, (cc + 1) * L)
        ig = ig_all[rows]
        bcs = _dot(tril, lf_all[rows], HI)
        bcst = _dot(lft_all[:, rows], triu, HI)
        igt = igt_all[:, rows]
        for g in range(NH // HG):
            ln = slice(g * BD, (g + 1) * BD)
            heads = range(g * HG, (g + 1) * HG)
            qexp = msk(tile(m_ref[rows, ln]))
            kt = tile(m_ref[rows, GW + g * BD:GW + (g + 1) * BD] * HD ** -0.5)
            kexp = msk(kt)
            vexp = msk(tile(m_ref[rows, 2 * GW + g * BD:2 * GW + (g + 1) * BD]))
            og = m_ref[rows, 3 * GW + g * BD:3 * GW + (g + 1) * BD]
            bcol = stack_cols(bcs, g)
            icol = stack_cols(ig, g)
            brow = stack_rows(bcst, g)
            irow = stack_rows(igt, g)
            mprev = [mm_ref[0, :, h:h + 1] for h in heads]
            bl = [bcs[L - 1:L, h:h + 1] for h in heads]
            percol = lambda zs: jnp.concatenate([jnp.broadcast_to(z, (L, 1)) for z in zs], axis=0)
            perrow = lambda zs: jnp.concatenate([jnp.broadcast_to(z, (1, HD)) for z in zs], axis=1)
            mprev_col = percol(mprev)
            dmat = jnp.where(causal, bcol - brow + irow, -jnp.inf)
            inter = bcol + mprev_col
            mj = jnp.maximum(inter, jnp.max(dmat, axis=-1, keepdims=True))
            s = _dot_nt(qexp, kt) * jnp.exp(dmat - mj)
            iw = jnp.exp(inter - mj)
            cbd = c_ref[0, g]
            nrow = n_ref[0, :, ln]
            num = _dot(s, vexp) + iw * _dot_nt(qexp, cbd)
            den = jnp.sum(s, axis=-1, keepdims=True) + iw * jnp.sum(qexp * nrow, axis=-1, keepdims=True)
            hc = num / jnp.maximum(jnp.abs(den), jnp.exp(-mj))
            wl = percol(bl) - bcol + icol
            m_new = [jnp.maximum(bl[j] + mprev[j], jnp.max(wl[j * L:(j + 1) * L], axis=0, keepdims=True))
                     for j in range(HG)]
            dec = [jnp.exp(bl[j] + mprev[j] - m_new[j]) for j in range(HG)]
            ws = jnp.exp(wl - percol(m_new))
            c_ref[0, g] = percol(dec) * cbd + _dot_tn(vexp * ws, kexp)
            n_ref[0, :, ln] = perrow(dec) * nrow + jnp.sum(kexp * ws, axis=0, keepdims=True)
            for j, h in enumerate(heads):
                mm_ref[0, :, h:h + 1] = m_new[j]
            hn = hc * lax.rsqrt(jnp.sum(hc * hc, axis=-1, keepdims=True) * (1.0 / HD) + NORM_EPS)
            hn = sum(hn[j * L:(j + 1) * L] for j in range(HG))
            y_ref[rows, ln] = hn * nw_ref[:, ln] * _sigmoid(og)


def _gate_bias_row(b_i, b_f):
    z = lambda n: jnp.zeros((1, n), F32)
    return jnp.concatenate([z(SM_MI), b_i, b_f, z(128 - SM_MF - NH)], axis=1)


def _mlstm_prompt(proj, b_i, b_f, nw, nb, t):
    tm = MLSTM_ROWS
    nc = t // tm
    n = nb * t
    ng = NH // HG
    row = lambda b, c: b * nc + c
    y, cbd, nrow, m = pl.pallas_call(
        _mlstm_prompt_kernel,
        out_shape=(jax.ShapeDtypeStruct((n, GW), F32), jax.ShapeDtypeStruct((nb, ng, BD, BD), F32),
                   jax.ShapeDtypeStruct((nb, 1, GW), F32), jax.ShapeDtypeStruct((nb, 1, NH), F32)),
        grid=(nb, nc),
        in_specs=[pl.BlockSpec((tm, 4 * GW), lambda b, c: (row(b, c), 0)),
                  pl.BlockSpec((tm, 128), lambda b, c: (row(b, c), C_SM // 128)),
                  pl.BlockSpec((1, 128), lambda b, c: (0, 0)),
                  pl.BlockSpec((1, GW), lambda b, c: (0, 0))],
        out_specs=(pl.BlockSpec((tm, GW), lambda b, c: (row(b, c), 0)),
                   pl.BlockSpec((1, ng, BD, BD), lambda b, c: (b, 0, 0, 0)),
                   pl.BlockSpec((1, 1, GW), lambda b, c: (b, 0, 0)),
                   pl.BlockSpec((1, 1, NH), lambda b, c: (b, 0, 0))),
        compiler_params=_cp("parallel", "arbitrary"),
    )(proj, proj, _gate_bias_row(b_i, b_f), nw)
    return y, _bd_diag(cbd), nrow.reshape(nb, NH, HD), m


def _col_of_row(x_row, eye):
    return jnp.sum(jnp.where(eye, x_row, 0.0), axis=1, keepdims=True)


def _row_of_col(x_col, eye):
    return jnp.sum(jnp.where(eye, x_col, 0.0), axis=0, keepdims=True)


def _mlstm_step_kernel(m_ref, sm_ref, bi_ref, bf_ref, nw_ref, c_ref, n_ref, mm_ref, y_ref, co_ref, no_ref, mo_ref):
    xr = m_ref[0]
    sm = sm_ref[0]
    ig = sm[:, SM_MI:SM_MI + NH] + bi_ref[...]
    lf = _log_sigmoid(sm[:, SM_MF:SM_MF + NH] + bf_ref[...])
    eye = _iota((HD, HD), 0) == _iota((HD, HD), 1)
    for h in range(NH):
        q = xr[:, h * HD:(h + 1) * HD]
        k = xr[:, GW + h * HD:GW + (h + 1) * HD] * HD ** -0.5
        v = xr[:, 2 * GW + h * HD:2 * GW + (h + 1) * HD]
        og = xr[:, 3 * GW + h * HD:3 * GW + (h + 1) * HD]
        cmat = c_ref[0, h]
        nrow = n_ref[0, h:h + 1, :]
        i_ = ig[:, h:h + 1]
        inter = lf[:, h:h + 1] + mm_ref[0, :, h:h + 1]
        mj = jnp.maximum(inter, i_)
        vcol = _col_of_row(v, eye)
        s = jnp.sum(q * k, axis=-1, keepdims=True) * jnp.exp(i_ - mj)
        iw = jnp.exp(inter - mj)
        cq = jnp.sum(cmat * q, axis=1, keepdims=True)
        num = s * vcol + iw * cq
        den = s + iw * jnp.sum(nrow * q, axis=-1, keepdims=True)
        hcol = num / jnp.maximum(jnp.abs(den), jnp.exp(-mj))
        dec = jnp.exp(inter - mj)
        ws = jnp.exp(i_ - mj)
        co_ref[0, h] = dec * cmat + (ws * vcol) * k
        no_ref[0, h:h + 1, :] = dec * nrow + ws * k
        mo_ref[0, :, h:h + 1] = mj
        hn = hcol * lax.rsqrt(jnp.mean(hcol * hcol, axis=0, keepdims=True) + NORM_EPS)
        y_ref[0, :, h * HD:(h + 1) * HD] = _row_of_col(hn, eye) * nw_ref[:, h * HD:(h + 1) * HD] * _sigmoid(og)


def _mlstm_step(proj, b_i, b_f, nw, c_all, l, n0, m0):
    bd = proj.shape[0]
    per_b3 = lambda b: (b, 0, 0)
    return pl.pallas_call(
        _mlstm_step_kernel,
        out_shape=(jax.ShapeDtypeStruct((bd, 1, GW), F32), jax.ShapeDtypeStruct((bd, NH, HD, HD), F32),
                   jax.ShapeDtypeStruct((bd, NH, HD), F32), jax.ShapeDtypeStruct((bd, 1, NH), F32)),
        grid=(bd,),
        in_specs=[pl.BlockSpec((1, 1, 4 * GW), lambda b: (b, 0, 0)),
                  pl.BlockSpec((1, 1, 128), lambda b: (b, 0, C_SM // 128)),
                  pl.BlockSpec((1, NH), lambda b: (0, 0)),
                  pl.BlockSpec((1, NH), lambda b: (0, 0)),
                  pl.BlockSpec((1, GW), lambda b: (0, 0)),
                  pl.BlockSpec((1, NH, HD, HD), lambda b: (l * bd + b, 0, 0, 0)),
                  pl.BlockSpec((1, NH, HD), per_b3),
                  pl.BlockSpec((1, 1, NH), per_b3)],
        out_specs=(pl.BlockSpec((1, 1, GW), per_b3),
                   pl.BlockSpec((1, NH, HD, HD), lambda b: (b, 0, 0, 0)),
                   pl.BlockSpec((1, NH, HD), per_b3),
                   pl.BlockSpec((1, 1, NH), per_b3)),
        compiler_params=_cp("parallel"),
    )(proj.reshape(bd, 1, NP), proj.reshape(bd, 1, NP), b_i, b_f, nw,
      c_all.reshape((-1,) + c_all.shape[2:]), n0, m0.reshape(bd, 1, NH))


def _rwkv_prep_kernel(x_ref, prev_ref, mu_ref, w0_ref, ww2_ref, a0_ref, wa2_ref, wg2_ref, kk_ref, ka_ref,
                      r_o, lw_o, k_o, v_o, kk_o, a_o, g_o, carry_scr, *, seq):
    x = x_ref[...]
    tm = x.shape[0]
    if seq:
        first = jnp.where(pl.program_id(1) == 0, prev_ref[0], carry_scr[...])
        xprev = jnp.where(_iota((tm, 1), 0) == 0, first, pltpu.roll(x, 1, 0))
        carry_scr[...] = x[tm - 1:tm, :]
    else:
        xprev = prev_ref[...]
    xm = x + (xprev - x) * mu_ref[...]
    r = xm[:, 0:GW]
    kx = xm[:, GW:2 * GW]
    v = xm[:, 2 * GW:3 * GW]
    xw = xm[:, 3 * GW:3 * GW + 64]
    xa = xm[:, 3 * GW + 64:3 * GW + 128]
    xg = xm[:, 3 * GW + 128:R_IN]
    w = -_softplus(-(w0_ref[...] + _dot(jnp.tanh(xw), ww2_ref[...]))) - 0.5
    a = _sigmoid(a0_ref[...] + _dot(xa, wa2_ref[...]))
    r_o[...] = r
    lw_o[...] = -jnp.exp(w)
    v_o[...] = v
    a_o[...] = a
    g_o[...] = _dot(_sigmoid(xg), wg2_ref[...])
    k_o[...] = kx * (1.0 + (a - 1.0) * ka_ref[...])
    kk = kx * kk_ref[...]
    for h in range(NH):
        kh = kk[:, h * HD:(h + 1) * HD]
        nrm = jnp.sqrt(jnp.sum(kh * kh, axis=-1, keepdims=True))
        kk_o[:, h * HD:(h + 1) * HD] = kh / jnp.maximum(nrm, 1e-12)


def _rwkv_prep(proj, prev, mu, w0, ww2, a0, wa2, wg2, k_k, k_a, nb, t):
    n = nb * t
    seq = t > 1
    full = lambda *_: (0, 0)
    if seq:
        tm = min(256, t)
        nt = t // tm
        grid = (nb, nt)
        xmap = lambda b, i: (b * nt + i, C_RIN // R_IN)
        pspec = pl.BlockSpec((1, 1, R_IN), lambda b, i: (b, 0, 0))
        omap = lambda b, i: (b * nt + i, 0)
        sem = ("parallel", "arbitrary")
    else:
        tm = n
        grid = (1,)
        xmap = lambda i: (0, C_RIN // R_IN)
        pspec = pl.BlockSpec((tm, R_IN), lambda i: (0, 0))
        omap = lambda i: (0, 0)
        sem = ("arbitrary",)
    wspecs = [pl.BlockSpec((1, R_IN), full), pl.BlockSpec((1, GW), full), pl.BlockSpec((64, GW), full),
              pl.BlockSpec((1, GW), full), pl.BlockSpec((64, GW), full), pl.BlockSpec((128, GW), full),
              pl.BlockSpec((1, GW), full), pl.BlockSpec((1, GW), full)]
    return pl.pallas_call(
        functools.partial(_rwkv_prep_kernel, seq=seq),
        out_shape=tuple(jax.ShapeDtypeStruct((n, GW), F32) for _ in range(7)),
        grid=grid,
        in_specs=[pl.BlockSpec((tm, R_IN), xmap), pspec] + wspecs,
        out_specs=tuple(pl.BlockSpec((tm, GW), omap) for _ in range(7)),
        scratch_shapes=[pltpu.VMEM((1, R_IN), F32)],
        compiler_params=_cp(*sem),
    )(proj, prev, mu, w0, ww2, a0, wa2, wg2, k_k, k_a)


def _rwkv_scan_kernel(r_ref, lw_ref, k_ref, v_ref, kk_ref, a_ref, g_ref, rk_ref, lnw_ref, lnb_ref, y_ref, s_ref):
    C = CHUNK

    @pl.when(pl.program_id(1) == 0)
    def _():
        s_ref[...] = jnp.zeros(s_ref.shape, F32)

    ri, ci = _iota((C, C), 0), _iota((C, C), 1)
    lw = lw_ref[...]
    cs = _dot(jnp.where(ci <= ri, 1.0, 0.0), lw, HI)
    gam = jnp.exp(cs)
    ginv = jnp.exp(-cs)
    r = r_ref[...]
    k = k_ref[...]
    v = v_ref[...]
    kk = kk_ref[...]
    at = -kk * jnp.exp(cs - lw)
    bt = kk * a_ref[...] * ginv
    kt = k * ginv
    rt = r * gam
    glast = gam[C - 1:C, :]
    bonus_in = r * k * rk_ref[...]

    same, incl, strict = _bd_masks(C)
    row, col = _iota((BD, BD), 0), _iota((BD, BD), 1)
    eye = jnp.where(row == col, 1.0, 0.0)
    tile = lambda z: jnp.concatenate([z] * HG, axis=0)
    msk = lambda z: jnp.where(same, z, 0.0)
    fold = lambda z: sum(z[j * C:(j + 1) * C] for j in range(HG))
    for g in range(NH // HG):
        ln = slice(g * BD, (g + 1) * BD)
        btl, ktl = tile(bt[:, ln]), tile(kt[:, ln])
        vexp = msk(tile(v[:, ln]))
        ar = jnp.concatenate([msk(tile(at[:, ln])), msk(tile(rt[:, ln]))], axis=0)
        sbd = s_ref[0, g]
        gb = _dot_nt(ar, btl)
        gk = _dot_nt(ar, ktl)
        gs = _dot_nt(ar, sbd)
        n_ab = jnp.where(strict, gb[:BD], 0.0)
        a_ak = jnp.where(strict, gk[:BD], 0.0)
        x = eye + n_ab
        pm = _dot(n_ab, n_ab)
        for j in range(5):
            xn = x + _dot(x, pm)
            if j < 4:
                pm = _dot(pm, pm)
            x = xn
        u = _dot(x, gs[:BD] + _dot(a_ak, vexp))
        o = gs[BD:] + _dot(jnp.where(incl, gb[BD:], 0.0), u) + _dot(jnp.where(incl, gk[BD:], 0.0), vexp)
        gl = glast[:, ln]
        uv = jnp.concatenate([u, vexp], axis=0)
        bk = jnp.concatenate([msk(btl * gl), msk(ktl * gl)], axis=0)
        s_ref[0, g] = sbd * gl + _dot_tn(uv, bk)
        mean = jnp.sum(o, axis=-1, keepdims=True) * (1.0 / HD)
        dev = msk(o - mean)
        var = jnp.sum(dev * dev, axis=-1, keepdims=True) * (1.0 / HD)
        on = fold(dev * lax.rsqrt(var + R_LN_EPS)) * lnw_ref[:, ln] + lnb_ref[:, ln]
        bonus = fold(jnp.sum(msk(tile(bonus_in[:, ln])), axis=-1, keepdims=True) * vexp)
        y_ref[:, ln] = (on + bonus) * g_ref[:, ln]


def _rwkv_scan(rs, rk, lnw, lnb, nb, t):
    nc = t // CHUNK
    n = nb * t
    ng = NH // HG
    rowmap = lambda b, c: (b * nc + c, 0)
    full = lambda b, c: (0, 0)
    y, sbd = pl.pallas_call(
        _rwkv_scan_kernel,
        out_shape=(jax.ShapeDtypeStruct((n, GW), F32), jax.ShapeDtypeStruct((nb, ng, BD, BD), F32)),
        grid=(nb, nc),
        in_specs=[pl.BlockSpec((CHUNK, GW), rowmap)] * 7 + [pl.BlockSpec((1, GW), full)] * 3,
        out_specs=(pl.BlockSpec((CHUNK, GW), rowmap), pl.BlockSpec((1, ng, BD, BD), lambda b, c: (b, 0, 0, 0))),
        compiler_params=_cp("parallel", "arbitrary"),
    )(*rs, rk, lnw, lnb)
    return y, _bd_diag(sbd)


def _rwkv_step_kernel(r_ref, lw_ref, k_ref, v_ref, kk_ref, a_ref, g_ref, rk_ref, lnw_ref, lnb_ref, s_ref,
                      y_ref, so_ref):
    eye = _iota((HD, HD), 0) == _iota((HD, HD), 1)
    r, lw, k, v, kk, a, g = (ref[0] for ref in (r_ref, lw_ref, k_ref, v_ref, kk_ref, a_ref, g_ref))
    bonus_in = r * k * rk_ref[...]
    for h in range(NH):
        sl = slice(h * HD, (h + 1) * HD)
        s0 = s_ref[0, h]
        kkr = kk[:, sl]
        sk = jnp.sum(s0 * kkr, axis=1, keepdims=True)
        vcol = _col_of_row(v[:, sl], eye)
        s1 = s0 * jnp.exp(lw[:, sl]) - sk * (kkr * a[:, sl]) + vcol * k[:, sl]
        so_ref[0, h] = s1
        ocol = jnp.sum(s1 * r[:, sl], axis=1, keepdims=True)
        mean = jnp.mean(ocol, axis=0, keepdims=True)
        var = jnp.mean(jnp.square(ocol - mean), axis=0, keepdims=True)
        on = _row_of_col((ocol - mean) * lax.rsqrt(var + R_LN_EPS), eye) * lnw_ref[:, sl] + lnb_ref[:, sl]
        bonus = jnp.sum(bonus_in[:, sl], axis=-1, keepdims=True) * v[:, sl]
        y_ref[0, :, sl] = (on + bonus) * g[:, sl]


def _rwkv_step(rs, rk, lnw, lnb, s_all, l):
    bd = s_all.shape[1]
    per_b = lambda b: (b, 0, 0)
    full = lambda b: (0, 0)
    return pl.pallas_call(
        _rwkv_step_kernel,
        out_shape=(jax.ShapeDtypeStruct((bd, 1, GW), F32), jax.ShapeDtypeStruct((bd, NH, HD, HD), F32)),
        grid=(bd,),
        in_specs=[pl.BlockSpec((1, 1, GW), per_b)] * 7 + [pl.BlockSpec((1, GW), full)] * 3
                 + [pl.BlockSpec((1, NH, HD, HD), lambda b: (l * bd + b, 0, 0, 0))],
        out_specs=(pl.BlockSpec((1, 1, GW), per_b), pl.BlockSpec((1, NH, HD, HD), lambda b: (b, 0, 0, 0))),
        compiler_params=_cp("parallel"),
    )(*(z.reshape(bd, 1, GW) for z in rs), rk, lnw, lnb, s_all.reshape((-1,) + s_all.shape[2:]))


def _gelu_tanh(x):
    return 0.5 * x * (1.0 + jnp.tanh(math.sqrt(2.0 / math.pi) * (x + 0.044715 * (x * x * x))))


def _s5_kernel(u_ref, bre_ref, bim_ref, lre_ref, lim_ref, cre_ref, cim_ref, d_ref, wg_ref, bg_ref, h0r_ref, h0i_ref,
               y_ref, hr_ref, hi_ref, hre_scr, him_scr, *, nb, tb):
    @pl.when(pl.program_id(0) == 0)
    def _():
        hr_ref[...] = h0r_ref[...]
        hi_ref[...] = h0i_ref[...]

    u = u_ref[...]
    hre_scr[...] = _dot(u, bre_ref[...])
    him_scr[...] = _dot(u, bim_ref[...])
    lr = lre_ref[...]
    li = lim_ref[...]

    def body(t, carry):
        hr, hi = carry
        rows = pl.ds(pl.multiple_of(t * nb, nb), nb)
        nr = lr * hr - li * hi + hre_scr[rows, :]
        ni = lr * hi + li * hr + him_scr[rows, :]
        hre_scr[rows, :] = nr
        him_scr[rows, :] = ni
        return nr, ni

    hr, hi = lax.fori_loop(0, tb, body, (hr_ref[...], hi_ref[...]))
    hr_ref[...] = hr
    hi_ref[...] = hi
    y = _dot(hre_scr[...], cre_ref[...]) - _dot(him_scr[...], cim_ref[...]) + d_ref[...] * u
    y = _gelu_tanh(y)
    y_ref[...] = y * _sigmoid(_dot(y, wg_ref[...]) + bg_ref[...])


def _s5(u_tm, mats, h0r, h0i, nb, t):
    bre, bim, lre, lim, cre, cim, d, wg, bg = mats
    tb = min(64, t)
    full = lambda i: (0, 0)
    return pl.pallas_call(
        functools.partial(_s5_kernel, nb=nb, tb=tb),
        out_shape=(jax.ShapeDtypeStruct((t * nb, GW), F32), jax.ShapeDtypeStruct((nb, S5_W), F32),
                   jax.ShapeDtypeStruct((nb, S5_W), F32)),
        grid=(t // tb,),
        in_specs=[pl.BlockSpec((tb * nb, GW), lambda i: (i, 0)),
                  pl.BlockSpec((GW, S5_W), full), pl.BlockSpec((GW, S5_W), full),
                  pl.BlockSpec((1, S5_W), full), pl.BlockSpec((1, S5_W), full),
                  pl.BlockSpec((S5_W, GW), full), pl.BlockSpec((S5_W, GW), full),
                  pl.BlockSpec((1, GW), full), pl.BlockSpec((GW, GW), full), pl.BlockSpec((1, GW), full),
                  pl.BlockSpec((nb, S5_W), full), pl.BlockSpec((nb, S5_W), full)],
        out_specs=(pl.BlockSpec((tb * nb, GW), lambda i: (i, 0)),
                   pl.BlockSpec((nb, S5_W), full), pl.BlockSpec((nb, S5_W), full)),
        scratch_shapes=[pltpu.VMEM((tb * nb, S5_W), F32), pltpu.VMEM((tb * nb, S5_W), F32)],
        compiler_params=_cp("arbitrary"),
    )(u_tm, bre, bim, lre, lim, cre, cim, d, wg, bg, h0r, h0i)


def _s5_mats(a_re, a_im, b_re, b_im, c_re, c_im, d_skip, log_dt, w_glu, b_glu):
    dt = jnp.exp(log_dt)
    mag = jnp.exp(a_re * dt)
    lb_re, lb_im = mag * jnp.cos(a_im * dt), mag * jnp.sin(a_im * dt)
    den = a_re * a_re + a_im * a_im
    f_re = ((lb_re - 1.0) * a_re + lb_im * a_im) / den
    f_im = (lb_im * a_re - (lb_re - 1.0) * a_im) / den
    bb_re = f_re[..., None] * b_re - f_im[..., None] * b_im
    bb_im = f_re[..., None] * b_im + f_im[..., None] * b_re
    eye = jnp.eye(S5_G, dtype=F32)
    bd = lambda bb: jnp.einsum('gpc,gh->gchp', bb, eye).reshape(GW, S5_W)
    cd = lambda cc: jnp.einsum('gcp,gh->gphc', cc, eye).reshape(S5_W, GW)
    return (bd(bb_re), bd(bb_im), lb_re.reshape(1, S5_W), lb_im.reshape(1, S5_W), cd(c_re), cd(c_im),
            d_skip.reshape(1, GW), w_glu, b_glu.reshape(1, GW))


def _ffn_up_kernel(h_ref, wa_ref, wb_ref, cw_ref, cb_ref, s0_ref, s1_ref, y_ref, a_ref, carry_scr, *, seq):
    h = h_ref[...]
    a = _dot(h, wa_ref[...])
    b = _dot(h, wb_ref[...])
    tm = a.shape[0]
    if seq:
        @pl.when(pl.program_id(2) == 0)
        def _():
            carry_scr[...] = s0_ref[0]

        rowid = _iota((tm, 1), 0)
        c0 = carry_scr[0:1, :]
        c1 = carry_scr[1:2, :]
        a1 = jnp.where(rowid == 0, c1, pltpu.roll(a, 1, 0))
        a2 = jnp.where(rowid == 0, c0, jnp.where(rowid == 1, c1, pltpu.roll(a, 2, 0)))
        carry_scr[...] = a[tm - 2:tm, :]
        a_ref[0] = a[tm - 2:tm, :]
    else:
        a2 = s0_ref[...]
        a1 = s1_ref[...]
        a_ref[...] = a
    c = cb_ref[...] + a2 * cw_ref[0:1, :] + a1 * cw_ref[1:2, :] + a * cw_ref[2:3, :]
    y_ref[...] = (c * _sigmoid(c) * b).astype(BF16)


def _ffn_up(h2, w_up, l, cw, cb, st, nb, t):
    n, d = h2.shape
    dff = w_up.shape[2] // 2
    tn = 512
    nj = dff // tn
    if t > 1:
        tm = min(MM_ROWS, t)
        nt = t // tm
        y, fc = pl.pallas_call(
            functools.partial(_ffn_up_kernel, seq=True),
            out_shape=(jax.ShapeDtypeStruct((n, dff), BF16), jax.ShapeDtypeStruct((nb, 2, dff), F32)),
            grid=(nb, nj, nt),
            in_specs=[pl.BlockSpec((tm, d), lambda b, j, i: (b * nt + i, 0)),
                      pl.BlockSpec((None, d, tn), lambda b, j, i: (l, 0, j)),
                      pl.BlockSpec((None, d, tn), lambda b, j, i: (l, 0, nj + j)),
                      pl.BlockSpec((3, tn), lambda b, j, i: (0, j)),
                      pl.BlockSpec((1, tn), lambda b, j, i: (0, j)),
                      pl.BlockSpec((1, 2, tn), lambda b, j, i: (b, 0, j)),
                      pl.BlockSpec((1, 2, tn), lambda b, j, i: (b, 0, j))],
            out_specs=(pl.BlockSpec((tm, tn), lambda b, j, i: (b * nt + i, j)),
                       pl.BlockSpec((1, 2, tn), lambda b, j, i: (b, 0, j))),
            scratch_shapes=[pltpu.VMEM((2, tn), F32)],
            compiler_params=_cp("parallel", "parallel", "arbitrary"),
        )(h2, w_up, w_up, cw, cb, st, st)
        return y, fc
    s0, s1 = st[:, 0, :], st[:, 1, :]
    y, a = pl.pallas_call(
        functools.partial(_ffn_up_kernel, seq=False),
        out_shape=(jax.ShapeDtypeStruct((n, dff), BF16), jax.ShapeDtypeStruct((n, dff), F32)),
        grid=(nj,),
        in_specs=[pl.BlockSpec((n, d), lambda j: (0, 0)),
                  pl.BlockSpec((None, d, tn), lambda j: (l, 0, j)),
                  pl.BlockSpec((None, d, tn), lambda j: (l, 0, nj + j)),
                  pl.BlockSpec((3, tn), lambda j: (0, j)),
                  pl.BlockSpec((1, tn), lambda j: (0, j)),
                  pl.BlockSpec((n, tn), lambda j: (0, j)),
                  pl.BlockSpec((n, tn), lambda j: (0, j))],
        out_specs=(pl.BlockSpec((n, tn), lambda j: (0, j)), pl.BlockSpec((n, tn), lambda j: (0, j))),
        scratch_shapes=[pltpu.VMEM((2, tn), F32)],
        compiler_params=_cp("parallel"),
    )(h2, w_up, w_up, cw, cb, s0, s1)
    return y, jnp.stack([s1, a], axis=1)


def _permute_w_in(w):
    cols = lambda s, n: w[:, :, s:s + n]
    parts = [cols(0, 4 * GW), cols(2064, GW), cols(2832, GW), cols(5208, GW), cols(3416, R_IN),
             cols(2576, 128), cols(2704, 128), cols(3344, HD), cols(2048, NH), cols(2056, NH), cols(3408, NH),
             jnp.zeros(w.shape[:2] + (NP - C_SM - HD - 3 * NH,), w.dtype)]
    return jnp.concatenate(parts, axis=2).astype(BF16)


def _layer(x2, nb, t, l, W, st, tables, cache):
    n = nb * t
    c0, n0, m0, rs0, rsh0, sre0, sim0, conv0 = st
    proj = _in_proj(x2, W['norm_mix'], W['w_in'], l)
    aq_r, iq_r, k_r, v_r, ik_r = _rope_call(proj, tables[0], tables[1], *tables[2])

    if cache is None:
        ym, c1, n1, m1 = _mlstm_prompt(proj, W['m_b_i'], W['m_b_f'], W['m_norm'], nb, t)
        ya = _dsa_prompt(iq_r, proj, ik_r, aq_r, k_r, v_r, nb, t)
    else:
        ym, c1, n1, m1 = _mlstm_step(proj, W['m_b_i'], W['m_b_f'], W['m_norm'], c0, l, n0, m0)
        ym = ym.reshape(n, GW)
        cki, ck, cv, page_table, n_pool = cache
        ya = _dsa_sample(l, page_table, iq_r, proj[:, C_SM + SM_IW:C_SM + SM_IW + NH], ik_r, aq_r, k_r, v_r,
                         cki, ck, cv, n_pool)

    prev = rsh0.reshape(nb, 1, R_IN) if t > 1 else rsh0
    rs = _rwkv_prep(proj, prev, W['r_mu'], W['r_w0'], W['r_w_w2'], W['r_a0'], W['r_w_a2'], W['r_w_g2'],
                    W['r_k_k'], W['r_k_a'], nb, t)
    if t > 1:
        yr, rs1 = _rwkv_scan(rs, W['r_r_k'], W['r_ln_w'], W['r_ln_b'], nb, t)
    else:
        yr, rs1 = _rwkv_step(rs, W['r_r_k'], W['r_ln_w'], W['r_ln_b'], rs0, l)
        yr = yr.reshape(n, GW)
    rsh1 = proj.reshape(nb, t, NP)[:, t - 1, C_RIN:C_RIN + R_IN]

    su = proj[:, C_SU:C_SU + GW]
    u_tm = su.reshape(nb, t, GW).transpose(1, 0, 2).reshape(t * nb, GW)
    ys_tm, sre1, sim1 = _s5(u_tm, W['s5'], sre0.reshape(nb, S5_W), sim0.reshape(nb, S5_W), nb, t)
    ys = ys_tm.reshape(t, nb, GW).transpose(1, 0, 2).reshape(n, GW)

    x2 = _res_matmul(x2, [ym, ya, yr, ys], W['w_out'], l)
    h2 = _rmsnorm(x2, W['norm_ffn'], BF16)
    y, conv1 = _ffn_up(h2, W['ffn_w_up'], l, W['ffn_conv_w'], W['ffn_conv_b'], conv0, nb, t)
    x2 = _res_matmul(x2, [y], W['ffn_w_down'], l)
    outs = (k_r.reshape(nb, t, A_KV, HD), v_r.reshape(nb, t, A_KV, HD), ik_r.reshape(nb, t, HD),
            c1, n1, m1.reshape(nb, NH), rs1, rsh1, sre1.reshape(nb, S5_G, S5_P), sim1.reshape(nb, S5_G, S5_P), conv1)
    return x2, outs


def kernel(x_prompt, x_sample, cache_k, cache_v, cache_kidx, page_table, state_mlstm_c, state_mlstm_n,
           state_mlstm_m, state_rwkv_s, state_rwkv_shift, state_s5_re, state_s5_im, state_ffn_conv,
           norm_mix, w_in, w_out, m_b_i, m_b_f, m_norm, r_mu, r_w0, r_w_w2, r_a0, r_w_a2, r_w_g2,
           r_k_k, r_k_a, r_r_k, r_ln_w, r_ln_b, s5_a_re, s5_a_im, s5_b_re, s5_b_im, s5_c_re, s5_c_im,
           s5_d, s5_log_dt, s5_w_glu, s5_b_glu, norm_ffn, ffn_w_up, ffn_conv_w, ffn_conv_b, ffn_w_down,
           norm_final):
    bp, tp, d = x_prompt.shape
    bs, ts, _ = x_sample.shape
    assert ts == 1 and tp % CHUNK == 0
    depth = w_in.shape[0]
    n_pool = cache_k.shape[1]
    past = page_table.shape[1] * PAGE
    dff = ffn_conv_b.shape[-1]

    row = lambda z: z.reshape(1, -1)
    layers = []
    w_in_all, w_out_all = _permute_w_in(w_in), w_out.astype(BF16)
    w_up_all, w_down_all = ffn_w_up.astype(BF16), ffn_w_down.astype(BF16)
    for l in range(depth):
        layers.append(dict(
            norm_mix=row(norm_mix[l]), w_in=w_in_all, w_out=w_out_all,
            m_b_i=row(m_b_i[l]), m_b_f=row(m_b_f[l]), m_norm=row(m_norm[l]),
            r_mu=row(r_mu[l]), r_w0=row(r_w0[l]), r_w_w2=r_w_w2[l], r_a0=row(r_a0[l]), r_w_a2=r_w_a2[l],
            r_w_g2=r_w_g2[l], r_k_k=row(r_k_k[l]), r_k_a=row(r_k_a[l]), r_r_k=row(r_r_k[l]),
            r_ln_w=row(r_ln_w[l]), r_ln_b=row(r_ln_b[l]),
            s5=_s5_mats(s5_a_re[l], s5_a_im[l], s5_b_re[l], s5_b_im[l], s5_c_re[l], s5_c_im[l], s5_d[l],
                        s5_log_dt[l], s5_w_glu[l], s5_b_glu[l]),
            norm_ffn=row(norm_ffn[l]), ffn_w_up=w_up_all, ffn_conv_w=ffn_conv_w[l],
            ffn_conv_b=row(ffn_conv_b[l]), ffn_w_down=w_down_all))

    cos_p, sin_p = _rope_tables(jnp.arange(tp))
    cos_s, sin_s = _rope_tables(jnp.full((bs,), past))
    tab_p = (cos_p, sin_p, (bp, tp))
    tab_s = (cos_s, sin_s, (1, bs))

    zeros = lambda *s: jnp.zeros(s, F32)
    st_p = (zeros(bp, NH, HD, HD), zeros(bp, NH, HD), zeros(bp, NH), zeros(bp, NH, HD, HD), zeros(bp, R_IN),
            zeros(bp, S5_G, S5_P), zeros(bp, S5_G, S5_P), zeros(bp, 2, dff))
    cki = cache_kidx.reshape(depth * n_pool, PAGE, HD).transpose(0, 2, 1)
    ck = cache_k.reshape(depth * n_pool, PAGE, A_KV * HD).transpose(0, 2, 1)
    cv = cache_v.reshape(depth * n_pool, PAGE, A_KV * HD).transpose(0, 2, 1)

    xp = x_prompt.reshape(bp * tp, d)
    xs = x_sample.reshape(bs, d)
    new_p, new_s = [], []
    for l in range(depth):
        xp, sp = _layer(xp, bp, tp, l, layers[l], st_p, tab_p, None)
        st_s = (state_mlstm_c, state_mlstm_n[l], state_mlstm_m[l], state_rwkv_s, state_rwkv_shift[l],
                state_s5_re[l], state_s5_im[l], state_ffn_conv[l])
        xs, ss = _layer(xs, bs, 1, l, layers[l], st_s, tab_s, (cki, ck, cv, page_table, n_pool))
        new_p.append(sp)
        new_s.append(ss)
    (k_p, v_p, ki_p, mc_p, mn_p, mm_p, rs_p, rsh_p, sre_p, sim_p, fc_p) = [jnp.stack(z) for z in zip(*new_p)]
    (k_s, v_s, ki_s, mc_s, mn_s, mm_s, rs_s, rsh_s, sre_s, sim_s, fc_s) = [jnp.stack(z) for z in zip(*new_s)]
    y_prompt = _rmsnorm(xp, row(norm_final), F32).reshape(bp, tp, d)
    y_sample = _rmsnorm(xs, row(norm_final), F32).reshape(bs, ts, d)
    return (y_prompt, y_sample, k_p, k_s, v_p, v_s, ki_p, ki_s, mc_p, mc_s, mn_p, mn_s, mm_p, mm_s,
            rs_p, rs_s, rsh_p, rsh_s, sre_p, sre_s, sim_p, sim_s, fc_p, fc_s)
```

```python
import functools
import math

import jax
import jax.numpy as jnp
from jax import lax
from jax.experimental import pallas as pl
from jax.experimental.pallas import tpu as pltpu

F32 = jnp.float32
BF16 = jnp.bfloat16
HI = lax.Precision.HIGHEST

HD = 64
NH = 8
GW = NH * HD
A_KV = 2
PAGE = 128
TOPK_MAX = 256
ROPE_THETA = 10000.0
R_IN = 3 * GW + 64 + 64 + 128
R_LN_EPS = 64e-5
S5_G, S5_CH, S5_P = 32, 16, 64
S5_W = S5_G * S5_P
NORM_EPS = 1e-6
CHUNK = 64
IDX_SCALE = HD ** -0.5 * NH ** -0.5

C_M, C_AQ, C_IQ, C_SU, C_RIN, C_AK, C_AV, C_SM = 0, 2048, 2560, 3072, 3584, 5376, 5504, 5632
NP = 5760
SM_IK, SM_MI, SM_MF, SM_IW = 0, 64, 72, 80

VMEM_LIMIT = 56 * 1024 * 1024
MM_ROWS = 1024


def _cp(*sem):
    return pltpu.CompilerParams(dimension_semantics=sem, vmem_limit_bytes=VMEM_LIMIT)


def _dot(a, b, prec=None):
    return jnp.dot(a, b, preferred_element_type=F32, precision=prec)


def _dot_nt(a, b, prec=None):
    return lax.dot_general(a, b, (((1,), (1,)), ((), ())), preferred_element_type=F32, precision=prec)


def _dot_tn(a, b, prec=None):
    return lax.dot_general(a, b, (((0,), (0,)), ((), ())), preferred_element_type=F32, precision=prec)


def _sigmoid(x):
    return 1.0 / (1.0 + jnp.exp(-x))


def _softplus(x):
    return jnp.maximum(x, 0.0) + jnp.log(1.0 + jnp.exp(-jnp.abs(x)))


def _iota(shape, dim):
    return lax.broadcasted_iota(jnp.int32, shape, dim)


HG = 4
BD = HG * HD
assert CHUNK == HD


def _bd_masks(c):
    row, col = _iota((HG * c, HG * c), 0), _iota((HG * c, HG * c), 1)
    same = (row // c) == (col // c)
    t, s = row % c, col % c
    return same, jnp.logical_and(same, s <= t), jnp.logical_and(same, s < t)


def _bd_diag(x):
    nb, ng = x.shape[:2]
    x6 = x.reshape(nb, ng, HG, HD, HG, HD)
    return jnp.stack([x6[:, :, j, :, j, :] for j in range(HG)], axis=2).reshape(nb, ng * HG, HD, HD)


def _inproj_kernel(x_ref, g_ref, w_ref, o_ref, h_scr):
    @pl.when(pl.program_id(1) == 0)
    def _():
        x = x_ref[...]
        ms = jnp.mean(x * x, axis=-1, keepdims=True)
        h_scr[...] = (x * lax.rsqrt(ms + NORM_EPS) * g_ref[...]).astype(BF16)

    o_ref[...] = _dot(h_scr[...], w_ref[...])


def _in_proj(x2, g, w, l):
    n, d = x2.shape
    npad = w.shape[2]
    tm = min(MM_ROWS, n)
    tn = 640
    return pl.pallas_call(
        _inproj_kernel,
        out_shape=jax.ShapeDtypeStruct((n, npad), F32),
        grid=(n // tm, npad // tn),
        in_specs=[pl.BlockSpec((tm, d), lambda i, j: (i, 0)),
                  pl.BlockSpec((1, d), lambda i, j: (0, 0)),
                  pl.BlockSpec((None, d, tn), lambda i, j: (l, 0, j))],
        out_specs=pl.BlockSpec((tm, tn), lambda i, j: (i, j)),
        scratch_shapes=[pltpu.VMEM((tm, d), BF16)],
        compiler_params=_cp("parallel", "arbitrary"),
    )(x2, g, w)


def _rmsnorm_kernel(x_ref, g_ref, o_ref):
    x = x_ref[...]
    ms = jnp.mean(x * x, axis=-1, keepdims=True)
    o_ref[...] = (x * lax.rsqrt(ms + NORM_EPS) * g_ref[...]).astype(o_ref.dtype)


def _rmsnorm(x2, g, dtype):
    n, d = x2.shape
    tm = min(512, n)
    return pl.pallas_call(
        _rmsnorm_kernel,
        out_shape=jax.ShapeDtypeStruct((n, d), dtype),
        grid=(n // tm,),
        in_specs=[pl.BlockSpec((tm, d), lambda i: (i, 0)), pl.BlockSpec((1, d), lambda i: (0, 0))],
        out_specs=pl.BlockSpec((tm, d), lambda i: (i, 0)),
        compiler_params=_cp("parallel"),
    )(x2, g)


def _resmm_kernel(r_ref, *refs):
    y_refs, w_ref, o_ref = refs[:-2], refs[-2], refs[-1]
    acc = r_ref[...]
    k0 = 0
    for y_ref in y_refs:
        kw = y_ref.shape[1]
        acc = acc + _dot(y_ref[...].astype(BF16), w_ref[k0:k0 + kw, :])
        k0 += kw
    o_ref[...] = acc


def _res_matmul(res, ys, w, l):
    n = res.shape[0]
    k, d = w.shape[1:]
    assert sum(y.shape[1] for y in ys) == k
    tm = min(MM_ROWS, n)
    tn = 512
    return pl.pallas_call(
        _resmm_kernel,
        out_shape=jax.ShapeDtypeStruct((n, d), F32),
        grid=(n // tm, d // tn),
        in_specs=[pl.BlockSpec((tm, tn), lambda i, j: (i, j))]
                 + [pl.BlockSpec((tm, y.shape[1]), lambda i, j: (i, 0)) for y in ys]
                 + [pl.BlockSpec((None, k, tn), lambda i, j: (l, 0, j))],
        out_specs=pl.BlockSpec((tm, tn), lambda i, j: (i, j)),
        compiler_params=_cp("parallel", "arbitrary"),
    )(res, *ys, w)


def _rope(x, cos, sin):
    w = x.shape[1]
    first = (_iota(x.shape, 1) & (HD - 1)) < HD // 2
    sw = jnp.where(first, pltpu.roll(x, w - HD // 2, 1), pltpu.roll(x, HD // 2, 1))
    return x * cos + sw * sin


def _rope_kernel(aq_ref, iq_ref, ak_ref, av_ref, sm_ref, cos_ref, sin_ref, aqo, iqo, ko, vo, iko):
    cos = cos_ref[...]
    sin = sin_ref[...]
    aqo[...] = _rope(aq_ref[...], cos, sin)
    iqo[...] = _rope(iq_ref[...], cos, sin)
    ko[...] = _rope(ak_ref[...], cos[:, :128], sin[:, :128])
    vo[...] = av_ref[...]
    iko[...] = _rope(sm_ref[...], cos[:, :128], sin[:, :128])[:, :HD]


def _rope_call(proj, cos, sin, nb, nt_rows):
    n = proj.shape[0]
    tm = min(512, nt_rows)
    nt = nt_rows // tm
    row = lambda b, i: b * nt + i
    return pl.pallas_call(
        _rope_kernel,
        out_shape=(jax.ShapeDtypeStruct((n, GW), F32), jax.ShapeDtypeStruct((n, GW), F32),
                   jax.ShapeDtypeStruct((n, 128), F32), jax.ShapeDtypeStruct((n, 128), F32),
                   jax.ShapeDtypeStruct((n, HD), F32)),
        grid=(nb, nt),
        in_specs=[pl.BlockSpec((tm, GW), lambda b, i: (row(b, i), C_AQ // GW)),
                  pl.BlockSpec((tm, GW), lambda b, i: (row(b, i), C_IQ // GW)),
                  pl.BlockSpec((tm, 128), lambda b, i: (row(b, i), C_AK // 128)),
                  pl.BlockSpec((tm, 128), lambda b, i: (row(b, i), C_AV // 128)),
                  pl.BlockSpec((tm, 128), lambda b, i: (row(b, i), C_SM // 128)),
                  pl.BlockSpec((tm, GW), lambda b, i: (i, 0)),
                  pl.BlockSpec((tm, GW), lambda b, i: (i, 0))],
        out_specs=(pl.BlockSpec((tm, GW), lambda b, i: (row(b, i), 0)),
                   pl.BlockSpec((tm, GW), lambda b, i: (row(b, i), 0)),
                   pl.BlockSpec((tm, 128), lambda b, i: (row(b, i), 0)),
                   pl.BlockSpec((tm, 128), lambda b, i: (row(b, i), 0)),
                   pl.BlockSpec((tm, HD), lambda b, i: (row(b, i), 0))),
        compiler_params=_cp("parallel", "parallel"),
    )(proj, proj, proj, proj, proj, cos, sin)


def _rope_tables(pos):
    half = HD // 2
    inv = ROPE_THETA ** (-jnp.arange(half, dtype=F32) / half)
    ang = pos.astype(F32)[:, None] * inv[None, :]
    cos, sin = jnp.cos(ang), jnp.sin(ang)
    cos64 = jnp.concatenate([cos, cos], axis=-1)
    sin64 = jnp.concatenate([-sin, sin], axis=-1)
    return jnp.tile(cos64, (1, NH)), jnp.tile(sin64, (1, NH))


def _kth_largest(sc, extra, kk):
    kf = jnp.float32(kk)

    def count_ge(c):
        n = jnp.sum(jnp.where(sc >= c, 1.0, 0.0), axis=-1, keepdims=True)
        if extra is not None:
            n = n + jnp.where(extra >= c, 1.0, 0.0)
        return n

    def key_to_f(key):
        bits = key ^ ((key >> 31) & jnp.int32(0x7FFFFFFF))
        return lax.bitcast_convert_type(bits, F32)

    r = sc.shape[0]
    int_min = jnp.int32(-2 ** 31)
    lo = jnp.where(count_ge(jnp.zeros((r, 1), F32)) >= kf, jnp.int32(0), int_min)

    def body(j, lo):
        cand = lo + jnp.left_shift(jnp.int32(1), jnp.int32(30) - j)
        ok = count_ge(key_to_f(cand)) >= kf
        return jnp.where(ok, cand, lo)

    lo = lax.fori_loop(0, 31, body, lo)
    key_neg_inf = jnp.int32(-2 ** 31 + 0x7FFFFF)
    return jnp.where(lo <= key_neg_inf, -jnp.inf, key_to_f(lo))


def _strict_upper_bf16(n):
    return jnp.where(_iota((n, n), 0) < _iota((n, n), 1), 1.0, 0.0).astype(BF16)


def _split_bf16(x):
    hi = x.astype(BF16).astype(F32)
    return hi, x - hi


def _dsa_prompt_kernel(iq_ref, sm_ref, ik_ref, aq_ref, k_ref, v_ref, o_ref, sel_scr, kcat_scr, *, topk, qb, n_ext):
    t_keys = ik_ref.shape[0]
    i = pl.program_id(1)
    per_ext = (t_keys // qb) // n_ext

    @pl.when(i == 0)
    def _():
        hi, lo = _split_bf16(ik_ref[...])
        kcat_scr[...] = jnp.concatenate([hi, lo, hi], axis=1)

    iq = iq_ref[...]
    aq = aq_ref[...] * HD ** -0.5
    wts = sm_ref[:, SM_IW:SM_IW + NH] * IDX_SCALE
    tq = i * qb + _iota((qb, 1), 0)

    def body(ext):
        kcat = kcat_scr[0:ext, :]
        sc = jnp.zeros((qb, ext), F32)
        for h in range(NH):
            hi, lo = _split_bf16(iq[:, h * HD:(h + 1) * HD])
            qk = _dot_nt(jnp.concatenate([hi, hi, lo], axis=1), kcat)
            sc = sc + jnp.maximum(qk, 0.0) * wts[:, h:h + 1]
        causal = _iota((1, ext), 1) <= tq
        sc = jnp.where(causal, sc, -jnp.inf)

        thr = _kth_largest(sc, None, topk)
        gt = sc > thr
        eq = sc == thr
        n_gt = jnp.sum(jnp.where(gt, 1.0, 0.0), axis=-1, keepdims=True)
        n_eq = jnp.sum(jnp.where(eq, 1.0, 0.0), axis=-1, keepdims=True)
        need = jnp.float32(topk) - n_gt
        sel_scr[:, 0:ext] = jnp.where(jnp.logical_and(sc >= thr, causal), 1.0, 0.0)
        tie = jnp.logical_and(n_eq > need, thr > -jnp.inf)

        @pl.when(jnp.max(jnp.where(tie, 1.0, 0.0)) > 0.5)
        def _():
            ut = _strict_upper_bf16(128)
            run = jnp.zeros((qb, 1), F32)
            for c in range(ext // 128):
                sl = slice(c * 128, (c + 1) * 128)
                eqc = jnp.where(eq[:, sl], 1.0, 0.0)
                pref = _dot(eqc.astype(BF16), ut) + run
                keep = jnp.logical_or(gt[:, sl], jnp.logical_and(eq[:, sl], pref < need))
                sel_scr[:, sl] = jnp.where(jnp.logical_and(keep, causal[:, sl]), 1.0, 0.0)
                run = run + jnp.sum(eqc, axis=-1, keepdims=True)

        sel = sel_scr[:, 0:ext] > 0.5
        for g in range(A_KV):
            kg = k_ref[0:ext, g * HD:(g + 1) * HD]
            vg = v_ref[0:ext, g * HD:(g + 1) * HD]
            for j in range(NH // A_KV):
                h = g * (NH // A_KV) + j
                s = _dot_nt(aq[:, h * HD:(h + 1) * HD], kg)
                s = jnp.where(sel, s, -jnp.inf)
                m = jnp.max(s, axis=-1, keepdims=True)
                p = jnp.exp(s - m)
                l = jnp.sum(p, axis=-1, keepdims=True)
                o_ref[:, h * HD:(h + 1) * HD] = _dot(p, vg) / l

    for j in range(n_ext):
        pl.when(i // per_ext == j)(functools.partial(body, (j + 1) * (t_keys // n_ext)))


def _dsa_prompt(iq_r, proj, ik_r, aq_r, k_r, v_r, nb, t):
    qb = min(128, t)
    nq = t // qb
    n_ext = min(4, nq)
    assert nq % n_ext == 0
    topk = min(TOPK_MAX, t // 4)
    n = nb * t
    row = lambda b, i: b * nq + i
    return pl.pallas_call(
        functools.partial(_dsa_prompt_kernel, topk=topk, qb=qb, n_ext=n_ext),
        out_shape=jax.ShapeDtypeStruct((n, GW), F32),
        grid=(nb, nq),
        in_specs=[pl.BlockSpec((qb, GW), lambda b, i: (row(b, i), 0)),
                  pl.BlockSpec((qb, 128), lambda b, i: (row(b, i), C_SM // 128)),
                  pl.BlockSpec((t, HD), lambda b, i: (b, 0)),
                  pl.BlockSpec((qb, GW), lambda b, i: (row(b, i), 0)),
                  pl.BlockSpec((t, 128), lambda b, i: (b, 0)),
                  pl.BlockSpec((t, 128), lambda b, i: (b, 0))],
        out_specs=pl.BlockSpec((qb, GW), lambda b, i: (row(b, i), 0)),
        scratch_shapes=[pltpu.VMEM((qb, t), F32), pltpu.VMEM((t, 3 * HD), F32)],
        compiler_params=_cp("parallel", "arbitrary"),
    )(iq_r, proj, ik_r, aq_r, k_r, v_r)


def _dsa_sample_score_kernel(*refs, n_pages):
    _, iq_ref, w_ref, ikn_ref = refs[:4]
    pages = refs[4:4 + n_pages]
    sc_ref, sn_ref = refs[4 + n_pages:]
    iq = iq_ref[0]
    w = w_ref[0] * IDX_SCALE
    for c in range(n_pages):
        qk = _dot(iq, pages[c][0], HI)
        sc_ref[0, :, c * PAGE:(c + 1) * PAGE] = jnp.sum(jnp.maximum(qk, 0.0) * w, axis=0, keepdims=True)
    qkn = jnp.sum(iq * ikn_ref[0], axis=-1, keepdims=True)
    sn = jnp.sum(jnp.maximum(qkn, 0.0) * w, axis=0, keepdims=True)
    sn_ref[0] = jnp.broadcast_to(sn, (1, 128))


def _dsa_sample_select_kernel(sc_ref, sn_ref, sel_ref, seln_ref, *, topk):
    sc = sc_ref[...]
    sn = sn_ref[:, 0:1]
    bd, s_keys = sc.shape
    thr = _kth_largest(sc, sn, topk)
    gt = sc > thr
    eq = sc == thr
    n_gt = jnp.sum(jnp.where(gt, 1.0, 0.0), axis=-1, keepdims=True) + jnp.where(sn > thr, 1.0, 0.0)
    need = jnp.float32(topk) - n_gt
    ut = _strict_upper_bf16(PAGE)
    run = jnp.zeros((bd, 1), F32)
    for c in range(s_keys // PAGE):
        sl = slice(c * PAGE, (c + 1) * PAGE)
        eqc = jnp.where(eq[:, sl], 1.0, 0.0)
        pref = _dot(eqc.astype(BF16), ut) + run
        keep = jnp.logical_or(gt[:, sl], jnp.logical_and(eq[:, sl], pref < need))
        sel_ref[:, sl] = jnp.where(keep, 1.0, 0.0)
        run = run + jnp.sum(eqc, axis=-1, keepdims=True)
    sel_new = jnp.logical_or(sn > thr, jnp.logical_and(sn == thr, run < need))
    seln_ref[...] = jnp.broadcast_to(jnp.where(sel_new, 1.0, 0.0), seln_ref.shape)


def _dsa_sample_attn_kernel(*refs, n_pages):
    _, aq_ref, kn_ref, vn_ref, sel_ref, seln_ref = refs[:6]
    kpages = refs[6:6 + n_pages]
    vpages = refs[6 + n_pages:6 + 2 * n_pages]
    o_ref, k_scr, v_scr = refs[6 + 2 * n_pages:]
    for c in range(n_pages):
        k_scr[:, c * PAGE:(c + 1) * PAGE] = kpages[c][0]
        v_scr[:, c * PAGE:(c + 1) * PAGE] = vpages[c][0]
    sel = sel_ref[0] > 0.5
    sel_new = seln_ref[0][:, 0:1] > 0.5
    aq = aq_ref[0]
    kn = kn_ref[0]
    vn = vn_ref[0]
    hpg = NH // A_KV
    for g in range(A_KV):
        qg = aq[g * hpg:(g + 1) * hpg, :]
        s = _dot(qg, k_scr[g * HD:(g + 1) * HD, :]) * HD ** -0.5
        s = jnp.where(sel, s, -jnp.inf)
        s_new = jnp.sum(qg * kn[:, g * HD:(g + 1) * HD], axis=-1, keepdims=True) * HD ** -0.5
        s_new = jnp.where(sel_new, s_new, -jnp.inf)
        m = jnp.maximum(jnp.max(s, axis=-1, keepdims=True), s_new)
        pr = jnp.exp(s - m)
        pn = jnp.exp(s_new - m)
        l = jnp.sum(pr, axis=-1, keepdims=True) + pn
        o = _dot_nt(pr, v_scr[g * HD:(g + 1) * HD, :]) + pn * vn[:, g * HD:(g + 1) * HD]
        o_ref[0, g * hpg:(g + 1) * hpg, :] = o / l


def _dsa_sample(layer, page_table, iq_r, iw, ik_r, aq_r, k_r, v_r, cki, ck, cv, n_pool):
    bd, n_pages = page_table.shape
    past = n_pages * PAGE
    topk = min(TOPK_MAX, (past + 1) // 4)
    base = layer * n_pool
    per_b = lambda b, pt: (b, 0, 0)

    def page_specs(width):
        return [pl.BlockSpec((1, width, PAGE), lambda b, pt, c=c: (base + pt[b, c], 0, 0)) for c in range(n_pages)]

    sc, sn = pl.pallas_call(
        functools.partial(_dsa_sample_score_kernel, n_pages=n_pages),
        out_shape=(jax.ShapeDtypeStruct((bd, 1, past), F32), jax.ShapeDtypeStruct((bd, 1, 128), F32)),
        grid_spec=pltpu.PrefetchScalarGridSpec(
            num_scalar_prefetch=1,
            grid=(bd,),
            in_specs=[pl.BlockSpec((1, NH, HD), per_b), pl.BlockSpec((1, NH, 1), per_b),
                      pl.BlockSpec((1, 1, HD), per_b)] + page_specs(HD),
            out_specs=(pl.BlockSpec((1, 1, past), per_b), pl.BlockSpec((1, 1, 128), per_b))),
        compiler_params=_cp("arbitrary"),
    )(page_table, iq_r.reshape(bd, NH, HD), iw.reshape(bd, NH, 1), ik_r.reshape(bd, 1, HD), *([cki] * n_pages))

    sel, seln = pl.pallas_call(
        functools.partial(_dsa_sample_select_kernel, topk=topk),
        out_shape=(jax.ShapeDtypeStruct((bd, past), F32), jax.ShapeDtypeStruct((bd, 128), F32)),
    )(sc.reshape(bd, past), sn.reshape(bd, 128))

    out = pl.pallas_call(
        functools.partial(_dsa_sample_attn_kernel, n_pages=n_pages),
        out_shape=jax.ShapeDtypeStruct((bd, NH, HD), F32),
        grid_spec=pltpu.PrefetchScalarGridSpec(
            num_scalar_prefetch=1,
            grid=(bd,),
            in_specs=[pl.BlockSpec((1, NH, HD), per_b), pl.BlockSpec((1, 1, 128), per_b),
                      pl.BlockSpec((1, 1, 128), per_b), pl.BlockSpec((1, 1, past), per_b),
                      pl.BlockSpec((1, 1, 128), per_b)] + page_specs(128) + page_specs(128),
            out_specs=pl.BlockSpec((1, NH, HD), per_b),
            scratch_shapes=[pltpu.VMEM((128, past), F32), pltpu.VMEM((128, past), F32)]),
        compiler_params=_cp("arbitrary"),
    )(page_table, aq_r.reshape(bd, NH, HD), k_r.reshape(bd, 1, 128), v_r.reshape(bd, 1, 128),
      sel.reshape(bd, 1, past), seln.reshape(bd, 1, 128), *([ck] * n_pages), *([cv] * n_pages))
    return out.reshape(bd, GW)


def _log_sigmoid(x):
    return jnp.minimum(x, 0.0) - jnp.log(1.0 + jnp.exp(-jnp.abs(x)))


MLSTM_ROWS = 2 * CHUNK


def _mlstm_prompt_kernel(m_ref, sm_ref, gb_ref, nw_ref, y_ref, c_ref, n_ref, mm_ref):
    L = CHUNK

    @pl.when(pl.program_id(1) == 0)
    def _():
        c_ref[...] = jnp.zeros(c_ref.shape, F32)
        n_ref[...] = jnp.zeros(n_ref.shape, F32)
        mm_ref[...] = jnp.zeros(mm_ref.shape, F32)

    smb = sm_ref[...] + gb_ref[...]
    smt = smb.T
    ig_all = smb[:, SM_MI:SM_MI + NH]
    lf_all = _log_sigmoid(smb[:, SM_MF:SM_MF + NH])
    igt_all = smt[SM_MI:SM_MI + NH, :]
    lft_all = _log_sigmoid(smt[SM_MF:SM_MF + NH, :])
    ri, ci = _iota((L, L), 0), _iota((L, L), 1)
    tril = jnp.where(ci <= ri, 1.0, 0.0)
    triu = jnp.where(ci >= ri, 1.0, 0.0)
    same, causal, _ = _bd_masks(L)
    tile = lambda z: jnp.concatenate([z] * HG, axis=0)
    msk = lambda z: jnp.where(same, z, 0.0)
    stack_cols = lambda z, g: jnp.concatenate([z[:, g * HG + j:g * HG + j + 1] for j in range(HG)], axis=0)
    stack_rows = lambda z, g: jnp.concatenate([z[g * HG + j:g * HG + j + 1, :] for j in range(HG)], axis=1)
    for cc in range(MLSTM_ROWS // L):
        rows = slice(cc * L, (cc + 1) * L)
        ig = ig_all[rows]
        bcs = _dot(tril, lf_all[rows], HI)
        bcst = _dot(lft_all[:, rows], triu, HI)
        igt = igt_all[:, rows]
        for g in range(NH // HG):
            ln = slice(g * BD, (g + 1) * BD)
            heads = range(g * HG, (g + 1) * HG)
            qexp = msk(tile(m_ref[rows, ln]))
            kt = tile(m_ref[rows, GW + g * BD:GW + (g + 1) * BD] * HD ** -0.5)
            kexp = msk(kt)
            vexp = msk(tile(m_ref[rows, 2 * GW + g * BD:2 * GW + (g + 1) * BD]))
            og = m_ref[rows, 3 * GW + g * BD:3 * GW + (g + 1) * BD]
            bcol = stack_cols(bcs, g)
            icol = stack_cols(ig, g)
            brow = stack_rows(bcst, g)
            irow = stack_rows(igt, g)
            mprev = [mm_ref[0, :, h:h + 1] for h in heads]
            bl = [bcs[L - 1:L, h:h + 1] for h in heads]
            percol = lambda zs: jnp.concatenate([jnp.broadcast_to(z, (L, 1)) for z in zs], axis=0)
            perrow = lambda zs: jnp.concatenate([jnp.broadcast_to(z, (1, HD)) for z in zs], axis=1)
            mprev_col = percol(mprev)
            dmat = jnp.where(causal, bcol - brow + irow, -jnp.inf)
            inter = bcol + mprev_col
            mj = jnp.maximum(inter, jnp.max(dmat, axis=-1, keepdims=True))
            s = _dot_nt(qexp, kt) * jnp.exp(dmat - mj)
            iw = jnp.exp(inter - mj)
            cbd = c_ref[0, g]
            nrow = n_ref[0, :, ln]
            num = _dot(s, vexp) + iw * _dot_nt(qexp, cbd)
            den = jnp.sum(s, axis=-1, keepdims=True) + iw * jnp.sum(qexp * nrow, axis=-1, keepdims=True)
            hc = num / jnp.maximum(jnp.abs(den), jnp.exp(-mj))
            wl = percol(bl) - bcol + icol
            m_new = [jnp.maximum(bl[j] + mprev[j], jnp.max(wl[j * L:(j + 1) * L], axis=0, keepdims=True))
                     for j in range(HG)]
            dec = [jnp.exp(bl[j] + mprev[j] - m_new[j]) for j in range(HG)]
            ws = jnp.exp(wl - percol(m_new))
            c_ref[0, g] = percol(dec) * cbd + _dot_tn(vexp * ws, kexp)
            n_ref[0, :, ln] = perrow(dec) * nrow + jnp.sum(kexp * ws, axis=0, keepdims=True)
            for j, h in enumerate(heads):
                mm_ref[0, :, h:h + 1] = m_new[j]
            hn = hc * lax.rsqrt(jnp.sum(hc * hc, axis=-1, keepdims=True) * (1.0 / HD) + NORM_EPS)
            hn = sum(hn[j * L:(j + 1) * L] for j in range(HG))
            y_ref[rows, ln] = hn * nw_ref[:, ln] * _sigmoid(og)


def _gate_bias_row(b_i, b_f):
    z = lambda n: jnp.zeros((1, n), F32)
    return jnp.concatenate([z(SM_MI), b_i, b_f, z(128 - SM_MF - NH)], axis=1)


def _mlstm_prompt(proj, b_i, b_f, nw, nb, t):
    tm = MLSTM_ROWS
    nc = t // tm
    n = nb * t
    ng = NH // HG
    row = lambda b, c: b * nc + c
    y, cbd, nrow, m = pl.pallas_call(
        _mlstm_prompt_kernel,
        out_shape=(jax.ShapeDtypeStruct((n, GW), F32), jax.ShapeDtypeStruct((nb, ng, BD, BD), F32),
                   jax.ShapeDtypeStruct((nb, 1, GW), F32), jax.ShapeDtypeStruct((nb, 1, NH), F32)),
        grid=(nb, nc),
        in_specs=[pl.BlockSpec((tm, 4 * GW), lambda b, c: (row(b, c), 0)),
                  pl.BlockSpec((tm, 128), lambda b, c: (row(b, c), C_SM // 128)),
                  pl.BlockSpec((1, 128), lambda b, c: (0, 0)),
                  pl.BlockSpec((1, GW), lambda b, c: (0, 0))],
        out_specs=(pl.BlockSpec((tm, GW), lambda b, c: (row(b, c), 0)),
                   pl.BlockSpec((1, ng, BD, BD), lambda b, c: (b, 0, 0, 0)),
                   pl.BlockSpec((1, 1, GW), lambda b, c: (b, 0, 0)),
                   pl.BlockSpec((1, 1, NH), lambda b, c: (b, 0, 0))),
        compiler_params=_cp("parallel", "arbitrary"),
    )(proj, proj, _gate_bias_row(b_i, b_f), nw)
    return y, _bd_diag(cbd), nrow.reshape(nb, NH, HD), m


def _head_expanders():
    diag = (_iota((NH * HD, HD), 0) % HD) == _iota((NH * HD, HD), 1)

    def rep(z):
        return jnp.concatenate([jnp.broadcast_to(z[h:h + 1, :], (HD, z.shape[1])) for h in range(NH)], axis=0)

    def fold(col):
        m = jnp.where(diag, col, 0.0)
        return jnp.concatenate([jnp.sum(m[h * HD:(h + 1) * HD], axis=0, keepdims=True) for h in range(NH)], axis=0)

    return rep, fold, diag


def _mlstm_step_kernel(x_ref, gt_ref, gb_ref, nw_ref, c_ref, n_ref, mm_ref, y_ref, co_ref, no_ref, mo_ref):
    x = x_ref[0]
    q, k, v, og = x[0:NH], x[NH:2 * NH] * HD ** -0.5, x[2 * NH:3 * NH], x[3 * NH:4 * NH]
    gates = gt_ref[0] + gb_ref[...]
    ig = gates[:, 0:1]
    rep, fold, diag = _head_expanders()
    c0 = c_ref[0].reshape(NH * HD, HD)
    n0 = n_ref[0]
    inter = _log_sigmoid(gates[:, 1:2]) + mm_ref[0]
    mj = jnp.maximum(inter, ig)
    s = jnp.sum(q * k, axis=-1, keepdims=True) * jnp.exp(ig - mj)
    iw = jnp.exp(inter - mj)
    ws = jnp.exp(ig - mj)
    den = s + iw * jnp.sum(n0 * q, axis=-1, keepdims=True)
    inv = 1.0 / jnp.maximum(jnp.abs(den), jnp.exp(-mj))
    lane = _iota((NH, 128), 1)
    per_head = jnp.where(lane == 0, s, jnp.where(lane == 1, iw, jnp.where(lane == 2, ws, inv)))
    ph = rep(per_head)
    s_c, iw_c, ws_c, inv_c = ph[:, 0:1], ph[:, 1:2], ph[:, 2:3], ph[:, 3:4]
    cq = jnp.sum(c0 * rep(q), axis=1, keepdims=True)
    vcol = jnp.sum(jnp.where(diag, rep(v), 0.0), axis=1, keepdims=True)
    hcol = (s_c * vcol + iw_c * cq) * inv_c
    co_ref[0] = (iw_c * c0 + (ws_c * vcol) * rep(k)).reshape(NH, HD, HD)
    no_ref[0] = iw * n0 + ws * k
    mo_ref[0] = mj
    h8 = fold(hcol)
    hn = h8 * lax.rsqrt(jnp.mean(h8 * h8, axis=-1, keepdims=True) + NORM_EPS)
    y_ref[0] = hn * nw_ref[...] * _sigmoid(og)


def _mlstm_step(proj, b_i, b_f, nw, c_all, l, n0, m0):
    bd = proj.shape[0]
    per_b3 = lambda b: (b, 0, 0)
    full = lambda b: (0, 0)
    x = proj[:, :4 * GW].reshape(bd, 4 * NH, HD)
    gates = proj[:, C_SM + SM_MI:C_SM + SM_MI + 2 * NH].reshape(bd, 2, NH).transpose(0, 2, 1)
    gate_bias = jnp.concatenate([b_i, b_f], axis=0).T
    y, c1, n1, m1 = pl.pallas_call(
        _mlstm_step_kernel,
        out_shape=(jax.ShapeDtypeStruct((bd, NH, HD), F32), jax.ShapeDtypeStruct((bd, NH, HD, HD), F32),
                   jax.ShapeDtypeStruct((bd, NH, HD), F32), jax.ShapeDtypeStruct((bd, NH, 1), F32)),
        grid=(bd,),
        in_specs=[pl.BlockSpec((1, 4 * NH, HD), per_b3),
                  pl.BlockSpec((1, NH, 2), per_b3),
                  pl.BlockSpec((NH, 2), full),
                  pl.BlockSpec((NH, HD), full),
                  pl.BlockSpec((1, NH, HD, HD), lambda b: (l * bd + b, 0, 0, 0)),
                  pl.BlockSpec((1, NH, HD), per_b3),
                  pl.BlockSpec((1, NH, 1), per_b3)],
        out_specs=(pl.BlockSpec((1, NH, HD), per_b3),
                   pl.BlockSpec((1, NH, HD, HD), lambda b: (b, 0, 0, 0)),
                   pl.BlockSpec((1, NH, HD), per_b3),
                   pl.BlockSpec((1, NH, 1), per_b3)),
        compiler_params=_cp("parallel"),
    )(x, gates, gate_bias, nw.reshape(NH, HD), c_all.reshape((-1,) + c_all.shape[2:]), n0, m0.reshape(bd, NH, 1))
    return y, c1, n1, m1


def _rwkv_prep_kernel(x_ref, prev_ref, mu_ref, w0_ref, ww2_ref, a0_ref, wa2_ref, wg2_ref, kk_ref, ka_ref,
                      r_o, lw_o, k_o, v_o, kk_o, a_o, g_o, carry_scr, *, seq):
    x = x_ref[...]
    tm = x.shape[0]
    if seq:
        first = jnp.where(pl.program_id(1) == 0, prev_ref[0], carry_scr[...])
        xprev = jnp.where(_iota((tm, 1), 0) == 0, first, pltpu.roll(x, 1, 0))
        carry_scr[...] = x[tm - 1:tm, :]
    else:
        xprev = prev_ref[...]
    xm = x + (xprev - x) * mu_ref[...]
    r = xm[:, 0:GW]
    kx = xm[:, GW:2 * GW]
    v = xm[:, 2 * GW:3 * GW]
    xw = xm[:, 3 * GW:3 * GW + 64]
    xa = xm[:, 3 * GW + 64:3 * GW + 128]
    xg = xm[:, 3 * GW + 128:R_IN]
    w = -_softplus(-(w0_ref[...] + _dot(jnp.tanh(xw), ww2_ref[...]))) - 0.5
    a = _sigmoid(a0_ref[...] + _dot(xa, wa2_ref[...]))
    r_o[...] = r
    lw_o[...] = -jnp.exp(w)
    v_o[...] = v
    a_o[...] = a
    g_o[...] = _dot(_sigmoid(xg), wg2_ref[...])
    k_o[...] = kx * (1.0 + (a - 1.0) * ka_ref[...])
    kk = kx * kk_ref[...]
    for h in range(NH):
        kh = kk[:, h * HD:(h + 1) * HD]
        nrm = jnp.sqrt(jnp.sum(kh * kh, axis=-1, keepdims=True))
        kk_o[:, h * HD:(h + 1) * HD] = kh / jnp.maximum(nrm, 1e-12)


def _rwkv_prep(proj, prev, mu, w0, ww2, a0, wa2, wg2, k_k, k_a, nb, t):
    n = nb * t
    seq = t > 1
    full = lambda *_: (0, 0)
    if seq:
        tm = min(256, t)
        nt = t // tm
        grid = (nb, nt)
        xmap = lambda b, i: (b * nt + i, C_RIN // R_IN)
        pspec = pl.BlockSpec((1, 1, R_IN), lambda b, i: (b, 0, 0))
        omap = lambda b, i: (b * nt + i, 0)
        sem = ("parallel", "arbitrary")
    else:
        tm = n
        grid = (1,)
        xmap = lambda i: (0, C_RIN // R_IN)
        pspec = pl.BlockSpec((tm, R_IN), lambda i: (0, 0))
        omap = lambda i: (0, 0)
        sem = ("arbitrary",)
    wspecs = [pl.BlockSpec((1, R_IN), full), pl.BlockSpec((1, GW), full), pl.BlockSpec((64, GW), full),
              pl.BlockSpec((1, GW), full), pl.BlockSpec((64, GW), full), pl.BlockSpec((128, GW), full),
              pl.BlockSpec((1, GW), full), pl.BlockSpec((1, GW), full)]
    return pl.pallas_call(
        functools.partial(_rwkv_prep_kernel, seq=seq),
        out_shape=tuple(jax.ShapeDtypeStruct((n, GW), F32) for _ in range(7)),
        grid=grid,
        in_specs=[pl.BlockSpec((tm, R_IN), xmap), pspec] + wspecs,
        out_specs=tuple(pl.BlockSpec((tm, GW), omap) for _ in range(7)),
        scratch_shapes=[pltpu.VMEM((1, R_IN), F32)],
        compiler_params=_cp(*sem),
    )(proj, prev, mu, w0, ww2, a0, wa2, wg2, k_k, k_a)


def _rwkv_scan_kernel(r_ref, lw_ref, k_ref, v_ref, kk_ref, a_ref, g_ref, rk_ref, lnw_ref, lnb_ref, y_ref, s_ref):
    C = CHUNK

    @pl.when(pl.program_id(1) == 0)
    def _():
        s_ref[...] = jnp.zeros(s_ref.shape, F32)

    ri, ci = _iota((C, C), 0), _iota((C, C), 1)
    lw = lw_ref[...]
    cs = _dot(jnp.where(ci <= ri, 1.0, 0.0), lw, HI)
    gam = jnp.exp(cs)
    ginv = jnp.exp(-cs)
    r = r_ref[...]
    k = k_ref[...]
    v = v_ref[...]
    kk = kk_ref[...]
    at = -kk * jnp.exp(cs - lw)
    bt = kk * a_ref[...] * ginv
    kt = k * ginv
    rt = r * gam
    glast = gam[C - 1:C, :]
    bonus_in = r * k * rk_ref[...]

    same, incl, strict = _bd_masks(C)
    row, col = _iota((BD, BD), 0), _iota((BD, BD), 1)
    eye = jnp.where(row == col, 1.0, 0.0)
    tile = lambda z: jnp.concatenate([z] * HG, axis=0)
    msk = lambda z: jnp.where(same, z, 0.0)
    fold = lambda z: sum(z[j * C:(j + 1) * C] for j in range(HG))
    for g in range(NH // HG):
        ln = slice(g * BD, (g + 1) * BD)
        btl, ktl = tile(bt[:, ln]), tile(kt[:, ln])
        vexp = msk(tile(v[:, ln]))
        ar = jnp.concatenate([msk(tile(at[:, ln])), msk(tile(rt[:, ln]))], axis=0)
        sbd = s_ref[0, g]
        gb = _dot_nt(ar, btl)
        gk = _dot_nt(ar, ktl)
        gs = _dot_nt(ar, sbd)
        n_ab = jnp.where(strict, gb[:BD], 0.0)
        a_ak = jnp.where(strict, gk[:BD], 0.0)
        x = eye + n_ab
        pm = _dot(n_ab, n_ab)
        for j in range(5):
            xn = x + _dot(x, pm)
            if j < 4:
                pm = _dot(pm, pm)
            x = xn
        u = _dot(x, gs[:BD] + _dot(a_ak, vexp))
        o = gs[BD:] + _dot(jnp.where(incl, gb[BD:], 0.0), u) + _dot(jnp.where(incl, gk[BD:], 0.0), vexp)
        gl = glast[:, ln]
        uv = jnp.concatenate([u, vexp], axis=0)
        bk = jnp.concatenate([msk(btl * gl), msk(ktl * gl)], axis=0)
        s_ref[0, g] = sbd * gl + _dot_tn(uv, bk)
        mean = jnp.sum(o, axis=-1, keepdims=True) * (1.0 / HD)
        dev = msk(o - mean)
        var = jnp.sum(dev * dev, axis=-1, keepdims=True) * (1.0 / HD)
        on = fold(dev * lax.rsqrt(var + R_LN_EPS)) * lnw_ref[:, ln] + lnb_ref[:, ln]
        bonus = fold(jnp.sum(msk(tile(bonus_in[:, ln])), axis=-1, keepdims=True) * vexp)
        y_ref[:, ln] = (on + bonus) * g_ref[:, ln]


def _rwkv_scan(rs, rk, lnw, lnb, nb, t):
    nc = t // CHUNK
    n = nb * t
    ng = NH // HG
    rowmap = lambda b, c: (b * nc + c, 0)
    full = lambda b, c: (0, 0)
    y, sbd = pl.pallas_call(
        _rwkv_scan_kernel,
        out_shape=(jax.ShapeDtypeStruct((n, GW), F32), jax.ShapeDtypeStruct((nb, ng, BD, BD), F32)),
        grid=(nb, nc),
        in_specs=[pl.BlockSpec((CHUNK, GW), rowmap)] * 7 + [pl.BlockSpec((1, GW), full)] * 3,
        out_specs=(pl.BlockSpec((CHUNK, GW), rowmap), pl.BlockSpec((1, ng, BD, BD), lambda b, c: (b, 0, 0, 0))),
        compiler_params=_cp("parallel", "arbitrary"),
    )(*rs, rk, lnw, lnb)
    return y, _bd_diag(sbd)


def _rwkv_step_kernel(r_ref, lw_ref, k_ref, v_ref, kk_ref, a_ref, g_ref, rk_ref, lnw_ref, lnb_ref, s_ref,
                      y_ref, so_ref):
    r, lw, k, v, kk, a, g = (ref[0] for ref in (r_ref, lw_ref, k_ref, v_ref, kk_ref, a_ref, g_ref))
    rep, fold, diag = _head_expanders()
    s0 = s_ref[0].reshape(NH * HD, HD)
    kk_rep = rep(kk)
    sk = jnp.sum(s0 * kk_rep, axis=1, keepdims=True)
    vcol = jnp.sum(jnp.where(diag, rep(v), 0.0), axis=1, keepdims=True)
    s1 = s0 * rep(jnp.exp(lw)) - sk * rep(kk * a) + vcol * rep(k)
    so_ref[0] = s1.reshape(NH, HD, HD)
    ocol = jnp.sum(s1 * rep(r), axis=1, keepdims=True)
    o = fold(ocol)
    mean = jnp.mean(o, axis=-1, keepdims=True)
    var = jnp.mean(jnp.square(o - mean), axis=-1, keepdims=True)
    on = (o - mean) * lax.rsqrt(var + R_LN_EPS) * lnw_ref[...] + lnb_ref[...]
    bonus = jnp.sum(r * k * rk_ref[...], axis=-1, keepdims=True) * v
    y_ref[0] = (on + bonus) * g


def _rwkv_step(rs, rk, lnw, lnb, s_all, l):
    bd = s_all.shape[1]
    per_b = lambda b: (b, 0, 0)
    full = lambda b: (0, 0)
    return pl.pallas_call(
        _rwkv_step_kernel,
        out_shape=(jax.ShapeDtypeStruct((bd, NH, HD), F32), jax.ShapeDtypeStruct((bd, NH, HD, HD), F32)),
        grid=(bd,),
        in_specs=[pl.BlockSpec((1, NH, HD), per_b)] * 7 + [pl.BlockSpec((NH, HD), full)] * 3
                 + [pl.BlockSpec((1, NH, HD, HD), lambda b: (l * bd + b, 0, 0, 0))],
        out_specs=(pl.BlockSpec((1, NH, HD), per_b), pl.BlockSpec((1, NH, HD, HD), lambda b: (b, 0, 0, 0))),
        compiler_params=_cp("parallel"),
    )(*(z.reshape(bd, NH, HD) for z in rs), *(z.reshape(NH, HD) for z in (rk, lnw, lnb)),
      s_all.reshape((-1,) + s_all.shape[2:]))


def _gelu_tanh(x):
    return 0.5 * x * (1.0 + jnp.tanh(math.sqrt(2.0 / math.pi) * (x + 0.044715 * (x * x * x))))


def _s5_kernel(u_ref, bre_ref, bim_ref, lre_ref, lim_ref, cre_ref, cim_ref, d_ref, wg_ref, bg_ref, h0r_ref, h0i_ref,
               y_ref, hr_ref, hi_ref, hre_scr, him_scr, *, nb, tb):
    @pl.when(pl.program_id(0) == 0)
    def _():
        hr_ref[...] = h0r_ref[...]
        hi_ref[...] = h0i_ref[...]

    u = u_ref[...]
    hre_scr[...] = _dot(u, bre_ref[...])
    him_scr[...] = _dot(u, bim_ref[...])
    lr = lre_ref[...]
    li = lim_ref[...]

    def body(t, carry):
        hr, hi = carry
        rows = pl.ds(pl.multiple_of(t * nb, nb), nb)
        nr = lr * hr - li * hi + hre_scr[rows, :]
        ni = lr * hi + li * hr + him_scr[rows, :]
        hre_scr[rows, :] = nr
        him_scr[rows, :] = ni
        return nr, ni

    hr, hi = lax.fori_loop(0, tb, body, (hr_ref[...], hi_ref[...]))
    hr_ref[...] = hr
    hi_ref[...] = hi
    y = _dot(hre_scr[...], cre_ref[...]) - _dot(him_scr[...], cim_ref[...]) + d_ref[...] * u
    y = _gelu_tanh(y)
    y_ref[...] = y * _sigmoid(_dot(y, wg_ref[...]) + bg_ref[...])


def _s5(u_tm, mats, h0r, h0i, nb, t):
    bre, bim, lre, lim, cre, cim, d, wg, bg = mats
    tb = min(64, t)
    full = lambda i: (0, 0)
    return pl.pallas_call(
        functools.partial(_s5_kernel, nb=nb, tb=tb),
        out_shape=(jax.ShapeDtypeStruct((t * nb, GW), F32), jax.ShapeDtypeStruct((nb, S5_W), F32),
                   jax.ShapeDtypeStruct((nb, S5_W), F32)),
        grid=(t // tb,),
        in_specs=[pl.BlockSpec((tb * nb, GW), lambda i: (i, 0)),
                  pl.BlockSpec((GW, S5_W), full), pl.BlockSpec((GW, S5_W), full),
                  pl.BlockSpec((1, S5_W), full), pl.BlockSpec((1, S5_W), full),
                  pl.BlockSpec((S5_W, GW), full), pl.BlockSpec((S5_W, GW), full),
                  pl.BlockSpec((1, GW), full), pl.BlockSpec((GW, GW), full), pl.BlockSpec((1, GW), full),
                  pl.BlockSpec((nb, S5_W), full), pl.BlockSpec((nb, S5_W), full)],
        out_specs=(pl.BlockSpec((tb * nb, GW), lambda i: (i, 0)),
                   pl.BlockSpec((nb, S5_W), full), pl.BlockSpec((nb, S5_W), full)),
        scratch_shapes=[pltpu.VMEM((tb * nb, S5_W), F32), pltpu.VMEM((tb * nb, S5_W), F32)],
        compiler_params=_cp("arbitrary"),
    )(u_tm, bre, bim, lre, lim, cre, cim, d, wg, bg, h0r, h0i)


def _s5_mats(a_re, a_im, b_re, b_im, c_re, c_im, d_skip, log_dt, w_glu, b_glu):
    dt = jnp.exp(log_dt)
    mag = jnp.exp(a_re * dt)
    lb_re, lb_im = mag * jnp.cos(a_im * dt), mag * jnp.sin(a_im * dt)
    den = a_re * a_re + a_im * a_im
    f_re = ((lb_re - 1.0) * a_re + lb_im * a_im) / den
    f_im = (lb_im * a_re - (lb_re - 1.0) * a_im) / den
    bb_re = f_re[..., None] * b_re - f_im[..., None] * b_im
    bb_im = f_re[..., None] * b_im + f_im[..., None] * b_re
    eye = jnp.eye(S5_G, dtype=F32)
    bd = lambda bb: jnp.einsum('gpc,gh->gchp', bb, eye).reshape(GW, S5_W)
    cd = lambda cc: jnp.einsum('gcp,gh->gphc', cc, eye).reshape(S5_W, GW)
    return (bd(bb_re), bd(bb_im), lb_re.reshape(1, S5_W), lb_im.reshape(1, S5_W), cd(c_re), cd(c_im),
            d_skip.reshape(1, GW), w_glu, b_glu.reshape(1, GW))


def _ffn_up_kernel(h_ref, wa_ref, wb_ref, cw_ref, cb_ref, s0_ref, s1_ref, y_ref, a_ref, carry_scr, *, seq):
    h = h_ref[...]
    a = _dot(h, wa_ref[...])
    b = _dot(h, wb_ref[...])
    tm = a.shape[0]
    if seq:
        @pl.when(pl.program_id(2) == 0)
        def _():
            carry_scr[...] = s0_ref[0]

        rowid = _iota((tm, 1), 0)
        c0 = carry_scr[0:1, :]
        c1 = carry_scr[1:2, :]
        a1 = jnp.where(rowid == 0, c1, pltpu.roll(a, 1, 0))
        a2 = jnp.where(rowid == 0, c0, jnp.where(rowid == 1, c1, pltpu.roll(a, 2, 0)))
        carry_scr[...] = a[tm - 2:tm, :]
        a_ref[0] = a[tm - 2:tm, :]
    else:
        a2 = s0_ref[...]
        a1 = s1_ref[...]
        a_ref[...] = a
    c = cb_ref[...] + a2 * cw_ref[0:1, :] + a1 * cw_ref[1:2, :] + a * cw_ref[2:3, :]
    y_ref[...] = (c * _sigmoid(c) * b).astype(BF16)


def _ffn_up(h2, w_up, l, cw, cb, st, nb, t):
    n, d = h2.shape
    dff = w_up.shape[2] // 2
    tn = 512
    nj = dff // tn
    if t > 1:
        tm = min(MM_ROWS, t)
        nt = t // tm
        y, fc = pl.pallas_call(
            functools.partial(_ffn_up_kernel, seq=True),
            out_shape=(jax.ShapeDtypeStruct((n, dff), BF16), jax.ShapeDtypeStruct((nb, 2, dff), F32)),
            grid=(nb, nj, nt),
            in_specs=[pl.BlockSpec((tm, d), lambda b, j, i: (b * nt + i, 0)),
                      pl.BlockSpec((None, d, tn), lambda b, j, i: (l, 0, j)),
                      pl.BlockSpec((None, d, tn), lambda b, j, i: (l, 0, nj + j)),
                      pl.BlockSpec((3, tn), lambda b, j, i: (0, j)),
                      pl.BlockSpec((1, tn), lambda b, j, i: (0, j)),
                      pl.BlockSpec((1, 2, tn), lambda b, j, i: (b, 0, j)),
                      pl.BlockSpec((1, 2, tn), lambda b, j, i: (b, 0, j))],
            out_specs=(pl.BlockSpec((tm, tn), lambda b, j, i: (b * nt + i, j)),
                       pl.BlockSpec((1, 2, tn), lambda b, j, i: (b, 0, j))),
            scratch_shapes=[pltpu.VMEM((2, tn), F32)],
            compiler_params=_cp("parallel", "parallel", "arbitrary"),
        )(h2, w_up, w_up, cw, cb, st, st)
        return y, fc
    s0, s1 = st[:, 0, :], st[:, 1, :]
    y, a = pl.pallas_call(
        functools.partial(_ffn_up_kernel, seq=False),
        out_shape=(jax.ShapeDtypeStruct((n, dff), BF16), jax.ShapeDtypeStruct((n, dff), F32)),
        grid=(nj,),
        in_specs=[pl.BlockSpec((n, d), lambda j: (0, 0)),
                  pl.BlockSpec((None, d, tn), lambda j: (l, 0, j)),
                  pl.BlockSpec((None, d, tn), lambda j: (l, 0, nj + j)),
                  pl.BlockSpec((3, tn), lambda j: (0, j)),
                  pl.BlockSpec((1, tn), lambda j: (0, j)),
                  pl.BlockSpec((n, tn), lambda j: (0, j)),
                  pl.BlockSpec((n, tn), lambda j: (0, j))],
        out_specs=(pl.BlockSpec((n, tn), lambda j: (0, j)), pl.BlockSpec((n, tn), lambda j: (0, j))),
        scratch_shapes=[pltpu.VMEM((2, tn), F32)],
        compiler_params=_cp("parallel"),
    )(h2, w_up, w_up, cw, cb, s0, s1)
    return y, jnp.stack([s1, a], axis=1)


def _permute_w_in(w):
    cols = lambda s, n: w[:, :, s:s + n]
    parts = [cols(0, 4 * GW), cols(2064, GW), cols(2832, GW), cols(5208, GW), cols(3416, R_IN),
             cols(2576, 128), cols(2704, 128), cols(3344, HD), cols(2048, NH), cols(2056, NH), cols(3408, NH),
             jnp.zeros(w.shape[:2] + (NP - C_SM - HD - 3 * NH,), w.dtype)]
    return jnp.concatenate(parts, axis=2).astype(BF16)


def _layer(x2, nb, t, l, W, st, tables, cache):
    n = nb * t
    c0, n0, m0, rs0, rsh0, sre0, sim0, conv0 = st
    proj = _in_proj(x2, W['norm_mix'], W['w_in'], l)
    aq_r, iq_r, k_r, v_r, ik_r = _rope_call(proj, tables[0], tables[1], *tables[2])

    if cache is None:
        ym, c1, n1, m1 = _mlstm_prompt(proj, W['m_b_i'], W['m_b_f'], W['m_norm'], nb, t)
        ya = _dsa_prompt(iq_r, proj, ik_r, aq_r, k_r, v_r, nb, t)
    else:
        ym, c1, n1, m1 = _mlstm_step(proj, W['m_b_i'], W['m_b_f'], W['m_norm'], c0, l, n0, m0)
        ym = ym.reshape(n, GW)
        cki, ck, cv, page_table, n_pool = cache
        ya = _dsa_sample(l, page_table, iq_r, proj[:, C_SM + SM_IW:C_SM + SM_IW + NH], ik_r, aq_r, k_r, v_r,
                         cki, ck, cv, n_pool)

    prev = rsh0.reshape(nb, 1, R_IN) if t > 1 else rsh0
    rs = _rwkv_prep(proj, prev, W['r_mu'], W['r_w0'], W['r_w_w2'], W['r_a0'], W['r_w_a2'], W['r_w_g2'],
                    W['r_k_k'], W['r_k_a'], nb, t)
    if t > 1:
        yr, rs1 = _rwkv_scan(rs, W['r_r_k'], W['r_ln_w'], W['r_ln_b'], nb, t)
    else:
        yr, rs1 = _rwkv_step(rs, W['r_r_k'], W['r_ln_w'], W['r_ln_b'], rs0, l)
        yr = yr.reshape(n, GW)
    rsh1 = proj.reshape(nb, t, NP)[:, t - 1, C_RIN:C_RIN + R_IN]

    su = proj[:, C_SU:C_SU + GW]
    u_tm = su.reshape(nb, t, GW).transpose(1, 0, 2).reshape(t * nb, GW)
    ys_tm, sre1, sim1 = _s5(u_tm, W['s5'], sre0.reshape(nb, S5_W), sim0.reshape(nb, S5_W), nb, t)
    ys = ys_tm.reshape(t, nb, GW).transpose(1, 0, 2).reshape(n, GW)

    x2 = _res_matmul(x2, [ym, ya, yr, ys], W['w_out'], l)
    h2 = _rmsnorm(x2, W['norm_ffn'], BF16)
    y, conv1 = _ffn_up(h2, W['ffn_w_up'], l, W['ffn_conv_w'], W['ffn_conv_b'], conv0, nb, t)
    x2 = _res_matmul(x2, [y], W['ffn_w_down'], l)
    outs = (k_r.reshape(nb, t, A_KV, HD), v_r.reshape(nb, t, A_KV, HD), ik_r.reshape(nb, t, HD),
            c1, n1, m1.reshape(nb, NH), rs1, rsh1, sre1.reshape(nb, S5_G, S5_P), sim1.reshape(nb, S5_G, S5_P), conv1)
    return x2, outs


def kernel(x_prompt, x_sample, cache_k, cache_v, cache_kidx, page_table, state_mlstm_c, state_mlstm_n,
           state_mlstm_m, state_rwkv_s, state_rwkv_shift, state_s5_re, state_s5_im, state_ffn_conv,
           norm_mix, w_in, w_out, m_b_i, m_b_f, m_norm, r_mu, r_w0, r_w_w2, r_a0, r_w_a2, r_w_g2,
           r_k_k, r_k_a, r_r_k, r_ln_w, r_ln_b, s5_a_re, s5_a_im, s5_b_re, s5_b_im, s5_c_re, s5_c_im,
           s5_d, s5_log_dt, s5_w_glu, s5_b_glu, norm_ffn, ffn_w_up, ffn_conv_w, ffn_conv_b, ffn_w_down,
           norm_final):
    bp, tp, d = x_prompt.shape
    bs, ts, _ = x_sample.shape
    assert ts == 1 and tp % CHUNK == 0
    depth = w_in.shape[0]
    n_pool = cache_k.shape[1]
    past = page_table.shape[1] * PAGE
    dff = ffn_conv_b.shape[-1]

    row = lambda z: z.reshape(1, -1)
    layers = []
    w_in_all, w_out_all = _permute_w_in(w_in), w_out.astype(BF16)
    w_up_all, w_down_all = ffn_w_up.astype(BF16), ffn_w_down.astype(BF16)
    for l in range(depth):
        layers.append(dict(
            norm_mix=row(norm_mix[l]), w_in=w_in_all, w_out=w_out_all,
            m_b_i=row(m_b_i[l]), m_b_f=row(m_b_f[l]), m_norm=row(m_norm[l]),
            r_mu=row(r_mu[l]), r_w0=row(r_w0[l]), r_w_w2=r_w_w2[l], r_a0=row(r_a0[l]), r_w_a2=r_w_a2[l],
            r_w_g2=r_w_g2[l], r_k_k=row(r_k_k[l]), r_k_a=row(r_k_a[l]), r_r_k=row(r_r_k[l]),
            r_ln_w=row(r_ln_w[l]), r_ln_b=row(r_ln_b[l]),
            s5=_s5_mats(s5_a_re[l], s5_a_im[l], s5_b_re[l], s5_b_im[l], s5_c_re[l], s5_c_im[l], s5_d[l],
                        s5_log_dt[l], s5_w_glu[l], s5_b_glu[l]),
            norm_ffn=row(norm_ffn[l]), ffn_w_up=w_up_all, ffn_conv_w=ffn_conv_w[l],
            ffn_conv_b=row(ffn_conv_b[l]), ffn_w_down=w_down_all))

    cos_p, sin_p = _rope_tables(jnp.arange(tp))
    cos_s, sin_s = _rope_tables(jnp.full((bs,), past))
    tab_p = (cos_p, sin_p, (bp, tp))
    tab_s = (cos_s, sin_s, (1, bs))

    zeros = lambda *s: jnp.zeros(s, F32)
    st_p = (zeros(bp, NH, HD, HD), zeros(bp, NH, HD), zeros(bp, NH), zeros(bp, NH, HD, HD), zeros(bp, R_IN),
            zeros(bp, S5_G, S5_P), zeros(bp, S5_G, S5_P), zeros(bp, 2, dff))
    cki = cache_kidx.reshape(depth * n_pool, PAGE, HD).transpose(0, 2, 1)
    ck = cache_k.reshape(depth * n_pool, PAGE, A_KV * HD).transpose(0, 2, 1)
    cv = cache_v.reshape(depth * n_pool, PAGE, A_KV * HD).transpose(0, 2, 1)

    xp = x_prompt.reshape(bp * tp, d)
    xs = x_sample.reshape(bs, d)
    new_p, new_s = [], []
    for l in range(depth):
        xp, sp = _layer(xp, bp, tp, l, layers[l], st_p, tab_p, None)
        st_s = (state_mlstm_c, state_mlstm_n[l], state_mlstm_m[l], state_rwkv_s, state_rwkv_shift[l],
                state_s5_re[l], state_s5_im[l], state_ffn_conv[l])
        xs, ss = _layer(xs, bs, 1, l, layers[l], st_s, tab_s, (cki, ck, cv, page_table, n_pool))
        new_p.append(sp)
        new_s.append(ss)
    (k_p, v_p, ki_p, mc_p, mn_p, mm_p, rs_p, rsh_p, sre_p, sim_p, fc_p) = [jnp.stack(z) for z in zip(*new_p)]
    (k_s, v_s, ki_s, mc_s, mn_s, mm_s, rs_s, rsh_s, sre_s, sim_s, fc_s) = [jnp.stack(z) for z in zip(*new_s)]
    y_prompt = _rmsnorm(xp, row(norm_final), F32).reshape(bp, tp, d)
    y_sample = _rmsnorm(xs, row(norm_final), F32).reshape(bs, ts, d)
    return (y_prompt, y_sample, k_p, k_s, v_p, v_s, ki_p, ki_s, mc_p, mc_s, mn_p, mn_s, mm_p, mm_s,
            rs_p, rs_s, rsh_p, rsh_s, sre_p, sre_s, sim_p, sim_s, fc_p, fc_s)
```

```python
import functools
import math

import jax
import jax.numpy as jnp
from jax import lax
from jax.experimental import pallas as pl
from jax.experimental.pallas import tpu as pltpu

F32 = jnp.float32
BF16 = jnp.bfloat16
HI = lax.Precision.HIGHEST

HD = 64
NH = 8
GW = NH * HD
A_KV = 2
PAGE = 128
TOPK_MAX = 256
ROPE_THETA = 10000.0
R_IN = 3 * GW + 64 + 64 + 128
R_LN_EPS = 64e-5
S5_G, S5_CH, S5_P = 32, 16, 64
S5_W = S5_G * S5_P
NORM_EPS = 1e-6
CHUNK = 64
IDX_SCALE = HD ** -0.5 * NH ** -0.5

C_M, C_AQ, C_IQ, C_SU, C_RIN, C_AK, C_AV, C_SM = 0, 2048, 2560, 3072, 3584, 5376, 5504, 5632
NP = 5760
SM_IK, SM_MI, SM_MF, SM_IW = 0, 64, 72, 80

VMEM_LIMIT = 56 * 1024 * 1024
MM_ROWS = 1024
RWKV_SEQS = 2
FFN_PARTS = 2


def _cp(*sem):
    return pltpu.CompilerParams(dimension_semantics=sem, vmem_limit_bytes=VMEM_LIMIT)


def _dot(a, b, prec=None):
    return jnp.dot(a, b, preferred_element_type=F32, precision=prec)


def _dot_nt(a, b, prec=None):
    return lax.dot_general(a, b, (((1,), (1,)), ((), ())), preferred_element_type=F32, precision=prec)


def _dot_tn(a, b, prec=None):
    return lax.dot_general(a, b, (((0,), (0,)), ((), ())), preferred_element_type=F32, precision=prec)


def _sigmoid(x):
    return 1.0 / (1.0 + jnp.exp(-x))


def _softplus(x):
    return jnp.maximum(x, 0.0) + jnp.log(1.0 + jnp.exp(-jnp.abs(x)))


def _iota(shape, dim):
    return lax.broadcasted_iota(jnp.int32, shape, dim)


HG = 4
BD = HG * HD
assert CHUNK == HD


def _bd_masks(c):
    row, col = _iota((HG * c, HG * c), 0), _iota((HG * c, HG * c), 1)
    same = (row // c) == (col // c)
    t, s = row % c, col % c
    return same, jnp.logical_and(same, s <= t), jnp.logical_and(same, s < t)


def _bd_diag(x):
    nb, ng = x.shape[:2]
    x6 = x.reshape(nb, ng, HG, HD, HG, HD)
    return jnp.stack([x6[:, :, j, :, j, :] for j in range(HG)], axis=2).reshape(nb, ng * HG, HD, HD)


def _inproj_kernel(x_ref, g_ref, w_ref, o_ref, h_scr):
    @pl.when(pl.program_id(1) == 0)
    def _():
        x = x_ref[...]
        ms = jnp.mean(x * x, axis=-1, keepdims=True)
        h_scr[...] = (x * lax.rsqrt(ms + NORM_EPS) * g_ref[...]).astype(BF16)

    o_ref[...] = _dot(h_scr[...], w_ref[...])


def _in_proj(x2, g, w, l):
    n, d = x2.shape
    npad = w.shape[2]
    tm = min(MM_ROWS, n)
    tn = 640
    return pl.pallas_call(
        _inproj_kernel,
        out_shape=jax.ShapeDtypeStruct((n, npad), F32),
        grid=(n // tm, npad // tn),
        in_specs=[pl.BlockSpec((tm, d), lambda i, j: (i, 0)),
                  pl.BlockSpec((1, d), lambda i, j: (0, 0)),
                  pl.BlockSpec((None, d, tn), lambda i, j: (l, 0, j))],
        out_specs=pl.BlockSpec((tm, tn), lambda i, j: (i, j)),
        scratch_shapes=[pltpu.VMEM((tm, d), BF16)],
        compiler_params=_cp("parallel", "arbitrary"),
    )(x2, g, w)


def _rmsnorm_kernel(x_ref, g_ref, o_ref):
    x = x_ref[...]
    ms = jnp.mean(x * x, axis=-1, keepdims=True)
    o_ref[...] = (x * lax.rsqrt(ms + NORM_EPS) * g_ref[...]).astype(o_ref.dtype)


def _rmsnorm(x2, g, dtype):
    n, d = x2.shape
    tm = min(512, n)
    return pl.pallas_call(
        _rmsnorm_kernel,
        out_shape=jax.ShapeDtypeStruct((n, d), dtype),
        grid=(n // tm,),
        in_specs=[pl.BlockSpec((tm, d), lambda i: (i, 0)), pl.BlockSpec((1, d), lambda i: (0, 0))],
        out_specs=pl.BlockSpec((tm, d), lambda i: (i, 0)),
        compiler_params=_cp("parallel"),
    )(x2, g)


def _resmm_kernel(r_ref, *refs):
    y_refs, w_ref, o_ref = refs[:-2], refs[-2], refs[-1]
    acc = r_ref[...]
    k0 = 0
    for y_ref in y_refs:
        kw = y_ref.shape[1]
        acc = acc + _dot(y_ref[...].astype(BF16), w_ref[k0:k0 + kw, :])
        k0 += kw
    o_ref[...] = acc


def _res_matmul(res, ys, w, l):
    n = res.shape[0]
    k, d = w.shape[1:]
    assert sum(y.shape[1] for y in ys) == k
    tm = min(MM_ROWS, n)
    tn = 512
    return pl.pallas_call(
        _resmm_kernel,
        out_shape=jax.ShapeDtypeStruct((n, d), F32),
        grid=(n // tm, d // tn),
        in_specs=[pl.BlockSpec((tm, tn), lambda i, j: (i, j))]
                 + [pl.BlockSpec((tm, y.shape[1]), lambda i, j: (i, 0)) for y in ys]
                 + [pl.BlockSpec((None, k, tn), lambda i, j: (l, 0, j))],
        out_specs=pl.BlockSpec((tm, tn), lambda i, j: (i, j)),
        compiler_params=_cp("parallel", "arbitrary"),
    )(res, *ys, w)


def _rope(x, cos, sin):
    w = x.shape[1]
    first = (_iota(x.shape, 1) & (HD - 1)) < HD // 2
    sw = jnp.where(first, pltpu.roll(x, w - HD // 2, 1), pltpu.roll(x, HD // 2, 1))
    return x * cos + sw * sin


def _rope_kernel(aq_ref, iq_ref, ak_ref, av_ref, sm_ref, cos_ref, sin_ref, aqo, iqo, ko, vo, iko):
    cos = cos_ref[...]
    sin = sin_ref[...]
    aqo[...] = _rope(aq_ref[...], cos, sin)
    iqo[...] = _rope(iq_ref[...], cos, sin)
    ko[...] = _rope(ak_ref[...], cos[:, :128], sin[:, :128])
    vo[...] = av_ref[...]
    iko[...] = _rope(sm_ref[...], cos[:, :128], sin[:, :128])[:, :HD]


def _rope_call(proj, cos, sin, nb, nt_rows):
    n = proj.shape[0]
    tm = min(512, nt_rows)
    nt = nt_rows // tm
    row = lambda b, i: b * nt + i
    return pl.pallas_call(
        _rope_kernel,
        out_shape=(jax.ShapeDtypeStruct((n, GW), F32), jax.ShapeDtypeStruct((n, GW), F32),
                   jax.ShapeDtypeStruct((n, 128), F32), jax.ShapeDtypeStruct((n, 128), F32),
                   jax.ShapeDtypeStruct((n, HD), F32)),
        grid=(nb, nt),
        in_specs=[pl.BlockSpec((tm, GW), lambda b, i: (row(b, i), C_AQ // GW)),
                  pl.BlockSpec((tm, GW), lambda b, i: (row(b, i), C_IQ // GW)),
                  pl.BlockSpec((tm, 128), lambda b, i: (row(b, i), C_AK // 128)),
                  pl.BlockSpec((tm, 128), lambda b, i: (row(b, i), C_AV // 128)),
                  pl.BlockSpec((tm, 128), lambda b, i: (row(b, i), C_SM // 128)),
                  pl.BlockSpec((tm, GW), lambda b, i: (i, 0)),
                  pl.BlockSpec((tm, GW), lambda b, i: (i, 0))],
        out_specs=(pl.BlockSpec((tm, GW), lambda b, i: (row(b, i), 0)),
                   pl.BlockSpec((tm, GW), lambda b, i: (row(b, i), 0)),
                   pl.BlockSpec((tm, 128), lambda b, i: (row(b, i), 0)),
                   pl.BlockSpec((tm, 128), lambda b, i: (row(b, i), 0)),
                   pl.BlockSpec((tm, HD), lambda b, i: (row(b, i), 0))),
        compiler_params=_cp("parallel", "parallel"),
    )(proj, proj, proj, proj, proj, cos, sin)


def _rope_tables(pos):
    half = HD // 2
    inv = ROPE_THETA ** (-jnp.arange(half, dtype=F32) / half)
    ang = pos.astype(F32)[:, None] * inv[None, :]
    cos, sin = jnp.cos(ang), jnp.sin(ang)
    cos64 = jnp.concatenate([cos, cos], axis=-1)
    sin64 = jnp.concatenate([-sin, sin], axis=-1)
    return jnp.tile(cos64, (1, NH)), jnp.tile(sin64, (1, NH))


def _kth_largest(sc, extra, kk):
    kf = jnp.float32(kk)

    def count_ge(c):
        n = jnp.sum(jnp.where(sc >= c, 1.0, 0.0), axis=-1, keepdims=True)
        if extra is not None:
            n = n + jnp.where(extra >= c, 1.0, 0.0)
        return n

    def key_to_f(key):
        bits = key ^ ((key >> 31) & jnp.int32(0x7FFFFFFF))
        return lax.bitcast_convert_type(bits, F32)

    r = sc.shape[0]
    int_min = jnp.int32(-2 ** 31)
    lo = jnp.where(count_ge(jnp.zeros((r, 1), F32)) >= kf, jnp.int32(0), int_min)

    def body(j, lo):
        cand = lo + jnp.left_shift(jnp.int32(1), jnp.int32(30) - j)
        ok = count_ge(key_to_f(cand)) >= kf
        return jnp.where(ok, cand, lo)

    lo = lax.fori_loop(0, 31, body, lo)
    key_neg_inf = jnp.int32(-2 ** 31 + 0x7FFFFF)
    return jnp.where(lo <= key_neg_inf, -jnp.inf, key_to_f(lo))


def _strict_upper_bf16(n):
    return jnp.where(_iota((n, n), 0) < _iota((n, n), 1), 1.0, 0.0).astype(BF16)


def _split_bf16(x):
    hi = x.astype(BF16).astype(F32)
    return hi, x - hi


def _dsa_prompt_kernel(iq_ref, sm_ref, ik_ref, aq_ref, k_ref, v_ref, o_ref, sel_scr, kcat_scr, *, topk, qb, n_ext):
    t_keys = ik_ref.shape[0]
    i = pl.program_id(1)
    per_ext = (t_keys // qb) // n_ext

    @pl.when(i == 0)
    def _():
        hi, lo = _split_bf16(ik_ref[...])
        kcat_scr[...] = jnp.concatenate([hi, lo, hi], axis=1)

    iq = iq_ref[...]
    aq = aq_ref[...] * HD ** -0.5
    wts = sm_ref[:, SM_IW:SM_IW + NH] * IDX_SCALE
    tq = i * qb + _iota((qb, 1), 0)

    def body(ext):
        kcat = kcat_scr[0:ext, :]
        sc = jnp.zeros((qb, ext), F32)
        for h in range(NH):
            hi, lo = _split_bf16(iq[:, h * HD:(h + 1) * HD])
            qk = _dot_nt(jnp.concatenate([hi, hi, lo], axis=1), kcat)
            sc = sc + jnp.maximum(qk, 0.0) * wts[:, h:h + 1]
        causal = _iota((1, ext), 1) <= tq
        sc = jnp.where(causal, sc, -jnp.inf)

        thr = _kth_largest(sc, None, topk)
        gt = sc > thr
        eq = sc == thr
        n_gt = jnp.sum(jnp.where(gt, 1.0, 0.0), axis=-1, keepdims=True)
        n_eq = jnp.sum(jnp.where(eq, 1.0, 0.0), axis=-1, keepdims=True)
        need = jnp.float32(topk) - n_gt
        sel_scr[:, 0:ext] = jnp.where(jnp.logical_and(sc >= thr, causal), 0.0, -jnp.inf)
        tie = jnp.logical_and(n_eq > need, thr > -jnp.inf)

        @pl.when(jnp.max(jnp.where(tie, 1.0, 0.0)) > 0.5)
        def _():
            ut = _strict_upper_bf16(128)
            run = jnp.zeros((qb, 1), F32)
            for c in range(ext // 128):
                sl = slice(c * 128, (c + 1) * 128)
                eqc = jnp.where(eq[:, sl], 1.0, 0.0)
                pref = _dot(eqc.astype(BF16), ut) + run
                keep = jnp.logical_or(gt[:, sl], jnp.logical_and(eq[:, sl], pref < need))
                sel_scr[:, sl] = jnp.where(jnp.logical_and(keep, causal[:, sl]), 0.0, -jnp.inf)
                run = run + jnp.sum(eqc, axis=-1, keepdims=True)

        bias = sel_scr[:, 0:ext]
        for g in range(A_KV):
            kg = k_ref[0:ext, g * HD:(g + 1) * HD]
            vg = v_ref[0:ext, g * HD:(g + 1) * HD]
            for j in range(NH // A_KV):
                h = g * (NH // A_KV) + j
                s = _dot_nt(aq[:, h * HD:(h + 1) * HD], kg) + bias
                m = jnp.max(s, axis=-1, keepdims=True)
                p = jnp.exp(s - m)
                l = jnp.sum(p, axis=-1, keepdims=True)
                o_ref[:, h * HD:(h + 1) * HD] = _dot(p, vg) / l

    for j in range(n_ext):
        pl.when(i // per_ext == j)(functools.partial(body, (j + 1) * (t_keys // n_ext)))


def _dsa_prompt(iq_r, proj, ik_r, aq_r, k_r, v_r, nb, t):
    qb = min(128, t)
    nq = t // qb
    n_ext = min(4, nq)
    assert nq % n_ext == 0
    topk = min(TOPK_MAX, t // 4)
    n = nb * t
    row = lambda b, i: b * nq + i
    return pl.pallas_call(
        functools.partial(_dsa_prompt_kernel, topk=topk, qb=qb, n_ext=n_ext),
        out_shape=jax.ShapeDtypeStruct((n, GW), F32),
        grid=(nb, nq),
        in_specs=[pl.BlockSpec((qb, GW), lambda b, i: (row(b, i), 0)),
                  pl.BlockSpec((qb, 128), lambda b, i: (row(b, i), C_SM // 128)),
                  pl.BlockSpec((t, HD), lambda b, i: (b, 0)),
                  pl.BlockSpec((qb, GW), lambda b, i: (row(b, i), 0)),
                  pl.BlockSpec((t, 128), lambda b, i: (b, 0)),
                  pl.BlockSpec((t, 128), lambda b, i: (b, 0))],
        out_specs=pl.BlockSpec((qb, GW), lambda b, i: (row(b, i), 0)),
        scratch_shapes=[pltpu.VMEM((qb, t), F32), pltpu.VMEM((t, 3 * HD), F32)],
        compiler_params=_cp("parallel", "arbitrary"),
    )(iq_r, proj, ik_r, aq_r, k_r, v_r)


def _dsa_sample_score_kernel(*refs, n_pages):
    _, iq_ref, w_ref, ikn_ref = refs[:4]
    pages = refs[4:4 + n_pages]
    sc_ref, sn_ref = refs[4 + n_pages:]
    iq = iq_ref[0]
    w = w_ref[0] * IDX_SCALE
    for c in range(n_pages):
        qk = _dot(iq, pages[c][0], HI)
        sc_ref[0, :, c * PAGE:(c + 1) * PAGE] = jnp.sum(jnp.maximum(qk, 0.0) * w, axis=0, keepdims=True)
    qkn = jnp.sum(iq * ikn_ref[0], axis=-1, keepdims=True)
    sn = jnp.sum(jnp.maximum(qkn, 0.0) * w, axis=0, keepdims=True)
    sn_ref[0] = jnp.broadcast_to(sn, (1, 128))


def _dsa_sample_select_kernel(sc_ref, sn_ref, sel_ref, seln_ref, *, topk):
    sc = sc_ref[...]
    sn = sn_ref[:, 0:1]
    bd, s_keys = sc.shape
    thr = _kth_largest(sc, sn, topk)
    gt = sc > thr
    eq = sc == thr
    n_gt = jnp.sum(jnp.where(gt, 1.0, 0.0), axis=-1, keepdims=True) + jnp.where(sn > thr, 1.0, 0.0)
    need = jnp.float32(topk) - n_gt
    ut = _strict_upper_bf16(PAGE)
    run = jnp.zeros((bd, 1), F32)
    for c in range(s_keys // PAGE):
        sl = slice(c * PAGE, (c + 1) * PAGE)
        eqc = jnp.where(eq[:, sl], 1.0, 0.0)
        pref = _dot(eqc.astype(BF16), ut) + run
        keep = jnp.logical_or(gt[:, sl], jnp.logical_and(eq[:, sl], pref < need))
        sel_ref[:, sl] = jnp.where(keep, 1.0, 0.0)
        run = run + jnp.sum(eqc, axis=-1, keepdims=True)
    sel_new = jnp.logical_or(sn > thr, jnp.logical_and(sn == thr, run < need))
    seln_ref[...] = jnp.broadcast_to(jnp.where(sel_new, 1.0, 0.0), seln_ref.shape)


def _dsa_sample_attn_kernel(*refs, n_pages):
    _, aq_ref, kn_ref, vn_ref, sel_ref, seln_ref = refs[:6]
    kpages = refs[6:6 + n_pages]
    vpages = refs[6 + n_pages:6 + 2 * n_pages]
    o_ref, k_scr, v_scr = refs[6 + 2 * n_pages:]
    for c in range(n_pages):
        k_scr[:, c * PAGE:(c + 1) * PAGE] = kpages[c][0]
        v_scr[:, c * PAGE:(c + 1) * PAGE] = vpages[c][0]
    sel = sel_ref[0] > 0.5
    sel_new = seln_ref[0][:, 0:1] > 0.5
    aq = aq_ref[0]
    kn = kn_ref[0]
    vn = vn_ref[0]
    hpg = NH // A_KV
    for g in range(A_KV):
        qg = aq[g * hpg:(g + 1) * hpg, :]
        s = _dot(qg, k_scr[g * HD:(g + 1) * HD, :]) * HD ** -0.5
        s = jnp.where(sel, s, -jnp.inf)
        s_new = jnp.sum(qg * kn[:, g * HD:(g + 1) * HD], axis=-1, keepdims=True) * HD ** -0.5
        s_new = jnp.where(sel_new, s_new, -jnp.inf)
        m = jnp.maximum(jnp.max(s, axis=-1, keepdims=True), s_new)
        pr = jnp.exp(s - m)
        pn = jnp.exp(s_new - m)
        l = jnp.sum(pr, axis=-1, keepdims=True) + pn
        o = _dot_nt(pr, v_scr[g * HD:(g + 1) * HD, :]) + pn * vn[:, g * HD:(g + 1) * HD]
        o_ref[0, g * hpg:(g + 1) * hpg, :] = o / l


def _dsa_sample(layer, page_table, iq_r, iw, ik_r, aq_r, k_r, v_r, cki, ck, cv, n_pool):
    bd, n_pages = page_table.shape
    past = n_pages * PAGE
    topk = min(TOPK_MAX, (past + 1) // 4)
    base = layer * n_pool
    per_b = lambda b, pt: (b, 0, 0)

    def page_specs(width):
        return [pl.BlockSpec((1, width, PAGE), lambda b, pt, c=c: (base + pt[b, c], 0, 0)) for c in range(n_pages)]

    sc, sn = pl.pallas_call(
        functools.partial(_dsa_sample_score_kernel, n_pages=n_pages),
        out_shape=(jax.ShapeDtypeStruct((bd, 1, past), F32), jax.ShapeDtypeStruct((bd, 1, 128), F32)),
        grid_spec=pltpu.PrefetchScalarGridSpec(
            num_scalar_prefetch=1,
            grid=(bd,),
            in_specs=[pl.BlockSpec((1, NH, HD), per_b), pl.BlockSpec((1, NH, 1), per_b),
                      pl.BlockSpec((1, 1, HD), per_b)] + page_specs(HD),
            out_specs=(pl.BlockSpec((1, 1, past), per_b), pl.BlockSpec((1, 1, 128), per_b))),
        compiler_params=_cp("arbitrary"),
    )(page_table, iq_r.reshape(bd, NH, HD), iw.reshape(bd, NH, 1), ik_r.reshape(bd, 1, HD), *([cki] * n_pages))

    sel, seln = pl.pallas_call(
        functools.partial(_dsa_sample_select_kernel, topk=topk),
        out_shape=(jax.ShapeDtypeStruct((bd, past), F32), jax.ShapeDtypeStruct((bd, 128), F32)),
    )(sc.reshape(bd, past), sn.reshape(bd, 128))

    out = pl.pallas_call(
        functools.partial(_dsa_sample_attn_kernel, n_pages=n_pages),
        out_shape=jax.ShapeDtypeStruct((bd, NH, HD), F32),
        grid_spec=pltpu.PrefetchScalarGridSpec(
            num_scalar_prefetch=1,
            grid=(bd,),
            in_specs=[pl.BlockSpec((1, NH, HD), per_b), pl.BlockSpec((1, 1, 128), per_b),
                      pl.BlockSpec((1, 1, 128), per_b), pl.BlockSpec((1, 1, past), per_b),
                      pl.BlockSpec((1, 1, 128), per_b)] + page_specs(128) + page_specs(128),
            out_specs=pl.BlockSpec((1, NH, HD), per_b),
            scratch_shapes=[pltpu.VMEM((128, past), F32), pltpu.VMEM((128, past), F32)]),
        compiler_params=_cp("arbitrary"),
    )(page_table, aq_r.reshape(bd, NH, HD), k_r.reshape(bd, 1, 128), v_r.reshape(bd, 1, 128),
      sel.reshape(bd, 1, past), seln.reshape(bd, 1, 128), *([ck] * n_pages), *([cv] * n_pages))
    return out.reshape(bd, GW)


def _log_sigmoid(x):
    return jnp.minimum(x, 0.0) - jnp.log(1.0 + jnp.exp(-jnp.abs(x)))


MLSTM_ROWS = 2 * CHUNK


def _mlstm_prompt_kernel(m_ref, sm_ref, gb_ref, nw_ref, y_ref, c_ref, n_ref, mm_ref):
    L = CHUNK

    @pl.when(pl.program_id(1) == 0)
    def _():
        c_ref[...] = jnp.zeros(c_ref.shape, F32)
        n_ref[...] = jnp.zeros(n_ref.shape, F32)
        mm_ref[...] = jnp.zeros(mm_ref.shape, F32)

    smb = sm_ref[...] + gb_ref[...]
    smt = smb.T
    ig_all = smb[:, SM_MI:SM_MI + NH]
    lf_all = _log_sigmoid(smb[:, SM_MF:SM_MF + NH])
    igt_all = smt[SM_MI:SM_MI + NH, :]
    lft_all = _log_sigmoid(smt[SM_MF:SM_MF + NH, :])
    ri, ci = _iota((L, L), 0), _iota((L, L), 1)
    tril = jnp.where(ci <= ri, 1.0, 0.0)
    triu = jnp.where(ci >= ri, 1.0, 0.0)
    same, causal, _ = _bd_masks(L)
    tile = lambda z: jnp.concatenate([z] * HG, axis=0)
    msk = lambda z: jnp.where(same, z, 0.0)
    stack_cols = lambda z, g: jnp.concatenate([z[:, g * HG + j:g * HG + j + 1] for j in range(HG)], axis=0)
    stack_rows = lambda z, g: jnp.concatenate([z[g * HG + j:g * HG + j + 1, :] for j in range(HG)], axis=1)
    for cc in range(MLSTM_ROWS // L):
        rows = slice(cc * L, (cc + 1) * L)
        ig = ig_all[rows]
        bcs = _dot(tril, lf_all[rows], HI)
        bcst = _dot(lft_all[:, rows], triu, HI)
        igt = igt_all[:, rows]
        for g in range(NH // HG):
            ln = slice(g * BD, (g + 1) * BD)
            heads = range(g * HG, (g + 1) * HG)
            qexp = msk(tile(m_ref[rows, ln]))
            kt = tile(m_ref[rows, GW + g * BD:GW + (g + 1) * BD] * HD ** -0.5)
            kexp = msk(kt)
            vexp = msk(tile(m_ref[rows, 2 * GW + g * BD:2 * GW + (g + 1) * BD]))
            og = m_ref[rows, 3 * GW + g * BD:3 * GW + (g + 1) * BD]
            bcol = stack_cols(bcs, g)
            icol = stack_cols(ig, g)
            brow = stack_rows(bcst, g)
            irow = stack_rows(igt, g)
            mprev = [mm_ref[0, :, h:h + 1] for h in heads]
            bl = [bcs[L - 1:L, h:h + 1] for h in heads]
            percol = lambda zs: jnp.concatenate([jnp.broadcast_to(z, (L, 1)) for z in zs], axis=0)
            perrow = lambda zs: jnp.concatenate([jnp.broadcast_to(z, (1, HD)) for z in zs], axis=1)
            mprev_col = percol(mprev)
            dmat = jnp.where(causal, bcol - brow + irow, -jnp.inf)
            inter = bcol + mprev_col
            mj = jnp.maximum(inter, jnp.max(dmat, axis=-1, keepdims=True))
            s = _dot_nt(qexp, kt) * jnp.exp(dmat - mj)
            iw = jnp.exp(inter - mj)
            cbd = c_ref[0, g]
            nrow = n_ref[0, :, ln]
            num = _dot(s, vexp) + iw * _dot_nt(qexp, cbd)
            den = jnp.sum(s, axis=-1, keepdims=True) + iw * jnp.sum(qexp * nrow, axis=-1, keepdims=True)
            hc = num / jnp.maximum(jnp.abs(den), jnp.exp(-mj))
            wl = percol(bl) - bcol + icol
            m_new = [jnp.maximum(bl[j] + mprev[j], jnp.max(wl[j * L:(j + 1) * L], axis=0, keepdims=True))
                     for j in range(HG)]
            dec = [jnp.exp(bl[j] + mprev[j] - m_new[j]) for j in range(HG)]
            ws = jnp.exp(wl - percol(m_new))
            c_ref[0, g] = percol(dec) * cbd + _dot_tn(vexp * ws, kexp)
            n_ref[0, :, ln] = perrow(dec) * nrow + jnp.sum(kexp * ws, axis=0, keepdims=True)
            for j, h in enumerate(heads):
                mm_ref[0, :, h:h + 1] = m_new[j]
            hn = hc * lax.rsqrt(jnp.sum(hc * hc, axis=-1, keepdims=True) * (1.0 / HD) + NORM_EPS)
            hn = sum(hn[j * L:(j + 1) * L] for j in range(HG))
            y_ref[rows, ln] = hn * nw_ref[:, ln] * _sigmoid(og)


def _gate_bias_row(b_i, b_f):
    z = lambda n: jnp.zeros((1, n), F32)
    return jnp.concatenate([z(SM_MI), b_i, b_f, z(128 - SM_MF - NH)], axis=1)


def _mlstm_prompt(proj, b_i, b_f, nw, nb, t):
    tm = MLSTM_ROWS
    nc = t // tm
    n = nb * t
    ng = NH // HG
    row = lambda b, c: b * nc + c
    y, cbd, nrow, m = pl.pallas_call(
        _mlstm_prompt_kernel,
        out_shape=(jax.ShapeDtypeStruct((n, GW), F32), jax.ShapeDtypeStruct((nb, ng, BD, BD), F32),
                   jax.ShapeDtypeStruct((nb, 1, GW), F32), jax.ShapeDtypeStruct((nb, 1, NH), F32)),
        grid=(nb, nc),
        in_specs=[pl.BlockSpec((tm, 4 * GW), lambda b, c: (row(b, c), 0)),
                  pl.BlockSpec((tm, 128), lambda b, c: (row(b, c), C_SM // 128)),
                  pl.BlockSpec((1, 128), lambda b, c: (0, 0)),
                  pl.BlockSpec((1, GW), lambda b, c: (0, 0))],
        out_specs=(pl.BlockSpec((tm, GW), lambda b, c: (row(b, c), 0)),
                   pl.BlockSpec((1, ng, BD, BD), lambda b, c: (b, 0, 0, 0)),
                   pl.BlockSpec((1, 1, GW), lambda b, c: (b, 0, 0)),
                   pl.BlockSpec((1, 1, NH), lambda b, c: (b, 0, 0))),
        compiler_params=_cp("parallel", "arbitrary"),
    )(proj, proj, _gate_bias_row(b_i, b_f), nw)
    return y, _bd_diag(cbd), nrow.reshape(nb, NH, HD), m


def _head_expanders():
    diag = (_iota((NH * HD, HD), 0) % HD) == _iota((NH * HD, HD), 1)

    def rep(z):
        return jnp.concatenate([jnp.broadcast_to(z[h:h + 1, :], (HD, z.shape[1])) for h in range(NH)], axis=0)

    def fold(col):
        m = jnp.where(diag, col, 0.0)
        return jnp.concatenate([jnp.sum(m[h * HD:(h + 1) * HD], axis=0, keepdims=True) for h in range(NH)], axis=0)

    return rep, fold, diag


def _mlstm_step_kernel(x_ref, gt_ref, gb_ref, nw_ref, c_ref, n_ref, mm_ref, y_ref, co_ref, no_ref, mo_ref):
    x = x_ref[0]
    q, k, v, og = x[0:NH], x[NH:2 * NH] * HD ** -0.5, x[2 * NH:3 * NH], x[3 * NH:4 * NH]
    gates = gt_ref[0] + gb_ref[...]
    ig = gates[:, 0:1]
    rep, fold, diag = _head_expanders()
    c0 = c_ref[0].reshape(NH * HD, HD)
    n0 = n_ref[0]
    inter = _log_sigmoid(gates[:, 1:2]) + mm_ref[0]
    mj = jnp.maximum(inter, ig)
    s = jnp.sum(q * k, axis=-1, keepdims=True) * jnp.exp(ig - mj)
    iw = jnp.exp(inter - mj)
    ws = jnp.exp(ig - mj)
    den = s + iw * jnp.sum(n0 * q, axis=-1, keepdims=True)
    inv = 1.0 / jnp.maximum(jnp.abs(den), jnp.exp(-mj))
    lane = _iota((NH, 128), 1)
    per_head = jnp.where(lane == 0, s, jnp.where(lane == 1, iw, jnp.where(lane == 2, ws, inv)))
    ph = rep(per_head)
    s_c, iw_c, ws_c, inv_c = ph[:, 0:1], ph[:, 1:2], ph[:, 2:3], ph[:, 3:4]
    cq = jnp.sum(c0 * rep(q), axis=1, keepdims=True)
    vcol = jnp.sum(jnp.where(diag, rep(v), 0.0), axis=1, keepdims=True)
    hcol = (s_c * vcol + iw_c * cq) * inv_c
    co_ref[0] = (iw_c * c0 + (ws_c * vcol) * rep(k)).reshape(NH, HD, HD)
    no_ref[0] = iw * n0 + ws * k
    mo_ref[0] = mj
    h8 = fold(hcol)
    hn = h8 * lax.rsqrt(jnp.mean(h8 * h8, axis=-1, keepdims=True) + NORM_EPS)
    y_ref[0] = hn * nw_ref[...] * _sigmoid(og)


def _mlstm_step(proj, b_i, b_f, nw, c_all, l, n0, m0):
    bd = proj.shape[0]
    per_b3 = lambda b: (b, 0, 0)
    full = lambda b: (0, 0)
    x = proj[:, :4 * GW].reshape(bd, 4 * NH, HD)
    gates = proj[:, C_SM + SM_MI:C_SM + SM_MI + 2 * NH].reshape(bd, 2, NH).transpose(0, 2, 1)
    gate_bias = jnp.concatenate([b_i, b_f], axis=0).T
    y, c1, n1, m1 = pl.pallas_call(
        _mlstm_step_kernel,
        out_shape=(jax.ShapeDtypeStruct((bd, NH, HD), F32), jax.ShapeDtypeStruct((bd, NH, HD, HD), F32),
                   jax.ShapeDtypeStruct((bd, NH, HD), F32), jax.ShapeDtypeStruct((bd, NH, 1), F32)),
        grid=(bd,),
        in_specs=[pl.BlockSpec((1, 4 * NH, HD), per_b3),
                  pl.BlockSpec((1, NH, 2), per_b3),
                  pl.BlockSpec((NH, 2), full),
                  pl.BlockSpec((NH, HD), full),
                  pl.BlockSpec((1, NH, HD, HD), lambda b: (l * bd + b, 0, 0, 0)),
                  pl.BlockSpec((1, NH, HD), per_b3),
                  pl.BlockSpec((1, NH, 1), per_b3)],
        out_specs=(pl.BlockSpec((1, NH, HD), per_b3),
                   pl.BlockSpec((1, NH, HD, HD), lambda b: (b, 0, 0, 0)),
                   pl.BlockSpec((1, NH, HD), per_b3),
                   pl.BlockSpec((1, NH, 1), per_b3)),
        compiler_params=_cp("parallel"),
    )(x, gates, gate_bias, nw.reshape(NH, HD), c_all.reshape((-1,) + c_all.shape[2:]), n0, m0.reshape(bd, NH, 1))
    return y, c1, n1, m1


def _rwkv_prep_kernel(x_ref, prev_ref, mu_ref, w0_ref, ww2_ref, a0_ref, wa2_ref, wg2_ref, kk_ref, ka_ref,
                      r_o, lw_o, k_o, v_o, kk_o, a_o, g_o, carry_scr, *, seq):
    x = x_ref[...]
    tm = x.shape[0]
    if seq:
        first = jnp.where(pl.program_id(1) == 0, prev_ref[0], carry_scr[...])
        xprev = jnp.where(_iota((tm, 1), 0) == 0, first, pltpu.roll(x, 1, 0))
        carry_scr[...] = x[tm - 1:tm, :]
    else:
        xprev = prev_ref[...]
    xm = x + (xprev - x) * mu_ref[...]
    r = xm[:, 0:GW]
    kx = xm[:, GW:2 * GW]
    v = xm[:, 2 * GW:3 * GW]
    xw = xm[:, 3 * GW:3 * GW + 64]
    xa = xm[:, 3 * GW + 64:3 * GW + 128]
    xg = xm[:, 3 * GW + 128:R_IN]
    w = -_softplus(-(w0_ref[...] + _dot(jnp.tanh(xw), ww2_ref[...]))) - 0.5
    a = _sigmoid(a0_ref[...] + _dot(xa, wa2_ref[...]))
    r_o[...] = r
    lw_o[...] = -jnp.exp(w)
    v_o[...] = v
    a_o[...] = a
    g_o[...] = _dot(_sigmoid(xg), wg2_ref[...])
    k_o[...] = kx * (1.0 + (a - 1.0) * ka_ref[...])
    kk = kx * kk_ref[...]
    for h in range(NH):
        kh = kk[:, h * HD:(h + 1) * HD]
        nrm = jnp.sqrt(jnp.sum(kh * kh, axis=-1, keepdims=True))
        kk_o[:, h * HD:(h + 1) * HD] = kh / jnp.maximum(nrm, 1e-12)


def _rwkv_prep(proj, prev, mu, w0, ww2, a0, wa2, wg2, k_k, k_a, nb, t):
    n = nb * t
    seq = t > 1
    full = lambda *_: (0, 0)
    if seq:
        tm = min(256, t)
        nt = t // tm
        grid = (nb, nt)
        xmap = lambda b, i: (b * nt + i, C_RIN // R_IN)
        pspec = pl.BlockSpec((1, 1, R_IN), lambda b, i: (b, 0, 0))
        omap = lambda b, i: (b * nt + i, 0)
        sem = ("parallel", "arbitrary")
    else:
        tm = n
        grid = (1,)
        xmap = lambda i: (0, C_RIN // R_IN)
        pspec = pl.BlockSpec((tm, R_IN), lambda i: (0, 0))
        omap = lambda i: (0, 0)
        sem = ("arbitrary",)
    wspecs = [pl.BlockSpec((1, R_IN), full), pl.BlockSpec((1, GW), full), pl.BlockSpec((64, GW), full),
              pl.BlockSpec((1, GW), full), pl.BlockSpec((64, GW), full), pl.BlockSpec((128, GW), full),
              pl.BlockSpec((1, GW), full), pl.BlockSpec((1, GW), full)]
    return pl.pallas_call(
        functools.partial(_rwkv_prep_kernel, seq=seq),
        out_shape=tuple(jax.ShapeDtypeStruct((n, GW), F32) for _ in range(7)),
        grid=grid,
        in_specs=[pl.BlockSpec((tm, R_IN), xmap), pspec] + wspecs,
        out_specs=tuple(pl.BlockSpec((tm, GW), omap) for _ in range(7)),
        scratch_shapes=[pltpu.VMEM((1, R_IN), F32)],
        compiler_params=_cp(*sem),
    )(proj, prev, mu, w0, ww2, a0, wa2, wg2, k_k, k_a)


def _rwkv_scan_kernel(r_ref, lw_ref, k_ref, v_ref, kk_ref, a_ref, g_ref, rk_ref, lnw_ref, lnb_ref, y_ref, s_ref):
    C = CHUNK

    @pl.when(pl.program_id(1) == 0)
    def _():
        s_ref[...] = jnp.zeros(s_ref.shape, F32)

    ri, ci = _iota((C, C), 0), _iota((C, C), 1)
    same, incl, strict = _bd_masks(C)
    row, col = _iota((BD, BD), 0), _iota((BD, BD), 1)
    eye = jnp.where(row == col, 1.0, 0.0)
    strict_incl = jnp.concatenate([strict, incl], axis=0)
    tile = lambda z: jnp.concatenate([z] * HG, axis=0)
    msk = lambda z: jnp.where(same, z, 0.0)
    fold = lambda z: sum(z[j * C:(j + 1) * C] for j in range(HG))
    n_seq, n_grp = r_ref.shape[0], NH // HG
    s_old = [[s_ref[bb, g] for g in range(n_grp)] for bb in range(n_seq)]
    s_new, y_new = {}, {}
    for bb in range(n_seq):
        lw = lw_ref[bb]
        cs = _dot(jnp.where(ci <= ri, 1.0, 0.0), lw, HI)
        gam = jnp.exp(cs)
        ginv = jnp.exp(-cs)
        r = r_ref[bb]
        k = k_ref[bb]
        v = v_ref[bb]
        kk = kk_ref[bb]
        at = -kk * jnp.exp(cs - lw)
        bt = kk * a_ref[bb] * ginv
        kt = k * ginv
        rt = r * gam
        glast = gam[C - 1:C, :]
        bonus_in = r * k * rk_ref[...]
        for g in range(NH // HG):
            ln = slice(g * BD, (g + 1) * BD)
            btl, ktl = tile(bt[:, ln]), tile(kt[:, ln])
            vexp = msk(tile(v[:, ln]))
            ar = jnp.concatenate([msk(tile(at[:, ln])), msk(tile(rt[:, ln]))], axis=0)
            sbd = s_old[bb][g]
            gb = _dot_nt(ar, btl)
            gk = _dot_nt(ar, ktl)
            gs = _dot_nt(ar, sbd)
            n_ab = jnp.where(strict, gb[:BD], 0.0)
            x = eye + n_ab
            pm = _dot(n_ab, n_ab)
            for j in range(5):
                xn = x + _dot(x, pm)
                if j < 4:
                    pm = _dot(pm, pm)
                x = xn
            akv = _dot(jnp.where(strict_incl, gk, 0.0), vexp)
            u = _dot(x, gs[:BD] + akv[:BD])
            o = gs[BD:] + _dot(jnp.where(incl, gb[BD:], 0.0), u) + akv[BD:]
            gl = glast[:, ln]
            uv = jnp.concatenate([u, vexp], axis=0)
            bk = jnp.concatenate([msk(btl * gl), msk(ktl * gl)], axis=0)
            s_new[bb, g] = sbd * gl + _dot_tn(uv, bk)
            mean = jnp.sum(o, axis=-1, keepdims=True) * (1.0 / HD)
            dev = msk(o - mean)
            var = jnp.sum(dev * dev, axis=-1, keepdims=True) * (1.0 / HD)
            on = fold(dev * lax.rsqrt(var + R_LN_EPS)) * lnw_ref[:, ln] + lnb_ref[:, ln]
            bonus = fold(jnp.sum(msk(tile(bonus_in[:, ln])), axis=-1, keepdims=True) * vexp)
            y_new[bb, g] = (on + bonus) * g_ref[bb, :, ln]
    for (bb, g), val in s_new.items():
        s_ref[bb, g] = val
        y_ref[bb, :, g * BD:(g + 1) * BD] = y_new[bb, g]


def _rwkv_scan(rs, rk, lnw, lnb, nb, t):
    nc = t // CHUNK
    ng = NH // HG
    sb = RWKV_SEQS if nb % RWKV_SEQS == 0 else 1
    rowmap = lambda b, c: (b, c, 0)
    full = lambda b, c: (0, 0)
    y, sbd = pl.pallas_call(
        _rwkv_scan_kernel,
        out_shape=(jax.ShapeDtypeStruct((nb, t, GW), F32), jax.ShapeDtypeStruct((nb, ng, BD, BD), F32)),
        grid=(nb // sb, nc),
        in_specs=[pl.BlockSpec((sb, CHUNK, GW), rowmap)] * 7 + [pl.BlockSpec((1, GW), full)] * 3,
        out_specs=(pl.BlockSpec((sb, CHUNK, GW), rowmap), pl.BlockSpec((sb, ng, BD, BD), lambda b, c: (b, 0, 0, 0))),
        compiler_params=_cp("parallel", "arbitrary"),
    )(*(z.reshape(nb, t, GW) for z in rs), rk, lnw, lnb)
    return y.reshape(nb * t, GW), _bd_diag(sbd)


def _rwkv_step_kernel(r_ref, lw_ref, k_ref, v_ref, kk_ref, a_ref, g_ref, rk_ref, lnw_ref, lnb_ref, s_ref,
                      y_ref, so_ref):
    r, lw, k, v, kk, a, g = (ref[0] for ref in (r_ref, lw_ref, k_ref, v_ref, kk_ref, a_ref, g_ref))
    rep, fold, diag = _head_expanders()
    s0 = s_ref[0].reshape(NH * HD, HD)
    kk_rep = rep(kk)
    sk = jnp.sum(s0 * kk_rep, axis=1, keepdims=True)
    vcol = jnp.sum(jnp.where(diag, rep(v), 0.0), axis=1, keepdims=True)
    s1 = s0 * rep(jnp.exp(lw)) - sk * rep(kk * a) + vcol * rep(k)
    so_ref[0] = s1.reshape(NH, HD, HD)
    ocol = jnp.sum(s1 * rep(r), axis=1, keepdims=True)
    o = fold(ocol)
    mean = jnp.mean(o, axis=-1, keepdims=True)
    var = jnp.mean(jnp.square(o - mean), axis=-1, keepdims=True)
    on = (o - mean) * lax.rsqrt(var + R_LN_EPS) * lnw_ref[...] + lnb_ref[...]
    bonus = jnp.sum(r * k * rk_ref[...], axis=-1, keepdims=True) * v
    y_ref[0] = (on + bonus) * g


def _rwkv_step(rs, rk, lnw, lnb, s_all, l):
    bd = s_all.shape[1]
    per_b = lambda b: (b, 0, 0)
    full = lambda b: (0, 0)
    return pl.pallas_call(
        _rwkv_step_kernel,
        out_shape=(jax.ShapeDtypeStruct((bd, NH, HD), F32), jax.ShapeDtypeStruct((bd, NH, HD, HD), F32)),
        grid=(bd,),
        in_specs=[pl.BlockSpec((1, NH, HD), per_b)] * 7 + [pl.BlockSpec((NH, HD), full)] * 3
                 + [pl.BlockSpec((1, NH, HD, HD), lambda b: (l * bd + b, 0, 0, 0))],
        out_specs=(pl.BlockSpec((1, NH, HD), per_b), pl.BlockSpec((1, NH, HD, HD), lambda b: (b, 0, 0, 0))),
        compiler_params=_cp("parallel"),
    )(*(z.reshape(bd, NH, HD) for z in rs), *(z.reshape(NH, HD) for z in (rk, lnw, lnb)),
      s_all.reshape((-1,) + s_all.shape[2:]))


def _gelu_tanh(x):
    return 0.5 * x * (1.0 + jnp.tanh(math.sqrt(2.0 / math.pi) * (x + 0.044715 * (x * x * x))))


def _s5_kernel(u_ref, bre_ref, bim_ref, lre_ref, lim_ref, cre_ref, cim_ref, d_ref, wg_ref, bg_ref, h0r_ref, h0i_ref,
               y_ref, hr_ref, hi_ref, hre_scr, him_scr, *, nb, tb):
    @pl.when(pl.program_id(0) == 0)
    def _():
        hr_ref[...] = h0r_ref[...]
        hi_ref[...] = h0i_ref[...]

    u = u_ref[...]
    hre_scr[...] = _dot(u, bre_ref[...])
    him_scr[...] = _dot(u, bim_ref[...])
    lr = lre_ref[...]
    li = lim_ref[...]

    def body(t, carry):
        hr, hi = carry
        rows = pl.ds(pl.multiple_of(t * nb, nb), nb)
        nr = lr * hr - li * hi + hre_scr[rows, :]
        ni = lr * hi + li * hr + him_scr[rows, :]
        hre_scr[rows, :] = nr
        him_scr[rows, :] = ni
        return nr, ni

    hr, hi = lax.fori_loop(0, tb, body, (hr_ref[...], hi_ref[...]))
    hr_ref[...] = hr
    hi_ref[...] = hi
    y = _dot(hre_scr[...], cre_ref[...]) - _dot(him_scr[...], cim_ref[...]) + d_ref[...] * u
    y = _gelu_tanh(y)
    y_ref[...] = y * _sigmoid(_dot(y, wg_ref[...]) + bg_ref[...])


def _s5(u_tm, mats, h0r, h0i, nb, t):
    bre, bim, lre, lim, cre, cim, d, wg, bg = mats
    tb = min(64, t)
    full = lambda i: (0, 0)
    return pl.pallas_call(
        functools.partial(_s5_kernel, nb=nb, tb=tb),
        out_shape=(jax.ShapeDtypeStruct((t * nb, GW), F32), jax.ShapeDtypeStruct((nb, S5_W), F32),
                   jax.ShapeDtypeStruct((nb, S5_W), F32)),
        grid=(t // tb,),
        in_specs=[pl.BlockSpec((tb * nb, GW), lambda i: (i, 0)),
                  pl.BlockSpec((GW, S5_W), full), pl.BlockSpec((GW, S5_W), full),
                  pl.BlockSpec((1, S5_W), full), pl.BlockSpec((1, S5_W), full),
                  pl.BlockSpec((S5_W, GW), full), pl.BlockSpec((S5_W, GW), full),
                  pl.BlockSpec((1, GW), full), pl.BlockSpec((GW, GW), full), pl.BlockSpec((1, GW), full),
                  pl.BlockSpec((nb, S5_W), full), pl.BlockSpec((nb, S5_W), full)],
        out_specs=(pl.BlockSpec((tb * nb, GW), lambda i: (i, 0)),
                   pl.BlockSpec((nb, S5_W), full), pl.BlockSpec((nb, S5_W), full)),
        scratch_shapes=[pltpu.VMEM((tb * nb, S5_W), F32), pltpu.VMEM((tb * nb, S5_W), F32)],
        compiler_params=_cp("arbitrary"),
    )(u_tm, bre, bim, lre, lim, cre, cim, d, wg, bg, h0r, h0i)


def _s5_mats(a_re, a_im, b_re, b_im, c_re, c_im, d_skip, log_dt, w_glu, b_glu):
    dt = jnp.exp(log_dt)
    mag = jnp.exp(a_re * dt)
    lb_re, lb_im = mag * jnp.cos(a_im * dt), mag * jnp.sin(a_im * dt)
    den = a_re * a_re + a_im * a_im
    f_re = ((lb_re - 1.0) * a_re + lb_im * a_im) / den
    f_im = (lb_im * a_re - (lb_re - 1.0) * a_im) / den
    bb_re = f_re[..., None] * b_re - f_im[..., None] * b_im
    bb_im = f_re[..., None] * b_im + f_im[..., None] * b_re
    eye = jnp.eye(S5_G, dtype=F32)
    bd = lambda bb: jnp.einsum('gpc,gh->gchp', bb, eye).reshape(GW, S5_W)
    cd = lambda cc: jnp.einsum('gcp,gh->gphc', cc, eye).reshape(S5_W, GW)
    return (bd(bb_re), bd(bb_im), lb_re.reshape(1, S5_W), lb_im.reshape(1, S5_W), cd(c_re), cd(c_im),
            d_skip.reshape(1, GW), w_glu, b_glu.reshape(1, GW))


def _ffn_up_kernel(h_ref, wa_ref, wb_ref, cw_ref, cb_ref, s0_ref, s1_ref, y_ref, a_ref, carry_scr, *, seq):
    def gate(a, a1, a2, b):
        c = cb_ref[...] + a2 * cw_ref[0:1, :] + a1 * cw_ref[1:2, :] + a * cw_ref[2:3, :]
        return (c * _sigmoid(c) * b).astype(BF16)

    if not seq:
        h = h_ref[...]
        a = _dot(h, wa_ref[...])
        a_ref[...] = a
        y_ref[...] = gate(a, s1_ref[...], s0_ref[...], _dot(h, wb_ref[...]))
        return

    @pl.when(pl.program_id(2) == 0)
    def _():
        carry_scr[...] = s0_ref[0]

    hm = h_ref.shape[0] // FFN_PARTS
    rowid = _iota((hm, 1), 0)
    c0 = carry_scr[0:1, :]
    c1 = carry_scr[1:2, :]
    for p in range(FFN_PARTS):
        rows = slice(p * hm, (p + 1) * hm)
        h = h_ref[rows, :]
        a = _dot(h, wa_ref[...])
        b = _dot(h, wb_ref[...])
        a1 = jnp.where(rowid == 0, c1, pltpu.roll(a, 1, 0))
        a2 = jnp.where(rowid == 0, c0, jnp.where(rowid == 1, c1, pltpu.roll(a, 2, 0)))
        y_ref[rows, :] = gate(a, a1, a2, b)
        c0 = a[hm - 2:hm - 1, :]
        c1 = a[hm - 1:hm, :]
    last2 = jnp.concatenate([c0, c1], axis=0)
    carry_scr[...] = last2
    a_ref[0] = last2


def _ffn_up(h2, w_up, l, cw, cb, st, nb, t):
    n, d = h2.shape
    dff = w_up.shape[2] // 2
    tn = 512
    nj = dff // tn
    if t > 1:
        tm = min(MM_ROWS, t)
        nt = t // tm
        y, fc = pl.pallas_call(
            functools.partial(_ffn_up_kernel, seq=True),
            out_shape=(jax.ShapeDtypeStruct((n, dff), BF16), jax.ShapeDtypeStruct((nb, 2, dff), F32)),
            grid=(nb, nj, nt),
            in_specs=[pl.BlockSpec((tm, d), lambda b, j, i: (b * nt + i, 0)),
                      pl.BlockSpec((None, d, tn), lambda b, j, i: (l, 0, j)),
                      pl.BlockSpec((None, d, tn), lambda b, j, i: (l, 0, nj + j)),
                      pl.BlockSpec((3, tn), lambda b, j, i: (0, j)),
                      pl.BlockSpec((1, tn), lambda b, j, i: (0, j)),
                      pl.BlockSpec((1, 2, tn), lambda b, j, i: (b, 0, j)),
                      pl.BlockSpec((1, 2, tn), lambda b, j, i: (b, 0, j))],
            out_specs=(pl.BlockSpec((tm, tn), lambda b, j, i: (b * nt + i, j)),
                       pl.BlockSpec((1, 2, tn), lambda b, j, i: (b, 0, j))),
            scratch_shapes=[pltpu.VMEM((2, tn), F32)],
            compiler_params=_cp("parallel", "parallel", "arbitrary"),
        )(h2, w_up, w_up, cw, cb, st, st)
        return y, fc
    s0, s1 = st[:, 0, :], st[:, 1, :]
    y, a = pl.pallas_call(
        functools.partial(_ffn_up_kernel, seq=False),
        out_shape=(jax.ShapeDtypeStruct((n, dff), BF16), jax.ShapeDtypeStruct((n, dff), F32)),
        grid=(nj,),
        in_specs=[pl.BlockSpec((n, d), lambda j: (0, 0)),
                  pl.BlockSpec((None, d, tn), lambda j: (l, 0, j)),
                  pl.BlockSpec((None, d, tn), lambda j: (l, 0, nj + j)),
                  pl.BlockSpec((3, tn), lambda j: (0, j)),
                  pl.BlockSpec((1, tn), lambda j: (0, j)),
                  pl.BlockSpec((n, tn), lambda j: (0, j)),
                  pl.BlockSpec((n, tn), lambda j: (0, j))],
        out_specs=(pl.BlockSpec((n, tn), lambda j: (0, j)), pl.BlockSpec((n, tn), lambda j: (0, j))),
        scratch_shapes=[pltpu.VMEM((2, tn), F32)],
        compiler_params=_cp("parallel"),
    )(h2, w_up, w_up, cw, cb, s0, s1)
    return y, jnp.stack([s1, a], axis=1)


def _permute_w_in(w):
    cols = lambda s, n: w[:, :, s:s + n]
    parts = [cols(0, 4 * GW), cols(2064, GW), cols(2832, GW), cols(5208, GW), cols(3416, R_IN),
             cols(2576, 128), cols(2704, 128), cols(3344, HD), cols(2048, NH), cols(2056, NH), cols(3408, NH),
             jnp.zeros(w.shape[:2] + (NP - C_SM - HD - 3 * NH,), w.dtype)]
    return jnp.concatenate(parts, axis=2).astype(BF16)


def _layer(x2, nb, t, l, W, st, tables, cache):
    n = nb * t
    c0, n0, m0, rs0, rsh0, sre0, sim0, conv0 = st
    proj = _in_proj(x2, W['norm_mix'], W['w_in'], l)
    aq_r, iq_r, k_r, v_r, ik_r = _rope_call(proj, tables[0], tables[1], *tables[2])

    if cache is None:
        ym, c1, n1, m1 = _mlstm_prompt(proj, W['m_b_i'], W['m_b_f'], W['m_norm'], nb, t)
        ya = _dsa_prompt(iq_r, proj, ik_r, aq_r, k_r, v_r, nb, t)
    else:
        ym, c1, n1, m1 = _mlstm_step(proj, W['m_b_i'], W['m_b_f'], W['m_norm'], c0, l, n0, m0)
        ym = ym.reshape(n, GW)
        cki, ck, cv, page_table, n_pool = cache
        ya = _dsa_sample(l, page_table, iq_r, proj[:, C_SM + SM_IW:C_SM + SM_IW + NH], ik_r, aq_r, k_r, v_r,
                         cki, ck, cv, n_pool)

    prev = rsh0.reshape(nb, 1, R_IN) if t > 1 else rsh0
    rs = _rwkv_prep(proj, prev, W['r_mu'], W['r_w0'], W['r_w_w2'], W['r_a0'], W['r_w_a2'], W['r_w_g2'],
                    W['r_k_k'], W['r_k_a'], nb, t)
    if t > 1:
        yr, rs1 = _rwkv_scan(rs, W['r_r_k'], W['r_ln_w'], W['r_ln_b'], nb, t)
    else:
        yr, rs1 = _rwkv_step(rs, W['r_r_k'], W['r_ln_w'], W['r_ln_b'], rs0, l)
        yr = yr.reshape(n, GW)
    rsh1 = proj.reshape(nb, t, NP)[:, t - 1, C_RIN:C_RIN + R_IN]

    su = proj[:, C_SU:C_SU + GW]
    u_tm = su.reshape(nb, t, GW).transpose(1, 0, 2).reshape(t * nb, GW)
    ys_tm, sre1, sim1 = _s5(u_tm, W['s5'], sre0.reshape(nb, S5_W), sim0.reshape(nb, S5_W), nb, t)
    ys = ys_tm.reshape(t, nb, GW).transpose(1, 0, 2).reshape(n, GW)

    x2 = _res_matmul(x2, [ym, ya, yr, ys], W['w_out'], l)
    h2 = _rmsnorm(x2, W['norm_ffn'], BF16)
    y, conv1 = _ffn_up(h2, W['ffn_w_up'], l, W['ffn_conv_w'], W['ffn_conv_b'], conv0, nb, t)
    x2 = _res_matmul(x2, [y], W['ffn_w_down'], l)
    outs = (k_r.reshape(nb, t, A_KV, HD), v_r.reshape(nb, t, A_KV, HD), ik_r.reshape(nb, t, HD),
            c1, n1, m1.reshape(nb, NH), rs1, rsh1, sre1.reshape(nb, S5_G, S5_P), sim1.reshape(nb, S5_G, S5_P), conv1)
    return x2, outs


def kernel(x_prompt, x_sample, cache_k, cache_v, cache_kidx, page_table, state_mlstm_c, state_mlstm_n,
           state_mlstm_m, state_rwkv_s, state_rwkv_shift, state_s5_re, state_s5_im, state_ffn_conv,
           norm_mix, w_in, w_out, m_b_i, m_b_f, m_norm, r_mu, r_w0, r_w_w2, r_a0, r_w_a2, r_w_g2,
           r_k_k, r_k_a, r_r_k, r_ln_w, r_ln_b, s5_a_re, s5_a_im, s5_b_re, s5_b_im, s5_c_re, s5_c_im,
           s5_d, s5_log_dt, s5_w_glu, s5_b_glu, norm_ffn, ffn_w_up, ffn_conv_w, ffn_conv_b, ffn_w_down,
           norm_final):
    bp, tp, d = x_prompt.shape
    bs, ts, _ = x_sample.shape
    assert ts == 1 and tp % CHUNK == 0
    depth = w_in.shape[0]
    n_pool = cache_k.shape[1]
    past = page_table.shape[1] * PAGE
    dff = ffn_conv_b.shape[-1]

    row = lambda z: z.reshape(1, -1)
    layers = []
    w_in_all, w_out_all = _permute_w_in(w_in), w_out.astype(BF16)
    w_up_all, w_down_all = ffn_w_up.astype(BF16), ffn_w_down.astype(BF16)
    for l in range(depth):
        layers.append(dict(
            norm_mix=row(norm_mix[l]), w_in=w_in_all, w_out=w_out_all,
            m_b_i=row(m_b_i[l]), m_b_f=row(m_b_f[l]), m_norm=row(m_norm[l]),
            r_mu=row(r_mu[l]), r_w0=row(r_w0[l]), r_w_w2=r_w_w2[l], r_a0=row(r_a0[l]), r_w_a2=r_w_a2[l],
            r_w_g2=r_w_g2[l], r_k_k=row(r_k_k[l]), r_k_a=row(r_k_a[l]), r_r_k=row(r_r_k[l]),
            r_ln_w=row(r_ln_w[l]), r_ln_b=row(r_ln_b[l]),
            s5=_s5_mats(s5_a_re[l], s5_a_im[l], s5_b_re[l], s5_b_im[l], s5_c_re[l], s5_c_im[l], s5_d[l],
                        s5_log_dt[l], s5_w_glu[l], s5_b_glu[l]),
            norm_ffn=row(norm_ffn[l]), ffn_w_up=w_up_all, ffn_conv_w=ffn_conv_w[l],
            ffn_conv_b=row(ffn_conv_b[l]), ffn_w_down=w_down_all))

    cos_p, sin_p = _rope_tables(jnp.arange(tp))
    cos_s, sin_s = _rope_tables(jnp.full((bs,), past))
    tab_p = (cos_p, sin_p, (bp, tp))
    tab_s = (cos_s, sin_s, (1, bs))

    zeros = lambda *s: jnp.zeros(s, F32)
    st_p = (zeros(bp, NH, HD, HD), zeros(bp, NH, HD), zeros(bp, NH), zeros(bp, NH, HD, HD), zeros(bp, R_IN),
            zeros(bp, S5_G, S5_P), zeros(bp, S5_G, S5_P), zeros(bp, 2, dff))
    cki = cache_kidx.reshape(depth * n_pool, PAGE, HD).transpose(0, 2, 1)
    ck = cache_k.reshape(depth * n_pool, PAGE, A_KV * HD).transpose(0, 2, 1)
    cv = cache_v.reshape(depth * n_pool, PAGE, A_KV * HD).transpose(0, 2, 1)

    xp = x_prompt.reshape(bp * tp, d)
    xs = x_sample.reshape(bs, d)
    new_p, new_s = [], []
    for l in range(depth):
        xp, sp = _layer(xp, bp, tp, l, layers[l], st_p, tab_p, None)
        st_s = (state_mlstm_c, state_mlstm_n[l], state_mlstm_m[l], state_rwkv_s, state_rwkv_shift[l],
                state_s5_re[l], state_s5_im[l], state_ffn_conv[l])
        xs, ss = _layer(xs, bs, 1, l, layers[l], st_s, tab_s, (cki, ck, cv, page_table, n_pool))
        new_p.append(sp)
        new_s.append(ss)
    (k_p, v_p, ki_p, mc_p, mn_p, mm_p, rs_p, rsh_p, sre_p, sim_p, fc_p) = [jnp.stack(z) for z in zip(*new_p)]
    (k_s, v_s, ki_s, mc_s, mn_s, mm_s, rs_s, rsh_s, sre_s, sim_s, fc_s) = [jnp.stack(z) for z in zip(*new_s)]
    y_prompt = _rmsnorm(xp, row(norm_final), F32).reshape(bp, tp, d)
    y_sample = _rmsnorm(xs, row(norm_final), F32).reshape(bs, ts, d)
    return (y_prompt, y_sample, k_p, k_s, v_p, v_s, ki_p, ki_s, mc_p, mc_s, mn_p, mn_s, mm_p, mm_s,
            rs_p, rs_s, rsh_p, rsh_s, sre_p, sre_s, sim_p, sim_s, fc_p, fc_s)
```

```python
import functools
import math

import jax
import jax.numpy as jnp
from jax import lax
from jax.experimental import pallas as pl
from jax.experimental.pallas import tpu as pltpu

F32 = jnp.float32
BF16 = jnp.bfloat16
HI = lax.Precision.HIGHEST

HD = 64
NH = 8
GW = NH * HD
A_KV = 2
PAGE = 128
TOPK_MAX = 256
ROPE_THETA = 10000.0
R_IN = 3 * GW + 64 + 64 + 128
R_LN_EPS = 64e-5
S5_G, S5_CH, S5_P = 32, 16, 64
S5_W = S5_G * S5_P
NORM_EPS = 1e-6
CHUNK = 64
IDX_SCALE = HD ** -0.5 * NH ** -0.5

C_M, C_AQ, C_IQ, C_SU, C_RIN, C_AK, C_AV, C_SM = 0, 2048, 2560, 3072, 3584, 5376, 5504, 5632
NP = 5760
SM_IK, SM_MI, SM_MF, SM_IW = 0, 64, 72, 80

VMEM_LIMIT = 56 * 1024 * 1024
MM_ROWS = 1024
RWKV_SEQS = 4
FFN_PARTS = 2


def _cp(*sem):
    return pltpu.CompilerParams(dimension_semantics=sem, vmem_limit_bytes=VMEM_LIMIT)


def _dot(a, b, prec=None):
    return jnp.dot(a, b, preferred_element_type=F32, precision=prec)


def _dot_nt(a, b, prec=None):
    return lax.dot_general(a, b, (((1,), (1,)), ((), ())), preferred_element_type=F32, precision=prec)


def _dot_tn(a, b, prec=None):
    return lax.dot_general(a, b, (((0,), (0,)), ((), ())), preferred_element_type=F32, precision=prec)


def _sigmoid(x):
    return 1.0 / (1.0 + jnp.exp(-x))


def _softplus(x):
    return jnp.maximum(x, 0.0) + jnp.log(1.0 + jnp.exp(-jnp.abs(x)))


def _iota(shape, dim):
    return lax.broadcasted_iota(jnp.int32, shape, dim)


HG = 4
BD = HG * HD
assert CHUNK == HD


def _bd_masks(c):
    row, col = _iota((HG * c, HG * c), 0), _iota((HG * c, HG * c), 1)
    same = (row // c) == (col // c)
    t, s = row % c, col % c
    return same, jnp.logical_and(same, s <= t), jnp.logical_and(same, s < t)


def _bd_diag(x):
    nb, ng = x.shape[:2]
    x6 = x.reshape(nb, ng, HG, HD, HG, HD)
    return jnp.stack([x6[:, :, j, :, j, :] for j in range(HG)], axis=2).reshape(nb, ng * HG, HD, HD)


def _inproj_kernel(x_ref, g_ref, w_ref, o_ref, h_scr):
    @pl.when(pl.program_id(1) == 0)
    def _():
        x = x_ref[...]
        ms = jnp.mean(x * x, axis=-1, keepdims=True)
        h_scr[...] = (x * lax.rsqrt(ms + NORM_EPS) * g_ref[...]).astype(BF16)

    o_ref[...] = _dot(h_scr[...], w_ref[...])


def _in_proj(x2, g, w, l):
    n, d = x2.shape
    npad = w.shape[2]
    tm = min(MM_ROWS, n)
    tn = 640
    return pl.pallas_call(
        _inproj_kernel,
        out_shape=jax.ShapeDtypeStruct((n, npad), F32),
        grid=(n // tm, npad // tn),
        in_specs=[pl.BlockSpec((tm, d), lambda i, j: (i, 0)),
                  pl.BlockSpec((1, d), lambda i, j: (0, 0)),
                  pl.BlockSpec((None, d, tn), lambda i, j: (l, 0, j))],
        out_specs=pl.BlockSpec((tm, tn), lambda i, j: (i, j)),
        scratch_shapes=[pltpu.VMEM((tm, d), BF16)],
        compiler_params=_cp("parallel", "arbitrary"),
    )(x2, g, w)


def _rmsnorm_kernel(x_ref, g_ref, o_ref):
    x = x_ref[...]
    ms = jnp.mean(x * x, axis=-1, keepdims=True)
    o_ref[...] = (x * lax.rsqrt(ms + NORM_EPS) * g_ref[...]).astype(o_ref.dtype)


def _rmsnorm(x2, g, dtype):
    n, d = x2.shape
    tm = min(512, n)
    return pl.pallas_call(
        _rmsnorm_kernel,
        out_shape=jax.ShapeDtypeStruct((n, d), dtype),
        grid=(n // tm,),
        in_specs=[pl.BlockSpec((tm, d), lambda i: (i, 0)), pl.BlockSpec((1, d), lambda i: (0, 0))],
        out_specs=pl.BlockSpec((tm, d), lambda i: (i, 0)),
        compiler_params=_cp("parallel"),
    )(x2, g)


def _resmm_kernel(r_ref, *refs):
    y_refs, w_ref, o_ref = refs[:-2], refs[-2], refs[-1]
    acc = r_ref[...]
    k0 = 0
    for y_ref in y_refs:
        kw = y_ref.shape[1]
        acc = acc + _dot(y_ref[...].astype(BF16), w_ref[k0:k0 + kw, :])
        k0 += kw
    o_ref[...] = acc


def _res_matmul(res, ys, w, l):
    n = res.shape[0]
    k, d = w.shape[1:]
    assert sum(y.shape[1] for y in ys) == k
    tm = min(MM_ROWS, n)
    tn = 512
    return pl.pallas_call(
        _resmm_kernel,
        out_shape=jax.ShapeDtypeStruct((n, d), F32),
        grid=(n // tm, d // tn),
        in_specs=[pl.BlockSpec((tm, tn), lambda i, j: (i, j))]
                 + [pl.BlockSpec((tm, y.shape[1]), lambda i, j: (i, 0)) for y in ys]
                 + [pl.BlockSpec((None, k, tn), lambda i, j: (l, 0, j))],
        out_specs=pl.BlockSpec((tm, tn), lambda i, j: (i, j)),
        compiler_params=_cp("parallel", "arbitrary"),
    )(res, *ys, w)


def _rope(x, cos, sin):
    w = x.shape[1]
    first = (_iota(x.shape, 1) & (HD - 1)) < HD // 2
    sw = jnp.where(first, pltpu.roll(x, w - HD // 2, 1), pltpu.roll(x, HD // 2, 1))
    return x * cos + sw * sin


def _rope_kernel(aq_ref, iq_ref, ak_ref, av_ref, sm_ref, cos_ref, sin_ref, aqo, iqo, ko, vo, iko):
    cos = cos_ref[...]
    sin = sin_ref[...]
    aqo[...] = _rope(aq_ref[...], cos, sin)
    iqo[...] = _rope(iq_ref[...], cos, sin)
    ko[...] = _rope(ak_ref[...], cos[:, :128], sin[:, :128])
    vo[...] = av_ref[...]
    iko[...] = _rope(sm_ref[...], cos[:, :128], sin[:, :128])[:, :HD]


def _rope_call(proj, cos, sin, nb, nt_rows):
    n = proj.shape[0]
    tm = min(512, nt_rows)
    nt = nt_rows // tm
    row = lambda b, i: b * nt + i
    return pl.pallas_call(
        _rope_kernel,
        out_shape=(jax.ShapeDtypeStruct((n, GW), F32), jax.ShapeDtypeStruct((n, GW), F32),
                   jax.ShapeDtypeStruct((n, 128), F32), jax.ShapeDtypeStruct((n, 128), F32),
                   jax.ShapeDtypeStruct((n, HD), F32)),
        grid=(nb, nt),
        in_specs=[pl.BlockSpec((tm, GW), lambda b, i: (row(b, i), C_AQ // GW)),
                  pl.BlockSpec((tm, GW), lambda b, i: (row(b, i), C_IQ // GW)),
                  pl.BlockSpec((tm, 128), lambda b, i: (row(b, i), C_AK // 128)),
                  pl.BlockSpec((tm, 128), lambda b, i: (row(b, i), C_AV // 128)),
                  pl.BlockSpec((tm, 128), lambda b, i: (row(b, i), C_SM // 128)),
                  pl.BlockSpec((tm, GW), lambda b, i: (i, 0)),
                  pl.BlockSpec((tm, GW), lambda b, i: (i, 0))],
        out_specs=(pl.BlockSpec((tm, GW), lambda b, i: (row(b, i), 0)),
                   pl.BlockSpec((tm, GW), lambda b, i: (row(b, i), 0)),
                   pl.BlockSpec((tm, 128), lambda b, i: (row(b, i), 0)),
                   pl.BlockSpec((tm, 128), lambda b, i: (row(b, i), 0)),
                   pl.BlockSpec((tm, HD), lambda b, i: (row(b, i), 0))),
        compiler_params=_cp("parallel", "parallel"),
    )(proj, proj, proj, proj, proj, cos, sin)


def _rope_tables(pos):
    half = HD // 2
    inv = ROPE_THETA ** (-jnp.arange(half, dtype=F32) / half)
    ang = pos.astype(F32)[:, None] * inv[None, :]
    cos, sin = jnp.cos(ang), jnp.sin(ang)
    cos64 = jnp.concatenate([cos, cos], axis=-1)
    sin64 = jnp.concatenate([-sin, sin], axis=-1)
    return jnp.tile(cos64, (1, NH)), jnp.tile(sin64, (1, NH))


def _kth_largest(sc, extra, kk):
    kf = jnp.float32(kk)

    def count_ge(c):
        n = jnp.sum(jnp.where(sc >= c, 1.0, 0.0), axis=-1, keepdims=True)
        if extra is not None:
            n = n + jnp.where(extra >= c, 1.0, 0.0)
        return n

    def key_to_f(key):
        bits = key ^ ((key >> 31) & jnp.int32(0x7FFFFFFF))
        return lax.bitcast_convert_type(bits, F32)

    r = sc.shape[0]
    int_min = jnp.int32(-2 ** 31)
    lo = jnp.where(count_ge(jnp.zeros((r, 1), F32)) >= kf, jnp.int32(0), int_min)

    def body(j, lo):
        cand = lo + jnp.left_shift(jnp.int32(1), jnp.int32(30) - j)
        ok = count_ge(key_to_f(cand)) >= kf
        return jnp.where(ok, cand, lo)

    lo = lax.fori_loop(0, 31, body, lo)
    key_neg_inf = jnp.int32(-2 ** 31 + 0x7FFFFF)
    return jnp.where(lo <= key_neg_inf, -jnp.inf, key_to_f(lo))


def _strict_upper_bf16(n):
    return jnp.where(_iota((n, n), 0) < _iota((n, n), 1), 1.0, 0.0).astype(BF16)


def _split_bf16(x):
    hi = x.astype(BF16).astype(F32)
    return hi, x - hi


def _dsa_prompt_kernel(iq_ref, sm_ref, ik_ref, aq_ref, k_ref, v_ref, o_ref, sel_scr, kcat_scr, *, topk, qb, n_ext):
    t_keys = ik_ref.shape[0]
    i = pl.program_id(1)
    per_ext = (t_keys // qb) // n_ext

    @pl.when(i == 0)
    def _():
        hi, lo = _split_bf16(ik_ref[...])
        kcat_scr[...] = jnp.concatenate([hi, lo, hi], axis=1)

    iq = iq_ref[...]
    aq = aq_ref[...] * HD ** -0.5
    wts = sm_ref[:, SM_IW:SM_IW + NH] * IDX_SCALE
    tq = i * qb + _iota((qb, 1), 0)

    def body(ext):
        kcat = kcat_scr[0:ext, :]
        sc = jnp.zeros((qb, ext), F32)
        for h in range(NH):
            hi, lo = _split_bf16(iq[:, h * HD:(h + 1) * HD])
            qk = _dot_nt(jnp.concatenate([hi, hi, lo], axis=1), kcat)
            sc = sc + jnp.maximum(qk, 0.0) * wts[:, h:h + 1]
        causal = _iota((1, ext), 1) <= tq
        sc = jnp.where(causal, sc, -jnp.inf)

        thr = _kth_largest(sc, None, topk)
        gt = sc > thr
        eq = sc == thr
        n_gt = jnp.sum(jnp.where(gt, 1.0, 0.0), axis=-1, keepdims=True)
        n_eq = jnp.sum(jnp.where(eq, 1.0, 0.0), axis=-1, keepdims=True)
        need = jnp.float32(topk) - n_gt
        sel_scr[:, 0:ext] = jnp.where(jnp.logical_and(sc >= thr, causal), 0.0, -jnp.inf)
        tie = jnp.logical_and(n_eq > need, thr > -jnp.inf)

        @pl.when(jnp.max(jnp.where(tie, 1.0, 0.0)) > 0.5)
        def _():
            ut = _strict_upper_bf16(128)
            run = jnp.zeros((qb, 1), F32)
            for c in range(ext // 128):
                sl = slice(c * 128, (c + 1) * 128)
                eqc = jnp.where(eq[:, sl], 1.0, 0.0)
                pref = _dot(eqc.astype(BF16), ut) + run
                keep = jnp.logical_or(gt[:, sl], jnp.logical_and(eq[:, sl], pref < need))
                sel_scr[:, sl] = jnp.where(jnp.logical_and(keep, causal[:, sl]), 0.0, -jnp.inf)
                run = run + jnp.sum(eqc, axis=-1, keepdims=True)

        bias = sel_scr[:, 0:ext]
        for g in range(A_KV):
            kg = k_ref[0:ext, g * HD:(g + 1) * HD]
            vg = v_ref[0:ext, g * HD:(g + 1) * HD]
            for j in range(NH // A_KV):
                h = g * (NH // A_KV) + j
                s = _dot_nt(aq[:, h * HD:(h + 1) * HD], kg) + bias
                m = jnp.max(s, axis=-1, keepdims=True)
                p = jnp.exp(s - m)
                l = jnp.sum(p, axis=-1, keepdims=True)
                o_ref[:, h * HD:(h + 1) * HD] = _dot(p, vg) / l

    for j in range(n_ext):
        pl.when(i // per_ext == j)(functools.partial(body, (j + 1) * (t_keys // n_ext)))


def _dsa_prompt(iq_r, proj, ik_r, aq_r, k_r, v_r, nb, t):
    qb = min(128, t)
    nq = t // qb
    n_ext = min(4, nq)
    assert nq % n_ext == 0
    topk = min(TOPK_MAX, t // 4)
    n = nb * t
    row = lambda b, i: b * nq + i
    return pl.pallas_call(
        functools.partial(_dsa_prompt_kernel, topk=topk, qb=qb, n_ext=n_ext),
        out_shape=jax.ShapeDtypeStruct((n, GW), F32),
        grid=(nb, nq),
        in_specs=[pl.BlockSpec((qb, GW), lambda b, i: (row(b, i), 0)),
                  pl.BlockSpec((qb, 128), lambda b, i: (row(b, i), C_SM // 128)),
                  pl.BlockSpec((t, HD), lambda b, i: (b, 0)),
                  pl.BlockSpec((qb, GW), lambda b, i: (row(b, i), 0)),
                  pl.BlockSpec((t, 128), lambda b, i: (b, 0)),
                  pl.BlockSpec((t, 128), lambda b, i: (b, 0))],
        out_specs=pl.BlockSpec((qb, GW), lambda b, i: (row(b, i), 0)),
        scratch_shapes=[pltpu.VMEM((qb, t), F32), pltpu.VMEM((t, 3 * HD), F32)],
        compiler_params=_cp("parallel", "arbitrary"),
    )(iq_r, proj, ik_r, aq_r, k_r, v_r)


def _dsa_sample_score_kernel(*refs, n_pages):
    _, iq_ref, w_ref, ikn_ref = refs[:4]
    pages = refs[4:4 + n_pages]
    sc_ref, sn_ref = refs[4 + n_pages:]
    iq = iq_ref[0]
    w = w_ref[0] * IDX_SCALE
    for c in range(n_pages):
        qk = _dot(iq, pages[c][0], HI)
        sc_ref[0, :, c * PAGE:(c + 1) * PAGE] = jnp.sum(jnp.maximum(qk, 0.0) * w, axis=0, keepdims=True)
    qkn = jnp.sum(iq * ikn_ref[0], axis=-1, keepdims=True)
    sn = jnp.sum(jnp.maximum(qkn, 0.0) * w, axis=0, keepdims=True)
    sn_ref[0] = jnp.broadcast_to(sn, (1, 128))


def _dsa_sample_select_kernel(sc_ref, sn_ref, sel_ref, seln_ref, *, topk):
    sc = sc_ref[...]
    sn = sn_ref[:, 0:1]
    bd, s_keys = sc.shape
    thr = _kth_largest(sc, sn, topk)
    gt = sc > thr
    eq = sc == thr
    n_gt = jnp.sum(jnp.where(gt, 1.0, 0.0), axis=-1, keepdims=True) + jnp.where(sn > thr, 1.0, 0.0)
    need = jnp.float32(topk) - n_gt
    ut = _strict_upper_bf16(PAGE)
    run = jnp.zeros((bd, 1), F32)
    for c in range(s_keys // PAGE):
        sl = slice(c * PAGE, (c + 1) * PAGE)
        eqc = jnp.where(eq[:, sl], 1.0, 0.0)
        pref = _dot(eqc.astype(BF16), ut) + run
        keep = jnp.logical_or(gt[:, sl], jnp.logical_and(eq[:, sl], pref < need))
        sel_ref[:, sl] = jnp.where(keep, 1.0, 0.0)
        run = run + jnp.sum(eqc, axis=-1, keepdims=True)
    sel_new = jnp.logical_or(sn > thr, jnp.logical_and(sn == thr, run < need))
    seln_ref[...] = jnp.broadcast_to(jnp.where(sel_new, 1.0, 0.0), seln_ref.shape)


def _dsa_sample_attn_kernel(*refs, n_pages):
    _, aq_ref, kn_ref, vn_ref, sel_ref, seln_ref = refs[:6]
    kpages = refs[6:6 + n_pages]
    vpages = refs[6 + n_pages:6 + 2 * n_pages]
    o_ref, k_scr, v_scr = refs[6 + 2 * n_pages:]
    for c in range(n_pages):
        k_scr[:, c * PAGE:(c + 1) * PAGE] = kpages[c][0]
        v_scr[:, c * PAGE:(c + 1) * PAGE] = vpages[c][0]
    sel = sel_ref[0] > 0.5
    sel_new = seln_ref[0][:, 0:1] > 0.5
    aq = aq_ref[0]
    kn = kn_ref[0]
    vn = vn_ref[0]
    hpg = NH // A_KV
    for g in range(A_KV):
        qg = aq[g * hpg:(g + 1) * hpg, :]
        s = _dot(qg, k_scr[g * HD:(g + 1) * HD, :]) * HD ** -0.5
        s = jnp.where(sel, s, -jnp.inf)
        s_new = jnp.sum(qg * kn[:, g * HD:(g + 1) * HD], axis=-1, keepdims=True) * HD ** -0.5
        s_new = jnp.where(sel_new, s_new, -jnp.inf)
        m = jnp.maximum(jnp.max(s, axis=-1, keepdims=True), s_new)
        pr = jnp.exp(s - m)
        pn = jnp.exp(s_new - m)
        l = jnp.sum(pr, axis=-1, keepdims=True) + pn
        o = _dot_nt(pr, v_scr[g * HD:(g + 1) * HD, :]) + pn * vn[:, g * HD:(g + 1) * HD]
        o_ref[0, g * hpg:(g + 1) * hpg, :] = o / l


def _dsa_sample(layer, page_table, iq_r, iw, ik_r, aq_r, k_r, v_r, cki, ck, cv, n_pool):
    bd, n_pages = page_table.shape
    past = n_pages * PAGE
    topk = min(TOPK_MAX, (past + 1) // 4)
    base = layer * n_pool
    per_b = lambda b, pt: (b, 0, 0)

    def page_specs(width):
        return [pl.BlockSpec((1, width, PAGE), lambda b, pt, c=c: (base + pt[b, c], 0, 0)) for c in range(n_pages)]

    sc, sn = pl.pallas_call(
        functools.partial(_dsa_sample_score_kernel, n_pages=n_pages),
        out_shape=(jax.ShapeDtypeStruct((bd, 1, past), F32), jax.ShapeDtypeStruct((bd, 1, 128), F32)),
        grid_spec=pltpu.PrefetchScalarGridSpec(
            num_scalar_prefetch=1,
            grid=(bd,),
            in_specs=[pl.BlockSpec((1, NH, HD), per_b), pl.BlockSpec((1, NH, 1), per_b),
                      pl.BlockSpec((1, 1, HD), per_b)] + page_specs(HD),
            out_specs=(pl.BlockSpec((1, 1, past), per_b), pl.BlockSpec((1, 1, 128), per_b))),
        compiler_params=_cp("arbitrary"),
    )(page_table, iq_r.reshape(bd, NH, HD), iw.reshape(bd, NH, 1), ik_r.reshape(bd, 1, HD), *([cki] * n_pages))

    sel, seln = pl.pallas_call(
        functools.partial(_dsa_sample_select_kernel, topk=topk),
        out_shape=(jax.ShapeDtypeStruct((bd, past), F32), jax.ShapeDtypeStruct((bd, 128), F32)),
    )(sc.reshape(bd, past), sn.reshape(bd, 128))

    out = pl.pallas_call(
        functools.partial(_dsa_sample_attn_kernel, n_pages=n_pages),
        out_shape=jax.ShapeDtypeStruct((bd, NH, HD), F32),
        grid_spec=pltpu.PrefetchScalarGridSpec(
            num_scalar_prefetch=1,
            grid=(bd,),
            in_specs=[pl.BlockSpec((1, NH, HD), per_b), pl.BlockSpec((1, 1, 128), per_b),
                      pl.BlockSpec((1, 1, 128), per_b), pl.BlockSpec((1, 1, past), per_b),
                      pl.BlockSpec((1, 1, 128), per_b)] + page_specs(128) + page_specs(128),
            out_specs=pl.BlockSpec((1, NH, HD), per_b),
            scratch_shapes=[pltpu.VMEM((128, past), F32), pltpu.VMEM((128, past), F32)]),
        compiler_params=_cp("arbitrary"),
    )(page_table, aq_r.reshape(bd, NH, HD), k_r.reshape(bd, 1, 128), v_r.reshape(bd, 1, 128),
      sel.reshape(bd, 1, past), seln.reshape(bd, 1, 128), *([ck] * n_pages), *([cv] * n_pages))
    return out.reshape(bd, GW)


def _log_sigmoid(x):
    return jnp.minimum(x, 0.0) - jnp.log(1.0 + jnp.exp(-jnp.abs(x)))


MLSTM_ROWS = 2 * CHUNK


def _mlstm_prompt_kernel(m_ref, sm_ref, gb_ref, nw_ref, y_ref, c_ref, n_ref, mm_ref):
    L = CHUNK

    @pl.when(pl.program_id(1) == 0)
    def _():
        c_ref[...] = jnp.zeros(c_ref.shape, F32)
        n_ref[...] = jnp.zeros(n_ref.shape, F32)
        mm_ref[...] = jnp.zeros(mm_ref.shape, F32)

    ri, ci = _iota((L, L), 0), _iota((L, L), 1)
    tril = jnp.where(ci <= ri, 1.0, 0.0)
    triu = jnp.where(ci >= ri, 1.0, 0.0)
    same, causal, _ = _bd_masks(L)
    tile = lambda z: jnp.concatenate([z] * HG, axis=0)
    msk = lambda z: jnp.where(same, z, 0.0)
    stack_cols = lambda z, g: jnp.concatenate([z[:, g * HG + j:g * HG + j + 1] for j in range(HG)], axis=0)
    stack_rows = lambda z, g: jnp.concatenate([z[g * HG + j:g * HG + j + 1, :] for j in range(HG)], axis=1)
    percol = lambda zs: jnp.concatenate([jnp.broadcast_to(z, (L, 1)) for z in zs], axis=0)
    perrow = lambda zs: jnp.concatenate([jnp.broadcast_to(z, (1, HD)) for z in zs], axis=1)

    n_seq, n_grp = m_ref.shape[0], NH // HG
    chains = [(bb, g) for bb in range(n_seq) for g in range(n_grp)]
    each = lambda f: {c: f(c) for c in chains}
    lanes = lambda c: slice(c[1] * BD, (c[1] + 1) * BD)
    heads = lambda c: range(c[1] * HG, (c[1] + 1) * HG)
    gates = {}
    for bb in range(n_seq):
        smb = sm_ref[bb] + gb_ref[...]
        smt = smb.T
        gates[bb] = dict(ig=smb[:, SM_MI:SM_MI + NH], lf=_log_sigmoid(smb[:, SM_MF:SM_MF + NH]),
                         igt=smt[SM_MI:SM_MI + NH, :], lft=_log_sigmoid(smt[SM_MF:SM_MF + NH, :]))
    cbd = each(lambda c: c_ref[c[0], c[1]])
    nrow = each(lambda c: n_ref[c[0], :, lanes(c)])
    mprev = each(lambda c: [mm_ref[c[0], :, h:h + 1] for h in heads(c)])
    for cc in range(MLSTM_ROWS // L):
        rows = slice(cc * L, (cc + 1) * L)
        ig, bcs, bcst, igt = {}, {}, {}, {}
        for bb in range(n_seq):
            ig[bb] = gates[bb]['ig'][rows]
            bcs[bb] = _dot(tril, gates[bb]['lf'][rows], HI)
            bcst[bb] = _dot(gates[bb]['lft'][:, rows], triu, HI)
            igt[bb] = gates[bb]['igt'][:, rows]
        col = lambda c, w: m_ref[c[0], rows, w * GW + c[1] * BD:w * GW + (c[1] + 1) * BD]
        qexp = each(lambda c: msk(tile(col(c, 0))))
        kt = each(lambda c: tile(col(c, 1) * HD ** -0.5))
        kexp = each(lambda c: msk(kt[c]))
        vexp = each(lambda c: msk(tile(col(c, 2))))
        bcol = each(lambda c: stack_cols(bcs[c[0]], c[1]))
        icol = each(lambda c: stack_cols(ig[c[0]], c[1]))
        bl = each(lambda c: [bcs[c[0]][L - 1:L, h:h + 1] for h in heads(c)])
        dmat = each(lambda c: jnp.where(
            causal, bcol[c] - stack_rows(bcst[c[0]], c[1]) + stack_rows(igt[c[0]], c[1]), -jnp.inf))
        inter = each(lambda c: bcol[c] + percol(mprev[c]))
        mj = each(lambda c: jnp.maximum(inter[c], jnp.max(dmat[c], axis=-1, keepdims=True)))
        qk = each(lambda c: _dot_nt(qexp[c], kt[c]))
        s = each(lambda c: qk[c] * jnp.exp(dmat[c] - mj[c]))
        iw = each(lambda c: jnp.exp(inter[c] - mj[c]))
        sv = each(lambda c: _dot(s[c], vexp[c]))
        qc = each(lambda c: _dot_nt(qexp[c], cbd[c]))
        wl = each(lambda c: percol(bl[c]) - bcol[c] + icol[c])
        m_new = each(lambda c: [jnp.maximum(bl[c][j] + mprev[c][j],
                                            jnp.max(wl[c][j * L:(j + 1) * L], axis=0, keepdims=True))
                                for j in range(HG)])
        dec = each(lambda c: [jnp.exp(bl[c][j] + mprev[c][j] - m_new[c][j]) for j in range(HG)])
        ws = each(lambda c: jnp.exp(wl[c] - percol(m_new[c])))
        upd = each(lambda c: _dot_tn(vexp[c] * ws[c], kexp[c]))
        for c in chains:
            bb, ln = c[0], lanes(c)
            den = (jnp.sum(s[c], axis=-1, keepdims=True)
                   + iw[c] * jnp.sum(qexp[c] * nrow[c], axis=-1, keepdims=True))
            hc = (sv[c] + iw[c] * qc[c]) / jnp.maximum(jnp.abs(den), jnp.exp(-mj[c]))
            hn = hc * lax.rsqrt(jnp.sum(hc * hc, axis=-1, keepdims=True) * (1.0 / HD) + NORM_EPS)
            hn = sum(hn[j * L:(j + 1) * L] for j in range(HG))
            y_ref[bb, rows, ln] = hn * nw_ref[:, ln] * _sigmoid(col(c, 3))
        cbd = each(lambda c: percol(dec[c]) * cbd[c] + upd[c])
        nrow = each(lambda c: perrow(dec[c]) * nrow[c] + jnp.sum(kexp[c] * ws[c], axis=0, keepdims=True))
        mprev = m_new
    for c in chains:
        c_ref[c[0], c[1]] = cbd[c]
        n_ref[c[0], :, lanes(c)] = nrow[c]
        for j, h in enumerate(heads(c)):
            mm_ref[c[0], :, h:h + 1] = mprev[c][j]


def _gate_bias_row(b_i, b_f):
    z = lambda n: jnp.zeros((1, n), F32)
    return jnp.concatenate([z(SM_MI), b_i, b_f, z(128 - SM_MF - NH)], axis=1)


def _mlstm_prompt(proj, b_i, b_f, nw, nb, t):
    tm = MLSTM_ROWS
    nc = t // tm
    ng = NH // HG
    sb = RWKV_SEQS if nb % RWKV_SEQS == 0 else 1
    proj3 = proj.reshape(nb, t, NP)
    y, cbd, nrow, m = pl.pallas_call(
        _mlstm_prompt_kernel,
        out_shape=(jax.ShapeDtypeStruct((nb, t, GW), F32), jax.ShapeDtypeStruct((nb, ng, BD, BD), F32),
                   jax.ShapeDtypeStruct((nb, 1, GW), F32), jax.ShapeDtypeStruct((nb, 1, NH), F32)),
        grid=(nb // sb, nc),
        in_specs=[pl.BlockSpec((sb, tm, 4 * GW), lambda b, c: (b, c, 0)),
                  pl.BlockSpec((sb, tm, 128), lambda b, c: (b, c, C_SM // 128)),
                  pl.BlockSpec((1, 128), lambda b, c: (0, 0)),
                  pl.BlockSpec((1, GW), lambda b, c: (0, 0))],
        out_specs=(pl.BlockSpec((sb, tm, GW), lambda b, c: (b, c, 0)),
                   pl.BlockSpec((sb, ng, BD, BD), lambda b, c: (b, 0, 0, 0)),
                   pl.BlockSpec((sb, 1, GW), lambda b, c: (b, 0, 0)),
                   pl.BlockSpec((sb, 1, NH), lambda b, c: (b, 0, 0))),
        compiler_params=_cp("parallel", "arbitrary"),
    )(proj3, proj3, _gate_bias_row(b_i, b_f), nw)
    return y.reshape(nb * t, GW), _bd_diag(cbd), nrow.reshape(nb, NH, HD), m


def _head_expanders():
    diag = (_iota((NH * HD, HD), 0) % HD) == _iota((NH * HD, HD), 1)

    def rep(z):
        return jnp.concatenate([jnp.broadcast_to(z[h:h + 1, :], (HD, z.shape[1])) for h in range(NH)], axis=0)

    def fold(col):
        m = jnp.where(diag, col, 0.0)
        return jnp.concatenate([jnp.sum(m[h * HD:(h + 1) * HD], axis=0, keepdims=True) for h in range(NH)], axis=0)

    return rep, fold, diag


def _mlstm_step_kernel(x_ref, gt_ref, gb_ref, nw_ref, c_ref, n_ref, mm_ref, y_ref, co_ref, no_ref, mo_ref):
    x = x_ref[0]
    q, k, v, og = x[0:NH], x[NH:2 * NH] * HD ** -0.5, x[2 * NH:3 * NH], x[3 * NH:4 * NH]
    gates = gt_ref[0] + gb_ref[...]
    ig = gates[:, 0:1]
    rep, fold, diag = _head_expanders()
    c0 = c_ref[0].reshape(NH * HD, HD)
    n0 = n_ref[0]
    inter = _log_sigmoid(gates[:, 1:2]) + mm_ref[0]
    mj = jnp.maximum(inter, ig)
    s = jnp.sum(q * k, axis=-1, keepdims=True) * jnp.exp(ig - mj)
    iw = jnp.exp(inter - mj)
    ws = jnp.exp(ig - mj)
    den = s + iw * jnp.sum(n0 * q, axis=-1, keepdims=True)
    inv = 1.0 / jnp.maximum(jnp.abs(den), jnp.exp(-mj))
    lane = _iota((NH, 128), 1)
    per_head = jnp.where(lane == 0, s, jnp.where(lane == 1, iw, jnp.where(lane == 2, ws, inv)))
    ph = rep(per_head)
    s_c, iw_c, ws_c, inv_c = ph[:, 0:1], ph[:, 1:2], ph[:, 2:3], ph[:, 3:4]
    cq = jnp.sum(c0 * rep(q), axis=1, keepdims=True)
    vcol = jnp.sum(jnp.where(diag, rep(v), 0.0), axis=1, keepdims=True)
    hcol = (s_c * vcol + iw_c * cq) * inv_c
    co_ref[0] = (iw_c * c0 + (ws_c * vcol) * rep(k)).reshape(NH, HD, HD)
    no_ref[0] = iw * n0 + ws * k
    mo_ref[0] = mj
    h8 = fold(hcol)
    hn = h8 * lax.rsqrt(jnp.mean(h8 * h8, axis=-1, keepdims=True) + NORM_EPS)
    y_ref[0] = hn * nw_ref[...] * _sigmoid(og)


def _mlstm_step(proj, b_i, b_f, nw, c_all, l, n0, m0):
    bd = proj.shape[0]
    per_b3 = lambda b: (b, 0, 0)
    full = lambda b: (0, 0)
    x = proj[:, :4 * GW].reshape(bd, 4 * NH, HD)
    gates = proj[:, C_SM + SM_MI:C_SM + SM_MI + 2 * NH].reshape(bd, 2, NH).transpose(0, 2, 1)
    gate_bias = jnp.concatenate([b_i, b_f], axis=0).T
    y, c1, n1, m1 = pl.pallas_call(
        _mlstm_step_kernel,
        out_shape=(jax.ShapeDtypeStruct((bd, NH, HD), F32), jax.ShapeDtypeStruct((bd, NH, HD, HD), F32),
                   jax.ShapeDtypeStruct((bd, NH, HD), F32), jax.ShapeDtypeStruct((bd, NH, 1), F32)),
        grid=(bd,),
        in_specs=[pl.BlockSpec((1, 4 * NH, HD), per_b3),
                  pl.BlockSpec((1, NH, 2), per_b3),
                  pl.BlockSpec((NH, 2), full),
                  pl.BlockSpec((NH, HD), full),
                  pl.BlockSpec((1, NH, HD, HD), lambda b: (l * bd + b, 0, 0, 0)),
                  pl.BlockSpec((1, NH, HD), per_b3),
                  pl.BlockSpec((1, NH, 1), per_b3)],
        out_specs=(pl.BlockSpec((1, NH, HD), per_b3),
                   pl.BlockSpec((1, NH, HD, HD), lambda b: (b, 0, 0, 0)),
                   pl.BlockSpec((1, NH, HD), per_b3),
                   pl.BlockSpec((1, NH, 1), per_b3)),
        compiler_params=_cp("parallel"),
    )(x, gates, gate_bias, nw.reshape(NH, HD), c_all.reshape((-1,) + c_all.shape[2:]), n0, m0.reshape(bd, NH, 1))
    return y, c1, n1, m1


def _rwkv_prep_kernel(x_ref, prev_ref, mu_ref, w0_ref, ww2_ref, a0_ref, wa2_ref, wg2_ref, kk_ref, ka_ref,
                      r_o, lw_o, k_o, v_o, kk_o, a_o, g_o, carry_scr, *, seq):
    x = x_ref[...]
    tm = x.shape[0]
    if seq:
        first = jnp.where(pl.program_id(1) == 0, prev_ref[0], carry_scr[...])
        xprev = jnp.where(_iota((tm, 1), 0) == 0, first, pltpu.roll(x, 1, 0))
        carry_scr[...] = x[tm - 1:tm, :]
    else:
        xprev = prev_ref[...]
    xm = x + (xprev - x) * mu_ref[...]
    r = xm[:, 0:GW]
    kx = xm[:, GW:2 * GW]
    v = xm[:, 2 * GW:3 * GW]
    xw = xm[:, 3 * GW:3 * GW + 64]
    xa = xm[:, 3 * GW + 64:3 * GW + 128]
    xg = xm[:, 3 * GW + 128:R_IN]
    w = -_softplus(-(w0_ref[...] + _dot(jnp.tanh(xw), ww2_ref[...]))) - 0.5
    a = _sigmoid(a0_ref[...] + _dot(xa, wa2_ref[...]))
    r_o[...] = r
    lw_o[...] = -jnp.exp(w)
    v_o[...] = v
    a_o[...] = a
    g_o[...] = _dot(_sigmoid(xg), wg2_ref[...])
    k_o[...] = kx * (1.0 + (a - 1.0) * ka_ref[...])
    kk = kx * kk_ref[...]
    for h in range(NH):
        kh = kk[:, h * HD:(h + 1) * HD]
        nrm = jnp.sqrt(jnp.sum(kh * kh, axis=-1, keepdims=True))
        kk_o[:, h * HD:(h + 1) * HD] = kh / jnp.maximum(nrm, 1e-12)


def _rwkv_prep(proj, prev, mu, w0, ww2, a0, wa2, wg2, k_k, k_a, nb, t):
    n = nb * t
    seq = t > 1
    full = lambda *_: (0, 0)
    if seq:
        tm = min(256, t)
        nt = t // tm
        grid = (nb, nt)
        xmap = lambda b, i: (b * nt + i, C_RIN // R_IN)
        pspec = pl.BlockSpec((1, 1, R_IN), lambda b, i: (b, 0, 0))
        omap = lambda b, i: (b * nt + i, 0)
        sem = ("parallel", "arbitrary")
    else:
        tm = n
        grid = (1,)
        xmap = lambda i: (0, C_RIN // R_IN)
        pspec = pl.BlockSpec((tm, R_IN), lambda i: (0, 0))
        omap = lambda i: (0, 0)
        sem = ("arbitrary",)
    wspecs = [pl.BlockSpec((1, R_IN), full), pl.BlockSpec((1, GW), full), pl.BlockSpec((64, GW), full),
              pl.BlockSpec((1, GW), full), pl.BlockSpec((64, GW), full), pl.BlockSpec((128, GW), full),
              pl.BlockSpec((1, GW), full), pl.BlockSpec((1, GW), full)]
    return pl.pallas_call(
        functools.partial(_rwkv_prep_kernel, seq=seq),
        out_shape=tuple(jax.ShapeDtypeStruct((n, GW), F32) for _ in range(7)),
        grid=grid,
        in_specs=[pl.BlockSpec((tm, R_IN), xmap), pspec] + wspecs,
        out_specs=tuple(pl.BlockSpec((tm, GW), omap) for _ in range(7)),
        scratch_shapes=[pltpu.VMEM((1, R_IN), F32)],
        compiler_params=_cp(*sem),
    )(proj, prev, mu, w0, ww2, a0, wa2, wg2, k_k, k_a)


def _rwkv_scan_kernel(r_ref, lw_ref, k_ref, v_ref, kk_ref, a_ref, g_ref, rk_ref, lnw_ref, lnb_ref, y_ref, s_ref):
    C = CHUNK

    @pl.when(pl.program_id(1) == 0)
    def _():
        s_ref[...] = jnp.zeros(s_ref.shape, F32)

    ri, ci = _iota((C, C), 0), _iota((C, C), 1)
    same, incl, strict = _bd_masks(C)
    row, col = _iota((BD, BD), 0), _iota((BD, BD), 1)
    eye = jnp.where(row == col, 1.0, 0.0)
    strict_incl = jnp.concatenate([strict, incl], axis=0)
    tile = lambda z: jnp.concatenate([z] * HG, axis=0)
    msk = lambda z: jnp.where(same, z, 0.0)
    fold = lambda z: sum(z[j * C:(j + 1) * C] for j in range(HG))
    n_seq, n_grp = r_ref.shape[0], NH // HG
    chains = [(bb, g) for bb in range(n_seq) for g in range(n_grp)]
    each = lambda f: {c: f(c) for c in chains}
    lanes = lambda c: slice(c[1] * BD, (c[1] + 1) * BD)
    seq = {}
    for bb in range(n_seq):
        lw = lw_ref[bb]
        cs = _dot(jnp.where(ci <= ri, 1.0, 0.0), lw, HI)
        gam = jnp.exp(cs)
        ginv = jnp.exp(-cs)
        r, k, v, kk = r_ref[bb], k_ref[bb], v_ref[bb], kk_ref[bb]
        seq[bb] = dict(at=-kk * jnp.exp(cs - lw), bt=kk * a_ref[bb] * ginv, kt=k * ginv, rt=r * gam, v=v,
                       glast=gam[C - 1:C, :], bonus_in=r * k * rk_ref[...])
    part = lambda c, name: seq[c[0]][name][:, lanes(c)]
    btl = each(lambda c: tile(part(c, 'bt')))
    ktl = each(lambda c: tile(part(c, 'kt')))
    vexp = each(lambda c: msk(tile(part(c, 'v'))))
    ar = each(lambda c: jnp.concatenate([msk(tile(part(c, 'at'))), msk(tile(part(c, 'rt')))], axis=0))
    sbd = each(lambda c: s_ref[c[0], c[1]])
    gb = each(lambda c: _dot_nt(ar[c], btl[c]))
    gk = each(lambda c: _dot_nt(ar[c], ktl[c]))
    gs = each(lambda c: _dot_nt(ar[c], sbd[c]))
    n_ab = each(lambda c: jnp.where(strict, gb[c][:BD], 0.0))
    x = each(lambda c: eye + n_ab[c])
    pm = each(lambda c: _dot(n_ab[c], n_ab[c]))
    for j in range(5):
        xd = each(lambda c: _dot(x[c], pm[c]))
        if j < 4:
            pm = each(lambda c: _dot(pm[c], pm[c]))
        x = each(lambda c: x[c] + xd[c])
    akv = each(lambda c: _dot(jnp.where(strict_incl, gk[c], 0.0), vexp[c]))
    u = each(lambda c: _dot(x[c], gs[c][:BD] + akv[c][:BD]))
    o = each(lambda c: gs[c][BD:] + _dot(jnp.where(incl, gb[c][BD:], 0.0), u[c]) + akv[c][BD:])
    gl = each(lambda c: part(c, 'glast'))
    s_new = each(lambda c: sbd[c] * gl[c] + _dot_tn(
        jnp.concatenate([u[c], vexp[c]], axis=0),
        jnp.concatenate([msk(btl[c] * gl[c]), msk(ktl[c] * gl[c])], axis=0)))
    for c in chains:
        bb, ln = c[0], lanes(c)
        s_ref[c[0], c[1]] = s_new[c]
        mean = jnp.sum(o[c], axis=-1, keepdims=True) * (1.0 / HD)
        dev = msk(o[c] - mean)
        var = jnp.sum(dev * dev, axis=-1, keepdims=True) * (1.0 / HD)
        on = fold(dev * lax.rsqrt(var + R_LN_EPS)) * lnw_ref[:, ln] + lnb_ref[:, ln]
        bonus = fold(jnp.sum(msk(tile(part(c, 'bonus_in'))), axis=-1, keepdims=True) * vexp[c])
        y_ref[bb, :, ln] = (on + bonus) * g_ref[bb, :, ln]


def _rwkv_scan(rs, rk, lnw, lnb, nb, t):
    nc = t // CHUNK
    ng = NH // HG
    sb = RWKV_SEQS if nb % RWKV_SEQS == 0 else 1
    rowmap = lambda b, c: (b, c, 0)
    full = lambda b, c: (0, 0)
    y, sbd = pl.pallas_call(
        _rwkv_scan_kernel,
        out_shape=(jax.ShapeDtypeStruct((nb, t, GW), F32), jax.ShapeDtypeStruct((nb, ng, BD, BD), F32)),
        grid=(nb // sb, nc),
        in_specs=[pl.BlockSpec((sb, CHUNK, GW), rowmap)] * 7 + [pl.BlockSpec((1, GW), full)] * 3,
        out_specs=(pl.BlockSpec((sb, CHUNK, GW), rowmap), pl.BlockSpec((sb, ng, BD, BD), lambda b, c: (b, 0, 0, 0))),
        compiler_params=_cp("parallel", "arbitrary"),
    )(*(z.reshape(nb, t, GW) for z in rs), rk, lnw, lnb)
    return y.reshape(nb * t, GW), _bd_diag(sbd)


def _rwkv_step_kernel(r_ref, lw_ref, k_ref, v_ref, kk_ref, a_ref, g_ref, rk_ref, lnw_ref, lnb_ref, s_ref,
                      y_ref, so_ref):
    r, lw, k, v, kk, a, g = (ref[0] for ref in (r_ref, lw_ref, k_ref, v_ref, kk_ref, a_ref, g_ref))
    rep, fold, diag = _head_expanders()
    s0 = s_ref[0].reshape(NH * HD, HD)
    kk_rep = rep(kk)
    sk = jnp.sum(s0 * kk_rep, axis=1, keepdims=True)
    vcol = jnp.sum(jnp.where(diag, rep(v), 0.0), axis=1, keepdims=True)
    s1 = s0 * rep(jnp.exp(lw)) - sk * rep(kk * a) + vcol * rep(k)
    so_ref[0] = s1.reshape(NH, HD, HD)
    ocol = jnp.sum(s1 * rep(r), axis=1, keepdims=True)
    o = fold(ocol)
    mean = jnp.mean(o, axis=-1, keepdims=True)
    var = jnp.mean(jnp.square(o - mean), axis=-1, keepdims=True)
    on = (o - mean) * lax.rsqrt(var + R_LN_EPS) * lnw_ref[...] + lnb_ref[...]
    bonus = jnp.sum(r * k * rk_ref[...], axis=-1, keepdims=True) * v
    y_ref[0] = (on + bonus) * g


def _rwkv_step(rs, rk, lnw, lnb, s_all, l):
    bd = s_all.shape[1]
    per_b = lambda b: (b, 0, 0)
    full = lambda b: (0, 0)
    return pl.pallas_call(
        _rwkv_step_kernel,
        out_shape=(jax.ShapeDtypeStruct((bd, NH, HD), F32), jax.ShapeDtypeStruct((bd, NH, HD, HD), F32)),
        grid=(bd,),
        in_specs=[pl.BlockSpec((1, NH, HD), per_b)] * 7 + [pl.BlockSpec((NH, HD), full)] * 3
                 + [pl.BlockSpec((1, NH, HD, HD), lambda b: (l * bd + b, 0, 0, 0))],
        out_specs=(pl.BlockSpec((1, NH, HD), per_b), pl.BlockSpec((1, NH, HD, HD), lambda b: (b, 0, 0, 0))),
        compiler_params=_cp("parallel"),
    )(*(z.reshape(bd, NH, HD) for z in rs), *(z.reshape(NH, HD) for z in (rk, lnw, lnb)),
      s_all.reshape((-1,) + s_all.shape[2:]))


def _gelu_tanh(x):
    return 0.5 * x * (1.0 + jnp.tanh(math.sqrt(2.0 / math.pi) * (x + 0.044715 * (x * x * x))))


def _s5_kernel(u_ref, bre_ref, bim_ref, lre_ref, lim_ref, cre_ref, cim_ref, d_ref, wg_ref, bg_ref, h0r_ref, h0i_ref,
               y_ref, hr_ref, hi_ref, hre_scr, him_scr, *, nb, tb):
    @pl.when(pl.program_id(0) == 0)
    def _():
        hr_ref[...] = h0r_ref[...]
        hi_ref[...] = h0i_ref[...]

    u = u_ref[...]
    hre_scr[...] = _dot(u, bre_ref[...])
    him_scr[...] = _dot(u, bim_ref[...])
    lr = lre_ref[...]
    li = lim_ref[...]

    def body(t, carry):
        hr, hi = carry
        rows = pl.ds(pl.multiple_of(t * nb, nb), nb)
        nr = lr * hr - li * hi + hre_scr[rows, :]
        ni = lr * hi + li * hr + him_scr[rows, :]
        hre_scr[rows, :] = nr
        him_scr[rows, :] = ni
        return nr, ni

    hr, hi = lax.fori_loop(0, tb, body, (hr_ref[...], hi_ref[...]))
    hr_ref[...] = hr
    hi_ref[...] = hi
    y = _dot(hre_scr[...], cre_ref[...]) - _dot(him_scr[...], cim_ref[...]) + d_ref[...] * u
    y = _gelu_tanh(y)
    y_ref[...] = y * _sigmoid(_dot(y, wg_ref[...]) + bg_ref[...])


def _s5(u_tm, mats, h0r, h0i, nb, t):
    bre, bim, lre, lim, cre, cim, d, wg, bg = mats
    tb = min(64, t)
    full = lambda i: (0, 0)
    return pl.pallas_call(
        functools.partial(_s5_kernel, nb=nb, tb=tb),
        out_shape=(jax.ShapeDtypeStruct((t * nb, GW), F32), jax.ShapeDtypeStruct((nb, S5_W), F32),
                   jax.ShapeDtypeStruct((nb, S5_W), F32)),
        grid=(t // tb,),
        in_specs=[pl.BlockSpec((tb * nb, GW), lambda i: (i, 0)),
                  pl.BlockSpec((GW, S5_W), full), pl.BlockSpec((GW, S5_W), full),
                  pl.BlockSpec((1, S5_W), full), pl.BlockSpec((1, S5_W), full),
                  pl.BlockSpec((S5_W, GW), full), pl.BlockSpec((S5_W, GW), full),
                  pl.BlockSpec((1, GW), full), pl.BlockSpec((GW, GW), full), pl.BlockSpec((1, GW), full),
                  pl.BlockSpec((nb, S5_W), full), pl.BlockSpec((nb, S5_W), full)],
        out_specs=(pl.BlockSpec((tb * nb, GW), lambda i: (i, 0)),
                   pl.BlockSpec((nb, S5_W), full), pl.BlockSpec((nb, S5_W), full)),
        scratch_shapes=[pltpu.VMEM((tb * nb, S5_W), F32), pltpu.VMEM((tb * nb, S5_W), F32)],
        compiler_params=_cp("arbitrary"),
    )(u_tm, bre, bim, lre, lim, cre, cim, d, wg, bg, h0r, h0i)


def _s5_mats(a_re, a_im, b_re, b_im, c_re, c_im, d_skip, log_dt, w_glu, b_glu):
    dt = jnp.exp(log_dt)
    mag = jnp.exp(a_re * dt)
    lb_re, lb_im = mag * jnp.cos(a_im * dt), mag * jnp.sin(a_im * dt)
    den = a_re * a_re + a_im * a_im
    f_re = ((lb_re - 1.0) * a_re + lb_im * a_im) / den
    f_im = (lb_im * a_re - (lb_re - 1.0) * a_im) / den
    bb_re = f_re[..., None] * b_re - f_im[..., None] * b_im
    bb_im = f_re[..., None] * b_im + f_im[..., None] * b_re
    eye = jnp.eye(S5_G, dtype=F32)
    bd = lambda bb: jnp.einsum('gpc,gh->gchp', bb, eye).reshape(GW, S5_W)
    cd = lambda cc: jnp.einsum('gcp,gh->gphc', cc, eye).reshape(S5_W, GW)
    return (bd(bb_re), bd(bb_im), lb_re.reshape(1, S5_W), lb_im.reshape(1, S5_W), cd(c_re), cd(c_im),
            d_skip.reshape(1, GW), w_glu, b_glu.reshape(1, GW))


def _ffn_up_kernel(h_ref, wa_ref, wb_ref, cw_ref, cb_ref, s0_ref, s1_ref, y_ref, a_ref, carry_scr, *, seq):
    def gate(a, a1, a2, b):
        c = cb_ref[...] + a2 * cw_ref[0:1, :] + a1 * cw_ref[1:2, :] + a * cw_ref[2:3, :]
        return (c * _sigmoid(c) * b).astype(BF16)

    if not seq:
        h = h_ref[...]
        a = _dot(h, wa_ref[...])
        a_ref[...] = a
        y_ref[...] = gate(a, s1_ref[...], s0_ref[...], _dot(h, wb_ref[...]))
        return

    @pl.when(pl.program_id(2) == 0)
    def _():
        carry_scr[...] = s0_ref[0]

    hm = h_ref.shape[0] // FFN_PARTS
    rowid = _iota((hm, 1), 0)
    c0 = carry_scr[0:1, :]
    c1 = carry_scr[1:2, :]
    for p in range(FFN_PARTS):
        rows = slice(p * hm, (p + 1) * hm)
        h = h_ref[rows, :]
        a = _dot(h, wa_ref[...])
        b = _dot(h, wb_ref[...])
        a1 = jnp.where(rowid == 0, c1, pltpu.roll(a, 1, 0))
        a2 = jnp.where(rowid == 0, c0, jnp.where(rowid == 1, c1, pltpu.roll(a, 2, 0)))
        y_ref[rows, :] = gate(a, a1, a2, b)
        c0 = a[hm - 2:hm - 1, :]
        c1 = a[hm - 1:hm, :]
    last2 = jnp.concatenate([c0, c1], axis=0)
    carry_scr[...] = last2
    a_ref[0] = last2


def _ffn_up(h2, w_up, l, cw, cb, st, nb, t):
    n, d = h2.shape
    dff = w_up.shape[2] // 2
    tn = 512
    nj = dff // tn
    if t > 1:
        tm = min(MM_ROWS, t)
        nt = t // tm
        y, fc = pl.pallas_call(
            functools.partial(_ffn_up_kernel, seq=True),
            out_shape=(jax.ShapeDtypeStruct((n, dff), BF16), jax.ShapeDtypeStruct((nb, 2, dff), F32)),
            grid=(nb, nj, nt),
            in_specs=[pl.BlockSpec((tm, d), lambda b, j, i: (b * nt + i, 0)),
                      pl.BlockSpec((None, d, tn), lambda b, j, i: (l, 0, j)),
                      pl.BlockSpec((None, d, tn), lambda b, j, i: (l, 0, nj + j)),
                      pl.BlockSpec((3, tn), lambda b, j, i: (0, j)),
                      pl.BlockSpec((1, tn), lambda b, j, i: (0, j)),
                      pl.BlockSpec((1, 2, tn), lambda b, j, i: (b, 0, j)),
                      pl.BlockSpec((1, 2, tn), lambda b, j, i: (b, 0, j))],
            out_specs=(pl.BlockSpec((tm, tn), lambda b, j, i: (b * nt + i, j)),
                       pl.BlockSpec((1, 2, tn), lambda b, j, i: (b, 0, j))),
            scratch_shapes=[pltpu.VMEM((2, tn), F32)],
            compiler_params=_cp("parallel", "parallel", "arbitrary"),
        )(h2, w_up, w_up, cw, cb, st, st)
        return y, fc
    s0, s1 = st[:, 0, :], st[:, 1, :]
    y, a = pl.pallas_call(
        functools.partial(_ffn_up_kernel, seq=False),
        out_shape=(jax.ShapeDtypeStruct((n, dff), BF16), jax.ShapeDtypeStruct((n, dff), F32)),
        grid=(nj,),
        in_specs=[pl.BlockSpec((n, d), lambda j: (0, 0)),
                  pl.BlockSpec((None, d, tn), lambda j: (l, 0, j)),
                  pl.BlockSpec((None, d, tn), lambda j: (l, 0, nj + j)),
                  pl.BlockSpec((3, tn), lambda j: (0, j)),
                  pl.BlockSpec((1, tn), lambda j: (0, j)),
                  pl.BlockSpec((n, tn), lambda j: (0, j)),
                  pl.BlockSpec((n, tn), lambda j: (0, j))],
        out_specs=(pl.BlockSpec((n, tn), lambda j: (0, j)), pl.BlockSpec((n, tn), lambda j: (0, j))),
        scratch_shapes=[pltpu.VMEM((2, tn), F32)],
        compiler_params=_cp("parallel"),
    )(h2, w_up, w_up, cw, cb, s0, s1)
    return y, jnp.stack([s1, a], axis=1)


def _permute_w_in(w):
    cols = lambda s, n: w[:, :, s:s + n]
    parts = [cols(0, 4 * GW), cols(2064, GW), cols(2832, GW), cols(5208, GW), cols(3416, R_IN),
             cols(2576, 128), cols(2704, 128), cols(3344, HD), cols(2048, NH), cols(2056, NH), cols(3408, NH),
             jnp.zeros(w.shape[:2] + (NP - C_SM - HD - 3 * NH,), w.dtype)]
    return jnp.concatenate(parts, axis=2).astype(BF16)


def _layer(x2, nb, t, l, W, st, tables, cache):
    n = nb * t
    c0, n0, m0, rs0, rsh0, sre0, sim0, conv0 = st
    proj = _in_proj(x2, W['norm_mix'], W['w_in'], l)
    aq_r, iq_r, k_r, v_r, ik_r = _rope_call(proj, tables[0], tables[1], *tables[2])

    if cache is None:
        ym, c1, n1, m1 = _mlstm_prompt(proj, W['m_b_i'], W['m_b_f'], W['m_norm'], nb, t)
        ya = _dsa_prompt(iq_r, proj, ik_r, aq_r, k_r, v_r, nb, t)
    else:
        ym, c1, n1, m1 = _mlstm_step(proj, W['m_b_i'], W['m_b_f'], W['m_norm'], c0, l, n0, m0)
        ym = ym.reshape(n, GW)
        cki, ck, cv, page_table, n_pool = cache
        ya = _dsa_sample(l, page_table, iq_r, proj[:, C_SM + SM_IW:C_SM + SM_IW + NH], ik_r, aq_r, k_r, v_r,
                         cki, ck, cv, n_pool)

    prev = rsh0.reshape(nb, 1, R_IN) if t > 1 else rsh0
    rs = _rwkv_prep(proj, prev, W['r_mu'], W['r_w0'], W['r_w_w2'], W['r_a0'], W['r_w_a2'], W['r_w_g2'],
                    W['r_k_k'], W['r_k_a'], nb, t)
    if t > 1:
        yr, rs1 = _rwkv_scan(rs, W['r_r_k'], W['r_ln_w'], W['r_ln_b'], nb, t)
    else:
        yr, rs1 = _rwkv_step(rs, W['r_r_k'], W['r_ln_w'], W['r_ln_b'], rs0, l)
        yr = yr.reshape(n, GW)
    rsh1 = proj.reshape(nb, t, NP)[:, t - 1, C_RIN:C_RIN + R_IN]

    su = proj[:, C_SU:C_SU + GW]
    u_tm = su.reshape(nb, t, GW).transpose(1, 0, 2).reshape(t * nb, GW)
    ys_tm, sre1, sim1 = _s5(u_tm, W['s5'], sre0.reshape(nb, S5_W), sim0.reshape(nb, S5_W), nb, t)
    ys = ys_tm.reshape(t, nb, GW).transpose(1, 0, 2).reshape(n, GW)

    x2 = _res_matmul(x2, [ym, ya, yr, ys], W['w_out'], l)
    h2 = _rmsnorm(x2, W['norm_ffn'], BF16)
    y, conv1 = _ffn_up(h2, W['ffn_w_up'], l, W['ffn_conv_w'], W['ffn_conv_b'], conv0, nb, t)
    x2 = _res_matmul(x2, [y], W['ffn_w_down'], l)
    outs = (k_r.reshape(nb, t, A_KV, HD), v_r.reshape(nb, t, A_KV, HD), ik_r.reshape(nb, t, HD),
            c1, n1, m1.reshape(nb, NH), rs1, rsh1, sre1.reshape(nb, S5_G, S5_P), sim1.reshape(nb, S5_G, S5_P), conv1)
    return x2, outs


def kernel(x_prompt, x_sample, cache_k, cache_v, cache_kidx, page_table, state_mlstm_c, state_mlstm_n,
           state_mlstm_m, state_rwkv_s, state_rwkv_shift, state_s5_re, state_s5_im, state_ffn_conv,
           norm_mix, w_in, w_out, m_b_i, m_b_f, m_norm, r_mu, r_w0, r_w_w2, r_a0, r_w_a2, r_w_g2,
           r_k_k, r_k_a, r_r_k, r_ln_w, r_ln_b, s5_a_re, s5_a_im, s5_b_re, s5_b_im, s5_c_re, s5_c_im,
           s5_d, s5_log_dt, s5_w_glu, s5_b_glu, norm_ffn, ffn_w_up, ffn_conv_w, ffn_conv_b, ffn_w_down,
           norm_final):
    bp, tp, d = x_prompt.shape
    bs, ts, _ = x_sample.shape
    assert ts == 1 and tp % CHUNK == 0
    depth = w_in.shape[0]
    n_pool = cache_k.shape[1]
    past = page_table.shape[1] * PAGE
    dff = ffn_conv_b.shape[-1]

    row = lambda z: z.reshape(1, -1)
    layers = []
    w_in_all, w_out_all = _permute_w_in(w_in), w_out.astype(BF16)
    w_up_all, w_down_all = ffn_w_up.astype(BF16), ffn_w_down.astype(BF16)
    for l in range(depth):
        layers.append(dict(
            norm_mix=row(norm_mix[l]), w_in=w_in_all, w_out=w_out_all,
            m_b_i=row(m_b_i[l]), m_b_f=row(m_b_f[l]), m_norm=row(m_norm[l]),
            r_mu=row(r_mu[l]), r_w0=row(r_w0[l]), r_w_w2=r_w_w2[l], r_a0=row(r_a0[l]), r_w_a2=r_w_a2[l],
            r_w_g2=r_w_g2[l], r_k_k=row(r_k_k[l]), r_k_a=row(r_k_a[l]), r_r_k=row(r_r_k[l]),
            r_ln_w=row(r_ln_w[l]), r_ln_b=row(r_ln_b[l]),
            s5=_s5_mats(s5_a_re[l], s5_a_im[l], s5_b_re[l], s5_b_im[l], s5_c_re[l], s5_c_im[l], s5_d[l],
                        s5_log_dt[l], s5_w_glu[l], s5_b_glu[l]),
            norm_ffn=row(norm_ffn[l]), ffn_w_up=w_up_all, ffn_conv_w=ffn_conv_w[l],
            ffn_conv_b=row(ffn_conv_b[l]), ffn_w_down=w_down_all))

    cos_p, sin_p = _rope_tables(jnp.arange(tp))
    cos_s, sin_s = _rope_tables(jnp.full((bs,), past))
    tab_p = (cos_p, sin_p, (bp, tp))
    tab_s = (cos_s, sin_s, (1, bs))

    zeros = lambda *s: jnp.zeros(s, F32)
    st_p = (zeros(bp, NH, HD, HD), zeros(bp, NH, HD), zeros(bp, NH), zeros(bp, NH, HD, HD), zeros(bp, R_IN),
            zeros(bp, S5_G, S5_P), zeros(bp, S5_G, S5_P), zeros(bp, 2, dff))
    cki = cache_kidx.reshape(depth * n_pool, PAGE, HD).transpose(0, 2, 1)
    ck = cache_k.reshape(depth * n_pool, PAGE, A_KV * HD).transpose(0, 2, 1)
    cv = cache_v.reshape(depth * n_pool, PAGE, A_KV * HD).transpose(0, 2, 1)

    xp = x_prompt.reshape(bp * tp, d)
    xs = x_sample.reshape(bs, d)
    new_p, new_s = [], []
    for l in range(depth):
        xp, sp = _layer(xp, bp, tp, l, layers[l], st_p, tab_p, None)
        st_s = (state_mlstm_c, state_mlstm_n[l], state_mlstm_m[l], state_rwkv_s, state_rwkv_shift[l],
                state_s5_re[l], state_s5_im[l], state_ffn_conv[l])
        xs, ss = _layer(xs, bs, 1, l, layers[l], st_s, tab_s, (cki, ck, cv, page_table, n_pool))
        new_p.append(sp)
        new_s.append(ss)
    (k_p, v_p, ki_p, mc_p, mn_p, mm_p, rs_p, rsh_p, sre_p, sim_p, fc_p) = [jnp.stack(z) for z in zip(*new_p)]
    (k_s, v_s, ki_s, mc_s, mn_s, mm_s, rs_s, rsh_s, sre_s, sim_s, fc_s) = [jnp.stack(z) for z in zip(*new_s)]
    y_prompt = _rmsnorm(xp, row(norm_final), F32).reshape(bp, tp, d)
    y_sample = _rmsnorm(xs, row(norm_final), F32).reshape(bs, ts, d)
    return (y_prompt, y_sample, k_p, k_s, v_p, v_s, ki_p, ki_s, mc_p, mc_s, mn_p, mn_s, mm_p, mm_s,
            rs_p, rs_s, rsh_p, rsh_s, sre_p, sre_s, sim_p, sim_s, fc_p, fc_s)
```

```python
import functools
import math

import jax
import jax.numpy as jnp
from jax import lax
from jax.experimental import pallas as pl
from jax.experimental.pallas import tpu as pltpu

F32 = jnp.float32
BF16 = jnp.bfloat16
HI = lax.Precision.HIGHEST

HD = 64
NH = 8
GW = NH * HD
A_KV = 2
PAGE = 128
TOPK_MAX = 256
ROPE_THETA = 10000.0
R_IN = 3 * GW + 64 + 64 + 128
R_LN_EPS = 64e-5
S5_G, S5_CH, S5_P = 32, 16, 64
S5_W = S5_G * S5_P
NORM_EPS = 1e-6
CHUNK = 64
IDX_SCALE = HD ** -0.5 * NH ** -0.5

C_M, C_AQ, C_IQ, C_SU, C_RIN, C_AK, C_AV, C_SM = 0, 2048, 2560, 3072, 3584, 5376, 5504, 5632
NP = 6144
SM_IK, SM_MI, SM_MF, SM_IW = 0, 64, 72, 80

VMEM_LIMIT = 56 * 1024 * 1024
MM_ROWS = 1024
RWKV_SEQS = 4
FFN_PARTS = 2


def _cp(*sem):
    return pltpu.CompilerParams(dimension_semantics=sem, vmem_limit_bytes=VMEM_LIMIT)


def _dot(a, b, prec=None):
    return jnp.dot(a, b, preferred_element_type=F32, precision=prec)


def _dot_nt(a, b, prec=None):
    return lax.dot_general(a, b, (((1,), (1,)), ((), ())), preferred_element_type=F32, precision=prec)


def _dot_tn(a, b, prec=None):
    return lax.dot_general(a, b, (((0,), (0,)), ((), ())), preferred_element_type=F32, precision=prec)


def _sigmoid(x):
    return 1.0 / (1.0 + jnp.exp(-x))


def _softplus(x):
    return jnp.maximum(x, 0.0) + jnp.log(1.0 + jnp.exp(-jnp.abs(x)))


def _iota(shape, dim):
    return lax.broadcasted_iota(jnp.int32, shape, dim)


HG = 4
BD = HG * HD
assert CHUNK == HD


def _bd_masks(c):
    row, col = _iota((HG * c, HG * c), 0), _iota((HG * c, HG * c), 1)
    same = (row // c) == (col // c)
    t, s = row % c, col % c
    return same, jnp.logical_and(same, s <= t), jnp.logical_and(same, s < t)


def _bd_diag(x):
    nb, ng = x.shape[:2]
    x6 = x.reshape(nb, ng, HG, HD, HG, HD)
    return jnp.stack([x6[:, :, j, :, j, :] for j in range(HG)], axis=2).reshape(nb, ng * HG, HD, HD)


def _inproj_kernel(x_ref, g_ref, w_ref, o_ref, h_scr):
    @pl.when(pl.program_id(1) == 0)
    def _():
        x = x_ref[...]
        ms = jnp.mean(x * x, axis=-1, keepdims=True)
        h_scr[...] = (x * lax.rsqrt(ms + NORM_EPS) * g_ref[...]).astype(BF16)

    o_ref[...] = _dot(h_scr[...], w_ref[...])


def _in_proj(x2, g, w, l):
    n, d = x2.shape
    npad = w.shape[2]
    tm = min(MM_ROWS, n)
    tn = 768
    return pl.pallas_call(
        _inproj_kernel,
        out_shape=jax.ShapeDtypeStruct((n, npad), F32),
        grid=(n // tm, npad // tn),
        in_specs=[pl.BlockSpec((tm, d), lambda i, j: (i, 0)),
                  pl.BlockSpec((1, d), lambda i, j: (0, 0)),
                  pl.BlockSpec((None, d, tn), lambda i, j: (l, 0, j))],
        out_specs=pl.BlockSpec((tm, tn), lambda i, j: (i, j)),
        scratch_shapes=[pltpu.VMEM((tm, d), BF16)],
        compiler_params=_cp("parallel", "arbitrary"),
    )(x2, g, w)


def _rmsnorm_kernel(x_ref, g_ref, o_ref):
    x = x_ref[...]
    ms = jnp.mean(x * x, axis=-1, keepdims=True)
    o_ref[...] = (x * lax.rsqrt(ms + NORM_EPS) * g_ref[...]).astype(o_ref.dtype)


def _rmsnorm(x2, g, dtype):
    n, d = x2.shape
    tm = min(512, n)
    return pl.pallas_call(
        _rmsnorm_kernel,
        out_shape=jax.ShapeDtypeStruct((n, d), dtype),
        grid=(n // tm,),
        in_specs=[pl.BlockSpec((tm, d), lambda i: (i, 0)), pl.BlockSpec((1, d), lambda i: (0, 0))],
        out_specs=pl.BlockSpec((tm, d), lambda i: (i, 0)),
        compiler_params=_cp("parallel"),
    )(x2, g)


def _resmm_kernel(r_ref, *refs):
    y_refs, w_ref, o_ref = refs[:-2], refs[-2], refs[-1]
    acc = r_ref[...]
    k0 = 0
    for y_ref in y_refs:
        kw = y_ref.shape[1]
        acc = acc + _dot(y_ref[...].astype(BF16), w_ref[k0:k0 + kw, :])
        k0 += kw
    o_ref[...] = acc


def _res_matmul(res, ys, w, l):
    n = res.shape[0]
    k, d = w.shape[1:]
    assert sum(y.shape[1] for y in ys) == k
    tm = min(MM_ROWS, n)
    tn = 512
    return pl.pallas_call(
        _resmm_kernel,
        out_shape=jax.ShapeDtypeStruct((n, d), F32),
        grid=(n // tm, d // tn),
        in_specs=[pl.BlockSpec((tm, tn), lambda i, j: (i, j))]
                 + [pl.BlockSpec((tm, y.shape[1]), lambda i, j: (i, 0)) for y in ys]
                 + [pl.BlockSpec((None, k, tn), lambda i, j: (l, 0, j))],
        out_specs=pl.BlockSpec((tm, tn), lambda i, j: (i, j)),
        compiler_params=_cp("parallel", "arbitrary"),
    )(res, *ys, w)


def _rope(x, cos, sin):
    w = x.shape[1]
    first = (_iota(x.shape, 1) & (HD - 1)) < HD // 2
    sw = jnp.where(first, pltpu.roll(x, w - HD // 2, 1), pltpu.roll(x, HD // 2, 1))
    return x * cos + sw * sin


def _rope_kernel(aq_ref, iq_ref, ak_ref, av_ref, sm_ref, cos_ref, sin_ref, aqo, iqo, ko, vo, iko):
    cos = cos_ref[...]
    sin = sin_ref[...]
    aqo[...] = _rope(aq_ref[...], cos, sin)
    iqo[...] = _rope(iq_ref[...], cos, sin)
    ko[...] = _rope(ak_ref[...], cos[:, :128], sin[:, :128])
    vo[...] = av_ref[...]
    iko[...] = _rope(sm_ref[...], cos[:, :128], sin[:, :128])[:, :HD]


def _rope_call(proj, cos, sin, nb, nt_rows):
    n = proj.shape[0]
    tm = min(512, nt_rows)
    nt = nt_rows // tm
    row = lambda b, i: b * nt + i
    return pl.pallas_call(
        _rope_kernel,
        out_shape=(jax.ShapeDtypeStruct((n, GW), F32), jax.ShapeDtypeStruct((n, GW), F32),
                   jax.ShapeDtypeStruct((n, 128), F32), jax.ShapeDtypeStruct((n, 128), F32),
                   jax.ShapeDtypeStruct((n, HD), F32)),
        grid=(nb, nt),
        in_specs=[pl.BlockSpec((tm, GW), lambda b, i: (row(b, i), C_AQ // GW)),
                  pl.BlockSpec((tm, GW), lambda b, i: (row(b, i), C_IQ // GW)),
                  pl.BlockSpec((tm, 128), lambda b, i: (row(b, i), C_AK // 128)),
                  pl.BlockSpec((tm, 128), lambda b, i: (row(b, i), C_AV // 128)),
                  pl.BlockSpec((tm, 128), lambda b, i: (row(b, i), C_SM // 128)),
                  pl.BlockSpec((tm, GW), lambda b, i: (i, 0)),
                  pl.BlockSpec((tm, GW), lambda b, i: (i, 0))],
        out_specs=(pl.BlockSpec((tm, GW), lambda b, i: (row(b, i), 0)),
                   pl.BlockSpec((tm, GW), lambda b, i: (row(b, i), 0)),
                   pl.BlockSpec((tm, 128), lambda b, i: (row(b, i), 0)),
                   pl.BlockSpec((tm, 128), lambda b, i: (row(b, i), 0)),
                   pl.BlockSpec((tm, HD), lambda b, i: (row(b, i), 0))),
        compiler_params=_cp("parallel", "parallel"),
    )(proj, proj, proj, proj, proj, cos, sin)


def _rope_tables(pos):
    half = HD // 2
    inv = ROPE_THETA ** (-jnp.arange(half, dtype=F32) / half)
    ang = pos.astype(F32)[:, None] * inv[None, :]
    cos, sin = jnp.cos(ang), jnp.sin(ang)
    cos64 = jnp.concatenate([cos, cos], axis=-1)
    sin64 = jnp.concatenate([-sin, sin], axis=-1)
    return jnp.tile(cos64, (1, NH)), jnp.tile(sin64, (1, NH))


def _kth_largest(sc, extra, kk):
    kf = jnp.float32(kk)

    def count_ge(c):
        n = jnp.sum(jnp.where(sc >= c, 1.0, 0.0), axis=-1, keepdims=True)
        if extra is not None:
            n = n + jnp.where(extra >= c, 1.0, 0.0)
        return n

    def key_to_f(key):
        bits = key ^ ((key >> 31) & jnp.int32(0x7FFFFFFF))
        return lax.bitcast_convert_type(bits, F32)

    r = sc.shape[0]
    int_min = jnp.int32(-2 ** 31)
    lo = jnp.where(count_ge(jnp.zeros((r, 1), F32)) >= kf, jnp.int32(0), int_min)

    def body(j, lo):
        cand = lo + jnp.left_shift(jnp.int32(1), jnp.int32(30) - j)
        ok = count_ge(key_to_f(cand)) >= kf
        return jnp.where(ok, cand, lo)

    lo = lax.fori_loop(0, 31, body, lo)
    key_neg_inf = jnp.int32(-2 ** 31 + 0x7FFFFF)
    return jnp.where(lo <= key_neg_inf, -jnp.inf, key_to_f(lo))


def _strict_upper_bf16(n):
    return jnp.where(_iota((n, n), 0) < _iota((n, n), 1), 1.0, 0.0).astype(BF16)


def _split_bf16(x):
    hi = x.astype(BF16).astype(F32)
    return hi, x - hi


def _dsa_prompt_kernel(iq_ref, sm_ref, ik_ref, aq_ref, k_ref, v_ref, o_ref, sel_scr, kcat_scr, *, topk, qb, n_ext):
    t_keys = ik_ref.shape[0]
    i = pl.program_id(1)
    per_ext = (t_keys // qb) // n_ext

    @pl.when(i == 0)
    def _():
        hi, lo = _split_bf16(ik_ref[...])
        kcat_scr[...] = jnp.concatenate([hi, lo, hi], axis=1)

    iq = iq_ref[...]
    aq = aq_ref[...] * (HD ** -0.5 * math.log2(math.e))
    wts = sm_ref[:, SM_IW:SM_IW + NH] * IDX_SCALE
    tq = i * qb + _iota((qb, 1), 0)

    def body(ext):
        kcat = kcat_scr[0:ext, :]
        sc = jnp.zeros((qb, ext), F32)
        for h in range(NH):
            hi, lo = _split_bf16(iq[:, h * HD:(h + 1) * HD])
            qk = _dot_nt(jnp.concatenate([hi, hi, lo], axis=1), kcat)
            sc = sc + jnp.maximum(qk, 0.0) * wts[:, h:h + 1]
        causal = _iota((1, ext), 1) <= tq
        sc = jnp.where(causal, sc, -jnp.inf)

        thr = _kth_largest(sc, None, topk)
        gt = sc > thr
        eq = sc == thr
        n_gt = jnp.sum(jnp.where(gt, 1.0, 0.0), axis=-1, keepdims=True)
        n_eq = jnp.sum(jnp.where(eq, 1.0, 0.0), axis=-1, keepdims=True)
        need = jnp.float32(topk) - n_gt
        sel_scr[:, 0:ext] = jnp.where(jnp.logical_and(sc >= thr, causal), 0.0, -jnp.inf)
        tie = jnp.logical_and(n_eq > need, thr > -jnp.inf)

        @pl.when(jnp.max(jnp.where(tie, 1.0, 0.0)) > 0.5)
        def _():
            ut = _strict_upper_bf16(128)
            run = jnp.zeros((qb, 1), F32)
            for c in range(ext // 128):
                sl = slice(c * 128, (c + 1) * 128)
                eqc = jnp.where(eq[:, sl], 1.0, 0.0)
                pref = _dot(eqc.astype(BF16), ut) + run
                keep = jnp.logical_or(gt[:, sl], jnp.logical_and(eq[:, sl], pref < need))
                sel_scr[:, sl] = jnp.where(jnp.logical_and(keep, causal[:, sl]), 0.0, -jnp.inf)
                run = run + jnp.sum(eqc, axis=-1, keepdims=True)

        bias = sel_scr[:, 0:ext]
        for g in range(A_KV):
            kg = k_ref[0:ext, g * HD:(g + 1) * HD]
            vg = v_ref[0:ext, g * HD:(g + 1) * HD]
            for j in range(NH // A_KV):
                h = g * (NH // A_KV) + j
                s = _dot_nt(aq[:, h * HD:(h + 1) * HD], kg) + bias
                m = jnp.max(s, axis=-1, keepdims=True)
                p = jnp.exp2(s - m)
                l = jnp.sum(p, axis=-1, keepdims=True)
                o_ref[:, h * HD:(h + 1) * HD] = _dot(p, vg) / l

    for j in range(n_ext):
        pl.when(i // per_ext == j)(functools.partial(body, (j + 1) * (t_keys // n_ext)))


def _dsa_prompt(iq_r, proj, ik_r, aq_r, k_r, v_r, nb, t):
    qb = min(128, t)
    nq = t // qb
    n_ext = min(4, nq)
    assert nq % n_ext == 0
    topk = min(TOPK_MAX, t // 4)
    n = nb * t
    row = lambda b, i: b * nq + i
    return pl.pallas_call(
        functools.partial(_dsa_prompt_kernel, topk=topk, qb=qb, n_ext=n_ext),
        out_shape=jax.ShapeDtypeStruct((n, GW), F32),
        grid=(nb, nq),
        in_specs=[pl.BlockSpec((qb, GW), lambda b, i: (row(b, i), 0)),
                  pl.BlockSpec((qb, 128), lambda b, i: (row(b, i), C_SM // 128)),
                  pl.BlockSpec((t, HD), lambda b, i: (b, 0)),
                  pl.BlockSpec((qb, GW), lambda b, i: (row(b, i), 0)),
                  pl.BlockSpec((t, 128), lambda b, i: (b, 0)),
                  pl.BlockSpec((t, 128), lambda b, i: (b, 0))],
        out_specs=pl.BlockSpec((qb, GW), lambda b, i: (row(b, i), 0)),
        scratch_shapes=[pltpu.VMEM((qb, t), F32), pltpu.VMEM((t, 3 * HD), F32)],
        compiler_params=_cp("parallel", "arbitrary"),
    )(iq_r, proj, ik_r, aq_r, k_r, v_r)


def _dsa_sample_score_kernel(*refs, n_pages):
    _, iq_ref, w_ref, ikn_ref = refs[:4]
    pages = refs[4:4 + n_pages]
    sc_ref, sn_ref = refs[4 + n_pages:]
    iq = iq_ref[0]
    w = w_ref[0] * IDX_SCALE
    k_hi, k_lo = _split_bf16(jnp.concatenate([pages[c][0] for c in range(n_pages)], axis=1))
    q_hi, q_lo = _split_bf16(iq)
    qk = _dot(jnp.concatenate([q_hi, q_hi, q_lo], axis=1), jnp.concatenate([k_hi, k_lo, k_hi], axis=0))
    sc_ref[0] = jnp.sum(jnp.maximum(qk, 0.0) * w, axis=0, keepdims=True)
    qkn = jnp.sum(iq * ikn_ref[0], axis=-1, keepdims=True)
    sn = jnp.sum(jnp.maximum(qkn, 0.0) * w, axis=0, keepdims=True)
    sn_ref[0] = jnp.broadcast_to(sn, (1, 128))


def _dsa_sample_select_kernel(sc_ref, sn_ref, sel_ref, seln_ref, *, topk):
    sc = sc_ref[...]
    sn = sn_ref[:, 0:1]
    bd, s_keys = sc.shape
    thr = _kth_largest(sc, sn, topk)
    gt = sc > thr
    eq = sc == thr
    n_gt = jnp.sum(jnp.where(gt, 1.0, 0.0), axis=-1, keepdims=True) + jnp.where(sn > thr, 1.0, 0.0)
    need = jnp.float32(topk) - n_gt
    ut = _strict_upper_bf16(PAGE)
    run = jnp.zeros((bd, 1), F32)
    for c in range(s_keys // PAGE):
        sl = slice(c * PAGE, (c + 1) * PAGE)
        eqc = jnp.where(eq[:, sl], 1.0, 0.0)
        pref = _dot(eqc.astype(BF16), ut) + run
        keep = jnp.logical_or(gt[:, sl], jnp.logical_and(eq[:, sl], pref < need))
        sel_ref[:, sl] = jnp.where(keep, 1.0, 0.0)
        run = run + jnp.sum(eqc, axis=-1, keepdims=True)
    sel_new = jnp.logical_or(sn > thr, jnp.logical_and(sn == thr, run < need))
    seln_ref[...] = jnp.broadcast_to(jnp.where(sel_new, 1.0, 0.0), seln_ref.shape)


def _dsa_sample_attn_kernel(*refs, n_pages):
    _, aq_ref, kn_ref, vn_ref, sel_ref, seln_ref = refs[:6]
    kpages = refs[6:6 + n_pages]
    vpages = refs[6 + n_pages:6 + 2 * n_pages]
    o_ref, k_scr, v_scr = refs[6 + 2 * n_pages:]
    for c in range(n_pages):
        k_scr[:, c * PAGE:(c + 1) * PAGE] = kpages[c][0]
        v_scr[:, c * PAGE:(c + 1) * PAGE] = vpages[c][0]
    sel = sel_ref[0] > 0.5
    sel_new = seln_ref[0][:, 0:1] > 0.5
    aq = aq_ref[0]
    kn = kn_ref[0]
    vn = vn_ref[0]
    hpg = NH // A_KV
    for g in range(A_KV):
        qg = aq[g * hpg:(g + 1) * hpg, :]
        s = _dot(qg, k_scr[g * HD:(g + 1) * HD, :]) * HD ** -0.5
        s = jnp.where(sel, s, -jnp.inf)
        s_new = jnp.sum(qg * kn[:, g * HD:(g + 1) * HD], axis=-1, keepdims=True) * HD ** -0.5
        s_new = jnp.where(sel_new, s_new, -jnp.inf)
        m = jnp.maximum(jnp.max(s, axis=-1, keepdims=True), s_new)
        pr = jnp.exp(s - m)
        pn = jnp.exp(s_new - m)
        l = jnp.sum(pr, axis=-1, keepdims=True) + pn
        o = _dot_nt(pr, v_scr[g * HD:(g + 1) * HD, :]) + pn * vn[:, g * HD:(g + 1) * HD]
        o_ref[0, g * hpg:(g + 1) * hpg, :] = o / l


def _dsa_sample(layer, page_table, iq_r, iw, ik_r, aq_r, k_r, v_r, cki, ck, cv, n_pool):
    bd, n_pages = page_table.shape
    past = n_pages * PAGE
    topk = min(TOPK_MAX, (past + 1) // 4)
    base = layer * n_pool
    per_b = lambda b, pt: (b, 0, 0)

    def page_specs(width):
        return [pl.BlockSpec((1, width, PAGE), lambda b, pt, c=c: (base + pt[b, c], 0, 0)) for c in range(n_pages)]

    sc, sn = pl.pallas_call(
        functools.partial(_dsa_sample_score_kernel, n_pages=n_pages),
        out_shape=(jax.ShapeDtypeStruct((bd, 1, past), F32), jax.ShapeDtypeStruct((bd, 1, 128), F32)),
        grid_spec=pltpu.PrefetchScalarGridSpec(
            num_scalar_prefetch=1,
            grid=(bd,),
            in_specs=[pl.BlockSpec((1, NH, HD), per_b), pl.BlockSpec((1, NH, 1), per_b),
                      pl.BlockSpec((1, 1, HD), per_b)] + page_specs(HD),
            out_specs=(pl.BlockSpec((1, 1, past), per_b), pl.BlockSpec((1, 1, 128), per_b))),
        compiler_params=_cp("arbitrary"),
    )(page_table, iq_r.reshape(bd, NH, HD), iw.reshape(bd, NH, 1), ik_r.reshape(bd, 1, HD), *([cki] * n_pages))

    sel, seln = pl.pallas_call(
        functools.partial(_dsa_sample_select_kernel, topk=topk),
        out_shape=(jax.ShapeDtypeStruct((bd, past), F32), jax.ShapeDtypeStruct((bd, 128), F32)),
    )(sc.reshape(bd, past), sn.reshape(bd, 128))

    out = pl.pallas_call(
        functools.partial(_dsa_sample_attn_kernel, n_pages=n_pages),
        out_shape=jax.ShapeDtypeStruct((bd, NH, HD), F32),
        grid_spec=pltpu.PrefetchScalarGridSpec(
            num_scalar_prefetch=1,
            grid=(bd,),
            in_specs=[pl.BlockSpec((1, NH, HD), per_b), pl.BlockSpec((1, 1, 128), per_b),
                      pl.BlockSpec((1, 1, 128), per_b), pl.BlockSpec((1, 1, past), per_b),
                      pl.BlockSpec((1, 1, 128), per_b)] + page_specs(128) + page_specs(128),
            out_specs=pl.BlockSpec((1, NH, HD), per_b),
            scratch_shapes=[pltpu.VMEM((128, past), F32), pltpu.VMEM((128, past), F32)]),
        compiler_params=_cp("arbitrary"),
    )(page_table, aq_r.reshape(bd, NH, HD), k_r.reshape(bd, 1, 128), v_r.reshape(bd, 1, 128),
      sel.reshape(bd, 1, past), seln.reshape(bd, 1, 128), *([ck] * n_pages), *([cv] * n_pages))
    return out.reshape(bd, GW)


def _log_sigmoid(x):
    return jnp.minimum(x, 0.0) - jnp.log(1.0 + jnp.exp(-jnp.abs(x)))


MLSTM_ROWS = 2 * CHUNK


def _mlstm_prompt_kernel(m_ref, sm_ref, gb_ref, nw_ref, y_ref, c_ref, n_ref, mm_ref):
    L = CHUNK

    @pl.when(pl.program_id(1) == 0)
    def _():
        c_ref[...] = jnp.zeros(c_ref.shape, F32)
        n_ref[...] = jnp.zeros(n_ref.shape, F32)
        mm_ref[...] = jnp.zeros(mm_ref.shape, F32)

    ri, ci = _iota((L, L), 0), _iota((L, L), 1)
    tril = jnp.where(ci <= ri, 1.0, 0.0)
    triu = jnp.where(ci >= ri, 1.0, 0.0)
    same, causal, _ = _bd_masks(L)
    tile = lambda z: jnp.concatenate([z] * HG, axis=0)
    msk = lambda z: jnp.where(same, z, 0.0)
    stack_cols = lambda z, g: jnp.concatenate([z[:, g * HG + j:g * HG + j + 1] for j in range(HG)], axis=0)
    stack_rows = lambda z, g: jnp.concatenate([z[g * HG + j:g * HG + j + 1, :] for j in range(HG)], axis=1)
    percol = lambda zs: jnp.concatenate([jnp.broadcast_to(z, (L, 1)) for z in zs], axis=0)
    perrow = lambda zs: jnp.concatenate([jnp.broadcast_to(z, (1, HD)) for z in zs], axis=1)

    n_seq, n_grp = m_ref.shape[0], NH // HG
    chains = [(bb, g) for bb in range(n_seq) for g in range(n_grp)]
    each = lambda f: {c: f(c) for c in chains}
    lanes = lambda c: slice(c[1] * BD, (c[1] + 1) * BD)
    heads = lambda c: range(c[1] * HG, (c[1] + 1) * HG)
    gates = {}
    for bb in range(n_seq):
        smb = sm_ref[bb] + gb_ref[...]
        smt = smb.T
        gates[bb] = dict(ig=smb[:, SM_MI:SM_MI + NH], lf=_log_sigmoid(smb[:, SM_MF:SM_MF + NH]),
                         igt=smt[SM_MI:SM_MI + NH, :], lft=_log_sigmoid(smt[SM_MF:SM_MF + NH, :]))
    cbd = each(lambda c: c_ref[c[0], c[1]])
    nrow = each(lambda c: n_ref[c[0], :, lanes(c)])
    mprev = each(lambda c: [mm_ref[c[0], :, h:h + 1] for h in heads(c)])
    for cc in range(MLSTM_ROWS // L):
        rows = slice(cc * L, (cc + 1) * L)
        ig, bcs, bcst, igt = {}, {}, {}, {}
        for bb in range(n_seq):
            ig[bb] = gates[bb]['ig'][rows]
            bcs[bb] = _dot(tril, gates[bb]['lf'][rows], HI)
            bcst[bb] = _dot(gates[bb]['lft'][:, rows], triu, HI)
            igt[bb] = gates[bb]['igt'][:, rows]
        col = lambda c, w: m_ref[c[0], rows, w * GW + c[1] * BD:w * GW + (c[1] + 1) * BD]
        qexp = each(lambda c: msk(tile(col(c, 0))))
        kt = each(lambda c: tile(col(c, 1) * HD ** -0.5))
        kexp = each(lambda c: msk(kt[c]))
        vexp = each(lambda c: msk(tile(col(c, 2))))
        bcol = each(lambda c: stack_cols(bcs[c[0]], c[1]))
        icol = each(lambda c: stack_cols(ig[c[0]], c[1]))
        bl = each(lambda c: [bcs[c[0]][L - 1:L, h:h + 1] for h in heads(c)])
        dmat = each(lambda c: jnp.where(
            causal, bcol[c] - stack_rows(bcst[c[0]], c[1]) + stack_rows(igt[c[0]], c[1]), -jnp.inf))
        inter = each(lambda c: bcol[c] + percol(mprev[c]))
        mj = each(lambda c: jnp.maximum(inter[c], jnp.max(dmat[c], axis=-1, keepdims=True)))
        qk = each(lambda c: _dot_nt(qexp[c], kt[c]))
        s = each(lambda c: qk[c] * jnp.exp(dmat[c] - mj[c]))
        iw = each(lambda c: jnp.exp(inter[c] - mj[c]))
        sv = each(lambda c: _dot(s[c], vexp[c]))
        qc = each(lambda c: _dot_nt(qexp[c], cbd[c]))
        wl = each(lambda c: percol(bl[c]) - bcol[c] + icol[c])
        m_new = each(lambda c: [jnp.maximum(bl[c][j] + mprev[c][j],
                                            jnp.max(wl[c][j * L:(j + 1) * L], axis=0, keepdims=True))
                                for j in range(HG)])
        dec = each(lambda c: [jnp.exp(bl[c][j] + mprev[c][j] - m_new[c][j]) for j in range(HG)])
        ws = each(lambda c: jnp.exp(wl[c] - percol(m_new[c])))
        upd = each(lambda c: _dot_tn(vexp[c] * ws[c], kexp[c]))
        for c in chains:
            bb, ln = c[0], lanes(c)
            den = (jnp.sum(s[c], axis=-1, keepdims=True)
                   + iw[c] * jnp.sum(qexp[c] * nrow[c], axis=-1, keepdims=True))
            hc = (sv[c] + iw[c] * qc[c]) / jnp.maximum(jnp.abs(den), jnp.exp(-mj[c]))
            hn = hc * lax.rsqrt(jnp.sum(hc * hc, axis=-1, keepdims=True) * (1.0 / HD) + NORM_EPS)
            hn = sum(hn[j * L:(j + 1) * L] for j in range(HG))
            y_ref[bb, rows, ln] = hn * nw_ref[:, ln] * _sigmoid(col(c, 3))
        cbd = each(lambda c: percol(dec[c]) * cbd[c] + upd[c])
        nrow = each(lambda c: perrow(dec[c]) * nrow[c] + jnp.sum(kexp[c] * ws[c], axis=0, keepdims=True))
        mprev = m_new
    for c in chains:
        c_ref[c[0], c[1]] = cbd[c]
        n_ref[c[0], :, lanes(c)] = nrow[c]
        for j, h in enumerate(heads(c)):
            mm_ref[c[0], :, h:h + 1] = mprev[c][j]


def _gate_bias_row(b_i, b_f):
    z = lambda n: jnp.zeros((1, n), F32)
    return jnp.concatenate([z(SM_MI), b_i, b_f, z(128 - SM_MF - NH)], axis=1)


def _mlstm_prompt(proj, b_i, b_f, nw, nb, t):
    tm = MLSTM_ROWS
    nc = t // tm
    ng = NH // HG
    sb = RWKV_SEQS if nb % RWKV_SEQS == 0 else 1
    proj3 = proj.reshape(nb, t, NP)
    y, cbd, nrow, m = pl.pallas_call(
        _mlstm_prompt_kernel,
        out_shape=(jax.ShapeDtypeStruct((nb, t, GW), F32), jax.ShapeDtypeStruct((nb, ng, BD, BD), F32),
                   jax.ShapeDtypeStruct((nb, 1, GW), F32), jax.ShapeDtypeStruct((nb, 1, NH), F32)),
        grid=(nb // sb, nc),
        in_specs=[pl.BlockSpec((sb, tm, 4 * GW), lambda b, c: (b, c, 0)),
                  pl.BlockSpec((sb, tm, 128), lambda b, c: (b, c, C_SM // 128)),
                  pl.BlockSpec((1, 128), lambda b, c: (0, 0)),
                  pl.BlockSpec((1, GW), lambda b, c: (0, 0))],
        out_specs=(pl.BlockSpec((sb, tm, GW), lambda b, c: (b, c, 0)),
                   pl.BlockSpec((sb, ng, BD, BD), lambda b, c: (b, 0, 0, 0)),
                   pl.BlockSpec((sb, 1, GW), lambda b, c: (b, 0, 0)),
                   pl.BlockSpec((sb, 1, NH), lambda b, c: (b, 0, 0))),
        compiler_params=_cp("parallel", "arbitrary"),
    )(proj3, proj3, _gate_bias_row(b_i, b_f), nw)
    return y.reshape(nb * t, GW), _bd_diag(cbd), nrow.reshape(nb, NH, HD), m


def _head_expanders():
    diag = (_iota((NH * HD, HD), 0) % HD) == _iota((NH * HD, HD), 1)

    def rep(z):
        return jnp.concatenate([jnp.broadcast_to(z[h:h + 1, :], (HD, z.shape[1])) for h in range(NH)], axis=0)

    def fold(col):
        m = jnp.where(diag, col, 0.0)
        return jnp.concatenate([jnp.sum(m[h * HD:(h + 1) * HD], axis=0, keepdims=True) for h in range(NH)], axis=0)

    return rep, fold, diag


def _mlstm_step_kernel(x_ref, gt_ref, gb_ref, nw_ref, c_ref, n_ref, mm_ref, y_ref, co_ref, no_ref, mo_ref):
    x = x_ref[0]
    q, k, v, og = x[0:NH], x[NH:2 * NH] * HD ** -0.5, x[2 * NH:3 * NH], x[3 * NH:4 * NH]
    gates = gt_ref[0] + gb_ref[...]
    ig = gates[:, 0:1]
    rep, fold, diag = _head_expanders()
    c0 = c_ref[0].reshape(NH * HD, HD)
    n0 = n_ref[0]
    inter = _log_sigmoid(gates[:, 1:2]) + mm_ref[0]
    mj = jnp.maximum(inter, ig)
    s = jnp.sum(q * k, axis=-1, keepdims=True) * jnp.exp(ig - mj)
    iw = jnp.exp(inter - mj)
    ws = jnp.exp(ig - mj)
    den = s + iw * jnp.sum(n0 * q, axis=-1, keepdims=True)
    inv = 1.0 / jnp.maximum(jnp.abs(den), jnp.exp(-mj))
    lane = _iota((NH, 128), 1)
    per_head = jnp.where(lane == 0, s, jnp.where(lane == 1, iw, jnp.where(lane == 2, ws, inv)))
    ph = rep(per_head)
    s_c, iw_c, ws_c, inv_c = ph[:, 0:1], ph[:, 1:2], ph[:, 2:3], ph[:, 3:4]
    cq = jnp.sum(c0 * rep(q), axis=1, keepdims=True)
    vcol = jnp.sum(jnp.where(diag, rep(v), 0.0), axis=1, keepdims=True)
    hcol = (s_c * vcol + iw_c * cq) * inv_c
    co_ref[0] = (iw_c * c0 + (ws_c * vcol) * rep(k)).reshape(NH, HD, HD)
    no_ref[0] = iw * n0 + ws * k
    mo_ref[0] = mj
    h8 = fold(hcol)
    hn = h8 * lax.rsqrt(jnp.mean(h8 * h8, axis=-1, keepdims=True) + NORM_EPS)
    y_ref[0] = hn * nw_ref[...] * _sigmoid(og)


def _mlstm_step(proj, b_i, b_f, nw, c_all, l, n0, m0):
    bd = proj.shape[0]
    per_b3 = lambda b: (b, 0, 0)
    full = lambda b: (0, 0)
    x = proj[:, :4 * GW].reshape(bd, 4 * NH, HD)
    gates = proj[:, C_SM + SM_MI:C_SM + SM_MI + 2 * NH].reshape(bd, 2, NH).transpose(0, 2, 1)
    gate_bias = jnp.concatenate([b_i, b_f], axis=0).T
    y, c1, n1, m1 = pl.pallas_call(
        _mlstm_step_kernel,
        out_shape=(jax.ShapeDtypeStruct((bd, NH, HD), F32), jax.ShapeDtypeStruct((bd, NH, HD, HD), F32),
                   jax.ShapeDtypeStruct((bd, NH, HD), F32), jax.ShapeDtypeStruct((bd, NH, 1), F32)),
        grid=(bd,),
        in_specs=[pl.BlockSpec((1, 4 * NH, HD), per_b3),
                  pl.BlockSpec((1, NH, 2), per_b3),
                  pl.BlockSpec((NH, 2), full),
                  pl.BlockSpec((NH, HD), full),
                  pl.BlockSpec((1, NH, HD, HD), lambda b: (l * bd + b, 0, 0, 0)),
                  pl.BlockSpec((1, NH, HD), per_b3),
                  pl.BlockSpec((1, NH, 1), per_b3)],
        out_specs=(pl.BlockSpec((1, NH, HD), per_b3),
                   pl.BlockSpec((1, NH, HD, HD), lambda b: (b, 0, 0, 0)),
                   pl.BlockSpec((1, NH, HD), per_b3),
                   pl.BlockSpec((1, NH, 1), per_b3)),
        compiler_params=_cp("parallel"),
    )(x, gates, gate_bias, nw.reshape(NH, HD), c_all.reshape((-1,) + c_all.shape[2:]), n0, m0.reshape(bd, NH, 1))
    return y, c1, n1, m1


def _rwkv_prep_kernel(x_ref, prev_ref, mu_ref, w0_ref, ww2_ref, a0_ref, wa2_ref, wg2_ref, kk_ref, ka_ref,
                      r_o, lw_o, k_o, v_o, kk_o, a_o, g_o, carry_scr, *, seq):
    x = x_ref[...]
    tm = x.shape[0]
    if seq:
        first = jnp.where(pl.program_id(1) == 0, prev_ref[0], carry_scr[...])
        xprev = jnp.where(_iota((tm, 1), 0) == 0, first, pltpu.roll(x, 1, 0))
        carry_scr[...] = x[tm - 1:tm, :]
    else:
        xprev = prev_ref[...]
    xm = x + (xprev - x) * mu_ref[...]
    r = xm[:, 0:GW]
    kx = xm[:, GW:2 * GW]
    v = xm[:, 2 * GW:3 * GW]
    xw = xm[:, 3 * GW:3 * GW + 64]
    xa = xm[:, 3 * GW + 64:3 * GW + 128]
    xg = xm[:, 3 * GW + 128:R_IN]
    w = -_softplus(-(w0_ref[...] + _dot(jnp.tanh(xw), ww2_ref[...]))) - 0.5
    a = _sigmoid(a0_ref[...] + _dot(xa, wa2_ref[...]))
    r_o[...] = r
    lw_o[...] = -jnp.exp(w)
    v_o[...] = v
    a_o[...] = a
    g_o[...] = _dot(_sigmoid(xg), wg2_ref[...])
    k_o[...] = kx * (1.0 + (a - 1.0) * ka_ref[...])
    kk = kx * kk_ref[...]
    for h in range(NH):
        kh = kk[:, h * HD:(h + 1) * HD]
        nrm = jnp.sqrt(jnp.sum(kh * kh, axis=-1, keepdims=True))
        kk_o[:, h * HD:(h + 1) * HD] = kh / jnp.maximum(nrm, 1e-12)


def _rwkv_prep(proj, prev, mu, w0, ww2, a0, wa2, wg2, k_k, k_a, nb, t):
    n = nb * t
    seq = t > 1
    full = lambda *_: (0, 0)
    if seq:
        tm = min(256, t)
        nt = t // tm
        grid = (nb, nt)
        xmap = lambda b, i: (b * nt + i, C_RIN // R_IN)
        pspec = pl.BlockSpec((1, 1, R_IN), lambda b, i: (b, 0, 0))
        omap = lambda b, i: (b * nt + i, 0)
        sem = ("parallel", "arbitrary")
    else:
        tm = n
        grid = (1,)
        xmap = lambda i: (0, C_RIN // R_IN)
        pspec = pl.BlockSpec((tm, R_IN), lambda i: (0, 0))
        omap = lambda i: (0, 0)
        sem = ("arbitrary",)
    wspecs = [pl.BlockSpec((1, R_IN), full), pl.BlockSpec((1, GW), full), pl.BlockSpec((64, GW), full),
              pl.BlockSpec((1, GW), full), pl.BlockSpec((64, GW), full), pl.BlockSpec((128, GW), full),
              pl.BlockSpec((1, GW), full), pl.BlockSpec((1, GW), full)]
    return pl.pallas_call(
        functools.partial(_rwkv_prep_kernel, seq=seq),
        out_shape=tuple(jax.ShapeDtypeStruct((n, GW), F32) for _ in range(7)),
        grid=grid,
        in_specs=[pl.BlockSpec((tm, R_IN), xmap), pspec] + wspecs,
        out_specs=tuple(pl.BlockSpec((tm, GW), omap) for _ in range(7)),
        scratch_shapes=[pltpu.VMEM((1, R_IN), F32)],
        compiler_params=_cp(*sem),
    )(proj, prev, mu, w0, ww2, a0, wa2, wg2, k_k, k_a)


def _rwkv_scan_kernel(r_ref, lw_ref, k_ref, v_ref, kk_ref, a_ref, g_ref, rk_ref, lnw_ref, lnb_ref, y_ref, s_ref):
    C = CHUNK

    @pl.when(pl.program_id(1) == 0)
    def _():
        s_ref[...] = jnp.zeros(s_ref.shape, F32)

    ri, ci = _iota((C, C), 0), _iota((C, C), 1)
    same, incl, strict = _bd_masks(C)
    row, col = _iota((BD, BD), 0), _iota((BD, BD), 1)
    eye = jnp.where(row == col, 1.0, 0.0)
    strict_incl = jnp.concatenate([strict, incl], axis=0)
    tile = lambda z: jnp.concatenate([z] * HG, axis=0)
    msk = lambda z: jnp.where(same, z, 0.0)
    fold = lambda z: sum(z[j * C:(j + 1) * C] for j in range(HG))
    n_seq, n_grp = r_ref.shape[0], NH // HG
    chains = [(bb, g) for bb in range(n_seq) for g in range(n_grp)]
    each = lambda f: {c: f(c) for c in chains}
    lanes = lambda c: slice(c[1] * BD, (c[1] + 1) * BD)
    seq = {}
    for bb in range(n_seq):
        lw = lw_ref[bb]
        cs = _dot(jnp.where(ci <= ri, 1.0, 0.0), lw, HI)
        gam = jnp.exp(cs)
        ginv = jnp.exp(-cs)
        r, k, v, kk = r_ref[bb], k_ref[bb], v_ref[bb], kk_ref[bb]
        seq[bb] = dict(at=-kk * jnp.exp(cs - lw), bt=kk * a_ref[bb] * ginv, kt=k * ginv, rt=r * gam, v=v,
                       glast=gam[C - 1:C, :], bonus_in=r * k * rk_ref[...])
    part = lambda c, name: seq[c[0]][name][:, lanes(c)]
    btl = each(lambda c: tile(part(c, 'bt')))
    ktl = each(lambda c: tile(part(c, 'kt')))
    vexp = each(lambda c: msk(tile(part(c, 'v'))))
    ar = each(lambda c: jnp.concatenate([msk(tile(part(c, 'at'))), msk(tile(part(c, 'rt')))], axis=0))
    sbd = each(lambda c: s_ref[c[0], c[1]])
    gb = each(lambda c: _dot_nt(ar[c], btl[c]))
    gk = each(lambda c: _dot_nt(ar[c], ktl[c]))
    gs = each(lambda c: _dot_nt(ar[c], sbd[c]))
    n_ab = each(lambda c: jnp.where(strict, gb[c][:BD], 0.0))
    x = each(lambda c: eye + n_ab[c])
    pm = each(lambda c: _dot(n_ab[c], n_ab[c]))
    for j in range(5):
        xd = each(lambda c: _dot(x[c], pm[c]))
        if j < 4:
            pm = each(lambda c: _dot(pm[c], pm[c]))
        x = each(lambda c: x[c] + xd[c])
    akv = each(lambda c: _dot(jnp.where(strict_incl, gk[c], 0.0), vexp[c]))
    u = each(lambda c: _dot(x[c], gs[c][:BD] + akv[c][:BD]))
    o = each(lambda c: gs[c][BD:] + _dot(jnp.where(incl, gb[c][BD:], 0.0), u[c]) + akv[c][BD:])
    gl = each(lambda c: part(c, 'glast'))
    s_new = each(lambda c: sbd[c] * gl[c] + _dot_tn(
        jnp.concatenate([u[c], vexp[c]], axis=0),
        jnp.concatenate([msk(btl[c] * gl[c]), msk(ktl[c] * gl[c])], axis=0)))
    for c in chains:
        bb, ln = c[0], lanes(c)
        s_ref[c[0], c[1]] = s_new[c]
        mean = jnp.sum(o[c], axis=-1, keepdims=True) * (1.0 / HD)
        dev = msk(o[c] - mean)
        var = jnp.sum(dev * dev, axis=-1, keepdims=True) * (1.0 / HD)
        on = fold(dev * lax.rsqrt(var + R_LN_EPS)) * lnw_ref[:, ln] + lnb_ref[:, ln]
        bonus = fold(jnp.sum(msk(tile(part(c, 'bonus_in'))), axis=-1, keepdims=True) * vexp[c])
        y_ref[bb, :, ln] = (on + bonus) * g_ref[bb, :, ln]


def _rwkv_scan(rs, rk, lnw, lnb, nb, t):
    nc = t // CHUNK
    ng = NH // HG
    sb = RWKV_SEQS if nb % RWKV_SEQS == 0 else 1
    rowmap = lambda b, c: (b, c, 0)
    full = lambda b, c: (0, 0)
    y, sbd = pl.pallas_call(
        _rwkv_scan_kernel,
        out_shape=(jax.ShapeDtypeStruct((nb, t, GW), F32), jax.ShapeDtypeStruct((nb, ng, BD, BD), F32)),
        grid=(nb // sb, nc),
        in_specs=[pl.BlockSpec((sb, CHUNK, GW), rowmap)] * 7 + [pl.BlockSpec((1, GW), full)] * 3,
        out_specs=(pl.BlockSpec((sb, CHUNK, GW), rowmap), pl.BlockSpec((sb, ng, BD, BD), lambda b, c: (b, 0, 0, 0))),
        compiler_params=_cp("parallel", "arbitrary"),
    )(*(z.reshape(nb, t, GW) for z in rs), rk, lnw, lnb)
    return y.reshape(nb * t, GW), _bd_diag(sbd)


def _rwkv_step_kernel(r_ref, lw_ref, k_ref, v_ref, kk_ref, a_ref, g_ref, rk_ref, lnw_ref, lnb_ref, s_ref,
                      y_ref, so_ref):
    r, lw, k, v, kk, a, g = (ref[0] for ref in (r_ref, lw_ref, k_ref, v_ref, kk_ref, a_ref, g_ref))
    rep, fold, diag = _head_expanders()
    s0 = s_ref[0].reshape(NH * HD, HD)
    kk_rep = rep(kk)
    sk = jnp.sum(s0 * kk_rep, axis=1, keepdims=True)
    vcol = jnp.sum(jnp.where(diag, rep(v), 0.0), axis=1, keepdims=True)
    s1 = s0 * rep(jnp.exp(lw)) - sk * rep(kk * a) + vcol * rep(k)
    so_ref[0] = s1.reshape(NH, HD, HD)
    ocol = jnp.sum(s1 * rep(r), axis=1, keepdims=True)
    o = fold(ocol)
    mean = jnp.mean(o, axis=-1, keepdims=True)
    var = jnp.mean(jnp.square(o - mean), axis=-1, keepdims=True)
    on = (o - mean) * lax.rsqrt(var + R_LN_EPS) * lnw_ref[...] + lnb_ref[...]
    bonus = jnp.sum(r * k * rk_ref[...], axis=-1, keepdims=True) * v
    y_ref[0] = (on + bonus) * g


def _rwkv_step(rs, rk, lnw, lnb, s_all, l):
    bd = s_all.shape[1]
    per_b = lambda b: (b, 0, 0)
    full = lambda b: (0, 0)
    return pl.pallas_call(
        _rwkv_step_kernel,
        out_shape=(jax.ShapeDtypeStruct((bd, NH, HD), F32), jax.ShapeDtypeStruct((bd, NH, HD, HD), F32)),
        grid=(bd,),
        in_specs=[pl.BlockSpec((1, NH, HD), per_b)] * 7 + [pl.BlockSpec((NH, HD), full)] * 3
                 + [pl.BlockSpec((1, NH, HD, HD), lambda b: (l * bd + b, 0, 0, 0))],
        out_specs=(pl.BlockSpec((1, NH, HD), per_b), pl.BlockSpec((1, NH, HD, HD), lambda b: (b, 0, 0, 0))),
        compiler_params=_cp("parallel"),
    )(*(z.reshape(bd, NH, HD) for z in rs), *(z.reshape(NH, HD) for z in (rk, lnw, lnb)),
      s_all.reshape((-1,) + s_all.shape[2:]))


def _gelu_tanh(x):
    return 0.5 * x * (1.0 + jnp.tanh(math.sqrt(2.0 / math.pi) * (x + 0.044715 * (x * x * x))))


def _s5_kernel(u_ref, bre_ref, bim_ref, lre_ref, lim_ref, cre_ref, cim_ref, d_ref, wg_ref, bg_ref, h0r_ref, h0i_ref,
               y_ref, hr_ref, hi_ref, hre_scr, him_scr, *, nb, tb):
    @pl.when(pl.program_id(0) == 0)
    def _():
        hr_ref[...] = h0r_ref[...]
        hi_ref[...] = h0i_ref[...]

    u = u_ref[...]
    hre_scr[...] = _dot(u, bre_ref[...])
    him_scr[...] = _dot(u, bim_ref[...])
    lr = lre_ref[...]
    li = lim_ref[...]

    def body(t, carry):
        hr, hi = carry
        rows = pl.ds(pl.multiple_of(t * nb, nb), nb)
        nr = lr * hr - li * hi + hre_scr[rows, :]
        ni = lr * hi + li * hr + him_scr[rows, :]
        hre_scr[rows, :] = nr
        him_scr[rows, :] = ni
        return nr, ni

    hr, hi = lax.fori_loop(0, tb, body, (hr_ref[...], hi_ref[...]))
    hr_ref[...] = hr
    hi_ref[...] = hi
    y = _dot(hre_scr[...], cre_ref[...]) - _dot(him_scr[...], cim_ref[...]) + d_ref[...] * u
    y = _gelu_tanh(y)
    y_ref[...] = y * _sigmoid(_dot(y, wg_ref[...]) + bg_ref[...])


def _s5(u_tm, mats, h0r, h0i, nb, t):
    bre, bim, lre, lim, cre, cim, d, wg, bg = mats
    tb = min(64, t)
    full = lambda i: (0, 0)
    return pl.pallas_call(
        functools.partial(_s5_kernel, nb=nb, tb=tb),
        out_shape=(jax.ShapeDtypeStruct((t * nb, GW), F32), jax.ShapeDtypeStruct((nb, S5_W), F32),
                   jax.ShapeDtypeStruct((nb, S5_W), F32)),
        grid=(t // tb,),
        in_specs=[pl.BlockSpec((tb * nb, GW), lambda i: (i, 0)),
                  pl.BlockSpec((GW, S5_W), full), pl.BlockSpec((GW, S5_W), full),
                  pl.BlockSpec((1, S5_W), full), pl.BlockSpec((1, S5_W), full),
                  pl.BlockSpec((S5_W, GW), full), pl.BlockSpec((S5_W, GW), full),
                  pl.BlockSpec((1, GW), full), pl.BlockSpec((GW, GW), full), pl.BlockSpec((1, GW), full),
                  pl.BlockSpec((nb, S5_W), full), pl.BlockSpec((nb, S5_W), full)],
        out_specs=(pl.BlockSpec((tb * nb, GW), lambda i: (i, 0)),
                   pl.BlockSpec((nb, S5_W), full), pl.BlockSpec((nb, S5_W), full)),
        scratch_shapes=[pltpu.VMEM((tb * nb, S5_W), F32), pltpu.VMEM((tb * nb, S5_W), F32)],
        compiler_params=_cp("arbitrary"),
    )(u_tm, bre, bim, lre, lim, cre, cim, d, wg, bg, h0r, h0i)


def _s5_mats(a_re, a_im, b_re, b_im, c_re, c_im, d_skip, log_dt, w_glu, b_glu):
    dt = jnp.exp(log_dt)
    mag = jnp.exp(a_re * dt)
    lb_re, lb_im = mag * jnp.cos(a_im * dt), mag * jnp.sin(a_im * dt)
    den = a_re * a_re + a_im * a_im
    f_re = ((lb_re - 1.0) * a_re + lb_im * a_im) / den
    f_im = (lb_im * a_re - (lb_re - 1.0) * a_im) / den
    bb_re = f_re[..., None] * b_re - f_im[..., None] * b_im
    bb_im = f_re[..., None] * b_im + f_im[..., None] * b_re
    eye = jnp.eye(S5_G, dtype=F32)
    bd = lambda bb: jnp.einsum('gpc,gh->gchp', bb, eye).reshape(GW, S5_W)
    cd = lambda cc: jnp.einsum('gcp,gh->gphc', cc, eye).reshape(S5_W, GW)
    return (bd(bb_re), bd(bb_im), lb_re.reshape(1, S5_W), lb_im.reshape(1, S5_W), cd(c_re), cd(c_im),
            d_skip.reshape(1, GW), w_glu, b_glu.reshape(1, GW))


def _ffn_up_kernel(h_ref, wa_ref, wb_ref, cw_ref, cb_ref, s0_ref, s1_ref, y_ref, a_ref, carry_scr, *, seq):
    def gate(a, a1, a2, b):
        c = cb_ref[...] + a2 * cw_ref[0:1, :] + a1 * cw_ref[1:2, :] + a * cw_ref[2:3, :]
        return (c * _sigmoid(c) * b).astype(BF16)

    if not seq:
        h = h_ref[...]
        a = _dot(h, wa_ref[...])
        a_ref[...] = a
        y_ref[...] = gate(a, s1_ref[...], s0_ref[...], _dot(h, wb_ref[...]))
        return

    @pl.when(pl.program_id(2) == 0)
    def _():
        carry_scr[...] = s0_ref[0]

    hm = h_ref.shape[0] // FFN_PARTS
    rowid = _iota((hm, 1), 0)
    c0 = carry_scr[0:1, :]
    c1 = carry_scr[1:2, :]
    for p in range(FFN_PARTS):
        rows = slice(p * hm, (p + 1) * hm)
        h = h_ref[rows, :]
        a = _dot(h, wa_ref[...])
        b = _dot(h, wb_ref[...])
        a1 = jnp.where(rowid == 0, c1, pltpu.roll(a, 1, 0))
        a2 = jnp.where(rowid == 0, c0, jnp.where(rowid == 1, c1, pltpu.roll(a, 2, 0)))
        y_ref[rows, :] = gate(a, a1, a2, b)
        c0 = a[hm - 2:hm - 1, :]
        c1 = a[hm - 1:hm, :]
    last2 = jnp.concatenate([c0, c1], axis=0)
    carry_scr[...] = last2
    a_ref[0] = last2


def _ffn_up(h2, w_up, l, cw, cb, st, nb, t):
    n, d = h2.shape
    dff = w_up.shape[2] // 2
    tn = 512
    nj = dff // tn
    if t > 1:
        tm = min(MM_ROWS, t)
        nt = t // tm
        y, fc = pl.pallas_call(
            functools.partial(_ffn_up_kernel, seq=True),
            out_shape=(jax.ShapeDtypeStruct((n, dff), BF16), jax.ShapeDtypeStruct((nb, 2, dff), F32)),
            grid=(nb, nj, nt),
            in_specs=[pl.BlockSpec((tm, d), lambda b, j, i: (b * nt + i, 0)),
                      pl.BlockSpec((None, d, tn), lambda b, j, i: (l, 0, j)),
                      pl.BlockSpec((None, d, tn), lambda b, j, i: (l, 0, nj + j)),
                      pl.BlockSpec((3, tn), lambda b, j, i: (0, j)),
                      pl.BlockSpec((1, tn), lambda b, j, i: (0, j)),
                      pl.BlockSpec((1, 2, tn), lambda b, j, i: (b, 0, j)),
                      pl.BlockSpec((1, 2, tn), lambda b, j, i: (b, 0, j))],
            out_specs=(pl.BlockSpec((tm, tn), lambda b, j, i: (b * nt + i, j)),
                       pl.BlockSpec((1, 2, tn), lambda b, j, i: (b, 0, j))),
            scratch_shapes=[pltpu.VMEM((2, tn), F32)],
            compiler_params=_cp("parallel", "parallel", "arbitrary"),
        )(h2, w_up, w_up, cw, cb, st, st)
        return y, fc
    s0, s1 = st[:, 0, :], st[:, 1, :]
    y, a = pl.pallas_call(
        functools.partial(_ffn_up_kernel, seq=False),
        out_shape=(jax.ShapeDtypeStruct((n, dff), BF16), jax.ShapeDtypeStruct((n, dff), F32)),
        grid=(nj,),
        in_specs=[pl.BlockSpec((n, d), lambda j: (0, 0)),
                  pl.BlockSpec((None, d, tn), lambda j: (l, 0, j)),
                  pl.BlockSpec((None, d, tn), lambda j: (l, 0, nj + j)),
                  pl.BlockSpec((3, tn), lambda j: (0, j)),
                  pl.BlockSpec((1, tn), lambda j: (0, j)),
                  pl.BlockSpec((n, tn), lambda j: (0, j)),
                  pl.BlockSpec((n, tn), lambda j: (0, j))],
        out_specs=(pl.BlockSpec((n, tn), lambda j: (0, j)), pl.BlockSpec((n, tn), lambda j: (0, j))),
        scratch_shapes=[pltpu.VMEM((2, tn), F32)],
        compiler_params=_cp("parallel"),
    )(h2, w_up, w_up, cw, cb, s0, s1)
    return y, jnp.stack([s1, a], axis=1)


def _permute_w_in(w):
    wt = jnp.swapaxes(w, 1, 2)
    rows = lambda s, n: wt[:, s:s + n, :]
    parts = [rows(0, 4 * GW), rows(2064, GW), rows(2832, GW), rows(5208, GW), rows(3416, R_IN),
             rows(2576, 128), rows(2704, 128), rows(3344, HD), rows(2048, NH), rows(2056, NH), rows(3408, NH),
             jnp.zeros((w.shape[0], NP - C_SM - HD - 3 * NH, w.shape[1]), w.dtype)]
    return jnp.swapaxes(jnp.concatenate(parts, axis=1), 1, 2).astype(BF16)


def _layer(x2, nb, t, l, W, st, tables, cache):
    n = nb * t
    c0, n0, m0, rs0, rsh0, sre0, sim0, conv0 = st
    proj = _in_proj(x2, W['norm_mix'], W['w_in'], l)
    aq_r, iq_r, k_r, v_r, ik_r = _rope_call(proj, tables[0], tables[1], *tables[2])

    if cache is None:
        ym, c1, n1, m1 = _mlstm_prompt(proj, W['m_b_i'], W['m_b_f'], W['m_norm'], nb, t)
        ya = _dsa_prompt(iq_r, proj, ik_r, aq_r, k_r, v_r, nb, t)
    else:
        ym, c1, n1, m1 = _mlstm_step(proj, W['m_b_i'], W['m_b_f'], W['m_norm'], c0, l, n0, m0)
        ym = ym.reshape(n, GW)
        cki, ck, cv, page_table, n_pool = cache
        ya = _dsa_sample(l, page_table, iq_r, proj[:, C_SM + SM_IW:C_SM + SM_IW + NH], ik_r, aq_r, k_r, v_r,
                         cki, ck, cv, n_pool)

    prev = rsh0.reshape(nb, 1, R_IN) if t > 1 else rsh0
    rs = _rwkv_prep(proj, prev, W['r_mu'], W['r_w0'], W['r_w_w2'], W['r_a0'], W['r_w_a2'], W['r_w_g2'],
                    W['r_k_k'], W['r_k_a'], nb, t)
    if t > 1:
        yr, rs1 = _rwkv_scan(rs, W['r_r_k'], W['r_ln_w'], W['r_ln_b'], nb, t)
    else:
        yr, rs1 = _rwkv_step(rs, W['r_r_k'], W['r_ln_w'], W['r_ln_b'], rs0, l)
        yr = yr.reshape(n, GW)
    rsh1 = proj.reshape(nb, t, NP)[:, t - 1, C_RIN:C_RIN + R_IN]

    su = proj[:, C_SU:C_SU + GW]
    u_tm = su.reshape(nb, t, GW).transpose(1, 0, 2).reshape(t * nb, GW)
    ys_tm, sre1, sim1 = _s5(u_tm, W['s5'], sre0.reshape(nb, S5_W), sim0.reshape(nb, S5_W), nb, t)
    ys = ys_tm.reshape(t, nb, GW).transpose(1, 0, 2).reshape(n, GW)

    x2 = _res_matmul(x2, [ym, ya, yr, ys], W['w_out'], l)
    h2 = _rmsnorm(x2, W['norm_ffn'], BF16)
    y, conv1 = _ffn_up(h2, W['ffn_w_up'], l, W['ffn_conv_w'], W['ffn_conv_b'], conv0, nb, t)
    x2 = _res_matmul(x2, [y], W['ffn_w_down'], l)
    outs = (k_r.reshape(nb, t, A_KV, HD), v_r.reshape(nb, t, A_KV, HD), ik_r.reshape(nb, t, HD),
            c1, n1, m1.reshape(nb, NH), rs1, rsh1, sre1.reshape(nb, S5_G, S5_P), sim1.reshape(nb, S5_G, S5_P), conv1)
    return x2, outs


def kernel(x_prompt, x_sample, cache_k, cache_v, cache_kidx, page_table, state_mlstm_c, state_mlstm_n,
           state_mlstm_m, state_rwkv_s, state_rwkv_shift, state_s5_re, state_s5_im, state_ffn_conv,
           norm_mix, w_in, w_out, m_b_i, m_b_f, m_norm, r_mu, r_w0, r_w_w2, r_a0, r_w_a2, r_w_g2,
           r_k_k, r_k_a, r_r_k, r_ln_w, r_ln_b, s5_a_re, s5_a_im, s5_b_re, s5_b_im, s5_c_re, s5_c_im,
           s5_d, s5_log_dt, s5_w_glu, s5_b_glu, norm_ffn, ffn_w_up, ffn_conv_w, ffn_conv_b, ffn_w_down,
           norm_final):
    bp, tp, d = x_prompt.shape
    bs, ts, _ = x_sample.shape
    assert ts == 1 and tp % CHUNK == 0
    depth = w_in.shape[0]
    n_pool = cache_k.shape[1]
    past = page_table.shape[1] * PAGE
    dff = ffn_conv_b.shape[-1]

    row = lambda z: z.reshape(1, -1)
    layers = []
    w_in_all, w_out_all = _permute_w_in(w_in), w_out.astype(BF16)
    w_up_all, w_down_all = ffn_w_up.astype(BF16), ffn_w_down.astype(BF16)
    for l in range(depth):
        layers.append(dict(
            norm_mix=row(norm_mix[l]), w_in=w_in_all, w_out=w_out_all,
            m_b_i=row(m_b_i[l]), m_b_f=row(m_b_f[l]), m_norm=row(m_norm[l]),
            r_mu=row(r_mu[l]), r_w0=row(r_w0[l]), r_w_w2=r_w_w2[l], r_a0=row(r_a0[l]), r_w_a2=r_w_a2[l],
            r_w_g2=r_w_g2[l], r_k_k=row(r_k_k[l]), r_k_a=row(r_k_a[l]), r_r_k=row(r_r_k[l]),
            r_ln_w=row(r_ln_w[l]), r_ln_b=row(r_ln_b[l]),
            s5=_s5_mats(s5_a_re[l], s5_a_im[l], s5_b_re[l], s5_b_im[l], s5_c_re[l], s5_c_im[l], s5_d[l],
                        s5_log_dt[l], s5_w_glu[l], s5_b_glu[l]),
            norm_ffn=row(norm_ffn[l]), ffn_w_up=w_up_all, ffn_conv_w=ffn_conv_w[l],
            ffn_conv_b=row(ffn_conv_b[l]), ffn_w_down=w_down_all))

    cos_p, sin_p = _rope_tables(jnp.arange(tp))
    cos_s, sin_s = _rope_tables(jnp.full((bs,), past))
    tab_p = (cos_p, sin_p, (bp, tp))
    tab_s = (cos_s, sin_s, (1, bs))

    zeros = lambda *s: jnp.zeros(s, F32)
    st_p = (zeros(bp, NH, HD, HD), zeros(bp, NH, HD), zeros(bp, NH), zeros(bp, NH, HD, HD), zeros(bp, R_IN),
            zeros(bp, S5_G, S5_P), zeros(bp, S5_G, S5_P), zeros(bp, 2, dff))
    cki = cache_kidx.reshape(depth * n_pool, PAGE, HD).transpose(0, 2, 1)
    ck = cache_k.reshape(depth * n_pool, PAGE, A_KV * HD).transpose(0, 2, 1)
    cv = cache_v.reshape(depth * n_pool, PAGE, A_KV * HD).transpose(0, 2, 1)

    xp = x_prompt.reshape(bp * tp, d)
    xs = x_sample.reshape(bs, d)
    new_p, new_s = [], []
    for l in range(depth):
        xp, sp = _layer(xp, bp, tp, l, layers[l], st_p, tab_p, None)
        st_s = (state_mlstm_c, state_mlstm_n[l], state_mlstm_m[l], state_rwkv_s, state_rwkv_shift[l],
                state_s5_re[l], state_s5_im[l], state_ffn_conv[l])
        xs, ss = _layer(xs, bs, 1, l, layers[l], st_s, tab_s, (cki, ck, cv, page_table, n_pool))
        new_p.append(sp)
        new_s.append(ss)
    (k_p, v_p, ki_p, mc_p, mn_p, mm_p, rs_p, rsh_p, sre_p, sim_p, fc_p) = [jnp.stack(z) for z in zip(*new_p)]
    (k_s, v_s, ki_s, mc_s, mn_s, mm_s, rs_s, rsh_s, sre_s, sim_s, fc_s) = [jnp.stack(z) for z in zip(*new_s)]
    y_prompt = _rmsnorm(xp, row(norm_final), F32).reshape(bp, tp, d)
    y_sample = _rmsnorm(xs, row(norm_final), F32).reshape(bs, ts, d)
    return (y_prompt, y_sample, k_p, k_s, v_p, v_s, ki_p, ki_s, mc_p, mc_s, mn_p, mn_s, mm_p, mm_s,
            rs_p, rs_s, rsh_p, rsh_s, sre_p, sre_s, sim_p, sim_s, fc_p, fc_s)
```

```python
import functools
import math

import jax
import jax.numpy as jnp
from jax import lax
from jax.experimental import pallas as pl
from jax.experimental.pallas import tpu as pltpu

F32 = jnp.float32
BF16 = jnp.bfloat16
HI = lax.Precision.HIGHEST

HD = 64
NH = 8
GW = NH * HD
A_KV = 2
PAGE = 128
TOPK_MAX = 256
ROPE_THETA = 10000.0
R_IN = 3 * GW + 64 + 64 + 128
R_LN_EPS = 64e-5
S5_G, S5_CH, S5_P = 32, 16, 64
S5_W = S5_G * S5_P
S5_SPLIT = 4
NORM_EPS = 1e-6
CHUNK = 64
IDX_SCALE = HD ** -0.5 * NH ** -0.5

C_M, C_AQ, C_IQ, C_SU, C_RIN, C_AK, C_AV, C_SM = 0, 2048, 2560, 3072, 3584, 5376, 5504, 5632
NP = 6144
SM_IK, SM_MI, SM_MF, SM_IW = 0, 64, 72, 80

VMEM_LIMIT = 56 * 1024 * 1024
MM_ROWS = 1024
RWKV_SEQS = 4
FFN_PARTS = 2


def _cp(*sem):
    return pltpu.CompilerParams(dimension_semantics=sem, vmem_limit_bytes=VMEM_LIMIT)


def _dot(a, b, prec=None):
    return jnp.dot(a, b, preferred_element_type=F32, precision=prec)


def _dot_nt(a, b, prec=None):
    return lax.dot_general(a, b, (((1,), (1,)), ((), ())), preferred_element_type=F32, precision=prec)


def _dot_tn(a, b, prec=None):
    return lax.dot_general(a, b, (((0,), (0,)), ((), ())), preferred_element_type=F32, precision=prec)


def _sigmoid(x):
    return 1.0 / (1.0 + jnp.exp(-x))


def _softplus(x):
    return jnp.maximum(x, 0.0) + jnp.log(1.0 + jnp.exp(-jnp.abs(x)))


def _iota(shape, dim):
    return lax.broadcasted_iota(jnp.int32, shape, dim)


HG = 4
BD = HG * HD
assert CHUNK == HD


def _bd_masks(c):
    row, col = _iota((HG * c, HG * c), 0), _iota((HG * c, HG * c), 1)
    same = (row // c) == (col // c)
    t, s = row % c, col % c
    return same, jnp.logical_and(same, s <= t), jnp.logical_and(same, s < t)


def _bd_diag(x):
    nb, ng = x.shape[:2]
    x6 = x.reshape(nb, ng, HG, HD, HG, HD)
    return jnp.stack([x6[:, :, j, :, j, :] for j in range(HG)], axis=2).reshape(nb, ng * HG, HD, HD)


def _inproj_kernel(x_ref, g_ref, w_ref, o_ref, h_scr):
    @pl.when(pl.program_id(1) == 0)
    def _():
        x = x_ref[...]
        ms = jnp.mean(x * x, axis=-1, keepdims=True)
        h_scr[...] = (x * lax.rsqrt(ms + NORM_EPS) * g_ref[...]).astype(BF16)

    o_ref[...] = _dot(h_scr[...], w_ref[...])


def _in_proj(x2, g, w, l):
    n, d = x2.shape
    npad = w.shape[2]
    tm = min(MM_ROWS, n)
    tn = 768
    return pl.pallas_call(
        _inproj_kernel,
        out_shape=jax.ShapeDtypeStruct((n, npad), F32),
        grid=(n // tm, npad // tn),
        in_specs=[pl.BlockSpec((tm, d), lambda i, j: (i, 0)),
                  pl.BlockSpec((1, d), lambda i, j: (0, 0)),
                  pl.BlockSpec((None, d, tn), lambda i, j: (l, 0, j))],
        out_specs=pl.BlockSpec((tm, tn), lambda i, j: (i, j)),
        scratch_shapes=[pltpu.VMEM((tm, d), BF16)],
        compiler_params=_cp("parallel", "arbitrary"),
    )(x2, g, w)


def _rmsnorm_kernel(x_ref, g_ref, o_ref):
    x = x_ref[...]
    ms = jnp.mean(x * x, axis=-1, keepdims=True)
    o_ref[...] = (x * lax.rsqrt(ms + NORM_EPS) * g_ref[...]).astype(o_ref.dtype)


def _rmsnorm(x2, g, dtype):
    n, d = x2.shape
    tm = min(512, n)
    return pl.pallas_call(
        _rmsnorm_kernel,
        out_shape=jax.ShapeDtypeStruct((n, d), dtype),
        grid=(n // tm,),
        in_specs=[pl.BlockSpec((tm, d), lambda i: (i, 0)), pl.BlockSpec((1, d), lambda i: (0, 0))],
        out_specs=pl.BlockSpec((tm, d), lambda i: (i, 0)),
        compiler_params=_cp("parallel"),
    )(x2, g)


def _resmm_kernel(r_ref, *refs):
    y_refs, w_ref, o_ref = refs[:-2], refs[-2], refs[-1]
    acc = r_ref[...]
    k0 = 0
    for y_ref in y_refs:
        kw = y_ref.shape[1]
        acc = acc + _dot(y_ref[...].astype(BF16), w_ref[k0:k0 + kw, :])
        k0 += kw
    o_ref[...] = acc


def _res_matmul(res, ys, w, l):
    n = res.shape[0]
    k, d = w.shape[1:]
    assert sum(y.shape[1] for y in ys) == k
    tm = min(MM_ROWS, n)
    tn = 512
    return pl.pallas_call(
        _resmm_kernel,
        out_shape=jax.ShapeDtypeStruct((n, d), F32),
        grid=(n // tm, d // tn),
        in_specs=[pl.BlockSpec((tm, tn), lambda i, j: (i, j))]
                 + [pl.BlockSpec((tm, y.shape[1]), lambda i, j: (i, 0)) for y in ys]
                 + [pl.BlockSpec((None, k, tn), lambda i, j: (l, 0, j))],
        out_specs=pl.BlockSpec((tm, tn), lambda i, j: (i, j)),
        compiler_params=_cp("parallel", "arbitrary"),
    )(res, *ys, w)


def _rope(x, cos, sin):
    w = x.shape[1]
    first = (_iota(x.shape, 1) & (HD - 1)) < HD // 2
    sw = jnp.where(first, pltpu.roll(x, w - HD // 2, 1), pltpu.roll(x, HD // 2, 1))
    return x * cos + sw * sin


def _rope_kernel(aq_ref, iq_ref, ak_ref, av_ref, sm_ref, cos_ref, sin_ref, aqo, iqo, ko, vo, iko):
    cos = cos_ref[...]
    sin = sin_ref[...]
    aqo[...] = _rope(aq_ref[...], cos, sin)
    iqo[...] = _rope(iq_ref[...], cos, sin)
    ko[...] = _rope(ak_ref[...], cos[:, :128], sin[:, :128])
    vo[...] = av_ref[...]
    iko[...] = _rope(sm_ref[...], cos[:, :128], sin[:, :128])[:, :HD]


def _rope_call(proj, cos, sin, nb, nt_rows):
    n = proj.shape[0]
    tm = min(512, nt_rows)
    nt = nt_rows // tm
    row = lambda b, i: b * nt + i
    return pl.pallas_call(
        _rope_kernel,
        out_shape=(jax.ShapeDtypeStruct((n, GW), F32), jax.ShapeDtypeStruct((n, GW), F32),
                   jax.ShapeDtypeStruct((n, 128), F32), jax.ShapeDtypeStruct((n, 128), F32),
                   jax.ShapeDtypeStruct((n, HD), F32)),
        grid=(nb, nt),
        in_specs=[pl.BlockSpec((tm, GW), lambda b, i: (row(b, i), C_AQ // GW)),
                  pl.BlockSpec((tm, GW), lambda b, i: (row(b, i), C_IQ // GW)),
                  pl.BlockSpec((tm, 128), lambda b, i: (row(b, i), C_AK // 128)),
                  pl.BlockSpec((tm, 128), lambda b, i: (row(b, i), C_AV // 128)),
                  pl.BlockSpec((tm, 128), lambda b, i: (row(b, i), C_SM // 128)),
                  pl.BlockSpec((tm, GW), lambda b, i: (i, 0)),
                  pl.BlockSpec((tm, GW), lambda b, i: (i, 0))],
        out_specs=(pl.BlockSpec((tm, GW), lambda b, i: (row(b, i), 0)),
                   pl.BlockSpec((tm, GW), lambda b, i: (row(b, i), 0)),
                   pl.BlockSpec((tm, 128), lambda b, i: (row(b, i), 0)),
                   pl.BlockSpec((tm, 128), lambda b, i: (row(b, i), 0)),
                   pl.BlockSpec((tm, HD), lambda b, i: (row(b, i), 0))),
        compiler_params=_cp("parallel", "parallel"),
    )(proj, proj, proj, proj, proj, cos, sin)


def _rope_tables(pos):
    half = HD // 2
    inv = ROPE_THETA ** (-jnp.arange(half, dtype=F32) / half)
    ang = pos.astype(F32)[:, None] * inv[None, :]
    cos, sin = jnp.cos(ang), jnp.sin(ang)
    cos64 = jnp.concatenate([cos, cos], axis=-1)
    sin64 = jnp.concatenate([-sin, sin], axis=-1)
    return jnp.tile(cos64, (1, NH)), jnp.tile(sin64, (1, NH))


def _kth_largest(sc, extra, kk):
    kf = jnp.float32(kk)

    def count_ge(c):
        n = jnp.sum(jnp.where(sc >= c, 1.0, 0.0), axis=-1, keepdims=True)
        if extra is not None:
            n = n + jnp.where(extra >= c, 1.0, 0.0)
        return n

    def key_to_f(key):
        bits = key ^ ((key >> 31) & jnp.int32(0x7FFFFFFF))
        return lax.bitcast_convert_type(bits, F32)

    r = sc.shape[0]
    int_min = jnp.int32(-2 ** 31)
    lo = jnp.where(count_ge(jnp.zeros((r, 1), F32)) >= kf, jnp.int32(0), int_min)

    def body(j, lo):
        cand = lo + jnp.left_shift(jnp.int32(1), jnp.int32(30) - j)
        ok = count_ge(key_to_f(cand)) >= kf
        return jnp.where(ok, cand, lo)

    lo = lax.fori_loop(0, 31, body, lo)
    key_neg_inf = jnp.int32(-2 ** 31 + 0x7FFFFF)
    return jnp.where(lo <= key_neg_inf, -jnp.inf, key_to_f(lo))


def _strict_upper_bf16(n):
    return jnp.where(_iota((n, n), 0) < _iota((n, n), 1), 1.0, 0.0).astype(BF16)


def _split_bf16(x):
    hi = x.astype(BF16).astype(F32)
    return hi, x - hi


def _dsa_prompt_kernel(iq_ref, sm_ref, ik_ref, aq_ref, k_ref, v_ref, o_ref, sel_scr, kcat_scr, *, topk, qb, n_ext):
    t_keys = ik_ref.shape[0]
    i = pl.program_id(1)
    per_ext = (t_keys // qb) // n_ext

    @pl.when(i == 0)
    def _():
        hi, lo = _split_bf16(ik_ref[...])
        kcat_scr[...] = jnp.concatenate([hi, lo, hi], axis=1)

    iq = iq_ref[...]
    aq = aq_ref[...] * (HD ** -0.5 * math.log2(math.e))
    wts = sm_ref[:, SM_IW:SM_IW + NH] * IDX_SCALE
    tq = i * qb + _iota((qb, 1), 0)

    def body(ext):
        kcat = kcat_scr[0:ext, :]
        sc = jnp.zeros((qb, ext), F32)
        for h in range(NH):
            hi, lo = _split_bf16(iq[:, h * HD:(h + 1) * HD])
            qk = _dot_nt(jnp.concatenate([hi, hi, lo], axis=1), kcat)
            sc = sc + jnp.maximum(qk, 0.0) * wts[:, h:h + 1]
        causal = _iota((1, ext), 1) <= tq
        sc = jnp.where(causal, sc, -jnp.inf)

        thr = _kth_largest(sc, None, topk)
        gt = sc > thr
        eq = sc == thr
        n_gt = jnp.sum(jnp.where(gt, 1.0, 0.0), axis=-1, keepdims=True)
        n_eq = jnp.sum(jnp.where(eq, 1.0, 0.0), axis=-1, keepdims=True)
        need = jnp.float32(topk) - n_gt
        sel_scr[:, 0:ext] = jnp.where(jnp.logical_and(sc >= thr, causal), 0.0, -jnp.inf)
        tie = jnp.logical_and(n_eq > need, thr > -jnp.inf)

        @pl.when(jnp.max(jnp.where(tie, 1.0, 0.0)) > 0.5)
        def _():
            ut = _strict_upper_bf16(128)
            run = jnp.zeros((qb, 1), F32)
            for c in range(ext // 128):
                sl = slice(c * 128, (c + 1) * 128)
                eqc = jnp.where(eq[:, sl], 1.0, 0.0)
                pref = _dot(eqc.astype(BF16), ut) + run
                keep = jnp.logical_or(gt[:, sl], jnp.logical_and(eq[:, sl], pref < need))
                sel_scr[:, sl] = jnp.where(jnp.logical_and(keep, causal[:, sl]), 0.0, -jnp.inf)
                run = run + jnp.sum(eqc, axis=-1, keepdims=True)

        bias = sel_scr[:, 0:ext]
        for g in range(A_KV):
            kg = k_ref[0:ext, g * HD:(g + 1) * HD]
            vg = v_ref[0:ext, g * HD:(g + 1) * HD]
            for j in range(NH // A_KV):
                h = g * (NH // A_KV) + j
                s = _dot_nt(aq[:, h * HD:(h + 1) * HD], kg) + bias
                m = jnp.max(s, axis=-1, keepdims=True)
                p = jnp.exp2(s - m)
                l = jnp.sum(p, axis=-1, keepdims=True)
                o_ref[:, h * HD:(h + 1) * HD] = _dot(p, vg) / l

    for j in range(n_ext):
        pl.when(i // per_ext == j)(functools.partial(body, (j + 1) * (t_keys // n_ext)))


def _dsa_prompt(iq_r, proj, ik_r, aq_r, k_r, v_r, nb, t):
    qb = min(128, t)
    nq = t // qb
    n_ext = min(4, nq)
    assert nq % n_ext == 0
    topk = min(TOPK_MAX, t // 4)
    n = nb * t
    row = lambda b, i: b * nq + i
    return pl.pallas_call(
        functools.partial(_dsa_prompt_kernel, topk=topk, qb=qb, n_ext=n_ext),
        out_shape=jax.ShapeDtypeStruct((n, GW), F32),
        grid=(nb, nq),
        in_specs=[pl.BlockSpec((qb, GW), lambda b, i: (row(b, i), 0)),
                  pl.BlockSpec((qb, 128), lambda b, i: (row(b, i), C_SM // 128)),
                  pl.BlockSpec((t, HD), lambda b, i: (b, 0)),
                  pl.BlockSpec((qb, GW), lambda b, i: (row(b, i), 0)),
                  pl.BlockSpec((t, 128), lambda b, i: (b, 0)),
                  pl.BlockSpec((t, 128), lambda b, i: (b, 0))],
        out_specs=pl.BlockSpec((qb, GW), lambda b, i: (row(b, i), 0)),
        scratch_shapes=[pltpu.VMEM((qb, t), F32), pltpu.VMEM((t, 3 * HD), F32)],
        compiler_params=_cp("parallel", "arbitrary"),
    )(iq_r, proj, ik_r, aq_r, k_r, v_r)


def _dsa_sample_score_kernel(*refs, n_pages):
    _, iq_ref, w_ref, ikn_ref = refs[:4]
    pages = refs[4:4 + n_pages]
    sc_ref, sn_ref = refs[4 + n_pages:]
    iq = iq_ref[0]
    w = w_ref[0] * IDX_SCALE
    k_hi, k_lo = _split_bf16(jnp.concatenate([pages[c][0] for c in range(n_pages)], axis=1))
    q_hi, q_lo = _split_bf16(iq)
    qk = _dot(jnp.concatenate([q_hi, q_hi, q_lo], axis=1), jnp.concatenate([k_hi, k_lo, k_hi], axis=0))
    sc_ref[0] = jnp.sum(jnp.maximum(qk, 0.0) * w, axis=0, keepdims=True)
    qkn = jnp.sum(iq * ikn_ref[0], axis=-1, keepdims=True)
    sn = jnp.sum(jnp.maximum(qkn, 0.0) * w, axis=0, keepdims=True)
    sn_ref[0] = jnp.broadcast_to(sn, (1, 128))


def _dsa_sample_select_kernel(sc_ref, sn_ref, sel_ref, seln_ref, *, topk):
    sc = sc_ref[...]
    sn = sn_ref[:, 0:1]
    bd, s_keys = sc.shape
    thr = _kth_largest(sc, sn, topk)
    gt = sc > thr
    eq = sc == thr
    n_gt = jnp.sum(jnp.where(gt, 1.0, 0.0), axis=-1, keepdims=True) + jnp.where(sn > thr, 1.0, 0.0)
    need = jnp.float32(topk) - n_gt
    ut = _strict_upper_bf16(PAGE)
    run = jnp.zeros((bd, 1), F32)
    for c in range(s_keys // PAGE):
        sl = slice(c * PAGE, (c + 1) * PAGE)
        eqc = jnp.where(eq[:, sl], 1.0, 0.0)
        pref = _dot(eqc.astype(BF16), ut) + run
        keep = jnp.logical_or(gt[:, sl], jnp.logical_and(eq[:, sl], pref < need))
        sel_ref[:, sl] = jnp.where(keep, 1.0, 0.0)
        run = run + jnp.sum(eqc, axis=-1, keepdims=True)
    sel_new = jnp.logical_or(sn > thr, jnp.logical_and(sn == thr, run < need))
    seln_ref[...] = jnp.broadcast_to(jnp.where(sel_new, 1.0, 0.0), seln_ref.shape)


def _dsa_sample_attn_kernel(*refs, n_pages):
    _, aq_ref, kn_ref, vn_ref, sel_ref, seln_ref = refs[:6]
    kpages = refs[6:6 + n_pages]
    vpages = refs[6 + n_pages:6 + 2 * n_pages]
    o_ref, k_scr, v_scr = refs[6 + 2 * n_pages:]
    for c in range(n_pages):
        k_scr[:, c * PAGE:(c + 1) * PAGE] = kpages[c][0]
        v_scr[:, c * PAGE:(c + 1) * PAGE] = vpages[c][0]
    sel = sel_ref[0] > 0.5
    sel_new = seln_ref[0][:, 0:1] > 0.5
    aq = aq_ref[0]
    kn = kn_ref[0]
    vn = vn_ref[0]
    hpg = NH // A_KV
    groups = range(A_KV)
    qg = [aq[g * hpg:(g + 1) * hpg, :] for g in groups]
    s = [_dot(qg[g], k_scr[g * HD:(g + 1) * HD, :]) * HD ** -0.5 for g in groups]
    pr, pn, l = [], [], []
    for g in groups:
        sg = jnp.where(sel, s[g], -jnp.inf)
        s_new = jnp.sum(qg[g] * kn[:, g * HD:(g + 1) * HD], axis=-1, keepdims=True) * HD ** -0.5
        s_new = jnp.where(sel_new, s_new, -jnp.inf)
        m = jnp.maximum(jnp.max(sg, axis=-1, keepdims=True), s_new)
        pr.append(jnp.exp(sg - m))
        pn.append(jnp.exp(s_new - m))
        l.append(jnp.sum(pr[g], axis=-1, keepdims=True) + pn[g])
    pv = [_dot_nt(pr[g], v_scr[g * HD:(g + 1) * HD, :]) for g in groups]
    for g in groups:
        o_ref[0, g * hpg:(g + 1) * hpg, :] = (pv[g] + pn[g] * vn[:, g * HD:(g + 1) * HD]) / l[g]


def _dsa_sample(layer, page_table, iq_r, iw, ik_r, aq_r, k_r, v_r, cki, ck, cv, n_pool):
    bd, n_pages = page_table.shape
    past = n_pages * PAGE
    topk = min(TOPK_MAX, (past + 1) // 4)
    base = layer * n_pool
    per_b = lambda b, pt: (b, 0, 0)

    def page_specs(width):
        return [pl.BlockSpec((1, width, PAGE), lambda b, pt, c=c: (base + pt[b, c], 0, 0)) for c in range(n_pages)]

    sc, sn = pl.pallas_call(
        functools.partial(_dsa_sample_score_kernel, n_pages=n_pages),
        out_shape=(jax.ShapeDtypeStruct((bd, 1, past), F32), jax.ShapeDtypeStruct((bd, 1, 128), F32)),
        grid_spec=pltpu.PrefetchScalarGridSpec(
            num_scalar_prefetch=1,
            grid=(bd,),
            in_specs=[pl.BlockSpec((1, NH, HD), per_b), pl.BlockSpec((1, NH, 1), per_b),
                      pl.BlockSpec((1, 1, HD), per_b)] + page_specs(HD),
            out_specs=(pl.BlockSpec((1, 1, past), per_b), pl.BlockSpec((1, 1, 128), per_b))),
        compiler_params=_cp("arbitrary"),
    )(page_table, iq_r.reshape(bd, NH, HD), iw.reshape(bd, NH, 1), ik_r.reshape(bd, 1, HD), *([cki] * n_pages))

    sel, seln = pl.pallas_call(
        functools.partial(_dsa_sample_select_kernel, topk=topk),
        out_shape=(jax.ShapeDtypeStruct((bd, past), F32), jax.ShapeDtypeStruct((bd, 128), F32)),
    )(sc.reshape(bd, past), sn.reshape(bd, 128))

    out = pl.pallas_call(
        functools.partial(_dsa_sample_attn_kernel, n_pages=n_pages),
        out_shape=jax.ShapeDtypeStruct((bd, NH, HD), F32),
        grid_spec=pltpu.PrefetchScalarGridSpec(
            num_scalar_prefetch=1,
            grid=(bd,),
            in_specs=[pl.BlockSpec((1, NH, HD), per_b), pl.BlockSpec((1, 1, 128), per_b),
                      pl.BlockSpec((1, 1, 128), per_b), pl.BlockSpec((1, 1, past), per_b),
                      pl.BlockSpec((1, 1, 128), per_b)] + page_specs(128) + page_specs(128),
            out_specs=pl.BlockSpec((1, NH, HD), per_b),
            scratch_shapes=[pltpu.VMEM((128, past), F32), pltpu.VMEM((128, past), F32)]),
        compiler_params=_cp("arbitrary"),
    )(page_table, aq_r.reshape(bd, NH, HD), k_r.reshape(bd, 1, 128), v_r.reshape(bd, 1, 128),
      sel.reshape(bd, 1, past), seln.reshape(bd, 1, 128), *([ck] * n_pages), *([cv] * n_pages))
    return out.reshape(bd, GW)


def _log_sigmoid(x):
    return jnp.minimum(x, 0.0) - jnp.log(1.0 + jnp.exp(-jnp.abs(x)))


MLSTM_ROWS = 2 * CHUNK


def _mlstm_prompt_kernel(m_ref, sm_ref, gb_ref, nw_ref, y_ref, c_ref, n_ref, mm_ref):
    L = CHUNK

    @pl.when(pl.program_id(1) == 0)
    def _():
        c_ref[...] = jnp.zeros(c_ref.shape, F32)
        n_ref[...] = jnp.zeros(n_ref.shape, F32)
        mm_ref[...] = jnp.zeros(mm_ref.shape, F32)

    ri, ci = _iota((L, L), 0), _iota((L, L), 1)
    tril = jnp.where(ci <= ri, 1.0, 0.0)
    triu = jnp.where(ci >= ri, 1.0, 0.0)
    same, causal, _ = _bd_masks(L)
    tile = lambda z: jnp.concatenate([z] * HG, axis=0)
    msk = lambda z: jnp.where(same, z, 0.0)
    stack_cols = lambda z, g: jnp.concatenate([z[:, g * HG + j:g * HG + j + 1] for j in range(HG)], axis=0)
    stack_rows = lambda z, g: jnp.concatenate([z[g * HG + j:g * HG + j + 1, :] for j in range(HG)], axis=1)
    percol = lambda zs: jnp.concatenate([jnp.broadcast_to(z, (L, 1)) for z in zs], axis=0)
    perrow = lambda zs: jnp.concatenate([jnp.broadcast_to(z, (1, HD)) for z in zs], axis=1)

    n_seq, n_grp = m_ref.shape[0], NH // HG
    chains = [(bb, g) for bb in range(n_seq) for g in range(n_grp)]
    each = lambda f: {c: f(c) for c in chains}
    lanes = lambda c: slice(c[1] * BD, (c[1] + 1) * BD)
    heads = lambda c: range(c[1] * HG, (c[1] + 1) * HG)
    gates = {}
    for bb in range(n_seq):
        smb = sm_ref[bb] + gb_ref[...]
        smt = smb.T
        gates[bb] = dict(ig=smb[:, SM_MI:SM_MI + NH], lf=_log_sigmoid(smb[:, SM_MF:SM_MF + NH]),
                         igt=smt[SM_MI:SM_MI + NH, :], lft=_log_sigmoid(smt[SM_MF:SM_MF + NH, :]))
    cbd = each(lambda c: c_ref[c[0], c[1]])
    nrow = each(lambda c: n_ref[c[0], :, lanes(c)])
    mprev = each(lambda c: [mm_ref[c[0], :, h:h + 1] for h in heads(c)])
    for cc in range(MLSTM_ROWS // L):
        rows = slice(cc * L, (cc + 1) * L)
        ig, bcs, bcst, igt = {}, {}, {}, {}
        for bb in range(n_seq):
            ig[bb] = gates[bb]['ig'][rows]
            bcs[bb] = _dot(tril, gates[bb]['lf'][rows], HI)
            bcst[bb] = _dot(gates[bb]['lft'][:, rows], triu, HI)
            igt[bb] = gates[bb]['igt'][:, rows]
        col = lambda c, w: m_ref[c[0], rows, w * GW + c[1] * BD:w * GW + (c[1] + 1) * BD]
        qexp = each(lambda c: msk(tile(col(c, 0))))
        kt = each(lambda c: tile(col(c, 1) * HD ** -0.5))
        kexp = each(lambda c: msk(kt[c]))
        vexp = each(lambda c: msk(tile(col(c, 2))))
        bcol = each(lambda c: stack_cols(bcs[c[0]], c[1]))
        icol = each(lambda c: stack_cols(ig[c[0]], c[1]))
        bl = each(lambda c: [bcs[c[0]][L - 1:L, h:h + 1] for h in heads(c)])
        dmat = each(lambda c: jnp.where(
            causal, bcol[c] - stack_rows(bcst[c[0]], c[1]) + stack_rows(igt[c[0]], c[1]), -jnp.inf))
        inter = each(lambda c: bcol[c] + percol(mprev[c]))
        mj = each(lambda c: jnp.maximum(inter[c], jnp.max(dmat[c], axis=-1, keepdims=True)))
        qk = each(lambda c: _dot_nt(qexp[c], kt[c]))
        s = each(lambda c: qk[c] * jnp.exp(dmat[c] - mj[c]))
        iw = each(lambda c: jnp.exp(inter[c] - mj[c]))
        sv = each(lambda c: _dot(s[c], vexp[c]))
        qc = each(lambda c: _dot_nt(qexp[c], cbd[c]))
        wl = each(lambda c: percol(bl[c]) - bcol[c] + icol[c])
        m_new = each(lambda c: [jnp.maximum(bl[c][j] + mprev[c][j],
                                            jnp.max(wl[c][j * L:(j + 1) * L], axis=0, keepdims=True))
                                for j in range(HG)])
        dec = each(lambda c: [jnp.exp(bl[c][j] + mprev[c][j] - m_new[c][j]) for j in range(HG)])
        ws = each(lambda c: jnp.exp(wl[c] - percol(m_new[c])))
        upd = each(lambda c: _dot_tn(vexp[c] * ws[c], kexp[c]))
        for c in chains:
            bb, ln = c[0], lanes(c)
            den = (jnp.sum(s[c], axis=-1, keepdims=True)
                   + iw[c] * jnp.sum(qexp[c] * nrow[c], axis=-1, keepdims=True))
            hc = (sv[c] + iw[c] * qc[c]) / jnp.maximum(jnp.abs(den), jnp.exp(-mj[c]))
            hn = hc * lax.rsqrt(jnp.sum(hc * hc, axis=-1, keepdims=True) * (1.0 / HD) + NORM_EPS)
            hn = sum(hn[j * L:(j + 1) * L] for j in range(HG))
            y_ref[bb, rows, ln] = hn * nw_ref[:, ln] * _sigmoid(col(c, 3))
        cbd = each(lambda c: percol(dec[c]) * cbd[c] + upd[c])
        nrow = each(lambda c: perrow(dec[c]) * nrow[c] + jnp.sum(kexp[c] * ws[c], axis=0, keepdims=True))
        mprev = m_new
    for c in chains:
        c_ref[c[0], c[1]] = cbd[c]
        n_ref[c[0], :, lanes(c)] = nrow[c]
        for j, h in enumerate(heads(c)):
            mm_ref[c[0], :, h:h + 1] = mprev[c][j]


def _gate_bias_row(b_i, b_f):
    z = lambda n: jnp.zeros((1, n), F32)
    return jnp.concatenate([z(SM_MI), b_i, b_f, z(128 - SM_MF - NH)], axis=1)


def _mlstm_prompt(proj, b_i, b_f, nw, nb, t):
    tm = MLSTM_ROWS
    nc = t // tm
    ng = NH // HG
    sb = RWKV_SEQS if nb % RWKV_SEQS == 0 else 1
    proj3 = proj.reshape(nb, t, NP)
    y, cbd, nrow, m = pl.pallas_call(
        _mlstm_prompt_kernel,
        out_shape=(jax.ShapeDtypeStruct((nb, t, GW), F32), jax.ShapeDtypeStruct((nb, ng, BD, BD), F32),
                   jax.ShapeDtypeStruct((nb, 1, GW), F32), jax.ShapeDtypeStruct((nb, 1, NH), F32)),
        grid=(nb // sb, nc),
        in_specs=[pl.BlockSpec((sb, tm, 4 * GW), lambda b, c: (b, c, 0)),
                  pl.BlockSpec((sb, tm, 128), lambda b, c: (b, c, C_SM // 128)),
                  pl.BlockSpec((1, 128), lambda b, c: (0, 0)),
                  pl.BlockSpec((1, GW), lambda b, c: (0, 0))],
        out_specs=(pl.BlockSpec((sb, tm, GW), lambda b, c: (b, c, 0)),
                   pl.BlockSpec((sb, ng, BD, BD), lambda b, c: (b, 0, 0, 0)),
                   pl.BlockSpec((sb, 1, GW), lambda b, c: (b, 0, 0)),
                   pl.BlockSpec((sb, 1, NH), lambda b, c: (b, 0, 0))),
        compiler_params=_cp("parallel", "arbitrary"),
    )(proj3, proj3, _gate_bias_row(b_i, b_f), nw)
    return y.reshape(nb * t, GW), _bd_diag(cbd), nrow.reshape(nb, NH, HD), m


def _head_expanders():
    diag = (_iota((NH * HD, HD), 0) % HD) == _iota((NH * HD, HD), 1)

    def rep(z):
        return jnp.concatenate([jnp.broadcast_to(z[h:h + 1, :], (HD, z.shape[1])) for h in range(NH)], axis=0)

    def fold(col):
        m = jnp.where(diag, col, 0.0)
        return jnp.concatenate([jnp.sum(m[h * HD:(h + 1) * HD], axis=0, keepdims=True) for h in range(NH)], axis=0)

    return rep, fold, diag


def _mlstm_step_kernel(x_ref, gt_ref, gb_ref, nw_ref, c_ref, n_ref, mm_ref, y_ref, co_ref, no_ref, mo_ref):
    x = x_ref[0]
    q, k, v, og = x[0:NH], x[NH:2 * NH] * HD ** -0.5, x[2 * NH:3 * NH], x[3 * NH:4 * NH]
    gates = gt_ref[0] + gb_ref[...]
    ig = gates[:, 0:1]
    rep, fold, diag = _head_expanders()
    c0 = c_ref[0].reshape(NH * HD, HD)
    n0 = n_ref[0]
    inter = _log_sigmoid(gates[:, 1:2]) + mm_ref[0]
    mj = jnp.maximum(inter, ig)
    s = jnp.sum(q * k, axis=-1, keepdims=True) * jnp.exp(ig - mj)
    iw = jnp.exp(inter - mj)
    ws = jnp.exp(ig - mj)
    den = s + iw * jnp.sum(n0 * q, axis=-1, keepdims=True)
    inv = 1.0 / jnp.maximum(jnp.abs(den), jnp.exp(-mj))
    lane = _iota((NH, 128), 1)
    per_head = jnp.where(lane == 0, s, jnp.where(lane == 1, iw, jnp.where(lane == 2, ws, inv)))
    ph = rep(per_head)
    s_c, iw_c, ws_c, inv_c = ph[:, 0:1], ph[:, 1:2], ph[:, 2:3], ph[:, 3:4]
    cq = jnp.sum(c0 * rep(q), axis=1, keepdims=True)
    vcol = jnp.sum(jnp.where(diag, rep(v), 0.0), axis=1, keepdims=True)
    hcol = (s_c * vcol + iw_c * cq) * inv_c
    co_ref[0] = (iw_c * c0 + (ws_c * vcol) * rep(k)).reshape(NH, HD, HD)
    no_ref[0] = iw * n0 + ws * k
    mo_ref[0] = mj
    h8 = fold(hcol)
    hn = h8 * lax.rsqrt(jnp.mean(h8 * h8, axis=-1, keepdims=True) + NORM_EPS)
    y_ref[0] = hn * nw_ref[...] * _sigmoid(og)


def _mlstm_step(proj, b_i, b_f, nw, c_all, l, n0, m0):
    bd = proj.shape[0]
    per_b3 = lambda b: (b, 0, 0)
    full = lambda b: (0, 0)
    x = proj[:, :4 * GW].reshape(bd, 4 * NH, HD)
    gates = proj[:, C_SM + SM_MI:C_SM + SM_MI + 2 * NH].reshape(bd, 2, NH).transpose(0, 2, 1)
    gate_bias = jnp.concatenate([b_i, b_f], axis=0).T
    y, c1, n1, m1 = pl.pallas_call(
        _mlstm_step_kernel,
        out_shape=(jax.ShapeDtypeStruct((bd, NH, HD), F32), jax.ShapeDtypeStruct((bd, NH, HD, HD), F32),
                   jax.ShapeDtypeStruct((bd, NH, HD), F32), jax.ShapeDtypeStruct((bd, NH, 1), F32)),
        grid=(bd,),
        in_specs=[pl.BlockSpec((1, 4 * NH, HD), per_b3),
                  pl.BlockSpec((1, NH, 2), per_b3),
                  pl.BlockSpec((NH, 2), full),
                  pl.BlockSpec((NH, HD), full),
                  pl.BlockSpec((1, NH, HD, HD), lambda b: (l * bd + b, 0, 0, 0)),
                  pl.BlockSpec((1, NH, HD), per_b3),
                  pl.BlockSpec((1, NH, 1), per_b3)],
        out_specs=(pl.BlockSpec((1, NH, HD), per_b3),
                   pl.BlockSpec((1, NH, HD, HD), lambda b: (b, 0, 0, 0)),
                   pl.BlockSpec((1, NH, HD), per_b3),
                   pl.BlockSpec((1, NH, 1), per_b3)),
        compiler_params=_cp("parallel"),
    )(x, gates, gate_bias, nw.reshape(NH, HD), c_all.reshape((-1,) + c_all.shape[2:]), n0, m0.reshape(bd, NH, 1))
    return y, c1, n1, m1


def _rwkv_prep_kernel(x_ref, prev_ref, mu_ref, w0_ref, ww2_ref, a0_ref, wa2_ref, wg2_ref, kk_ref, ka_ref,
                      r_o, lw_o, k_o, v_o, kk_o, a_o, g_o, carry_scr, *, seq):
    x = x_ref[...]
    tm = x.shape[0]
    if seq:
        first = jnp.where(pl.program_id(1) == 0, prev_ref[0], carry_scr[...])
        xprev = jnp.where(_iota((tm, 1), 0) == 0, first, pltpu.roll(x, 1, 0))
        carry_scr[...] = x[tm - 1:tm, :]
    else:
        xprev = prev_ref[...]
    xm = x + (xprev - x) * mu_ref[...]
    r = xm[:, 0:GW]
    kx = xm[:, GW:2 * GW]
    v = xm[:, 2 * GW:3 * GW]
    xw = xm[:, 3 * GW:3 * GW + 64]
    xa = xm[:, 3 * GW + 64:3 * GW + 128]
    xg = xm[:, 3 * GW + 128:R_IN]
    w = -_softplus(-(w0_ref[...] + _dot(jnp.tanh(xw), ww2_ref[...]))) - 0.5
    a = _sigmoid(a0_ref[...] + _dot(xa, wa2_ref[...]))
    r_o[...] = r
    lw_o[...] = -jnp.exp(w)
    v_o[...] = v
    a_o[...] = a
    g_o[...] = _dot(_sigmoid(xg), wg2_ref[...])
    k_o[...] = kx * (1.0 + (a - 1.0) * ka_ref[...])
    kk = kx * kk_ref[...]
    for h in range(NH):
        kh = kk[:, h * HD:(h + 1) * HD]
        nrm = jnp.sqrt(jnp.sum(kh * kh, axis=-1, keepdims=True))
        kk_o[:, h * HD:(h + 1) * HD] = kh / jnp.maximum(nrm, 1e-12)


def _rwkv_prep(proj, prev, mu, w0, ww2, a0, wa2, wg2, k_k, k_a, nb, t):
    n = nb * t
    seq = t > 1
    full = lambda *_: (0, 0)
    if seq:
        tm = min(256, t)
        nt = t // tm
        grid = (nb, nt)
        xmap = lambda b, i: (b * nt + i, C_RIN // R_IN)
        pspec = pl.BlockSpec((1, 1, R_IN), lambda b, i: (b, 0, 0))
        omap = lambda b, i: (b * nt + i, 0)
        sem = ("parallel", "arbitrary")
    else:
        tm = n
        grid = (1,)
        xmap = lambda i: (0, C_RIN // R_IN)
        pspec = pl.BlockSpec((tm, R_IN), lambda i: (0, 0))
        omap = lambda i: (0, 0)
        sem = ("arbitrary",)
    wspecs = [pl.BlockSpec((1, R_IN), full), pl.BlockSpec((1, GW), full), pl.BlockSpec((64, GW), full),
              pl.BlockSpec((1, GW), full), pl.BlockSpec((64, GW), full), pl.BlockSpec((128, GW), full),
              pl.BlockSpec((1, GW), full), pl.BlockSpec((1, GW), full)]
    return pl.pallas_call(
        functools.partial(_rwkv_prep_kernel, seq=seq),
        out_shape=tuple(jax.ShapeDtypeStruct((n, GW), F32) for _ in range(7)),
        grid=grid,
        in_specs=[pl.BlockSpec((tm, R_IN), xmap), pspec] + wspecs,
        out_specs=tuple(pl.BlockSpec((tm, GW), omap) for _ in range(7)),
        scratch_shapes=[pltpu.VMEM((1, R_IN), F32)],
        compiler_params=_cp(*sem),
    )(proj, prev, mu, w0, ww2, a0, wa2, wg2, k_k, k_a)


def _rwkv_scan_kernel(r_ref, lw_ref, k_ref, v_ref, kk_ref, a_ref, g_ref, rk_ref, lnw_ref, lnb_ref, y_ref, s_ref):
    C = CHUNK

    @pl.when(pl.program_id(1) == 0)
    def _():
        s_ref[...] = jnp.zeros(s_ref.shape, F32)

    ri, ci = _iota((C, C), 0), _iota((C, C), 1)
    same, incl, strict = _bd_masks(C)
    row, col = _iota((BD, BD), 0), _iota((BD, BD), 1)
    eye = jnp.where(row == col, 1.0, 0.0)
    strict_incl = jnp.concatenate([strict, incl], axis=0)
    tile = lambda z: jnp.concatenate([z] * HG, axis=0)
    msk = lambda z: jnp.where(same, z, 0.0)
    fold = lambda z: sum(z[j * C:(j + 1) * C] for j in range(HG))
    n_seq, n_grp = r_ref.shape[0], NH // HG
    chains = [(bb, g) for bb in range(n_seq) for g in range(n_grp)]
    each = lambda f: {c: f(c) for c in chains}
    lanes = lambda c: slice(c[1] * BD, (c[1] + 1) * BD)
    seq = {}
    for bb in range(n_seq):
        lw = lw_ref[bb]
        cs = _dot(jnp.where(ci <= ri, 1.0, 0.0), lw, HI)
        gam = jnp.exp(cs)
        ginv = jnp.exp(-cs)
        r, k, v, kk = r_ref[bb], k_ref[bb], v_ref[bb], kk_ref[bb]
        seq[bb] = dict(at=-kk * jnp.exp(cs - lw), bt=kk * a_ref[bb] * ginv, kt=k * ginv, rt=r * gam, v=v,
                       glast=gam[C - 1:C, :], bonus_in=r * k * rk_ref[...])
    part = lambda c, name: seq[c[0]][name][:, lanes(c)]
    btl = each(lambda c: tile(part(c, 'bt')))
    ktl = each(lambda c: tile(part(c, 'kt')))
    vexp = each(lambda c: msk(tile(part(c, 'v'))))
    ar = each(lambda c: jnp.concatenate([msk(tile(part(c, 'at'))), msk(tile(part(c, 'rt')))], axis=0))
    sbd = each(lambda c: s_ref[c[0], c[1]])
    gb = each(lambda c: _dot_nt(ar[c], btl[c]))
    gk = each(lambda c: _dot_nt(ar[c], ktl[c]))
    gs = each(lambda c: _dot_nt(ar[c], sbd[c]))
    n_ab = each(lambda c: jnp.where(strict, gb[c][:BD], 0.0))
    x = each(lambda c: eye + n_ab[c])
    pm = each(lambda c: _dot(n_ab[c], n_ab[c]))
    for j in range(5):
        xd = each(lambda c: _dot(x[c], pm[c]))
        if j < 4:
            pm = each(lambda c: _dot(pm[c], pm[c]))
        x = each(lambda c: x[c] + xd[c])
    akv = each(lambda c: _dot(jnp.where(strict_incl, gk[c], 0.0), vexp[c]))
    u = each(lambda c: _dot(x[c], gs[c][:BD] + akv[c][:BD]))
    o = each(lambda c: gs[c][BD:] + _dot(jnp.where(incl, gb[c][BD:], 0.0), u[c]) + akv[c][BD:])
    gl = each(lambda c: part(c, 'glast'))
    s_new = each(lambda c: sbd[c] * gl[c] + _dot_tn(
        jnp.concatenate([u[c], vexp[c]], axis=0),
        jnp.concatenate([msk(btl[c] * gl[c]), msk(ktl[c] * gl[c])], axis=0)))
    for c in chains:
        bb, ln = c[0], lanes(c)
        s_ref[c[0], c[1]] = s_new[c]
        mean = jnp.sum(o[c], axis=-1, keepdims=True) * (1.0 / HD)
        dev = msk(o[c] - mean)
        var = jnp.sum(dev * dev, axis=-1, keepdims=True) * (1.0 / HD)
        on = fold(dev * lax.rsqrt(var + R_LN_EPS)) * lnw_ref[:, ln] + lnb_ref[:, ln]
        bonus = fold(jnp.sum(msk(tile(part(c, 'bonus_in'))), axis=-1, keepdims=True) * vexp[c])
        y_ref[bb, :, ln] = (on + bonus) * g_ref[bb, :, ln]


def _rwkv_scan(rs, rk, lnw, lnb, nb, t):
    nc = t // CHUNK
    ng = NH // HG
    sb = RWKV_SEQS if nb % RWKV_SEQS == 0 else 1
    rowmap = lambda b, c: (b, c, 0)
    full = lambda b, c: (0, 0)
    y, sbd = pl.pallas_call(
        _rwkv_scan_kernel,
        out_shape=(jax.ShapeDtypeStruct((nb, t, GW), F32), jax.ShapeDtypeStruct((nb, ng, BD, BD), F32)),
        grid=(nb // sb, nc),
        in_specs=[pl.BlockSpec((sb, CHUNK, GW), rowmap)] * 7 + [pl.BlockSpec((1, GW), full)] * 3,
        out_specs=(pl.BlockSpec((sb, CHUNK, GW), rowmap), pl.BlockSpec((sb, ng, BD, BD), lambda b, c: (b, 0, 0, 0))),
        compiler_params=_cp("parallel", "arbitrary"),
    )(*(z.reshape(nb, t, GW) for z in rs), rk, lnw, lnb)
    return y.reshape(nb * t, GW), _bd_diag(sbd)


def _rwkv_step_kernel(r_ref, lw_ref, k_ref, v_ref, kk_ref, a_ref, g_ref, rk_ref, lnw_ref, lnb_ref, s_ref,
                      y_ref, so_ref):
    r, lw, k, v, kk, a, g = (ref[0] for ref in (r_ref, lw_ref, k_ref, v_ref, kk_ref, a_ref, g_ref))
    rep, fold, diag = _head_expanders()
    s0 = s_ref[0].reshape(NH * HD, HD)
    kk_rep = rep(kk)
    sk = jnp.sum(s0 * kk_rep, axis=1, keepdims=True)
    vcol = jnp.sum(jnp.where(diag, rep(v), 0.0), axis=1, keepdims=True)
    s1 = s0 * rep(jnp.exp(lw)) - sk * rep(kk * a) + vcol * rep(k)
    so_ref[0] = s1.reshape(NH, HD, HD)
    ocol = jnp.sum(s1 * rep(r), axis=1, keepdims=True)
    o = fold(ocol)
    mean = jnp.mean(o, axis=-1, keepdims=True)
    var = jnp.mean(jnp.square(o - mean), axis=-1, keepdims=True)
    on = (o - mean) * lax.rsqrt(var + R_LN_EPS) * lnw_ref[...] + lnb_ref[...]
    bonus = jnp.sum(r * k * rk_ref[...], axis=-1, keepdims=True) * v
    y_ref[0] = (on + bonus) * g


def _rwkv_step(rs, rk, lnw, lnb, s_all, l):
    bd = s_all.shape[1]
    per_b = lambda b: (b, 0, 0)
    full = lambda b: (0, 0)
    return pl.pallas_call(
        _rwkv_step_kernel,
        out_shape=(jax.ShapeDtypeStruct((bd, NH, HD), F32), jax.ShapeDtypeStruct((bd, NH, HD, HD), F32)),
        grid=(bd,),
        in_specs=[pl.BlockSpec((1, NH, HD), per_b)] * 7 + [pl.BlockSpec((NH, HD), full)] * 3
                 + [pl.BlockSpec((1, NH, HD, HD), lambda b: (l * bd + b, 0, 0, 0))],
        out_specs=(pl.BlockSpec((1, NH, HD), per_b), pl.BlockSpec((1, NH, HD, HD), lambda b: (b, 0, 0, 0))),
        compiler_params=_cp("parallel"),
    )(*(z.reshape(bd, NH, HD) for z in rs), *(z.reshape(NH, HD) for z in (rk, lnw, lnb)),
      s_all.reshape((-1,) + s_all.shape[2:]))


def _gelu_tanh(x):
    return 0.5 * x * (1.0 + jnp.tanh(math.sqrt(2.0 / math.pi) * (x + 0.044715 * (x * x * x))))


def _s5_kernel(u_ref, bre_ref, bim_ref, lre_ref, lim_ref, cre_ref, cim_ref, d_ref, wg_ref, bg_ref, h0r_ref, h0i_ref,
               y_ref, hr_ref, hi_ref, hre_scr, him_scr, *, nb, tb):
    @pl.when(pl.program_id(0) == 0)
    def _():
        hr_ref[...] = h0r_ref[...]
        hi_ref[...] = h0i_ref[...]

    u = u_ref[...]
    cw, sw = GW // S5_SPLIT, S5_W // S5_SPLIT
    for j in range(S5_SPLIT):
        cs_, ss_ = slice(j * cw, (j + 1) * cw), slice(j * sw, (j + 1) * sw)
        hre_scr[:, ss_] = _dot(u[:, cs_], bre_ref[cs_, ss_])
        him_scr[:, ss_] = _dot(u[:, cs_], bim_ref[cs_, ss_])
    lr = lre_ref[...]
    li = lim_ref[...]

    def body(t, carry):
        hr, hi = carry
        rows = pl.ds(pl.multiple_of(t * nb, nb), nb)
        nr = lr * hr - li * hi + hre_scr[rows, :]
        ni = lr * hi + li * hr + him_scr[rows, :]
        hre_scr[rows, :] = nr
        him_scr[rows, :] = ni
        return nr, ni

    hr, hi = lax.fori_loop(0, tb, body, (hr_ref[...], hi_ref[...]))
    hr_ref[...] = hr
    hi_ref[...] = hi
    y = jnp.concatenate(
        [_dot(hre_scr[:, j * sw:(j + 1) * sw], cre_ref[j * sw:(j + 1) * sw, j * cw:(j + 1) * cw])
         - _dot(him_scr[:, j * sw:(j + 1) * sw], cim_ref[j * sw:(j + 1) * sw, j * cw:(j + 1) * cw])
         for j in range(S5_SPLIT)], axis=1) + d_ref[...] * u
    y = _gelu_tanh(y)
    y_ref[...] = y * _sigmoid(_dot(y, wg_ref[...]) + bg_ref[...])


def _s5(u_tm, mats, h0r, h0i, nb, t):
    bre, bim, lre, lim, cre, cim, d, wg, bg = mats
    tb = min(64, t)
    full = lambda i: (0, 0)
    return pl.pallas_call(
        functools.partial(_s5_kernel, nb=nb, tb=tb),
        out_shape=(jax.ShapeDtypeStruct((t * nb, GW), F32), jax.ShapeDtypeStruct((nb, S5_W), F32),
                   jax.ShapeDtypeStruct((nb, S5_W), F32)),
        grid=(t // tb,),
        in_specs=[pl.BlockSpec((tb * nb, GW), lambda i: (i, 0)),
                  pl.BlockSpec((GW, S5_W), full), pl.BlockSpec((GW, S5_W), full),
                  pl.BlockSpec((1, S5_W), full), pl.BlockSpec((1, S5_W), full),
                  pl.BlockSpec((S5_W, GW), full), pl.BlockSpec((S5_W, GW), full),
                  pl.BlockSpec((1, GW), full), pl.BlockSpec((GW, GW), full), pl.BlockSpec((1, GW), full),
                  pl.BlockSpec((nb, S5_W), full), pl.BlockSpec((nb, S5_W), full)],
        out_specs=(pl.BlockSpec((tb * nb, GW), lambda i: (i, 0)),
                   pl.BlockSpec((nb, S5_W), full), pl.BlockSpec((nb, S5_W), full)),
        scratch_shapes=[pltpu.VMEM((tb * nb, S5_W), F32), pltpu.VMEM((tb * nb, S5_W), F32)],
        compiler_params=_cp("arbitrary"),
    )(u_tm, bre, bim, lre, lim, cre, cim, d, wg, bg, h0r, h0i)


def _s5_mats(a_re, a_im, b_re, b_im, c_re, c_im, d_skip, log_dt, w_glu, b_glu):
    dt = jnp.exp(log_dt)
    mag = jnp.exp(a_re * dt)
    lb_re, lb_im = mag * jnp.cos(a_im * dt), mag * jnp.sin(a_im * dt)
    den = a_re * a_re + a_im * a_im
    f_re = ((lb_re - 1.0) * a_re + lb_im * a_im) / den
    f_im = (lb_im * a_re - (lb_re - 1.0) * a_im) / den
    bb_re = f_re[..., None] * b_re - f_im[..., None] * b_im
    bb_im = f_re[..., None] * b_im + f_im[..., None] * b_re
    eye = jnp.eye(S5_G, dtype=F32)
    bd = lambda bb: jnp.einsum('gpc,gh->gchp', bb, eye).reshape(GW, S5_W)
    cd = lambda cc: jnp.einsum('gcp,gh->gphc', cc, eye).reshape(S5_W, GW)
    return (bd(bb_re), bd(bb_im), lb_re.reshape(1, S5_W), lb_im.reshape(1, S5_W), cd(c_re), cd(c_im),
            d_skip.reshape(1, GW), w_glu, b_glu.reshape(1, GW))


def _ffn_up_kernel(h_ref, wa_ref, wb_ref, cw_ref, cb_ref, s0_ref, s1_ref, y_ref, a_ref, carry_scr, *, seq):
    def gate(a, a1, a2, b):
        c = cb_ref[...] + a2 * cw_ref[0:1, :] + a1 * cw_ref[1:2, :] + a * cw_ref[2:3, :]
        return (c * _sigmoid(c) * b).astype(BF16)

    if not seq:
        h = h_ref[...]
        a = _dot(h, wa_ref[...])
        a_ref[...] = a
        y_ref[...] = gate(a, s1_ref[...], s0_ref[...], _dot(h, wb_ref[...]))
        return

    @pl.when(pl.program_id(2) == 0)
    def _():
        carry_scr[...] = s0_ref[0]

    hm = h_ref.shape[0] // FFN_PARTS
    rowid = _iota((hm, 1), 0)
    c0 = carry_scr[0:1, :]
    c1 = carry_scr[1:2, :]
    for p in range(FFN_PARTS):
        rows = slice(p * hm, (p + 1) * hm)
        h = h_ref[rows, :]
        a = _dot(h, wa_ref[...])
        b = _dot(h, wb_ref[...])
        a1 = jnp.where(rowid == 0, c1, pltpu.roll(a, 1, 0))
        a2 = jnp.where(rowid == 0, c0, jnp.where(rowid == 1, c1, pltpu.roll(a, 2, 0)))
        y_ref[rows, :] = gate(a, a1, a2, b)
        c0 = a[hm - 2:hm - 1, :]
        c1 = a[hm - 1:hm, :]
    last2 = jnp.concatenate([c0, c1], axis=0)
    carry_scr[...] = last2
    a_ref[0] = last2


def _ffn_up(h2, w_up, l, cw, cb, st, nb, t):
    n, d = h2.shape
    dff = w_up.shape[2] // 2
    tn = 512
    nj = dff // tn
    if t > 1:
        tm = min(MM_ROWS, t)
        nt = t // tm
        y, fc = pl.pallas_call(
            functools.partial(_ffn_up_kernel, seq=True),
            out_shape=(jax.ShapeDtypeStruct((n, dff), BF16), jax.ShapeDtypeStruct((nb, 2, dff), F32)),
            grid=(nb, nj, nt),
            in_specs=[pl.BlockSpec((tm, d), lambda b, j, i: (b * nt + i, 0)),
                      pl.BlockSpec((None, d, tn), lambda b, j, i: (l, 0, j)),
                      pl.BlockSpec((None, d, tn), lambda b, j, i: (l, 0, nj + j)),
                      pl.BlockSpec((3, tn), lambda b, j, i: (0, j)),
                      pl.BlockSpec((1, tn), lambda b, j, i: (0, j)),
                      pl.BlockSpec((1, 2, tn), lambda b, j, i: (b, 0, j)),
                      pl.BlockSpec((1, 2, tn), lambda b, j, i: (b, 0, j))],
            out_specs=(pl.BlockSpec((tm, tn), lambda b, j, i: (b * nt + i, j)),
                       pl.BlockSpec((1, 2, tn), lambda b, j, i: (b, 0, j))),
            scratch_shapes=[pltpu.VMEM((2, tn), F32)],
            compiler_params=_cp("parallel", "parallel", "arbitrary"),
        )(h2, w_up, w_up, cw, cb, st, st)
        return y, fc
    s0, s1 = st[:, 0, :], st[:, 1, :]
    y, a = pl.pallas_call(
        functools.partial(_ffn_up_kernel, seq=False),
        out_shape=(jax.ShapeDtypeStruct((n, dff), BF16), jax.ShapeDtypeStruct((n, dff), F32)),
        grid=(nj,),
        in_specs=[pl.BlockSpec((n, d), lambda j: (0, 0)),
                  pl.BlockSpec((None, d, tn), lambda j: (l, 0, j)),
                  pl.BlockSpec((None, d, tn), lambda j: (l, 0, nj + j)),
                  pl.BlockSpec((3, tn), lambda j: (0, j)),
                  pl.BlockSpec((1, tn), lambda j: (0, j)),
                  pl.BlockSpec((n, tn), lambda j: (0, j)),
                  pl.BlockSpec((n, tn), lambda j: (0, j))],
        out_specs=(pl.BlockSpec((n, tn), lambda j: (0, j)), pl.BlockSpec((n, tn), lambda j: (0, j))),
        scratch_shapes=[pltpu.VMEM((2, tn), F32)],
        compiler_params=_cp("parallel"),
    )(h2, w_up, w_up, cw, cb, s0, s1)
    return y, jnp.stack([s1, a], axis=1)


def _permute_w_in(w):
    wt = jnp.swapaxes(w, 1, 2)
    rows = lambda s, n: wt[:, s:s + n, :]
    parts = [rows(0, 4 * GW), rows(2064, GW), rows(2832, GW), rows(5208, GW), rows(3416, R_IN),
             rows(2576, 128), rows(2704, 128), rows(3344, HD), rows(2048, NH), rows(2056, NH), rows(3408, NH),
             jnp.zeros((w.shape[0], NP - C_SM - HD - 3 * NH, w.shape[1]), w.dtype)]
    return jnp.swapaxes(jnp.concatenate(parts, axis=1), 1, 2).astype(BF16)


def _layer(x2, nb, t, l, W, st, tables, cache):
    n = nb * t
    c0, n0, m0, rs0, rsh0, sre0, sim0, conv0 = st
    proj = _in_proj(x2, W['norm_mix'], W['w_in'], l)
    aq_r, iq_r, k_r, v_r, ik_r = _rope_call(proj, tables[0], tables[1], *tables[2])

    if cache is None:
        ym, c1, n1, m1 = _mlstm_prompt(proj, W['m_b_i'], W['m_b_f'], W['m_norm'], nb, t)
        ya = _dsa_prompt(iq_r, proj, ik_r, aq_r, k_r, v_r, nb, t)
    else:
        ym, c1, n1, m1 = _mlstm_step(proj, W['m_b_i'], W['m_b_f'], W['m_norm'], c0, l, n0, m0)
        ym = ym.reshape(n, GW)
        cki, ck, cv, page_table, n_pool = cache
        ya = _dsa_sample(l, page_table, iq_r, proj[:, C_SM + SM_IW:C_SM + SM_IW + NH], ik_r, aq_r, k_r, v_r,
                         cki, ck, cv, n_pool)

    prev = rsh0.reshape(nb, 1, R_IN) if t > 1 else rsh0
    rs = _rwkv_prep(proj, prev, W['r_mu'], W['r_w0'], W['r_w_w2'], W['r_a0'], W['r_w_a2'], W['r_w_g2'],
                    W['r_k_k'], W['r_k_a'], nb, t)
    if t > 1:
        yr, rs1 = _rwkv_scan(rs, W['r_r_k'], W['r_ln_w'], W['r_ln_b'], nb, t)
    else:
        yr, rs1 = _rwkv_step(rs, W['r_r_k'], W['r_ln_w'], W['r_ln_b'], rs0, l)
        yr = yr.reshape(n, GW)
    rsh1 = proj.reshape(nb, t, NP)[:, t - 1, C_RIN:C_RIN + R_IN]

    su = proj[:, C_SU:C_SU + GW]
    u_tm = su.reshape(nb, t, GW).transpose(1, 0, 2).reshape(t * nb, GW)
    ys_tm, sre1, sim1 = _s5(u_tm, W['s5'], sre0.reshape(nb, S5_W), sim0.reshape(nb, S5_W), nb, t)
    ys = ys_tm.reshape(t, nb, GW).transpose(1, 0, 2).reshape(n, GW)

    x2 = _res_matmul(x2, [ym, ya, yr, ys], W['w_out'], l)
    h2 = _rmsnorm(x2, W['norm_ffn'], BF16)
    y, conv1 = _ffn_up(h2, W['ffn_w_up'], l, W['ffn_conv_w'], W['ffn_conv_b'], conv0, nb, t)
    x2 = _res_matmul(x2, [y], W['ffn_w_down'], l)
    outs = (k_r.reshape(nb, t, A_KV, HD), v_r.reshape(nb, t, A_KV, HD), ik_r.reshape(nb, t, HD),
            c1, n1, m1.reshape(nb, NH), rs1, rsh1, sre1.reshape(nb, S5_G, S5_P), sim1.reshape(nb, S5_G, S5_P), conv1)
    return x2, outs


def kernel(x_prompt, x_sample, cache_k, cache_v, cache_kidx, page_table, state_mlstm_c, state_mlstm_n,
           state_mlstm_m, state_rwkv_s, state_rwkv_shift, state_s5_re, state_s5_im, state_ffn_conv,
           norm_mix, w_in, w_out, m_b_i, m_b_f, m_norm, r_mu, r_w0, r_w_w2, r_a0, r_w_a2, r_w_g2,
           r_k_k, r_k_a, r_r_k, r_ln_w, r_ln_b, s5_a_re, s5_a_im, s5_b_re, s5_b_im, s5_c_re, s5_c_im,
           s5_d, s5_log_dt, s5_w_glu, s5_b_glu, norm_ffn, ffn_w_up, ffn_conv_w, ffn_conv_b, ffn_w_down,
           norm_final):
    bp, tp, d = x_prompt.shape
    bs, ts, _ = x_sample.shape
    assert ts == 1 and tp % CHUNK == 0
    depth = w_in.shape[0]
    n_pool = cache_k.shape[1]
    past = page_table.shape[1] * PAGE
    dff = ffn_conv_b.shape[-1]

    row = lambda z: z.reshape(1, -1)
    layers = []
    w_in_all, w_out_all = _permute_w_in(w_in), w_out.astype(BF16)
    w_up_all, w_down_all = ffn_w_up.astype(BF16), ffn_w_down.astype(BF16)
    for l in range(depth):
        layers.append(dict(
            norm_mix=row(norm_mix[l]), w_in=w_in_all, w_out=w_out_all,
            m_b_i=row(m_b_i[l]), m_b_f=row(m_b_f[l]), m_norm=row(m_norm[l]),
            r_mu=row(r_mu[l]), r_w0=row(r_w0[l]), r_w_w2=r_w_w2[l], r_a0=row(r_a0[l]), r_w_a2=r_w_a2[l],
            r_w_g2=r_w_g2[l], r_k_k=row(r_k_k[l]), r_k_a=row(r_k_a[l]), r_r_k=row(r_r_k[l]),
            r_ln_w=row(r_ln_w[l]), r_ln_b=row(r_ln_b[l]),
            s5=_s5_mats(s5_a_re[l], s5_a_im[l], s5_b_re[l], s5_b_im[l], s5_c_re[l], s5_c_im[l], s5_d[l],
                        s5_log_dt[l], s5_w_glu[l], s5_b_glu[l]),
            norm_ffn=row(norm_ffn[l]), ffn_w_up=w_up_all, ffn_conv_w=ffn_conv_w[l],
            ffn_conv_b=row(ffn_conv_b[l]), ffn_w_down=w_down_all))

    cos_p, sin_p = _rope_tables(jnp.arange(tp))
    cos_s, sin_s = _rope_tables(jnp.full((bs,), past))
    tab_p = (cos_p, sin_p, (bp, tp))
    tab_s = (cos_s, sin_s, (1, bs))

    zeros = lambda *s: jnp.zeros(s, F32)
    st_p = (zeros(bp, NH, HD, HD), zeros(bp, NH, HD), zeros(bp, NH), zeros(bp, NH, HD, HD), zeros(bp, R_IN),
            zeros(bp, S5_G, S5_P), zeros(bp, S5_G, S5_P), zeros(bp, 2, dff))
    cki = cache_kidx.reshape(depth * n_pool, PAGE, HD).transpose(0, 2, 1)
    ck = cache_k.reshape(depth * n_pool, PAGE, A_KV * HD).transpose(0, 2, 1)
    cv = cache_v.reshape(depth * n_pool, PAGE, A_KV * HD).transpose(0, 2, 1)

    xp = x_prompt.reshape(bp * tp, d)
    xs = x_sample.reshape(bs, d)
    new_p, new_s = [], []
    for l in range(depth):
        xp, sp = _layer(xp, bp, tp, l, layers[l], st_p, tab_p, None)
        st_s = (state_mlstm_c, state_mlstm_n[l], state_mlstm_m[l], state_rwkv_s, state_rwkv_shift[l],
                state_s5_re[l], state_s5_im[l], state_ffn_conv[l])
        xs, ss = _layer(xs, bs, 1, l, layers[l], st_s, tab_s, (cki, ck, cv, page_table, n_pool))
        new_p.append(sp)
        new_s.append(ss)
    (k_p, v_p, ki_p, mc_p, mn_p, mm_p, rs_p, rsh_p, sre_p, sim_p, fc_p) = [jnp.stack(z) for z in zip(*new_p)]
    (k_s, v_s, ki_s, mc_s, mn_s, mm_s, rs_s, rsh_s, sre_s, sim_s, fc_s) = [jnp.stack(z) for z in zip(*new_s)]
    y_prompt = _rmsnorm(xp, row(norm_final), F32).reshape(bp, tp, d)
    y_sample = _rmsnorm(xs, row(norm_final), F32).reshape(bs, ts, d)
    return (y_prompt, y_sample, k_p, k_s, v_p, v_s, ki_p, ki_s, mc_p, mc_s, mn_p, mn_s, mm_p, mm_s,
            rs_p, rs_s, rsh_p, rsh_s, sre_p, sre_s, sim_p, sim_s, fc_p, fc_s)
```

```python
import functools
import math

import jax
import jax.numpy as jnp
from jax import lax
from jax.experimental import pallas as pl
from jax.experimental.pallas import tpu as pltpu

F32 = jnp.float32
BF16 = jnp.bfloat16
HI = lax.Precision.HIGHEST

HD = 64
NH = 8
GW = NH * HD
A_KV = 2
PAGE = 128
TOPK_MAX = 256
ROPE_THETA = 10000.0
R_IN = 3 * GW + 64 + 64 + 128
R_LN_EPS = 64e-5
S5_G, S5_CH, S5_P = 32, 16, 64
S5_W = S5_G * S5_P
S5_SPLIT = 4
NORM_EPS = 1e-6
CHUNK = 64
IDX_SCALE = HD ** -0.5 * NH ** -0.5

C_M, C_AQ, C_IQ, C_SU, C_RIN, C_AK, C_AV, C_SM = 0, 2048, 2560, 3072, 3584, 5376, 5504, 5632
NP = 6144
SM_IK, SM_MI, SM_MF, SM_IW = 0, 64, 72, 80

VMEM_LIMIT = 56 * 1024 * 1024
MM_ROWS = 1024
RWKV_SEQS = 4
FFN_PARTS = 2


def _cp(*sem):
    return pltpu.CompilerParams(dimension_semantics=sem, vmem_limit_bytes=VMEM_LIMIT)


def _dot(a, b, prec=None):
    return jnp.dot(a, b, preferred_element_type=F32, precision=prec)


def _dot_nt(a, b, prec=None):
    return lax.dot_general(a, b, (((1,), (1,)), ((), ())), preferred_element_type=F32, precision=prec)


def _dot_tn(a, b, prec=None):
    return lax.dot_general(a, b, (((0,), (0,)), ((), ())), preferred_element_type=F32, precision=prec)


def _sigmoid(x):
    return 1.0 / (1.0 + jnp.exp(-x))


def _softplus(x):
    return jnp.maximum(x, 0.0) + jnp.log(1.0 + jnp.exp(-jnp.abs(x)))


def _iota(shape, dim):
    return lax.broadcasted_iota(jnp.int32, shape, dim)


HG = 4
BD = HG * HD
assert CHUNK == HD


def _bd_masks(c):
    row, col = _iota((HG * c, HG * c), 0), _iota((HG * c, HG * c), 1)
    same = (row // c) == (col // c)
    t, s = row % c, col % c
    return same, jnp.logical_and(same, s <= t), jnp.logical_and(same, s < t)


def _bd_diag(x):
    nb, ng = x.shape[:2]
    x6 = x.reshape(nb, ng, HG, HD, HG, HD)
    return jnp.stack([x6[:, :, j, :, j, :] for j in range(HG)], axis=2).reshape(nb, ng * HG, HD, HD)


def _inproj_kernel(x_ref, g_ref, w_ref, o_ref, h_scr):
    @pl.when(pl.program_id(1) == 0)
    def _():
        x = x_ref[...]
        ms = jnp.mean(x * x, axis=-1, keepdims=True)
        h_scr[...] = (x * lax.rsqrt(ms + NORM_EPS) * g_ref[...]).astype(BF16)

    o_ref[...] = _dot(h_scr[...], w_ref[...])


def _in_proj(x2, g, w, l):
    n, d = x2.shape
    npad = w.shape[2]
    tm = min(MM_ROWS, n)
    tn = 768
    return pl.pallas_call(
        _inproj_kernel,
        out_shape=jax.ShapeDtypeStruct((n, npad), F32),
        grid=(n // tm, npad // tn),
        in_specs=[pl.BlockSpec((tm, d), lambda i, j: (i, 0)),
                  pl.BlockSpec((1, d), lambda i, j: (0, 0)),
                  pl.BlockSpec((None, d, tn), lambda i, j: (l, 0, j))],
        out_specs=pl.BlockSpec((tm, tn), lambda i, j: (i, j)),
        scratch_shapes=[pltpu.VMEM((tm, d), BF16)],
        compiler_params=_cp("parallel", "arbitrary"),
    )(x2, g, w)


def _rmsnorm_kernel(x_ref, g_ref, o_ref):
    x = x_ref[...]
    ms = jnp.mean(x * x, axis=-1, keepdims=True)
    o_ref[...] = (x * lax.rsqrt(ms + NORM_EPS) * g_ref[...]).astype(o_ref.dtype)


def _rmsnorm(x2, g, dtype):
    n, d = x2.shape
    tm = min(512, n)
    return pl.pallas_call(
        _rmsnorm_kernel,
        out_shape=jax.ShapeDtypeStruct((n, d), dtype),
        grid=(n // tm,),
        in_specs=[pl.BlockSpec((tm, d), lambda i: (i, 0)), pl.BlockSpec((1, d), lambda i: (0, 0))],
        out_specs=pl.BlockSpec((tm, d), lambda i: (i, 0)),
        compiler_params=_cp("parallel"),
    )(x2, g)


def _resmm_kernel(r_ref, *refs):
    y_refs, w_ref, o_ref = refs[:-2], refs[-2], refs[-1]
    acc = r_ref[...]
    k0 = 0
    for y_ref in y_refs:
        kw = y_ref.shape[1]
        acc = acc + _dot(y_ref[...].astype(BF16), w_ref[k0:k0 + kw, :])
        k0 += kw
    o_ref[...] = acc


def _res_matmul(res, ys, w, l):
    n = res.shape[0]
    k, d = w.shape[1:]
    assert sum(y.shape[1] for y in ys) == k
    tm = min(MM_ROWS, n)
    tn = 512
    return pl.pallas_call(
        _resmm_kernel,
        out_shape=jax.ShapeDtypeStruct((n, d), F32),
        grid=(n // tm, d // tn),
        in_specs=[pl.BlockSpec((tm, tn), lambda i, j: (i, j))]
                 + [pl.BlockSpec((tm, y.shape[1]), lambda i, j: (i, 0)) for y in ys]
                 + [pl.BlockSpec((None, k, tn), lambda i, j: (l, 0, j))],
        out_specs=pl.BlockSpec((tm, tn), lambda i, j: (i, j)),
        compiler_params=_cp("parallel", "arbitrary"),
    )(res, *ys, w)


def _rope(x, cos, sin):
    w = x.shape[1]
    first = (_iota(x.shape, 1) & (HD - 1)) < HD // 2
    sw = jnp.where(first, pltpu.roll(x, w - HD // 2, 1), pltpu.roll(x, HD // 2, 1))
    return x * cos + sw * sin


def _rope_kernel(aq_ref, iq_ref, ak_ref, av_ref, sm_ref, cos_ref, sin_ref, aqo, iqo, ko, vo, iko):
    cos = cos_ref[...]
    sin = sin_ref[...]
    aqo[...] = _rope(aq_ref[...], cos, sin)
    iqo[...] = _rope(iq_ref[...], cos, sin)
    ko[...] = _rope(ak_ref[...], cos[:, :128], sin[:, :128])
    vo[...] = av_ref[...]
    iko[...] = _rope(sm_ref[...], cos[:, :128], sin[:, :128])[:, :HD]


def _rope_call(proj, cos, sin, nb, nt_rows):
    n = proj.shape[0]
    tm = min(512, nt_rows)
    nt = nt_rows // tm
    row = lambda b, i: b * nt + i
    return pl.pallas_call(
        _rope_kernel,
        out_shape=(jax.ShapeDtypeStruct((n, GW), F32), jax.ShapeDtypeStruct((n, GW), F32),
                   jax.ShapeDtypeStruct((n, 128), F32), jax.ShapeDtypeStruct((n, 128), F32),
                   jax.ShapeDtypeStruct((n, HD), F32)),
        grid=(nb, nt),
        in_specs=[pl.BlockSpec((tm, GW), lambda b, i: (row(b, i), C_AQ // GW)),
                  pl.BlockSpec((tm, GW), lambda b, i: (row(b, i), C_IQ // GW)),
                  pl.BlockSpec((tm, 128), lambda b, i: (row(b, i), C_AK // 128)),
                  pl.BlockSpec((tm, 128), lambda b, i: (row(b, i), C_AV // 128)),
                  pl.BlockSpec((tm, 128), lambda b, i: (row(b, i), C_SM // 128)),
                  pl.BlockSpec((tm, GW), lambda b, i: (i, 0)),
                  pl.BlockSpec((tm, GW), lambda b, i: (i, 0))],
        out_specs=(pl.BlockSpec((tm, GW), lambda b, i: (row(b, i), 0)),
                   pl.BlockSpec((tm, GW), lambda b, i: (row(b, i), 0)),
                   pl.BlockSpec((tm, 128), lambda b, i: (row(b, i), 0)),
                   pl.BlockSpec((tm, 128), lambda b, i: (row(b, i), 0)),
                   pl.BlockSpec((tm, HD), lambda b, i: (row(b, i), 0))),
        compiler_params=_cp("parallel", "parallel"),
    )(proj, proj, proj, proj, proj, cos, sin)


def _rope_tables(pos):
    half = HD // 2
    inv = ROPE_THETA ** (-jnp.arange(half, dtype=F32) / half)
    ang = pos.astype(F32)[:, None] * inv[None, :]
    cos, sin = jnp.cos(ang), jnp.sin(ang)
    cos64 = jnp.concatenate([cos, cos], axis=-1)
    sin64 = jnp.concatenate([-sin, sin], axis=-1)
    return jnp.tile(cos64, (1, NH)), jnp.tile(sin64, (1, NH))


def _kth_largest(sc, extra, kk):
    kf = jnp.float32(kk)

    def count_ge(c):
        n = jnp.sum(jnp.where(sc >= c, 1.0, 0.0), axis=-1, keepdims=True)
        if extra is not None:
            n = n + jnp.where(extra >= c, 1.0, 0.0)
        return n

    def key_to_f(key):
        bits = key ^ ((key >> 31) & jnp.int32(0x7FFFFFFF))
        return lax.bitcast_convert_type(bits, F32)

    r = sc.shape[0]
    int_min = jnp.int32(-2 ** 31)
    lo = jnp.where(count_ge(jnp.zeros((r, 1), F32)) >= kf, jnp.int32(0), int_min)

    def body(j, lo):
        cand = lo + jnp.left_shift(jnp.int32(1), jnp.int32(30) - j)
        ok = count_ge(key_to_f(cand)) >= kf
        return jnp.where(ok, cand, lo)

    lo = lax.fori_loop(0, 31, body, lo)
    key_neg_inf = jnp.int32(-2 ** 31 + 0x7FFFFF)
    return jnp.where(lo <= key_neg_inf, -jnp.inf, key_to_f(lo))


def _strict_upper_bf16(n):
    return jnp.where(_iota((n, n), 0) < _iota((n, n), 1), 1.0, 0.0).astype(BF16)


def _split_bf16(x):
    hi = x.astype(BF16).astype(F32)
    return hi, x - hi


def _dsa_prompt_kernel(iq_ref, sm_ref, ik_ref, aq_ref, k_ref, v_ref, o_ref, sel_scr, kcat_scr, *, topk, qb, n_ext):
    t_keys = ik_ref.shape[0]
    i = pl.program_id(1)
    per_ext = (t_keys // qb) // n_ext

    @pl.when(i == 0)
    def _():
        hi, lo = _split_bf16(ik_ref[...])
        kcat_scr[...] = jnp.concatenate([hi, lo, hi], axis=1)

    iq = iq_ref[...]
    aq = aq_ref[...] * (HD ** -0.5 * math.log2(math.e))
    wts = sm_ref[:, SM_IW:SM_IW + NH] * IDX_SCALE
    tq = i * qb + _iota((qb, 1), 0)

    def body(ext):
        kcat = kcat_scr[0:ext, :]
        sc = jnp.zeros((qb, ext), F32)
        for h in range(NH):
            hi, lo = _split_bf16(iq[:, h * HD:(h + 1) * HD])
            qk = _dot_nt(jnp.concatenate([hi, hi, lo], axis=1), kcat)
            sc = sc + jnp.maximum(qk, 0.0) * wts[:, h:h + 1]
        causal = _iota((1, ext), 1) <= tq
        sc = jnp.where(causal, sc, -jnp.inf)

        thr = _kth_largest(sc, None, topk)
        ge = sc >= thr
        n_ge = jnp.sum(jnp.where(ge, 1.0, 0.0), axis=-1, keepdims=True)
        sel_scr[:, 0:ext] = jnp.where(jnp.logical_and(ge, causal), 0.0, -jnp.inf)
        tie = jnp.logical_and(n_ge > jnp.float32(topk), thr > -jnp.inf)

        @pl.when(jnp.max(jnp.where(tie, 1.0, 0.0)) > 0.5)
        def _():
            gt = sc > thr
            eq = sc == thr
            need = jnp.float32(topk) - jnp.sum(jnp.where(gt, 1.0, 0.0), axis=-1, keepdims=True)
            ut = _strict_upper_bf16(128)
            run = jnp.zeros((qb, 1), F32)
            for c in range(ext // 128):
                sl = slice(c * 128, (c + 1) * 128)
                eqc = jnp.where(eq[:, sl], 1.0, 0.0)
                pref = _dot(eqc.astype(BF16), ut) + run
                keep = jnp.logical_or(gt[:, sl], jnp.logical_and(eq[:, sl], pref < need))
                sel_scr[:, sl] = jnp.where(jnp.logical_and(keep, causal[:, sl]), 0.0, -jnp.inf)
                run = run + jnp.sum(eqc, axis=-1, keepdims=True)

        bias = sel_scr[:, 0:ext]
        for g in range(A_KV):
            kg = k_ref[0:ext, g * HD:(g + 1) * HD]
            vg = v_ref[0:ext, g * HD:(g + 1) * HD]
            for j in range(NH // A_KV):
                h = g * (NH // A_KV) + j
                s = _dot_nt(aq[:, h * HD:(h + 1) * HD], kg) + bias
                m = jnp.max(s, axis=-1, keepdims=True)
                p = jnp.exp2(s - m)
                l = jnp.sum(p, axis=-1, keepdims=True)
                o_ref[:, h * HD:(h + 1) * HD] = _dot(p, vg) / l

    for j in range(n_ext):
        pl.when(i // per_ext == j)(functools.partial(body, (j + 1) * (t_keys // n_ext)))


def _dsa_prompt(iq_r, proj, ik_r, aq_r, k_r, v_r, nb, t):
    qb = min(128, t)
    nq = t // qb
    n_ext = min(4, nq)
    assert nq % n_ext == 0
    topk = min(TOPK_MAX, t // 4)
    n = nb * t
    row = lambda b, i: b * nq + i
    return pl.pallas_call(
        functools.partial(_dsa_prompt_kernel, topk=topk, qb=qb, n_ext=n_ext),
        out_shape=jax.ShapeDtypeStruct((n, GW), F32),
        grid=(nb, nq),
        in_specs=[pl.BlockSpec((qb, GW), lambda b, i: (row(b, i), 0)),
                  pl.BlockSpec((qb, 128), lambda b, i: (row(b, i), C_SM // 128)),
                  pl.BlockSpec((t, HD), lambda b, i: (b, 0)),
                  pl.BlockSpec((qb, GW), lambda b, i: (row(b, i), 0)),
                  pl.BlockSpec((t, 128), lambda b, i: (b, 0)),
                  pl.BlockSpec((t, 128), lambda b, i: (b, 0))],
        out_specs=pl.BlockSpec((qb, GW), lambda b, i: (row(b, i), 0)),
        scratch_shapes=[pltpu.VMEM((qb, t), F32), pltpu.VMEM((t, 3 * HD), F32)],
        compiler_params=_cp("parallel", "arbitrary"),
    )(iq_r, proj, ik_r, aq_r, k_r, v_r)


def _dsa_sample_score_kernel(*refs, n_pages):
    _, iq_ref, w_ref, ikn_ref = refs[:4]
    pages = refs[4:4 + n_pages]
    sc_ref, sn_ref = refs[4 + n_pages:]
    iq = iq_ref[0]
    w = w_ref[0] * IDX_SCALE
    k_hi, k_lo = _split_bf16(jnp.concatenate([pages[c][0] for c in range(n_pages)], axis=1))
    q_hi, q_lo = _split_bf16(iq)
    qk = _dot(jnp.concatenate([q_hi, q_hi, q_lo], axis=1), jnp.concatenate([k_hi, k_lo, k_hi], axis=0))
    sc_ref[0] = jnp.sum(jnp.maximum(qk, 0.0) * w, axis=0, keepdims=True)
    qkn = jnp.sum(iq * ikn_ref[0], axis=-1, keepdims=True)
    sn = jnp.sum(jnp.maximum(qkn, 0.0) * w, axis=0, keepdims=True)
    sn_ref[0] = jnp.broadcast_to(sn, (1, 128))


def _dsa_sample_select_kernel(sc_ref, sn_ref, sel_ref, seln_ref, *, topk):
    sc = sc_ref[...]
    sn = sn_ref[:, 0:1]
    bd, s_keys = sc.shape
    thr = _kth_largest(sc, sn, topk)
    gt = sc > thr
    eq = sc == thr
    n_gt = jnp.sum(jnp.where(gt, 1.0, 0.0), axis=-1, keepdims=True) + jnp.where(sn > thr, 1.0, 0.0)
    need = jnp.float32(topk) - n_gt
    ut = _strict_upper_bf16(PAGE)
    run = jnp.zeros((bd, 1), F32)
    for c in range(s_keys // PAGE):
        sl = slice(c * PAGE, (c + 1) * PAGE)
        eqc = jnp.where(eq[:, sl], 1.0, 0.0)
        pref = _dot(eqc.astype(BF16), ut) + run
        keep = jnp.logical_or(gt[:, sl], jnp.logical_and(eq[:, sl], pref < need))
        sel_ref[:, sl] = jnp.where(keep, 1.0, 0.0)
        run = run + jnp.sum(eqc, axis=-1, keepdims=True)
    sel_new = jnp.logical_or(sn > thr, jnp.logical_and(sn == thr, run < need))
    seln_ref[...] = jnp.broadcast_to(jnp.where(sel_new, 1.0, 0.0), seln_ref.shape)


def _dsa_sample_attn_kernel(*refs, n_pages):
    _, aq_ref, kn_ref, vn_ref, sel_ref, seln_ref = refs[:6]
    kpages = refs[6:6 + n_pages]
    vpages = refs[6 + n_pages:6 + 2 * n_pages]
    o_ref, k_scr, v_scr = refs[6 + 2 * n_pages:]
    for c in range(n_pages):
        k_scr[:, c * PAGE:(c + 1) * PAGE] = kpages[c][0]
        v_scr[:, c * PAGE:(c + 1) * PAGE] = vpages[c][0]
    sel = sel_ref[0] > 0.5
    sel_new = seln_ref[0][:, 0:1] > 0.5
    aq = aq_ref[0]
    kn = kn_ref[0]
    vn = vn_ref[0]
    hpg = NH // A_KV
    groups = range(A_KV)
    qg = [aq[g * hpg:(g + 1) * hpg, :] for g in groups]
    s = [_dot(qg[g], k_scr[g * HD:(g + 1) * HD, :]) * HD ** -0.5 for g in groups]
    pr, pn, l = [], [], []
    for g in groups:
        sg = jnp.where(sel, s[g], -jnp.inf)
        s_new = jnp.sum(qg[g] * kn[:, g * HD:(g + 1) * HD], axis=-1, keepdims=True) * HD ** -0.5
        s_new = jnp.where(sel_new, s_new, -jnp.inf)
        m = jnp.maximum(jnp.max(sg, axis=-1, keepdims=True), s_new)
        pr.append(jnp.exp(sg - m))
        pn.append(jnp.exp(s_new - m))
        l.append(jnp.sum(pr[g], axis=-1, keepdims=True) + pn[g])
    pv = [_dot_nt(pr[g], v_scr[g * HD:(g + 1) * HD, :]) for g in groups]
    for g in groups:
        o_ref[0, g * hpg:(g + 1) * hpg, :] = (pv[g] + pn[g] * vn[:, g * HD:(g + 1) * HD]) / l[g]


def _dsa_sample(layer, page_table, iq_r, iw, ik_r, aq_r, k_r, v_r, cki, ck, cv, n_pool):
    bd, n_pages = page_table.shape
    past = n_pages * PAGE
    topk = min(TOPK_MAX, (past + 1) // 4)
    base = layer * n_pool
    per_b = lambda b, pt: (b, 0, 0)

    def page_specs(width):
        return [pl.BlockSpec((1, width, PAGE), lambda b, pt, c=c: (base + pt[b, c], 0, 0)) for c in range(n_pages)]

    sc, sn = pl.pallas_call(
        functools.partial(_dsa_sample_score_kernel, n_pages=n_pages),
        out_shape=(jax.ShapeDtypeStruct((bd, 1, past), F32), jax.ShapeDtypeStruct((bd, 1, 128), F32)),
        grid_spec=pltpu.PrefetchScalarGridSpec(
            num_scalar_prefetch=1,
            grid=(bd,),
            in_specs=[pl.BlockSpec((1, NH, HD), per_b), pl.BlockSpec((1, NH, 1), per_b),
                      pl.BlockSpec((1, 1, HD), per_b)] + page_specs(HD),
            out_specs=(pl.BlockSpec((1, 1, past), per_b), pl.BlockSpec((1, 1, 128), per_b))),
        compiler_params=_cp("arbitrary"),
    )(page_table, iq_r.reshape(bd, NH, HD), iw.reshape(bd, NH, 1), ik_r.reshape(bd, 1, HD), *([cki] * n_pages))

    sel, seln = pl.pallas_call(
        functools.partial(_dsa_sample_select_kernel, topk=topk),
        out_shape=(jax.ShapeDtypeStruct((bd, past), F32), jax.ShapeDtypeStruct((bd, 128), F32)),
    )(sc.reshape(bd, past), sn.reshape(bd, 128))

    out = pl.pallas_call(
        functools.partial(_dsa_sample_attn_kernel, n_pages=n_pages),
        out_shape=jax.ShapeDtypeStruct((bd, NH, HD), F32),
        grid_spec=pltpu.PrefetchScalarGridSpec(
            num_scalar_prefetch=1,
            grid=(bd,),
            in_specs=[pl.BlockSpec((1, NH, HD), per_b), pl.BlockSpec((1, 1, 128), per_b),
                      pl.BlockSpec((1, 1, 128), per_b), pl.BlockSpec((1, 1, past), per_b),
                      pl.BlockSpec((1, 1, 128), per_b)] + page_specs(128) + page_specs(128),
            out_specs=pl.BlockSpec((1, NH, HD), per_b),
            scratch_shapes=[pltpu.VMEM((128, past), F32), pltpu.VMEM((128, past), F32)]),
        compiler_params=_cp("arbitrary"),
    )(page_table, aq_r.reshape(bd, NH, HD), k_r.reshape(bd, 1, 128), v_r.reshape(bd, 1, 128),
      sel.reshape(bd, 1, past), seln.reshape(bd, 1, 128), *([ck] * n_pages), *([cv] * n_pages))
    return out.reshape(bd, GW)


def _log_sigmoid(x):
    return jnp.minimum(x, 0.0) - jnp.log(1.0 + jnp.exp(-jnp.abs(x)))


MLSTM_ROWS = 2 * CHUNK


def _mlstm_prompt_kernel(m_ref, sm_ref, gb_ref, nw_ref, y_ref, c_ref, n_ref, mm_ref):
    L = CHUNK

    @pl.when(pl.program_id(1) == 0)
    def _():
        c_ref[...] = jnp.zeros(c_ref.shape, F32)
        n_ref[...] = jnp.zeros(n_ref.shape, F32)
        mm_ref[...] = jnp.zeros(mm_ref.shape, F32)

    ri, ci = _iota((L, L), 0), _iota((L, L), 1)
    tril = jnp.where(ci <= ri, 1.0, 0.0)
    triu = jnp.where(ci >= ri, 1.0, 0.0)
    same, causal, _ = _bd_masks(L)
    tile = lambda z: jnp.concatenate([z] * HG, axis=0)
    msk = lambda z: jnp.where(same, z, 0.0)
    stack_cols = lambda z, g: jnp.concatenate([z[:, g * HG + j:g * HG + j + 1] for j in range(HG)], axis=0)
    stack_rows = lambda z, g: jnp.concatenate([z[g * HG + j:g * HG + j + 1, :] for j in range(HG)], axis=1)
    percol = lambda zs: jnp.concatenate([jnp.broadcast_to(z, (L, 1)) for z in zs], axis=0)
    perrow = lambda zs: jnp.concatenate([jnp.broadcast_to(z, (1, HD)) for z in zs], axis=1)

    n_seq, n_grp = m_ref.shape[0], NH // HG
    chains = [(bb, g) for bb in range(n_seq) for g in range(n_grp)]
    each = lambda f: {c: f(c) for c in chains}
    lanes = lambda c: slice(c[1] * BD, (c[1] + 1) * BD)
    heads = lambda c: range(c[1] * HG, (c[1] + 1) * HG)
    gates = {}
    for bb in range(n_seq):
        smb = sm_ref[bb] + gb_ref[...]
        smt = smb.T
        gates[bb] = dict(ig=smb[:, SM_MI:SM_MI + NH], lf=_log_sigmoid(smb[:, SM_MF:SM_MF + NH]),
                         igt=smt[SM_MI:SM_MI + NH, :], lft=_log_sigmoid(smt[SM_MF:SM_MF + NH, :]))
    cbd = each(lambda c: c_ref[c[0], c[1]])
    nrow = each(lambda c: n_ref[c[0], :, lanes(c)])
    mprev = each(lambda c: [mm_ref[c[0], :, h:h + 1] for h in heads(c)])
    for cc in range(MLSTM_ROWS // L):
        rows = slice(cc * L, (cc + 1) * L)
        ig, bcs, bcst, igt = {}, {}, {}, {}
        for bb in range(n_seq):
            ig[bb] = gates[bb]['ig'][rows]
            bcs[bb] = _dot(tril, gates[bb]['lf'][rows], HI)
            bcst[bb] = _dot(gates[bb]['lft'][:, rows], triu, HI)
            igt[bb] = gates[bb]['igt'][:, rows]
        col = lambda c, w: m_ref[c[0], rows, w * GW + c[1] * BD:w * GW + (c[1] + 1) * BD]
        qexp = each(lambda c: msk(tile(col(c, 0))))
        kt = each(lambda c: tile(col(c, 1) * HD ** -0.5))
        kexp = each(lambda c: msk(kt[c]))
        vexp = each(lambda c: msk(tile(col(c, 2))))
        bcol = each(lambda c: stack_cols(bcs[c[0]], c[1]))
        icol = each(lambda c: stack_cols(ig[c[0]], c[1]))
        bl = each(lambda c: [bcs[c[0]][L - 1:L, h:h + 1] for h in heads(c)])
        dmat = each(lambda c: jnp.where(
            causal, bcol[c] - stack_rows(bcst[c[0]], c[1]) + stack_rows(igt[c[0]], c[1]), -jnp.inf))
        inter = each(lambda c: bcol[c] + percol(mprev[c]))
        mj = each(lambda c: jnp.maximum(inter[c], jnp.max(dmat[c], axis=-1, keepdims=True)))
        qk = each(lambda c: _dot_nt(qexp[c], kt[c]))
        s = each(lambda c: qk[c] * jnp.exp(dmat[c] - mj[c]))
        iw = each(lambda c: jnp.exp(inter[c] - mj[c]))
        sv = each(lambda c: _dot(s[c], vexp[c]))
        qc = each(lambda c: _dot_nt(qexp[c], cbd[c]))
        wl = each(lambda c: percol(bl[c]) - bcol[c] + icol[c])
        m_new = each(lambda c: [jnp.maximum(bl[c][j] + mprev[c][j],
                                            jnp.max(wl[c][j * L:(j + 1) * L], axis=0, keepdims=True))
                                for j in range(HG)])
        dec = each(lambda c: [jnp.exp(bl[c][j] + mprev[c][j] - m_new[c][j]) for j in range(HG)])
        ws = each(lambda c: jnp.exp(wl[c] - percol(m_new[c])))
        upd = each(lambda c: _dot_tn(vexp[c] * ws[c], kexp[c]))
        for c in chains:
            bb, ln = c[0], lanes(c)
            den = (jnp.sum(s[c], axis=-1, keepdims=True)
                   + iw[c] * jnp.sum(qexp[c] * nrow[c], axis=-1, keepdims=True))
            hc = (sv[c] + iw[c] * qc[c]) / jnp.maximum(jnp.abs(den), jnp.exp(-mj[c]))
            hn = hc * lax.rsqrt(jnp.sum(hc * hc, axis=-1, keepdims=True) * (1.0 / HD) + NORM_EPS)
            hn = sum(hn[j * L:(j + 1) * L] for j in range(HG))
            y_ref[bb, rows, ln] = hn * nw_ref[:, ln] * _sigmoid(col(c, 3))
        cbd = each(lambda c: percol(dec[c]) * cbd[c] + upd[c])
        nrow = each(lambda c: perrow(dec[c]) * nrow[c] + jnp.sum(kexp[c] * ws[c], axis=0, keepdims=True))
        mprev = m_new
    for c in chains:
        c_ref[c[0], c[1]] = cbd[c]
        n_ref[c[0], :, lanes(c)] = nrow[c]
        for j, h in enumerate(heads(c)):
            mm_ref[c[0], :, h:h + 1] = mprev[c][j]


def _gate_bias_row(b_i, b_f):
    z = lambda n: jnp.zeros((1, n), F32)
    return jnp.concatenate([z(SM_MI), b_i, b_f, z(128 - SM_MF - NH)], axis=1)


def _mlstm_prompt(proj, b_i, b_f, nw, nb, t):
    tm = MLSTM_ROWS
    nc = t // tm
    ng = NH // HG
    sb = RWKV_SEQS if nb % RWKV_SEQS == 0 else 1
    proj3 = proj.reshape(nb, t, NP)
    y, cbd, nrow, m = pl.pallas_call(
        _mlstm_prompt_kernel,
        out_shape=(jax.ShapeDtypeStruct((nb, t, GW), F32), jax.ShapeDtypeStruct((nb, ng, BD, BD), F32),
                   jax.ShapeDtypeStruct((nb, 1, GW), F32), jax.ShapeDtypeStruct((nb, 1, NH), F32)),
        grid=(nb // sb, nc),
        in_specs=[pl.BlockSpec((sb, tm, 4 * GW), lambda b, c: (b, c, 0)),
                  pl.BlockSpec((sb, tm, 128), lambda b, c: (b, c, C_SM // 128)),
                  pl.BlockSpec((1, 128), lambda b, c: (0, 0)),
                  pl.BlockSpec((1, GW), lambda b, c: (0, 0))],
        out_specs=(pl.BlockSpec((sb, tm, GW), lambda b, c: (b, c, 0)),
                   pl.BlockSpec((sb, ng, BD, BD), lambda b, c: (b, 0, 0, 0)),
                   pl.BlockSpec((sb, 1, GW), lambda b, c: (b, 0, 0)),
                   pl.BlockSpec((sb, 1, NH), lambda b, c: (b, 0, 0))),
        compiler_params=_cp("parallel", "arbitrary"),
    )(proj3, proj3, _gate_bias_row(b_i, b_f), nw)
    return y.reshape(nb * t, GW), _bd_diag(cbd), nrow.reshape(nb, NH, HD), m


def _head_expanders():
    diag = (_iota((NH * HD, HD), 0) % HD) == _iota((NH * HD, HD), 1)

    def rep(z):
        return jnp.concatenate([jnp.broadcast_to(z[h:h + 1, :], (HD, z.shape[1])) for h in range(NH)], axis=0)

    def fold(col):
        m = jnp.where(diag, col, 0.0)
        return jnp.concatenate([jnp.sum(m[h * HD:(h + 1) * HD], axis=0, keepdims=True) for h in range(NH)], axis=0)

    return rep, fold, diag


def _mlstm_step_kernel(x_ref, gt_ref, gb_ref, nw_ref, c_ref, n_ref, mm_ref, y_ref, co_ref, no_ref, mo_ref):
    x = x_ref[0]
    q, k, v, og = x[0:NH], x[NH:2 * NH] * HD ** -0.5, x[2 * NH:3 * NH], x[3 * NH:4 * NH]
    gates = gt_ref[0] + gb_ref[...]
    ig = gates[:, 0:1]
    rep, fold, diag = _head_expanders()
    c0 = c_ref[0].reshape(NH * HD, HD)
    n0 = n_ref[0]
    inter = _log_sigmoid(gates[:, 1:2]) + mm_ref[0]
    mj = jnp.maximum(inter, ig)
    s = jnp.sum(q * k, axis=-1, keepdims=True) * jnp.exp(ig - mj)
    iw = jnp.exp(inter - mj)
    ws = jnp.exp(ig - mj)
    den = s + iw * jnp.sum(n0 * q, axis=-1, keepdims=True)
    inv = 1.0 / jnp.maximum(jnp.abs(den), jnp.exp(-mj))
    lane = _iota((NH, 128), 1)
    per_head = jnp.where(lane == 0, s, jnp.where(lane == 1, iw, jnp.where(lane == 2, ws, inv)))
    ph = rep(per_head)
    s_c, iw_c, ws_c, inv_c = ph[:, 0:1], ph[:, 1:2], ph[:, 2:3], ph[:, 3:4]
    cq = jnp.sum(c0 * rep(q), axis=1, keepdims=True)
    vcol = jnp.sum(jnp.where(diag, rep(v), 0.0), axis=1, keepdims=True)
    hcol = (s_c * vcol + iw_c * cq) * inv_c
    co_ref[0] = (iw_c * c0 + (ws_c * vcol) * rep(k)).reshape(NH, HD, HD)
    no_ref[0] = iw * n0 + ws * k
    mo_ref[0] = mj
    h8 = fold(hcol)
    hn = h8 * lax.rsqrt(jnp.mean(h8 * h8, axis=-1, keepdims=True) + NORM_EPS)
    y_ref[0] = hn * nw_ref[...] * _sigmoid(og)


def _mlstm_step(proj, b_i, b_f, nw, c_all, l, n0, m0):
    bd = proj.shape[0]
    per_b3 = lambda b: (b, 0, 0)
    full = lambda b: (0, 0)
    x = proj[:, :4 * GW].reshape(bd, 4 * NH, HD)
    gates = proj[:, C_SM + SM_MI:C_SM + SM_MI + 2 * NH].reshape(bd, 2, NH).transpose(0, 2, 1)
    gate_bias = jnp.concatenate([b_i, b_f], axis=0).T
    y, c1, n1, m1 = pl.pallas_call(
        _mlstm_step_kernel,
        out_shape=(jax.ShapeDtypeStruct((bd, NH, HD), F32), jax.ShapeDtypeStruct((bd, NH, HD, HD), F32),
                   jax.ShapeDtypeStruct((bd, NH, HD), F32), jax.ShapeDtypeStruct((bd, NH, 1), F32)),
        grid=(bd,),
        in_specs=[pl.BlockSpec((1, 4 * NH, HD), per_b3),
                  pl.BlockSpec((1, NH, 2), per_b3),
                  pl.BlockSpec((NH, 2), full),
                  pl.BlockSpec((NH, HD), full),
                  pl.BlockSpec((1, NH, HD, HD), lambda b: (l * bd + b, 0, 0, 0)),
                  pl.BlockSpec((1, NH, HD), per_b3),
                  pl.BlockSpec((1, NH, 1), per_b3)],
        out_specs=(pl.BlockSpec((1, NH, HD), per_b3),
                   pl.BlockSpec((1, NH, HD, HD), lambda b: (b, 0, 0, 0)),
                   pl.BlockSpec((1, NH, HD), per_b3),
                   pl.BlockSpec((1, NH, 1), per_b3)),
        compiler_params=_cp("parallel"),
    )(x, gates, gate_bias, nw.reshape(NH, HD), c_all.reshape((-1,) + c_all.shape[2:]), n0, m0.reshape(bd, NH, 1))
    return y, c1, n1, m1


def _rwkv_prep_kernel(x_ref, prev_ref, mu_ref, w0_ref, ww2_ref, a0_ref, wa2_ref, wg2_ref, kk_ref, ka_ref,
                      r_o, lw_o, k_o, v_o, kk_o, a_o, g_o, carry_scr, *, seq):
    x = x_ref[...]
    tm = x.shape[0]
    if seq:
        first = jnp.where(pl.program_id(1) == 0, prev_ref[0], carry_scr[...])
        xprev = jnp.where(_iota((tm, 1), 0) == 0, first, pltpu.roll(x, 1, 0))
        carry_scr[...] = x[tm - 1:tm, :]
    else:
        xprev = prev_ref[...]
    xm = x + (xprev - x) * mu_ref[...]
    r = xm[:, 0:GW]
    kx = xm[:, GW:2 * GW]
    v = xm[:, 2 * GW:3 * GW]
    xw = xm[:, 3 * GW:3 * GW + 64]
    xa = xm[:, 3 * GW + 64:3 * GW + 128]
    xg = xm[:, 3 * GW + 128:R_IN]
    w = -_softplus(-(w0_ref[...] + _dot(jnp.tanh(xw), ww2_ref[...]))) - 0.5
    a = _sigmoid(a0_ref[...] + _dot(xa, wa2_ref[...]))
    r_o[...] = r
    lw_o[...] = -jnp.exp(w)
    v_o[...] = v
    a_o[...] = a
    g_o[...] = _dot(_sigmoid(xg), wg2_ref[...])
    k_o[...] = kx * (1.0 + (a - 1.0) * ka_ref[...])
    kk = kx * kk_ref[...]
    for h in range(NH):
        kh = kk[:, h * HD:(h + 1) * HD]
        nrm = jnp.sqrt(jnp.sum(kh * kh, axis=-1, keepdims=True))
        kk_o[:, h * HD:(h + 1) * HD] = kh / jnp.maximum(nrm, 1e-12)


def _rwkv_prep(proj, prev, mu, w0, ww2, a0, wa2, wg2, k_k, k_a, nb, t):
    n = nb * t
    seq = t > 1
    full = lambda *_: (0, 0)
    if seq:
        tm = min(256, t)
        nt = t // tm
        grid = (nb, nt)
        xmap = lambda b, i: (b * nt + i, C_RIN // R_IN)
        pspec = pl.BlockSpec((1, 1, R_IN), lambda b, i: (b, 0, 0))
        omap = lambda b, i: (b * nt + i, 0)
        sem = ("parallel", "arbitrary")
    else:
        tm = n
        grid = (1,)
        xmap = lambda i: (0, C_RIN // R_IN)
        pspec = pl.BlockSpec((tm, R_IN), lambda i: (0, 0))
        omap = lambda i: (0, 0)
        sem = ("arbitrary",)
    wspecs = [pl.BlockSpec((1, R_IN), full), pl.BlockSpec((1, GW), full), pl.BlockSpec((64, GW), full),
              pl.BlockSpec((1, GW), full), pl.BlockSpec((64, GW), full), pl.BlockSpec((128, GW), full),
              pl.BlockSpec((1, GW), full), pl.BlockSpec((1, GW), full)]
    return pl.pallas_call(
        functools.partial(_rwkv_prep_kernel, seq=seq),
        out_shape=tuple(jax.ShapeDtypeStruct((n, GW), F32) for _ in range(7)),
        grid=grid,
        in_specs=[pl.BlockSpec((tm, R_IN), xmap), pspec] + wspecs,
        out_specs=tuple(pl.BlockSpec((tm, GW), omap) for _ in range(7)),
        scratch_shapes=[pltpu.VMEM((1, R_IN), F32)],
        compiler_params=_cp(*sem),
    )(proj, prev, mu, w0, ww2, a0, wa2, wg2, k_k, k_a)


def _rwkv_scan_kernel(r_ref, lw_ref, k_ref, v_ref, kk_ref, a_ref, g_ref, rk_ref, lnw_ref, lnb_ref, y_ref, s_ref):
    C = CHUNK

    @pl.when(pl.program_id(1) == 0)
    def _():
        s_ref[...] = jnp.zeros(s_ref.shape, F32)

    ri, ci = _iota((C, C), 0), _iota((C, C), 1)
    same, incl, strict = _bd_masks(C)
    row, col = _iota((BD, BD), 0), _iota((BD, BD), 1)
    eye = jnp.where(row == col, 1.0, 0.0)
    strict_incl = jnp.concatenate([strict, incl], axis=0)
    tile = lambda z: jnp.concatenate([z] * HG, axis=0)
    msk = lambda z: jnp.where(same, z, 0.0)
    fold = lambda z: sum(z[j * C:(j + 1) * C] for j in range(HG))
    n_seq, n_grp = r_ref.shape[0], NH // HG
    chains = [(bb, g) for bb in range(n_seq) for g in range(n_grp)]
    each = lambda f: {c: f(c) for c in chains}
    lanes = lambda c: slice(c[1] * BD, (c[1] + 1) * BD)
    seq = {}
    for bb in range(n_seq):
        lw = lw_ref[bb]
        cs = _dot(jnp.where(ci <= ri, 1.0, 0.0), lw, HI)
        gam = jnp.exp(cs)
        ginv = jnp.exp(-cs)
        r, k, v, kk = r_ref[bb], k_ref[bb], v_ref[bb], kk_ref[bb]
        seq[bb] = dict(at=-kk * jnp.exp(cs - lw), bt=kk * a_ref[bb] * ginv, kt=k * ginv, rt=r * gam, v=v,
                       glast=gam[C - 1:C, :], bonus_in=r * k * rk_ref[...])
    part = lambda c, name: seq[c[0]][name][:, lanes(c)]
    btl = each(lambda c: tile(part(c, 'bt')))
    ktl = each(lambda c: tile(part(c, 'kt')))
    vexp = each(lambda c: msk(tile(part(c, 'v'))))
    ar = each(lambda c: jnp.concatenate([msk(tile(part(c, 'at'))), msk(tile(part(c, 'rt')))], axis=0))
    sbd = each(lambda c: s_ref[c[0], c[1]])
    gb = each(lambda c: _dot_nt(ar[c], btl[c]))
    gk = each(lambda c: _dot_nt(ar[c], ktl[c]))
    gs = each(lambda c: _dot_nt(ar[c], sbd[c]))
    n_ab = each(lambda c: jnp.where(strict, gb[c][:BD], 0.0))
    x = each(lambda c: eye + n_ab[c])
    pm = each(lambda c: _dot(n_ab[c], n_ab[c]))
    for j in range(5):
        xd = each(lambda c: _dot(x[c], pm[c]))
        if j < 4:
            pm = each(lambda c: _dot(pm[c], pm[c]))
        x = each(lambda c: x[c] + xd[c])
    akv = each(lambda c: _dot(jnp.where(strict_incl, gk[c], 0.0), vexp[c]))
    u = each(lambda c: _dot(x[c], gs[c][:BD] + akv[c][:BD]))
    o = each(lambda c: gs[c][BD:] + _dot(jnp.where(incl, gb[c][BD:], 0.0), u[c]) + akv[c][BD:])
    gl = each(lambda c: part(c, 'glast'))
    s_new = each(lambda c: sbd[c] * gl[c] + _dot_tn(
        jnp.concatenate([u[c], vexp[c]], axis=0),
        jnp.concatenate([msk(btl[c] * gl[c]), msk(ktl[c] * gl[c])], axis=0)))
    for c in chains:
        bb, ln = c[0], lanes(c)
        s_ref[c[0], c[1]] = s_new[c]
        mean = jnp.sum(o[c], axis=-1, keepdims=True) * (1.0 / HD)
        dev = msk(o[c] - mean)
        var = jnp.sum(dev * dev, axis=-1, keepdims=True) * (1.0 / HD)
        on = fold(dev * lax.rsqrt(var + R_LN_EPS)) * lnw_ref[:, ln] + lnb_ref[:, ln]
        bonus = fold(jnp.sum(msk(tile(part(c, 'bonus_in'))), axis=-1, keepdims=True) * vexp[c])
        y_ref[bb, :, ln] = (on + bonus) * g_ref[bb, :, ln]


def _rwkv_scan(rs, rk, lnw, lnb, nb, t):
    nc = t // CHUNK
    ng = NH // HG
    sb = RWKV_SEQS if nb % RWKV_SEQS == 0 else 1
    rowmap = lambda b, c: (b, c, 0)
    full = lambda b, c: (0, 0)
    y, sbd = pl.pallas_call(
        _rwkv_scan_kernel,
        out_shape=(jax.ShapeDtypeStruct((nb, t, GW), F32), jax.ShapeDtypeStruct((nb, ng, BD, BD), F32)),
        grid=(nb // sb, nc),
        in_specs=[pl.BlockSpec((sb, CHUNK, GW), rowmap)] * 7 + [pl.BlockSpec((1, GW), full)] * 3,
        out_specs=(pl.BlockSpec((sb, CHUNK, GW), rowmap), pl.BlockSpec((sb, ng, BD, BD), lambda b, c: (b, 0, 0, 0))),
        compiler_params=_cp("parallel", "arbitrary"),
    )(*(z.reshape(nb, t, GW) for z in rs), rk, lnw, lnb)
    return y.reshape(nb * t, GW), _bd_diag(sbd)


def _rwkv_step_kernel(r_ref, lw_ref, k_ref, v_ref, kk_ref, a_ref, g_ref, rk_ref, lnw_ref, lnb_ref, s_ref,
                      y_ref, so_ref):
    r, lw, k, v, kk, a, g = (ref[0] for ref in (r_ref, lw_ref, k_ref, v_ref, kk_ref, a_ref, g_ref))
    rep, fold, diag = _head_expanders()
    s0 = s_ref[0].reshape(NH * HD, HD)
    kk_rep = rep(kk)
    sk = jnp.sum(s0 * kk_rep, axis=1, keepdims=True)
    vcol = jnp.sum(jnp.where(diag, rep(v), 0.0), axis=1, keepdims=True)
    s1 = s0 * rep(jnp.exp(lw)) - sk * rep(kk * a) + vcol * rep(k)
    so_ref[0] = s1.reshape(NH, HD, HD)
    ocol = jnp.sum(s1 * rep(r), axis=1, keepdims=True)
    o = fold(ocol)
    mean = jnp.mean(o, axis=-1, keepdims=True)
    var = jnp.mean(jnp.square(o - mean), axis=-1, keepdims=True)
    on = (o - mean) * lax.rsqrt(var + R_LN_EPS) * lnw_ref[...] + lnb_ref[...]
    bonus = jnp.sum(r * k * rk_ref[...], axis=-1, keepdims=True) * v
    y_ref[0] = (on + bonus) * g


def _rwkv_step(rs, rk, lnw, lnb, s_all, l):
    bd = s_all.shape[1]
    per_b = lambda b: (b, 0, 0)
    full = lambda b: (0, 0)
    return pl.pallas_call(
        _rwkv_step_kernel,
        out_shape=(jax.ShapeDtypeStruct((bd, NH, HD), F32), jax.ShapeDtypeStruct((bd, NH, HD, HD), F32)),
        grid=(bd,),
        in_specs=[pl.BlockSpec((1, NH, HD), per_b)] * 7 + [pl.BlockSpec((NH, HD), full)] * 3
                 + [pl.BlockSpec((1, NH, HD, HD), lambda b: (l * bd + b, 0, 0, 0))],
        out_specs=(pl.BlockSpec((1, NH, HD), per_b), pl.BlockSpec((1, NH, HD, HD), lambda b: (b, 0, 0, 0))),
        compiler_params=_cp("parallel"),
    )(*(z.reshape(bd, NH, HD) for z in rs), *(z.reshape(NH, HD) for z in (rk, lnw, lnb)),
      s_all.reshape((-1,) + s_all.shape[2:]))


def _gelu_tanh(x):
    return 0.5 * x * (1.0 + jnp.tanh(math.sqrt(2.0 / math.pi) * (x + 0.044715 * (x * x * x))))


def _s5_kernel(u_ref, bre_ref, bim_ref, lre_ref, lim_ref, cre_ref, cim_ref, d_ref, wg_ref, bg_ref, h0r_ref, h0i_ref,
               y_ref, hr_ref, hi_ref, hre_scr, him_scr, *, nb, tb):
    @pl.when(pl.program_id(0) == 0)
    def _():
        hr_ref[...] = h0r_ref[...]
        hi_ref[...] = h0i_ref[...]

    u = u_ref[...]
    cw, sw = GW // S5_SPLIT, S5_W // S5_SPLIT
    for j in range(S5_SPLIT):
        cs_, ss_ = slice(j * cw, (j + 1) * cw), slice(j * sw, (j + 1) * sw)
        hre_scr[:, ss_] = _dot(u[:, cs_], bre_ref[cs_, ss_])
        him_scr[:, ss_] = _dot(u[:, cs_], bim_ref[cs_, ss_])
    lr = lre_ref[...]
    li = lim_ref[...]

    def body(t, carry):
        hr, hi = carry
        rows = pl.ds(pl.multiple_of(t * nb, nb), nb)
        nr = lr * hr - li * hi + hre_scr[rows, :]
        ni = lr * hi + li * hr + him_scr[rows, :]
        hre_scr[rows, :] = nr
        him_scr[rows, :] = ni
        return nr, ni

    hr, hi = lax.fori_loop(0, tb, body, (hr_ref[...], hi_ref[...]))
    hr_ref[...] = hr
    hi_ref[...] = hi
    y = jnp.concatenate(
        [_dot(hre_scr[:, j * sw:(j + 1) * sw], cre_ref[j * sw:(j + 1) * sw, j * cw:(j + 1) * cw])
         - _dot(him_scr[:, j * sw:(j + 1) * sw], cim_ref[j * sw:(j + 1) * sw, j * cw:(j + 1) * cw])
         for j in range(S5_SPLIT)], axis=1) + d_ref[...] * u
    y = _gelu_tanh(y)
    y_ref[...] = y * _sigmoid(_dot(y, wg_ref[...]) + bg_ref[...])


def _s5(u_tm, mats, h0r, h0i, nb, t):
    bre, bim, lre, lim, cre, cim, d, wg, bg = mats
    tb = min(64, t)
    full = lambda i: (0, 0)
    return pl.pallas_call(
        functools.partial(_s5_kernel, nb=nb, tb=tb),
        out_shape=(jax.ShapeDtypeStruct((t * nb, GW), F32), jax.ShapeDtypeStruct((nb, S5_W), F32),
                   jax.ShapeDtypeStruct((nb, S5_W), F32)),
        grid=(t // tb,),
        in_specs=[pl.BlockSpec((tb * nb, GW), lambda i: (i, 0)),
                  pl.BlockSpec((GW, S5_W), full), pl.BlockSpec((GW, S5_W), full),
                  pl.BlockSpec((1, S5_W), full), pl.BlockSpec((1, S5_W), full),
                  pl.BlockSpec((S5_W, GW), full), pl.BlockSpec((S5_W, GW), full),
                  pl.BlockSpec((1, GW), full), pl.BlockSpec((GW, GW), full), pl.BlockSpec((1, GW), full),
                  pl.BlockSpec((nb, S5_W), full), pl.BlockSpec((nb, S5_W), full)],
        out_specs=(pl.BlockSpec((tb * nb, GW), lambda i: (i, 0)),
                   pl.BlockSpec((nb, S5_W), full), pl.BlockSpec((nb, S5_W), full)),
        scratch_shapes=[pltpu.VMEM((tb * nb, S5_W), F32), pltpu.VMEM((tb * nb, S5_W), F32)],
        compiler_params=_cp("arbitrary"),
    )(u_tm, bre, bim, lre, lim, cre, cim, d, wg, bg, h0r, h0i)


def _s5_mats(a_re, a_im, b_re, b_im, c_re, c_im, d_skip, log_dt, w_glu, b_glu):
    dt = jnp.exp(log_dt)
    mag = jnp.exp(a_re * dt)
    lb_re, lb_im = mag * jnp.cos(a_im * dt), mag * jnp.sin(a_im * dt)
    den = a_re * a_re + a_im * a_im
    f_re = ((lb_re - 1.0) * a_re + lb_im * a_im) / den
    f_im = (lb_im * a_re - (lb_re - 1.0) * a_im) / den
    bb_re = f_re[..., None] * b_re - f_im[..., None] * b_im
    bb_im = f_re[..., None] * b_im + f_im[..., None] * b_re
    eye = jnp.eye(S5_G, dtype=F32)
    bd = lambda bb: jnp.einsum('gpc,gh->gchp', bb, eye).reshape(GW, S5_W)
    cd = lambda cc: jnp.einsum('gcp,gh->gphc', cc, eye).reshape(S5_W, GW)
    return (bd(bb_re), bd(bb_im), lb_re.reshape(1, S5_W), lb_im.reshape(1, S5_W), cd(c_re), cd(c_im),
            d_skip.reshape(1, GW), w_glu, b_glu.reshape(1, GW))


def _ffn_up_kernel(h_ref, wa_ref, wb_ref, cw_ref, cb_ref, s0_ref, s1_ref, y_ref, a_ref, carry_scr, *, seq):
    def gate(a, a1, a2, b):
        c = cb_ref[...] + a2 * cw_ref[0:1, :] + a1 * cw_ref[1:2, :] + a * cw_ref[2:3, :]
        return (c * _sigmoid(c) * b).astype(BF16)

    if not seq:
        h = h_ref[...]
        a = _dot(h, wa_ref[...])
        a_ref[...] = a
        y_ref[...] = gate(a, s1_ref[...], s0_ref[...], _dot(h, wb_ref[...]))
        return

    @pl.when(pl.program_id(2) == 0)
    def _():
        carry_scr[...] = s0_ref[0]

    hm = h_ref.shape[0] // FFN_PARTS
    rowid = _iota((hm, 1), 0)
    c0 = carry_scr[0:1, :]
    c1 = carry_scr[1:2, :]
    for p in range(FFN_PARTS):
        rows = slice(p * hm, (p + 1) * hm)
        h = h_ref[rows, :]
        a = _dot(h, wa_ref[...])
        b = _dot(h, wb_ref[...])
        a1 = jnp.where(rowid == 0, c1, pltpu.roll(a, 1, 0))
        a2 = jnp.where(rowid == 0, c0, jnp.where(rowid == 1, c1, pltpu.roll(a, 2, 0)))
        y_ref[rows, :] = gate(a, a1, a2, b)
        c0 = a[hm - 2:hm - 1, :]
        c1 = a[hm - 1:hm, :]
    last2 = jnp.concatenate([c0, c1], axis=0)
    carry_scr[...] = last2
    a_ref[0] = last2


def _ffn_up(h2, w_up, l, cw, cb, st, nb, t):
    n, d = h2.shape
    dff = w_up.shape[2] // 2
    tn = 512
    nj = dff // tn
    if t > 1:
        tm = min(MM_ROWS, t)
        nt = t // tm
        y, fc = pl.pallas_call(
            functools.partial(_ffn_up_kernel, seq=True),
            out_shape=(jax.ShapeDtypeStruct((n, dff), BF16), jax.ShapeDtypeStruct((nb, 2, dff), F32)),
            grid=(nb, nj, nt),
            in_specs=[pl.BlockSpec((tm, d), lambda b, j, i: (b * nt + i, 0)),
                      pl.BlockSpec((None, d, tn), lambda b, j, i: (l, 0, j)),
                      pl.BlockSpec((None, d, tn), lambda b, j, i: (l, 0, nj + j)),
                      pl.BlockSpec((3, tn), lambda b, j, i: (0, j)),
                      pl.BlockSpec((1, tn), lambda b, j, i: (0, j)),
                      pl.BlockSpec((1, 2, tn), lambda b, j, i: (b, 0, j)),
                      pl.BlockSpec((1, 2, tn), lambda b, j, i: (b, 0, j))],
            out_specs=(pl.BlockSpec((tm, tn), lambda b, j, i: (b * nt + i, j)),
                       pl.BlockSpec((1, 2, tn), lambda b, j, i: (b, 0, j))),
            scratch_shapes=[pltpu.VMEM((2, tn), F32)],
            compiler_params=_cp("parallel", "parallel", "arbitrary"),
        )(h2, w_up, w_up, cw, cb, st, st)
        return y, fc
    s0, s1 = st[:, 0, :], st[:, 1, :]
    y, a = pl.pallas_call(
        functools.partial(_ffn_up_kernel, seq=False),
        out_shape=(jax.ShapeDtypeStruct((n, dff), BF16), jax.ShapeDtypeStruct((n, dff), F32)),
        grid=(nj,),
        in_specs=[pl.BlockSpec((n, d), lambda j: (0, 0)),
                  pl.BlockSpec((None, d, tn), lambda j: (l, 0, j)),
                  pl.BlockSpec((None, d, tn), lambda j: (l, 0, nj + j)),
                  pl.BlockSpec((3, tn), lambda j: (0, j)),
                  pl.BlockSpec((1, tn), lambda j: (0, j)),
                  pl.BlockSpec((n, tn), lambda j: (0, j)),
                  pl.BlockSpec((n, tn), lambda j: (0, j))],
        out_specs=(pl.BlockSpec((n, tn), lambda j: (0, j)), pl.BlockSpec((n, tn), lambda j: (0, j))),
        scratch_shapes=[pltpu.VMEM((2, tn), F32)],
        compiler_params=_cp("parallel"),
    )(h2, w_up, w_up, cw, cb, s0, s1)
    return y, jnp.stack([s1, a], axis=1)


def _permute_w_in(w):
    wt = jnp.swapaxes(w, 1, 2)
    rows = lambda s, n: wt[:, s:s + n, :]
    parts = [rows(0, 4 * GW), rows(2064, GW), rows(2832, GW), rows(5208, GW), rows(3416, R_IN),
             rows(2576, 128), rows(2704, 128), rows(3344, HD), rows(2048, NH), rows(2056, NH), rows(3408, NH),
             jnp.zeros((w.shape[0], NP - C_SM - HD - 3 * NH, w.shape[1]), w.dtype)]
    return jnp.swapaxes(jnp.concatenate(parts, axis=1), 1, 2).astype(BF16)


def _layer(x2, nb, t, l, W, st, tables, cache):
    n = nb * t
    c0, n0, m0, rs0, rsh0, sre0, sim0, conv0 = st
    proj = _in_proj(x2, W['norm_mix'], W['w_in'], l)
    aq_r, iq_r, k_r, v_r, ik_r = _rope_call(proj, tables[0], tables[1], *tables[2])

    if cache is None:
        ym, c1, n1, m1 = _mlstm_prompt(proj, W['m_b_i'], W['m_b_f'], W['m_norm'], nb, t)
        ya = _dsa_prompt(iq_r, proj, ik_r, aq_r, k_r, v_r, nb, t)
    else:
        ym, c1, n1, m1 = _mlstm_step(proj, W['m_b_i'], W['m_b_f'], W['m_norm'], c0, l, n0, m0)
        ym = ym.reshape(n, GW)
        cki, ck, cv, page_table, n_pool = cache
        ya = _dsa_sample(l, page_table, iq_r, proj[:, C_SM + SM_IW:C_SM + SM_IW + NH], ik_r, aq_r, k_r, v_r,
                         cki, ck, cv, n_pool)

    prev = rsh0.reshape(nb, 1, R_IN) if t > 1 else rsh0
    rs = _rwkv_prep(proj, prev, W['r_mu'], W['r_w0'], W['r_w_w2'], W['r_a0'], W['r_w_a2'], W['r_w_g2'],
                    W['r_k_k'], W['r_k_a'], nb, t)
    if t > 1:
        yr, rs1 = _rwkv_scan(rs, W['r_r_k'], W['r_ln_w'], W['r_ln_b'], nb, t)
    else:
        yr, rs1 = _rwkv_step(rs, W['r_r_k'], W['r_ln_w'], W['r_ln_b'], rs0, l)
        yr = yr.reshape(n, GW)
    rsh1 = proj.reshape(nb, t, NP)[:, t - 1, C_RIN:C_RIN + R_IN]

    su = proj[:, C_SU:C_SU + GW]
    u_tm = su.reshape(nb, t, GW).transpose(1, 0, 2).reshape(t * nb, GW)
    ys_tm, sre1, sim1 = _s5(u_tm, W['s5'], sre0.reshape(nb, S5_W), sim0.reshape(nb, S5_W), nb, t)
    ys = ys_tm.reshape(t, nb, GW).transpose(1, 0, 2).reshape(n, GW)

    x2 = _res_matmul(x2, [ym, ya, yr, ys], W['w_out'], l)
    h2 = _rmsnorm(x2, W['norm_ffn'], BF16)
    y, conv1 = _ffn_up(h2, W['ffn_w_up'], l, W['ffn_conv_w'], W['ffn_conv_b'], conv0, nb, t)
    x2 = _res_matmul(x2, [y], W['ffn_w_down'], l)
    outs = (k_r.reshape(nb, t, A_KV, HD), v_r.reshape(nb, t, A_KV, HD), ik_r.reshape(nb, t, HD),
            c1, n1, m1.reshape(nb, NH), rs1, rsh1, sre1.reshape(nb, S5_G, S5_P), sim1.reshape(nb, S5_G, S5_P), conv1)
    return x2, outs


def kernel(x_prompt, x_sample, cache_k, cache_v, cache_kidx, page_table, state_mlstm_c, state_mlstm_n,
           state_mlstm_m, state_rwkv_s, state_rwkv_shift, state_s5_re, state_s5_im, state_ffn_conv,
           norm_mix, w_in, w_out, m_b_i, m_b_f, m_norm, r_mu, r_w0, r_w_w2, r_a0, r_w_a2, r_w_g2,
           r_k_k, r_k_a, r_r_k, r_ln_w, r_ln_b, s5_a_re, s5_a_im, s5_b_re, s5_b_im, s5_c_re, s5_c_im,
           s5_d, s5_log_dt, s5_w_glu, s5_b_glu, norm_ffn, ffn_w_up, ffn_conv_w, ffn_conv_b, ffn_w_down,
           norm_final):
    bp, tp, d = x_prompt.shape
    bs, ts, _ = x_sample.shape
    assert ts == 1 and tp % CHUNK == 0
    depth = w_in.shape[0]
    n_pool = cache_k.shape[1]
    past = page_table.shape[1] * PAGE
    dff = ffn_conv_b.shape[-1]

    row = lambda z: z.reshape(1, -1)
    layers = []
    w_in_all, w_out_all = _permute_w_in(w_in), w_out.astype(BF16)
    w_up_all, w_down_all = ffn_w_up.astype(BF16), ffn_w_down.astype(BF16)
    for l in range(depth):
        layers.append(dict(
            norm_mix=row(norm_mix[l]), w_in=w_in_all, w_out=w_out_all,
            m_b_i=row(m_b_i[l]), m_b_f=row(m_b_f[l]), m_norm=row(m_norm[l]),
            r_mu=row(r_mu[l]), r_w0=row(r_w0[l]), r_w_w2=r_w_w2[l], r_a0=row(r_a0[l]), r_w_a2=r_w_a2[l],
            r_w_g2=r_w_g2[l], r_k_k=row(r_k_k[l]), r_k_a=row(r_k_a[l]), r_r_k=row(r_r_k[l]),
            r_ln_w=row(r_ln_w[l]), r_ln_b=row(r_ln_b[l]),
            s5=_s5_mats(s5_a_re[l], s5_a_im[l], s5_b_re[l], s5_b_im[l], s5_c_re[l], s5_c_im[l], s5_d[l],
                        s5_log_dt[l], s5_w_glu[l], s5_b_glu[l]),
            norm_ffn=row(norm_ffn[l]), ffn_w_up=w_up_all, ffn_conv_w=ffn_conv_w[l],
            ffn_conv_b=row(ffn_conv_b[l]), ffn_w_down=w_down_all))

    cos_p, sin_p = _rope_tables(jnp.arange(tp))
    cos_s, sin_s = _rope_tables(jnp.full((bs,), past))
    tab_p = (cos_p, sin_p, (bp, tp))
    tab_s = (cos_s, sin_s, (1, bs))

    zeros = lambda *s: jnp.zeros(s, F32)
    st_p = (zeros(bp, NH, HD, HD), zeros(bp, NH, HD), zeros(bp, NH), zeros(bp, NH, HD, HD), zeros(bp, R_IN),
            zeros(bp, S5_G, S5_P), zeros(bp, S5_G, S5_P), zeros(bp, 2, dff))
    cki = cache_kidx.reshape(depth * n_pool, PAGE, HD).transpose(0, 2, 1)
    ck = cache_k.reshape(depth * n_pool, PAGE, A_KV * HD).transpose(0, 2, 1)
    cv = cache_v.reshape(depth * n_pool, PAGE, A_KV * HD).transpose(0, 2, 1)

    xp = x_prompt.reshape(bp * tp, d)
    xs = x_sample.reshape(bs, d)
    new_p, new_s = [], []
    for l in range(depth):
        xp, sp = _layer(xp, bp, tp, l, layers[l], st_p, tab_p, None)
        st_s = (state_mlstm_c, state_mlstm_n[l], state_mlstm_m[l], state_rwkv_s, state_rwkv_shift[l],
                state_s5_re[l], state_s5_im[l], state_ffn_conv[l])
        xs, ss = _layer(xs, bs, 1, l, layers[l], st_s, tab_s, (cki, ck, cv, page_table, n_pool))
        new_p.append(sp)
        new_s.append(ss)
    (k_p, v_p, ki_p, mc_p, mn_p, mm_p, rs_p, rsh_p, sre_p, sim_p, fc_p) = [jnp.stack(z) for z in zip(*new_p)]
    (k_s, v_s, ki_s, mc_s, mn_s, mm_s, rs_s, rsh_s, sre_s, sim_s, fc_s) = [jnp.stack(z) for z in zip(*new_s)]
    y_prompt = _rmsnorm(xp, row(norm_final), F32).reshape(bp, tp, d)
    y_sample = _rmsnorm(xs, row(norm_final), F32).reshape(bs, ts, d)
    return (y_prompt, y_sample, k_p, k_s, v_p, v_s, ki_p, ki_s, mc_p, mc_s, mn_p, mn_s, mm_p, mm_s,
            rs_p, rs_s, rsh_p, rsh_s, sre_p, sre_s, sim_p, sim_s, fc_p, fc_s)
```

```python
import functools
import math

import jax
import jax.numpy as jnp
from jax import lax
from jax.experimental import pallas as pl
from jax.experimental.pallas import tpu as pltpu

F32 = jnp.float32
BF16 = jnp.bfloat16
HI = lax.Precision.HIGHEST

HD = 64
NH = 8
GW = NH * HD
A_KV = 2
PAGE = 128
TOPK_MAX = 256
ROPE_THETA = 10000.0
R_IN = 3 * GW + 64 + 64 + 128
R_LN_EPS = 64e-5
S5_G, S5_CH, S5_P = 32, 16, 64
S5_W = S5_G * S5_P
S5_SPLIT = 4
NORM_EPS = 1e-6
CHUNK = 64
IDX_SCALE = HD ** -0.5 * NH ** -0.5

C_M, C_AQ, C_IQ, C_SU, C_RIN, C_AK, C_AV, C_SM = 0, 2048, 2560, 3072, 3584, 5376, 5504, 5632
NP = 6144
SM_IK, SM_MI, SM_MF, SM_IW = 0, 64, 72, 80

VMEM_LIMIT = 56 * 1024 * 1024
MM_ROWS = 1024
STEP_SEQS = 4
RWKV_SEQS = 4
FFN_PARTS = 2


def _cp(*sem):
    return pltpu.CompilerParams(dimension_semantics=sem, vmem_limit_bytes=VMEM_LIMIT)


def _dot(a, b, prec=None):
    return jnp.dot(a, b, preferred_element_type=F32, precision=prec)


def _dot_nt(a, b, prec=None):
    return lax.dot_general(a, b, (((1,), (1,)), ((), ())), preferred_element_type=F32, precision=prec)


def _dot_tn(a, b, prec=None):
    return lax.dot_general(a, b, (((0,), (0,)), ((), ())), preferred_element_type=F32, precision=prec)


def _sigmoid(x):
    return 1.0 / (1.0 + jnp.exp(-x))


def _softplus(x):
    return jnp.maximum(x, 0.0) + jnp.log(1.0 + jnp.exp(-jnp.abs(x)))


def _iota(shape, dim):
    return lax.broadcasted_iota(jnp.int32, shape, dim)


HG = 4
BD = HG * HD
assert CHUNK == HD


def _bd_masks(c):
    row, col = _iota((HG * c, HG * c), 0), _iota((HG * c, HG * c), 1)
    same = (row // c) == (col // c)
    t, s = row % c, col % c
    return same, jnp.logical_and(same, s <= t), jnp.logical_and(same, s < t)


def _bd_diag(x):
    nb, ng = x.shape[:2]
    x6 = x.reshape(nb, ng, HG, HD, HG, HD)
    return jnp.stack([x6[:, :, j, :, j, :] for j in range(HG)], axis=2).reshape(nb, ng * HG, HD, HD)


def _inproj_kernel(x_ref, g_ref, w_ref, o_ref, h_scr):
    @pl.when(pl.program_id(1) == 0)
    def _():
        x = x_ref[...]
        ms = jnp.mean(x * x, axis=-1, keepdims=True)
        h_scr[...] = (x * lax.rsqrt(ms + NORM_EPS) * g_ref[...]).astype(BF16)

    o_ref[...] = _dot(h_scr[...], w_ref[...])


def _in_proj(x2, g, w, l):
    n, d = x2.shape
    npad = w.shape[2]
    tm = min(MM_ROWS, n)
    tn = 768
    return pl.pallas_call(
        _inproj_kernel,
        out_shape=jax.ShapeDtypeStruct((n, npad), F32),
        grid=(n // tm, npad // tn),
        in_specs=[pl.BlockSpec((tm, d), lambda i, j: (i, 0)),
                  pl.BlockSpec((1, d), lambda i, j: (0, 0)),
                  pl.BlockSpec((None, d, tn), lambda i, j: (l, 0, j))],
        out_specs=pl.BlockSpec((tm, tn), lambda i, j: (i, j)),
        scratch_shapes=[pltpu.VMEM((tm, d), BF16)],
        compiler_params=_cp("parallel", "arbitrary"),
    )(x2, g, w)


def _rmsnorm_kernel(x_ref, g_ref, o_ref):
    x = x_ref[...]
    ms = jnp.mean(x * x, axis=-1, keepdims=True)
    o_ref[...] = (x * lax.rsqrt(ms + NORM_EPS) * g_ref[...]).astype(o_ref.dtype)


def _rmsnorm(x2, g, dtype):
    n, d = x2.shape
    tm = min(512, n)
    return pl.pallas_call(
        _rmsnorm_kernel,
        out_shape=jax.ShapeDtypeStruct((n, d), dtype),
        grid=(n // tm,),
        in_specs=[pl.BlockSpec((tm, d), lambda i: (i, 0)), pl.BlockSpec((1, d), lambda i: (0, 0))],
        out_specs=pl.BlockSpec((tm, d), lambda i: (i, 0)),
        compiler_params=_cp("parallel"),
    )(x2, g)


def _resmm_kernel(r_ref, *refs):
    y_refs, w_ref, o_ref = refs[:-2], refs[-2], refs[-1]
    acc = r_ref[...]
    k0 = 0
    for y_ref in y_refs:
        kw = y_ref.shape[1]
        acc = acc + _dot(y_ref[...].astype(BF16), w_ref[k0:k0 + kw, :])
        k0 += kw
    o_ref[...] = acc


def _res_matmul(res, ys, w, l):
    n = res.shape[0]
    k, d = w.shape[1:]
    assert sum(y.shape[1] for y in ys) == k
    tm = min(MM_ROWS, n)
    tn = 512
    return pl.pallas_call(
        _resmm_kernel,
        out_shape=jax.ShapeDtypeStruct((n, d), F32),
        grid=(n // tm, d // tn),
        in_specs=[pl.BlockSpec((tm, tn), lambda i, j: (i, j))]
                 + [pl.BlockSpec((tm, y.shape[1]), lambda i, j: (i, 0)) for y in ys]
                 + [pl.BlockSpec((None, k, tn), lambda i, j: (l, 0, j))],
        out_specs=pl.BlockSpec((tm, tn), lambda i, j: (i, j)),
        compiler_params=_cp("parallel", "arbitrary"),
    )(res, *ys, w)


def _rope(x, cos, sin):
    w = x.shape[1]
    first = (_iota(x.shape, 1) & (HD - 1)) < HD // 2
    sw = jnp.where(first, pltpu.roll(x, w - HD // 2, 1), pltpu.roll(x, HD // 2, 1))
    return x * cos + sw * sin


def _rope_kernel(aq_ref, iq_ref, ak_ref, av_ref, sm_ref, cos_ref, sin_ref, aqo, iqo, ko, vo, iko):
    cos = cos_ref[...]
    sin = sin_ref[...]
    aqo[...] = _rope(aq_ref[...], cos, sin)
    iqo[...] = _rope(iq_ref[...], cos, sin)
    ko[...] = _rope(ak_ref[...], cos[:, :128], sin[:, :128])
    vo[...] = av_ref[...]
    iko[...] = _rope(sm_ref[...], cos[:, :128], sin[:, :128])[:, :HD]


def _rope_call(proj, cos, sin, nb, nt_rows):
    n = proj.shape[0]
    tm = min(512, nt_rows)
    nt = nt_rows // tm
    row = lambda b, i: b * nt + i
    return pl.pallas_call(
        _rope_kernel,
        out_shape=(jax.ShapeDtypeStruct((n, GW), F32), jax.ShapeDtypeStruct((n, GW), F32),
                   jax.ShapeDtypeStruct((n, 128), F32), jax.ShapeDtypeStruct((n, 128), F32),
                   jax.ShapeDtypeStruct((n, HD), F32)),
        grid=(nb, nt),
        in_specs=[pl.BlockSpec((tm, GW), lambda b, i: (row(b, i), C_AQ // GW)),
                  pl.BlockSpec((tm, GW), lambda b, i: (row(b, i), C_IQ // GW)),
                  pl.BlockSpec((tm, 128), lambda b, i: (row(b, i), C_AK // 128)),
                  pl.BlockSpec((tm, 128), lambda b, i: (row(b, i), C_AV // 128)),
                  pl.BlockSpec((tm, 128), lambda b, i: (row(b, i), C_SM // 128)),
                  pl.BlockSpec((tm, GW), lambda b, i: (i, 0)),
                  pl.BlockSpec((tm, GW), lambda b, i: (i, 0))],
        out_specs=(pl.BlockSpec((tm, GW), lambda b, i: (row(b, i), 0)),
                   pl.BlockSpec((tm, GW), lambda b, i: (row(b, i), 0)),
                   pl.BlockSpec((tm, 128), lambda b, i: (row(b, i), 0)),
                   pl.BlockSpec((tm, 128), lambda b, i: (row(b, i), 0)),
                   pl.BlockSpec((tm, HD), lambda b, i: (row(b, i), 0))),
        compiler_params=_cp("parallel", "parallel"),
    )(proj, proj, proj, proj, proj, cos, sin)


def _rope_tables(pos):
    half = HD // 2
    inv = ROPE_THETA ** (-jnp.arange(half, dtype=F32) / half)
    ang = pos.astype(F32)[:, None] * inv[None, :]
    cos, sin = jnp.cos(ang), jnp.sin(ang)
    cos64 = jnp.concatenate([cos, cos], axis=-1)
    sin64 = jnp.concatenate([-sin, sin], axis=-1)
    return jnp.tile(cos64, (1, NH)), jnp.tile(sin64, (1, NH))


def _kth_largest(sc, extra, kk):
    kf = jnp.float32(kk)

    def count_ge(c):
        n = jnp.sum(jnp.where(sc >= c, 1.0, 0.0), axis=-1, keepdims=True)
        if extra is not None:
            n = n + jnp.where(extra >= c, 1.0, 0.0)
        return n

    def key_to_f(key):
        bits = key ^ ((key >> 31) & jnp.int32(0x7FFFFFFF))
        return lax.bitcast_convert_type(bits, F32)

    r = sc.shape[0]
    int_min = jnp.int32(-2 ** 31)
    lo = jnp.where(count_ge(jnp.zeros((r, 1), F32)) >= kf, jnp.int32(0), int_min)

    def body(j, lo):
        cand = lo + jnp.left_shift(jnp.int32(1), jnp.int32(30) - j)
        ok = count_ge(key_to_f(cand)) >= kf
        return jnp.where(ok, cand, lo)

    lo = lax.fori_loop(0, 31, body, lo)
    key_neg_inf = jnp.int32(-2 ** 31 + 0x7FFFFF)
    return jnp.where(lo <= key_neg_inf, -jnp.inf, key_to_f(lo))


def _strict_upper_bf16(n):
    return jnp.where(_iota((n, n), 0) < _iota((n, n), 1), 1.0, 0.0).astype(BF16)


def _split_bf16(x):
    hi = x.astype(BF16).astype(F32)
    return hi, x - hi


def _dsa_prompt_kernel(iq_ref, sm_ref, ik_ref, aq_ref, k_ref, v_ref, o_ref, sel_scr, kcat_scr, *, topk, qb, n_ext):
    t_keys = ik_ref.shape[0]
    i = pl.program_id(1)
    per_ext = (t_keys // qb) // n_ext

    @pl.when(i == 0)
    def _():
        hi, lo = _split_bf16(ik_ref[...])
        kcat_scr[...] = jnp.concatenate([hi, lo, hi], axis=1)

    iq = iq_ref[...]
    aq = aq_ref[...] * (HD ** -0.5 * math.log2(math.e))
    wts = sm_ref[:, SM_IW:SM_IW + NH] * IDX_SCALE
    tq = i * qb + _iota((qb, 1), 0)

    def body(ext):
        kcat = kcat_scr[0:ext, :]
        sc = jnp.zeros((qb, ext), F32)
        for h in range(NH):
            hi, lo = _split_bf16(iq[:, h * HD:(h + 1) * HD])
            qk = _dot_nt(jnp.concatenate([hi, hi, lo], axis=1), kcat)
            sc = sc + jnp.maximum(qk, 0.0) * wts[:, h:h + 1]
        causal = _iota((1, ext), 1) <= tq
        sc = jnp.where(causal, sc, -jnp.inf)

        thr = _kth_largest(sc, None, topk)
        ge = sc >= thr
        n_ge = jnp.sum(jnp.where(ge, 1.0, 0.0), axis=-1, keepdims=True)
        sel_scr[:, 0:ext] = jnp.where(jnp.logical_and(ge, causal), 0.0, -jnp.inf)
        tie = jnp.logical_and(n_ge > jnp.float32(topk), thr > -jnp.inf)

        @pl.when(jnp.max(jnp.where(tie, 1.0, 0.0)) > 0.5)
        def _():
            gt = sc > thr
            eq = sc == thr
            need = jnp.float32(topk) - jnp.sum(jnp.where(gt, 1.0, 0.0), axis=-1, keepdims=True)
            ut = _strict_upper_bf16(128)
            run = jnp.zeros((qb, 1), F32)
            for c in range(ext // 128):
                sl = slice(c * 128, (c + 1) * 128)
                eqc = jnp.where(eq[:, sl], 1.0, 0.0)
                pref = _dot(eqc.astype(BF16), ut) + run
                keep = jnp.logical_or(gt[:, sl], jnp.logical_and(eq[:, sl], pref < need))
                sel_scr[:, sl] = jnp.where(jnp.logical_and(keep, causal[:, sl]), 0.0, -jnp.inf)
                run = run + jnp.sum(eqc, axis=-1, keepdims=True)

        bias = sel_scr[:, 0:ext]
        for g in range(A_KV):
            kg = k_ref[0:ext, g * HD:(g + 1) * HD]
            vg = v_ref[0:ext, g * HD:(g + 1) * HD]
            for j in range(NH // A_KV):
                h = g * (NH // A_KV) + j
                s = _dot_nt(aq[:, h * HD:(h + 1) * HD], kg) + bias
                m = jnp.max(s, axis=-1, keepdims=True)
                p = jnp.exp2(s - m)
                l = jnp.sum(p, axis=-1, keepdims=True)
                o_ref[:, h * HD:(h + 1) * HD] = _dot(p, vg) / l

    for j in range(n_ext):
        pl.when(i // per_ext == j)(functools.partial(body, (j + 1) * (t_keys // n_ext)))


def _dsa_prompt(iq_r, proj, ik_r, aq_r, k_r, v_r, nb, t):
    qb = min(128, t)
    nq = t // qb
    n_ext = min(4, nq)
    assert nq % n_ext == 0
    topk = min(TOPK_MAX, t // 4)
    n = nb * t
    row = lambda b, i: b * nq + i
    return pl.pallas_call(
        functools.partial(_dsa_prompt_kernel, topk=topk, qb=qb, n_ext=n_ext),
        out_shape=jax.ShapeDtypeStruct((n, GW), F32),
        grid=(nb, nq),
        in_specs=[pl.BlockSpec((qb, GW), lambda b, i: (row(b, i), 0)),
                  pl.BlockSpec((qb, 128), lambda b, i: (row(b, i), C_SM // 128)),
                  pl.BlockSpec((t, HD), lambda b, i: (b, 0)),
                  pl.BlockSpec((qb, GW), lambda b, i: (row(b, i), 0)),
                  pl.BlockSpec((t, 128), lambda b, i: (b, 0)),
                  pl.BlockSpec((t, 128), lambda b, i: (b, 0))],
        out_specs=pl.BlockSpec((qb, GW), lambda b, i: (row(b, i), 0)),
        scratch_shapes=[pltpu.VMEM((qb, t), F32), pltpu.VMEM((t, 3 * HD), F32)],
        compiler_params=_cp("parallel", "arbitrary"),
    )(iq_r, proj, ik_r, aq_r, k_r, v_r)


def _dsa_sample_score_kernel(*refs, n_pages):
    _, iq_ref, w_ref, ikn_ref = refs[:4]
    pages = refs[4:4 + n_pages]
    sc_ref, sn_ref = refs[4 + n_pages:]
    iq = iq_ref[0]
    w = w_ref[0] * IDX_SCALE
    k_hi, k_lo = _split_bf16(jnp.concatenate([pages[c][0] for c in range(n_pages)], axis=1))
    q_hi, q_lo = _split_bf16(iq)
    qk = _dot(jnp.concatenate([q_hi, q_hi, q_lo], axis=1), jnp.concatenate([k_hi, k_lo, k_hi], axis=0))
    sc_ref[0] = jnp.sum(jnp.maximum(qk, 0.0) * w, axis=0, keepdims=True)
    qkn = jnp.sum(iq * ikn_ref[0], axis=-1, keepdims=True)
    sn = jnp.sum(jnp.maximum(qkn, 0.0) * w, axis=0, keepdims=True)
    sn_ref[0] = jnp.broadcast_to(sn, (1, 128))


def _dsa_sample_select_kernel(sc_ref, sn_ref, sel_ref, seln_ref, *, topk):
    sc = sc_ref[...]
    sn = sn_ref[:, 0:1]
    bd, s_keys = sc.shape
    thr = _kth_largest(sc, sn, topk)
    gt = sc > thr
    eq = sc == thr
    n_gt = jnp.sum(jnp.where(gt, 1.0, 0.0), axis=-1, keepdims=True) + jnp.where(sn > thr, 1.0, 0.0)
    need = jnp.float32(topk) - n_gt
    ut = _strict_upper_bf16(PAGE)
    run = jnp.zeros((bd, 1), F32)
    for c in range(s_keys // PAGE):
        sl = slice(c * PAGE, (c + 1) * PAGE)
        eqc = jnp.where(eq[:, sl], 1.0, 0.0)
        pref = _dot(eqc.astype(BF16), ut) + run
        keep = jnp.logical_or(gt[:, sl], jnp.logical_and(eq[:, sl], pref < need))
        sel_ref[:, sl] = jnp.where(keep, 1.0, 0.0)
        run = run + jnp.sum(eqc, axis=-1, keepdims=True)
    sel_new = jnp.logical_or(sn > thr, jnp.logical_and(sn == thr, run < need))
    seln_ref[...] = jnp.broadcast_to(jnp.where(sel_new, 1.0, 0.0), seln_ref.shape)


def _dsa_sample_attn_kernel(*refs, n_pages):
    _, aq_ref, kn_ref, vn_ref, sel_ref, seln_ref = refs[:6]
    kpages = refs[6:6 + n_pages]
    vpages = refs[6 + n_pages:6 + 2 * n_pages]
    o_ref, k_scr, v_scr = refs[6 + 2 * n_pages:]
    for c in range(n_pages):
        k_scr[:, c * PAGE:(c + 1) * PAGE] = kpages[c][0]
        v_scr[:, c * PAGE:(c + 1) * PAGE] = vpages[c][0]
    sel = sel_ref[0] > 0.5
    sel_new = seln_ref[0][:, 0:1] > 0.5
    aq = aq_ref[0]
    kn = kn_ref[0]
    vn = vn_ref[0]
    hpg = NH // A_KV
    groups = range(A_KV)
    qg = [aq[g * hpg:(g + 1) * hpg, :] for g in groups]
    s = [_dot(qg[g], k_scr[g * HD:(g + 1) * HD, :]) * HD ** -0.5 for g in groups]
    pr, pn, l = [], [], []
    for g in groups:
        sg = jnp.where(sel, s[g], -jnp.inf)
        s_new = jnp.sum(qg[g] * kn[:, g * HD:(g + 1) * HD], axis=-1, keepdims=True) * HD ** -0.5
        s_new = jnp.where(sel_new, s_new, -jnp.inf)
        m = jnp.maximum(jnp.max(sg, axis=-1, keepdims=True), s_new)
        pr.append(jnp.exp(sg - m))
        pn.append(jnp.exp(s_new - m))
        l.append(jnp.sum(pr[g], axis=-1, keepdims=True) + pn[g])
    pv = [_dot_nt(pr[g], v_scr[g * HD:(g + 1) * HD, :]) for g in groups]
    for g in groups:
        o_ref[0, g * hpg:(g + 1) * hpg, :] = (pv[g] + pn[g] * vn[:, g * HD:(g + 1) * HD]) / l[g]


def _dsa_sample(layer, page_table, iq_r, iw, ik_r, aq_r, k_r, v_r, cki, ck, cv, n_pool):
    bd, n_pages = page_table.shape
    past = n_pages * PAGE
    topk = min(TOPK_MAX, (past + 1) // 4)
    base = layer * n_pool
    per_b = lambda b, pt: (b, 0, 0)

    def page_specs(width):
        return [pl.BlockSpec((1, width, PAGE), lambda b, pt, c=c: (base + pt[b, c], 0, 0)) for c in range(n_pages)]

    sc, sn = pl.pallas_call(
        functools.partial(_dsa_sample_score_kernel, n_pages=n_pages),
        out_shape=(jax.ShapeDtypeStruct((bd, 1, past), F32), jax.ShapeDtypeStruct((bd, 1, 128), F32)),
        grid_spec=pltpu.PrefetchScalarGridSpec(
            num_scalar_prefetch=1,
            grid=(bd,),
            in_specs=[pl.BlockSpec((1, NH, HD), per_b), pl.BlockSpec((1, NH, 1), per_b),
                      pl.BlockSpec((1, 1, HD), per_b)] + page_specs(HD),
            out_specs=(pl.BlockSpec((1, 1, past), per_b), pl.BlockSpec((1, 1, 128), per_b))),
        compiler_params=_cp("arbitrary"),
    )(page_table, iq_r.reshape(bd, NH, HD), iw.reshape(bd, NH, 1), ik_r.reshape(bd, 1, HD), *([cki] * n_pages))

    sel, seln = pl.pallas_call(
        functools.partial(_dsa_sample_select_kernel, topk=topk),
        out_shape=(jax.ShapeDtypeStruct((bd, past), F32), jax.ShapeDtypeStruct((bd, 128), F32)),
    )(sc.reshape(bd, past), sn.reshape(bd, 128))

    out = pl.pallas_call(
        functools.partial(_dsa_sample_attn_kernel, n_pages=n_pages),
        out_shape=jax.ShapeDtypeStruct((bd, NH, HD), F32),
        grid_spec=pltpu.PrefetchScalarGridSpec(
            num_scalar_prefetch=1,
            grid=(bd,),
            in_specs=[pl.BlockSpec((1, NH, HD), per_b), pl.BlockSpec((1, 1, 128), per_b),
                      pl.BlockSpec((1, 1, 128), per_b), pl.BlockSpec((1, 1, past), per_b),
                      pl.BlockSpec((1, 1, 128), per_b)] + page_specs(128) + page_specs(128),
            out_specs=pl.BlockSpec((1, NH, HD), per_b),
            scratch_shapes=[pltpu.VMEM((128, past), F32), pltpu.VMEM((128, past), F32)]),
        compiler_params=_cp("arbitrary"),
    )(page_table, aq_r.reshape(bd, NH, HD), k_r.reshape(bd, 1, 128), v_r.reshape(bd, 1, 128),
      sel.reshape(bd, 1, past), seln.reshape(bd, 1, 128), *([ck] * n_pages), *([cv] * n_pages))
    return out.reshape(bd, GW)


def _log_sigmoid(x):
    return jnp.minimum(x, 0.0) - jnp.log(1.0 + jnp.exp(-jnp.abs(x)))


MLSTM_ROWS = 2 * CHUNK


def _mlstm_prompt_kernel(m_ref, sm_ref, gb_ref, nw_ref, y_ref, c_ref, n_ref, mm_ref):
    L = CHUNK

    @pl.when(pl.program_id(1) == 0)
    def _():
        c_ref[...] = jnp.zeros(c_ref.shape, F32)
        n_ref[...] = jnp.zeros(n_ref.shape, F32)
        mm_ref[...] = jnp.zeros(mm_ref.shape, F32)

    ri, ci = _iota((L, L), 0), _iota((L, L), 1)
    tril = jnp.where(ci <= ri, 1.0, 0.0)
    triu = jnp.where(ci >= ri, 1.0, 0.0)
    same, causal, _ = _bd_masks(L)
    tile = lambda z: jnp.concatenate([z] * HG, axis=0)
    msk = lambda z: jnp.where(same, z, 0.0)
    stack_cols = lambda z, g: jnp.concatenate([z[:, g * HG + j:g * HG + j + 1] for j in range(HG)], axis=0)
    stack_rows = lambda z, g: jnp.concatenate([z[g * HG + j:g * HG + j + 1, :] for j in range(HG)], axis=1)
    percol = lambda zs: jnp.concatenate([jnp.broadcast_to(z, (L, 1)) for z in zs], axis=0)
    perrow = lambda zs: jnp.concatenate([jnp.broadcast_to(z, (1, HD)) for z in zs], axis=1)

    n_seq, n_grp = m_ref.shape[0], NH // HG
    chains = [(bb, g) for bb in range(n_seq) for g in range(n_grp)]
    each = lambda f: {c: f(c) for c in chains}
    lanes = lambda c: slice(c[1] * BD, (c[1] + 1) * BD)
    heads = lambda c: range(c[1] * HG, (c[1] + 1) * HG)
    gates = {}
    for bb in range(n_seq):
        smb = sm_ref[bb] + gb_ref[...]
        smt = smb.T
        gates[bb] = dict(ig=smb[:, SM_MI:SM_MI + NH], lf=_log_sigmoid(smb[:, SM_MF:SM_MF + NH]),
                         igt=smt[SM_MI:SM_MI + NH, :], lft=_log_sigmoid(smt[SM_MF:SM_MF + NH, :]))
    cbd = each(lambda c: c_ref[c[0], c[1]])
    nrow = each(lambda c: n_ref[c[0], :, lanes(c)])
    mprev = each(lambda c: [mm_ref[c[0], :, h:h + 1] for h in heads(c)])
    for cc in range(MLSTM_ROWS // L):
        rows = slice(cc * L, (cc + 1) * L)
        ig, bcs, bcst, igt = {}, {}, {}, {}
        for bb in range(n_seq):
            ig[bb] = gates[bb]['ig'][rows]
            bcs[bb] = _dot(tril, gates[bb]['lf'][rows], HI)
            bcst[bb] = _dot(gates[bb]['lft'][:, rows], triu, HI)
            igt[bb] = gates[bb]['igt'][:, rows]
        col = lambda c, w: m_ref[c[0], rows, w * GW + c[1] * BD:w * GW + (c[1] + 1) * BD]
        qexp = each(lambda c: msk(tile(col(c, 0))))
        kt = each(lambda c: tile(col(c, 1) * HD ** -0.5))
        kexp = each(lambda c: msk(kt[c]))
        vexp = each(lambda c: msk(tile(col(c, 2))))
        bcol = each(lambda c: stack_cols(bcs[c[0]], c[1]))
        icol = each(lambda c: stack_cols(ig[c[0]], c[1]))
        bl = each(lambda c: [bcs[c[0]][L - 1:L, h:h + 1] for h in heads(c)])
        dmat = each(lambda c: jnp.where(
            causal, bcol[c] - stack_rows(bcst[c[0]], c[1]) + stack_rows(igt[c[0]], c[1]), -jnp.inf))
        inter = each(lambda c: bcol[c] + percol(mprev[c]))
        mj = each(lambda c: jnp.maximum(inter[c], jnp.max(dmat[c], axis=-1, keepdims=True)))
        qk = each(lambda c: _dot_nt(qexp[c], kt[c]))
        s = each(lambda c: qk[c] * jnp.exp(dmat[c] - mj[c]))
        iw = each(lambda c: jnp.exp(inter[c] - mj[c]))
        sv = each(lambda c: _dot(s[c], vexp[c]))
        qc = each(lambda c: _dot_nt(qexp[c], cbd[c]))
        wl = each(lambda c: percol(bl[c]) - bcol[c] + icol[c])
        m_new = each(lambda c: [jnp.maximum(bl[c][j] + mprev[c][j],
                                            jnp.max(wl[c][j * L:(j + 1) * L], axis=0, keepdims=True))
                                for j in range(HG)])
        dec = each(lambda c: [jnp.exp(bl[c][j] + mprev[c][j] - m_new[c][j]) for j in range(HG)])
        ws = each(lambda c: jnp.exp(wl[c] - percol(m_new[c])))
        upd = each(lambda c: _dot_tn(vexp[c] * ws[c], kexp[c]))
        for c in chains:
            bb, ln = c[0], lanes(c)
            den = (jnp.sum(s[c], axis=-1, keepdims=True)
                   + iw[c] * jnp.sum(qexp[c] * nrow[c], axis=-1, keepdims=True))
            hc = (sv[c] + iw[c] * qc[c]) / jnp.maximum(jnp.abs(den), jnp.exp(-mj[c]))
            hn = hc * lax.rsqrt(jnp.sum(hc * hc, axis=-1, keepdims=True) * (1.0 / HD) + NORM_EPS)
            hn = sum(hn[j * L:(j + 1) * L] for j in range(HG))
            y_ref[bb, rows, ln] = hn * nw_ref[:, ln] * _sigmoid(col(c, 3))
        cbd = each(lambda c: percol(dec[c]) * cbd[c] + upd[c])
        nrow = each(lambda c: perrow(dec[c]) * nrow[c] + jnp.sum(kexp[c] * ws[c], axis=0, keepdims=True))
        mprev = m_new
    for c in chains:
        c_ref[c[0], c[1]] = cbd[c]
        n_ref[c[0], :, lanes(c)] = nrow[c]
        for j, h in enumerate(heads(c)):
            mm_ref[c[0], :, h:h + 1] = mprev[c][j]


def _gate_bias_row(b_i, b_f):
    z = lambda n: jnp.zeros((1, n), F32)
    return jnp.concatenate([z(SM_MI), b_i, b_f, z(128 - SM_MF - NH)], axis=1)


def _mlstm_prompt(proj, b_i, b_f, nw, nb, t):
    tm = MLSTM_ROWS
    nc = t // tm
    ng = NH // HG
    sb = RWKV_SEQS if nb % RWKV_SEQS == 0 else 1
    proj3 = proj.reshape(nb, t, NP)
    y, cbd, nrow, m = pl.pallas_call(
        _mlstm_prompt_kernel,
        out_shape=(jax.ShapeDtypeStruct((nb, t, GW), F32), jax.ShapeDtypeStruct((nb, ng, BD, BD), F32),
                   jax.ShapeDtypeStruct((nb, 1, GW), F32), jax.ShapeDtypeStruct((nb, 1, NH), F32)),
        grid=(nb // sb, nc),
        in_specs=[pl.BlockSpec((sb, tm, 4 * GW), lambda b, c: (b, c, 0)),
                  pl.BlockSpec((sb, tm, 128), lambda b, c: (b, c, C_SM // 128)),
                  pl.BlockSpec((1, 128), lambda b, c: (0, 0)),
                  pl.BlockSpec((1, GW), lambda b, c: (0, 0))],
        out_specs=(pl.BlockSpec((sb, tm, GW), lambda b, c: (b, c, 0)),
                   pl.BlockSpec((sb, ng, BD, BD), lambda b, c: (b, 0, 0, 0)),
                   pl.BlockSpec((sb, 1, GW), lambda b, c: (b, 0, 0)),
                   pl.BlockSpec((sb, 1, NH), lambda b, c: (b, 0, 0))),
        compiler_params=_cp("parallel", "arbitrary"),
    )(proj3, proj3, _gate_bias_row(b_i, b_f), nw)
    return y.reshape(nb * t, GW), _bd_diag(cbd), nrow.reshape(nb, NH, HD), m


def _head_expanders():
    diag = (_iota((NH * HD, HD), 0) % HD) == _iota((NH * HD, HD), 1)

    def rep(z):
        return jnp.concatenate([jnp.broadcast_to(z[h:h + 1, :], (HD, z.shape[1])) for h in range(NH)], axis=0)

    def fold(col):
        m = jnp.where(diag, col, 0.0)
        return jnp.concatenate([jnp.sum(m[h * HD:(h + 1) * HD], axis=0, keepdims=True) for h in range(NH)], axis=0)

    return rep, fold, diag


def _mlstm_step_kernel(x_ref, gt_ref, gb_ref, nw_ref, c_ref, n_ref, mm_ref, y_ref, co_ref, no_ref, mo_ref):
    rep, fold, diag = _head_expanders()
    lane = _iota((NH, 128), 1)
    for bb in range(x_ref.shape[0]):
        x = x_ref[bb]
        q, k, v, og = x[0:NH], x[NH:2 * NH] * HD ** -0.5, x[2 * NH:3 * NH], x[3 * NH:4 * NH]
        gates = gt_ref[bb] + gb_ref[...]
        ig = gates[:, 0:1]
        c0 = c_ref[bb].reshape(NH * HD, HD)
        n0 = n_ref[bb]
        inter = _log_sigmoid(gates[:, 1:2]) + mm_ref[bb]
        mj = jnp.maximum(inter, ig)
        s = jnp.sum(q * k, axis=-1, keepdims=True) * jnp.exp(ig - mj)
        iw = jnp.exp(inter - mj)
        ws = jnp.exp(ig - mj)
        den = s + iw * jnp.sum(n0 * q, axis=-1, keepdims=True)
        inv = 1.0 / jnp.maximum(jnp.abs(den), jnp.exp(-mj))
        per_head = jnp.where(lane == 0, s, jnp.where(lane == 1, iw, jnp.where(lane == 2, ws, inv)))
        ph = rep(per_head)
        s_c, iw_c, ws_c, inv_c = ph[:, 0:1], ph[:, 1:2], ph[:, 2:3], ph[:, 3:4]
        cq = jnp.sum(c0 * rep(q), axis=1, keepdims=True)
        vcol = jnp.sum(jnp.where(diag, rep(v), 0.0), axis=1, keepdims=True)
        hcol = (s_c * vcol + iw_c * cq) * inv_c
        co_ref[bb] = (iw_c * c0 + (ws_c * vcol) * rep(k)).reshape(NH, HD, HD)
        no_ref[bb] = iw * n0 + ws * k
        mo_ref[bb] = mj
        h8 = fold(hcol)
        hn = h8 * lax.rsqrt(jnp.mean(h8 * h8, axis=-1, keepdims=True) + NORM_EPS)
        y_ref[bb] = hn * nw_ref[...] * _sigmoid(og)


def _mlstm_step(proj, b_i, b_f, nw, c_all, l, n0, m0):
    bd = proj.shape[0]
    ss = STEP_SEQS if bd % STEP_SEQS == 0 else 1
    per_b3 = lambda b: (b, 0, 0)
    full = lambda b: (0, 0)
    x = proj[:, :4 * GW].reshape(bd, 4 * NH, HD)
    gates = proj[:, C_SM + SM_MI:C_SM + SM_MI + 2 * NH].reshape(bd, 2, NH).transpose(0, 2, 1)
    gate_bias = jnp.concatenate([b_i, b_f], axis=0).T
    y, c1, n1, m1 = pl.pallas_call(
        _mlstm_step_kernel,
        out_shape=(jax.ShapeDtypeStruct((bd, NH, HD), F32), jax.ShapeDtypeStruct((bd, NH, HD, HD), F32),
                   jax.ShapeDtypeStruct((bd, NH, HD), F32), jax.ShapeDtypeStruct((bd, NH, 1), F32)),
        grid=(bd // ss,),
        in_specs=[pl.BlockSpec((ss, 4 * NH, HD), per_b3),
                  pl.BlockSpec((ss, NH, 2), per_b3),
                  pl.BlockSpec((NH, 2), full),
                  pl.BlockSpec((NH, HD), full),
                  pl.BlockSpec((ss, NH, HD, HD), lambda b: (l * (bd // ss) + b, 0, 0, 0)),
                  pl.BlockSpec((ss, NH, HD), per_b3),
                  pl.BlockSpec((ss, NH, 1), per_b3)],
        out_specs=(pl.BlockSpec((ss, NH, HD), per_b3),
                   pl.BlockSpec((ss, NH, HD, HD), lambda b: (b, 0, 0, 0)),
                   pl.BlockSpec((ss, NH, HD), per_b3),
                   pl.BlockSpec((ss, NH, 1), per_b3)),
        compiler_params=_cp("parallel"),
    )(x, gates, gate_bias, nw.reshape(NH, HD), c_all.reshape((-1,) + c_all.shape[2:]), n0, m0.reshape(bd, NH, 1))
    return y, c1, n1, m1


def _rwkv_prep_kernel(x_ref, prev_ref, mu_ref, w0_ref, ww2_ref, a0_ref, wa2_ref, wg2_ref, kk_ref, ka_ref,
                      r_o, lw_o, k_o, v_o, kk_o, a_o, g_o, carry_scr, *, seq):
    x = x_ref[...]
    tm = x.shape[0]
    if seq:
        first = jnp.where(pl.program_id(1) == 0, prev_ref[0], carry_scr[...])
        xprev = jnp.where(_iota((tm, 1), 0) == 0, first, pltpu.roll(x, 1, 0))
        carry_scr[...] = x[tm - 1:tm, :]
    else:
        xprev = prev_ref[...]
    xm = x + (xprev - x) * mu_ref[...]
    r = xm[:, 0:GW]
    kx = xm[:, GW:2 * GW]
    v = xm[:, 2 * GW:3 * GW]
    xw = xm[:, 3 * GW:3 * GW + 64]
    xa = xm[:, 3 * GW + 64:3 * GW + 128]
    xg = xm[:, 3 * GW + 128:R_IN]
    w = -_softplus(-(w0_ref[...] + _dot(jnp.tanh(xw), ww2_ref[...]))) - 0.5
    a = _sigmoid(a0_ref[...] + _dot(xa, wa2_ref[...]))
    r_o[...] = r
    lw_o[...] = -jnp.exp(w)
    v_o[...] = v
    a_o[...] = a
    g_o[...] = _dot(_sigmoid(xg), wg2_ref[...])
    k_o[...] = kx * (1.0 + (a - 1.0) * ka_ref[...])
    kk = kx * kk_ref[...]
    for h in range(NH):
        kh = kk[:, h * HD:(h + 1) * HD]
        nrm = jnp.sqrt(jnp.sum(kh * kh, axis=-1, keepdims=True))
        kk_o[:, h * HD:(h + 1) * HD] = kh / jnp.maximum(nrm, 1e-12)


def _rwkv_prep(proj, prev, mu, w0, ww2, a0, wa2, wg2, k_k, k_a, nb, t):
    n = nb * t
    seq = t > 1
    full = lambda *_: (0, 0)
    if seq:
        tm = min(256, t)
        nt = t // tm
        grid = (nb, nt)
        xmap = lambda b, i: (b * nt + i, C_RIN // R_IN)
        pspec = pl.BlockSpec((1, 1, R_IN), lambda b, i: (b, 0, 0))
        omap = lambda b, i: (b * nt + i, 0)
        sem = ("parallel", "arbitrary")
    else:
        tm = n
        grid = (1,)
        xmap = lambda i: (0, C_RIN // R_IN)
        pspec = pl.BlockSpec((tm, R_IN), lambda i: (0, 0))
        omap = lambda i: (0, 0)
        sem = ("arbitrary",)
    wspecs = [pl.BlockSpec((1, R_IN), full), pl.BlockSpec((1, GW), full), pl.BlockSpec((64, GW), full),
              pl.BlockSpec((1, GW), full), pl.BlockSpec((64, GW), full), pl.BlockSpec((128, GW), full),
              pl.BlockSpec((1, GW), full), pl.BlockSpec((1, GW), full)]
    return pl.pallas_call(
        functools.partial(_rwkv_prep_kernel, seq=seq),
        out_shape=tuple(jax.ShapeDtypeStruct((n, GW), F32) for _ in range(7)),
        grid=grid,
        in_specs=[pl.BlockSpec((tm, R_IN), xmap), pspec] + wspecs,
        out_specs=tuple(pl.BlockSpec((tm, GW), omap) for _ in range(7)),
        scratch_shapes=[pltpu.VMEM((1, R_IN), F32)],
        compiler_params=_cp(*sem),
    )(proj, prev, mu, w0, ww2, a0, wa2, wg2, k_k, k_a)


def _rwkv_scan_kernel(r_ref, lw_ref, k_ref, v_ref, kk_ref, a_ref, g_ref, rk_ref, lnw_ref, lnb_ref, y_ref, s_ref):
    C = CHUNK

    @pl.when(pl.program_id(1) == 0)
    def _():
        s_ref[...] = jnp.zeros(s_ref.shape, F32)

    ri, ci = _iota((C, C), 0), _iota((C, C), 1)
    same, incl, strict = _bd_masks(C)
    row, col = _iota((BD, BD), 0), _iota((BD, BD), 1)
    eye = jnp.where(row == col, 1.0, 0.0)
    strict_incl = jnp.concatenate([strict, incl], axis=0)
    tile = lambda z: jnp.concatenate([z] * HG, axis=0)
    msk = lambda z: jnp.where(same, z, 0.0)
    fold = lambda z: sum(z[j * C:(j + 1) * C] for j in range(HG))
    n_seq, n_grp = r_ref.shape[0], NH // HG
    chains = [(bb, g) for bb in range(n_seq) for g in range(n_grp)]
    each = lambda f: {c: f(c) for c in chains}
    lanes = lambda c: slice(c[1] * BD, (c[1] + 1) * BD)
    seq = {}
    for bb in range(n_seq):
        lw = lw_ref[bb]
        cs = _dot(jnp.where(ci <= ri, 1.0, 0.0), lw, HI)
        gam = jnp.exp(cs)
        ginv = jnp.exp(-cs)
        r, k, v, kk = r_ref[bb], k_ref[bb], v_ref[bb], kk_ref[bb]
        seq[bb] = dict(at=-kk * jnp.exp(cs - lw), bt=kk * a_ref[bb] * ginv, kt=k * ginv, rt=r * gam, v=v,
                       glast=gam[C - 1:C, :], bonus_in=r * k * rk_ref[...])
    part = lambda c, name: seq[c[0]][name][:, lanes(c)]
    btl = each(lambda c: tile(part(c, 'bt')))
    ktl = each(lambda c: tile(part(c, 'kt')))
    vexp = each(lambda c: msk(tile(part(c, 'v'))))
    ar = each(lambda c: jnp.concatenate([msk(tile(part(c, 'at'))), msk(tile(part(c, 'rt')))], axis=0))
    sbd = each(lambda c: s_ref[c[0], c[1]])
    gb = each(lambda c: _dot_nt(ar[c], btl[c]))
    gk = each(lambda c: _dot_nt(ar[c], ktl[c]))
    gs = each(lambda c: _dot_nt(ar[c], sbd[c]))
    n_ab = each(lambda c: jnp.where(strict, gb[c][:BD], 0.0))
    x = each(lambda c: eye + n_ab[c])
    pm = each(lambda c: _dot(n_ab[c], n_ab[c]))
    for j in range(5):
        xd = each(lambda c: _dot(x[c], pm[c]))
        if j < 4:
            pm = each(lambda c: _dot(pm[c], pm[c]))
        x = each(lambda c: x[c] + xd[c])
    akv = each(lambda c: _dot(jnp.where(strict_incl, gk[c], 0.0), vexp[c]))
    u = each(lambda c: _dot(x[c], gs[c][:BD] + akv[c][:BD]))
    o = each(lambda c: gs[c][BD:] + _dot(jnp.where(incl, gb[c][BD:], 0.0), u[c]) + akv[c][BD:])
    gl = each(lambda c: part(c, 'glast'))
    s_new = each(lambda c: sbd[c] * gl[c] + _dot_tn(
        jnp.concatenate([u[c], vexp[c]], axis=0),
        jnp.concatenate([msk(btl[c] * gl[c]), msk(ktl[c] * gl[c])], axis=0)))
    for c in chains:
        bb, ln = c[0], lanes(c)
        s_ref[c[0], c[1]] = s_new[c]
        mean = jnp.sum(o[c], axis=-1, keepdims=True) * (1.0 / HD)
        dev = msk(o[c] - mean)
        var = jnp.sum(dev * dev, axis=-1, keepdims=True) * (1.0 / HD)
        on = fold(dev * lax.rsqrt(var + R_LN_EPS)) * lnw_ref[:, ln] + lnb_ref[:, ln]
        bonus = fold(jnp.sum(msk(tile(part(c, 'bonus_in'))), axis=-1, keepdims=True) * vexp[c])
        y_ref[bb, :, ln] = (on + bonus) * g_ref[bb, :, ln]


def _rwkv_scan(rs, rk, lnw, lnb, nb, t):
    nc = t // CHUNK
    ng = NH // HG
    sb = RWKV_SEQS if nb % RWKV_SEQS == 0 else 1
    rowmap = lambda b, c: (b, c, 0)
    full = lambda b, c: (0, 0)
    y, sbd = pl.pallas_call(
        _rwkv_scan_kernel,
        out_shape=(jax.ShapeDtypeStruct((nb, t, GW), F32), jax.ShapeDtypeStruct((nb, ng, BD, BD), F32)),
        grid=(nb // sb, nc),
        in_specs=[pl.BlockSpec((sb, CHUNK, GW), rowmap)] * 7 + [pl.BlockSpec((1, GW), full)] * 3,
        out_specs=(pl.BlockSpec((sb, CHUNK, GW), rowmap), pl.BlockSpec((sb, ng, BD, BD), lambda b, c: (b, 0, 0, 0))),
        compiler_params=_cp("parallel", "arbitrary"),
    )(*(z.reshape(nb, t, GW) for z in rs), rk, lnw, lnb)
    return y.reshape(nb * t, GW), _bd_diag(sbd)


def _rwkv_step_kernel(r_ref, lw_ref, k_ref, v_ref, kk_ref, a_ref, g_ref, rk_ref, lnw_ref, lnb_ref, s_ref,
                      y_ref, so_ref):
    rep, fold, diag = _head_expanders()
    for bb in range(r_ref.shape[0]):
        r, lw, k, v, kk, a, g = (ref[bb] for ref in (r_ref, lw_ref, k_ref, v_ref, kk_ref, a_ref, g_ref))
        s0 = s_ref[bb].reshape(NH * HD, HD)
        sk = jnp.sum(s0 * rep(kk), axis=1, keepdims=True)
        vcol = jnp.sum(jnp.where(diag, rep(v), 0.0), axis=1, keepdims=True)
        s1 = s0 * rep(jnp.exp(lw)) - sk * rep(kk * a) + vcol * rep(k)
        so_ref[bb] = s1.reshape(NH, HD, HD)
        ocol = jnp.sum(s1 * rep(r), axis=1, keepdims=True)
        o = fold(ocol)
        mean = jnp.mean(o, axis=-1, keepdims=True)
        var = jnp.mean(jnp.square(o - mean), axis=-1, keepdims=True)
        on = (o - mean) * lax.rsqrt(var + R_LN_EPS) * lnw_ref[...] + lnb_ref[...]
        bonus = jnp.sum(r * k * rk_ref[...], axis=-1, keepdims=True) * v
        y_ref[bb] = (on + bonus) * g


def _rwkv_step(rs, rk, lnw, lnb, s_all, l):
    bd = s_all.shape[1]
    ss = STEP_SEQS if bd % STEP_SEQS == 0 else 1
    per_b = lambda b: (b, 0, 0)
    full = lambda b: (0, 0)
    return pl.pallas_call(
        _rwkv_step_kernel,
        out_shape=(jax.ShapeDtypeStruct((bd, NH, HD), F32), jax.ShapeDtypeStruct((bd, NH, HD, HD), F32)),
        grid=(bd // ss,),
        in_specs=[pl.BlockSpec((ss, NH, HD), per_b)] * 7 + [pl.BlockSpec((NH, HD), full)] * 3
                 + [pl.BlockSpec((ss, NH, HD, HD), lambda b: (l * (bd // ss) + b, 0, 0, 0))],
        out_specs=(pl.BlockSpec((ss, NH, HD), per_b), pl.BlockSpec((ss, NH, HD, HD), lambda b: (b, 0, 0, 0))),
        compiler_params=_cp("parallel"),
    )(*(z.reshape(bd, NH, HD) for z in rs), *(z.reshape(NH, HD) for z in (rk, lnw, lnb)),
      s_all.reshape((-1,) + s_all.shape[2:]))


def _gelu_tanh(x):
    return 0.5 * x * (1.0 + jnp.tanh(math.sqrt(2.0 / math.pi) * (x + 0.044715 * (x * x * x))))


def _s5_kernel(u_ref, bre_ref, bim_ref, lre_ref, lim_ref, cre_ref, cim_ref, d_ref, wg_ref, bg_ref, h0r_ref, h0i_ref,
               y_ref, hr_ref, hi_ref, hre_scr, him_scr, *, nb, tb):
    @pl.when(pl.program_id(0) == 0)
    def _():
        hr_ref[...] = h0r_ref[...]
        hi_ref[...] = h0i_ref[...]

    u = u_ref[...]
    cw, sw = GW // S5_SPLIT, S5_W // S5_SPLIT
    for j in range(S5_SPLIT):
        cs_, ss_ = slice(j * cw, (j + 1) * cw), slice(j * sw, (j + 1) * sw)
        hre_scr[:, ss_] = _dot(u[:, cs_], bre_ref[cs_, ss_])
        him_scr[:, ss_] = _dot(u[:, cs_], bim_ref[cs_, ss_])
    lr = lre_ref[...]
    li = lim_ref[...]

    def body(t, carry):
        hr, hi = carry
        rows = pl.ds(pl.multiple_of(t * nb, nb), nb)
        nr = lr * hr - li * hi + hre_scr[rows, :]
        ni = lr * hi + li * hr + him_scr[rows, :]
        hre_scr[rows, :] = nr
        him_scr[rows, :] = ni
        return nr, ni

    hr, hi = lax.fori_loop(0, tb, body, (hr_ref[...], hi_ref[...]))
    hr_ref[...] = hr
    hi_ref[...] = hi
    y = jnp.concatenate(
        [_dot(hre_scr[:, j * sw:(j + 1) * sw], cre_ref[j * sw:(j + 1) * sw, j * cw:(j + 1) * cw])
         - _dot(him_scr[:, j * sw:(j + 1) * sw], cim_ref[j * sw:(j + 1) * sw, j * cw:(j + 1) * cw])
         for j in range(S5_SPLIT)], axis=1) + d_ref[...] * u
    y = _gelu_tanh(y)
    y_ref[...] = y * _sigmoid(_dot(y, wg_ref[...]) + bg_ref[...])


def _s5(u_tm, mats, h0r, h0i, nb, t):
    bre, bim, lre, lim, cre, cim, d, wg, bg = mats
    tb = min(64, t)
    full = lambda i: (0, 0)
    return pl.pallas_call(
        functools.partial(_s5_kernel, nb=nb, tb=tb),
        out_shape=(jax.ShapeDtypeStruct((t * nb, GW), F32), jax.ShapeDtypeStruct((nb, S5_W), F32),
                   jax.ShapeDtypeStruct((nb, S5_W), F32)),
        grid=(t // tb,),
        in_specs=[pl.BlockSpec((tb * nb, GW), lambda i: (i, 0)),
                  pl.BlockSpec((GW, S5_W), full), pl.BlockSpec((GW, S5_W), full),
                  pl.BlockSpec((1, S5_W), full), pl.BlockSpec((1, S5_W), full),
                  pl.BlockSpec((S5_W, GW), full), pl.BlockSpec((S5_W, GW), full),
                  pl.BlockSpec((1, GW), full), pl.BlockSpec((GW, GW), full), pl.BlockSpec((1, GW), full),
                  pl.BlockSpec((nb, S5_W), full), pl.BlockSpec((nb, S5_W), full)],
        out_specs=(pl.BlockSpec((tb * nb, GW), lambda i: (i, 0)),
                   pl.BlockSpec((nb, S5_W), full), pl.BlockSpec((nb, S5_W), full)),
        scratch_shapes=[pltpu.VMEM((tb * nb, S5_W), F32), pltpu.VMEM((tb * nb, S5_W), F32)],
        compiler_params=_cp("arbitrary"),
    )(u_tm, bre, bim, lre, lim, cre, cim, d, wg, bg, h0r, h0i)


def _s5_mats(a_re, a_im, b_re, b_im, c_re, c_im, d_skip, log_dt, w_glu, b_glu):
    dt = jnp.exp(log_dt)
    mag = jnp.exp(a_re * dt)
    lb_re, lb_im = mag * jnp.cos(a_im * dt), mag * jnp.sin(a_im * dt)
    den = a_re * a_re + a_im * a_im
    f_re = ((lb_re - 1.0) * a_re + lb_im * a_im) / den
    f_im = (lb_im * a_re - (lb_re - 1.0) * a_im) / den
    bb_re = f_re[..., None] * b_re - f_im[..., None] * b_im
    bb_im = f_re[..., None] * b_im + f_im[..., None] * b_re
    eye = jnp.eye(S5_G, dtype=F32)
    bd = lambda bb: jnp.einsum('gpc,gh->gchp', bb, eye).reshape(GW, S5_W)
    cd = lambda cc: jnp.einsum('gcp,gh->gphc', cc, eye).reshape(S5_W, GW)
    return (bd(bb_re), bd(bb_im), lb_re.reshape(1, S5_W), lb_im.reshape(1, S5_W), cd(c_re), cd(c_im),
            d_skip.reshape(1, GW), w_glu, b_glu.reshape(1, GW))


def _ffn_up_kernel(h_ref, wa_ref, wb_ref, cw_ref, cb_ref, s0_ref, s1_ref, y_ref, a_ref, carry_scr, *, seq):
    def gate(a, a1, a2, b):
        c = cb_ref[...] + a2 * cw_ref[0:1, :] + a1 * cw_ref[1:2, :] + a * cw_ref[2:3, :]
        return (c * _sigmoid(c) * b).astype(BF16)

    if not seq:
        h = h_ref[...]
        a = _dot(h, wa_ref[...])
        a_ref[...] = a
        y_ref[...] = gate(a, s1_ref[...], s0_ref[...], _dot(h, wb_ref[...]))
        return

    @pl.when(pl.program_id(2) == 0)
    def _():
        carry_scr[...] = s0_ref[0]

    hm = h_ref.shape[0] // FFN_PARTS
    rowid = _iota((hm, 1), 0)
    c0 = carry_scr[0:1, :]
    c1 = carry_scr[1:2, :]
    for p in range(FFN_PARTS):
        rows = slice(p * hm, (p + 1) * hm)
        h = h_ref[rows, :]
        a = _dot(h, wa_ref[...])
        b = _dot(h, wb_ref[...])
        a1 = jnp.where(rowid == 0, c1, pltpu.roll(a, 1, 0))
        a2 = jnp.where(rowid == 0, c0, jnp.where(rowid == 1, c1, pltpu.roll(a, 2, 0)))
        y_ref[rows, :] = gate(a, a1, a2, b)
        c0 = a[hm - 2:hm - 1, :]
        c1 = a[hm - 1:hm, :]
    last2 = jnp.concatenate([c0, c1], axis=0)
    carry_scr[...] = last2
    a_ref[0] = last2


def _ffn_up(h2, w_up, l, cw, cb, st, nb, t):
    n, d = h2.shape
    dff = w_up.shape[2] // 2
    tn = 512
    nj = dff // tn
    if t > 1:
        tm = min(MM_ROWS, t)
        nt = t // tm
        y, fc = pl.pallas_call(
            functools.partial(_ffn_up_kernel, seq=True),
            out_shape=(jax.ShapeDtypeStruct((n, dff), BF16), jax.ShapeDtypeStruct((nb, 2, dff), F32)),
            grid=(nb, nj, nt),
            in_specs=[pl.BlockSpec((tm, d), lambda b, j, i: (b * nt + i, 0)),
                      pl.BlockSpec((None, d, tn), lambda b, j, i: (l, 0, j)),
                      pl.BlockSpec((None, d, tn), lambda b, j, i: (l, 0, nj + j)),
                      pl.BlockSpec((3, tn), lambda b, j, i: (0, j)),
                      pl.BlockSpec((1, tn), lambda b, j, i: (0, j)),
                      pl.BlockSpec((1, 2, tn), lambda b, j, i: (b, 0, j)),
                      pl.BlockSpec((1, 2, tn), lambda b, j, i: (b, 0, j))],
            out_specs=(pl.BlockSpec((tm, tn), lambda b, j, i: (b * nt + i, j)),
                       pl.BlockSpec((1, 2, tn), lambda b, j, i: (b, 0, j))),
            scratch_shapes=[pltpu.VMEM((2, tn), F32)],
            compiler_params=_cp("parallel", "parallel", "arbitrary"),
        )(h2, w_up, w_up, cw, cb, st, st)
        return y, fc
    s0, s1 = st[:, 0, :], st[:, 1, :]
    y, a = pl.pallas_call(
        functools.partial(_ffn_up_kernel, seq=False),
        out_shape=(jax.ShapeDtypeStruct((n, dff), BF16), jax.ShapeDtypeStruct((n, dff), F32)),
        grid=(nj,),
        in_specs=[pl.BlockSpec((n, d), lambda j: (0, 0)),
                  pl.BlockSpec((None, d, tn), lambda j: (l, 0, j)),
                  pl.BlockSpec((None, d, tn), lambda j: (l, 0, nj + j)),
                  pl.BlockSpec((3, tn), lambda j: (0, j)),
                  pl.BlockSpec((1, tn), lambda j: (0, j)),
                  pl.BlockSpec((n, tn), lambda j: (0, j)),
                  pl.BlockSpec((n, tn), lambda j: (0, j))],
        out_specs=(pl.BlockSpec((n, tn), lambda j: (0, j)), pl.BlockSpec((n, tn), lambda j: (0, j))),
        scratch_shapes=[pltpu.VMEM((2, tn), F32)],
        compiler_params=_cp("parallel"),
    )(h2, w_up, w_up, cw, cb, s0, s1)
    return y, jnp.stack([s1, a], axis=1)


def _permute_w_in(w):
    wt = jnp.swapaxes(w, 1, 2)
    rows = lambda s, n: wt[:, s:s + n, :]
    parts = [rows(0, 4 * GW), rows(2064, GW), rows(2832, GW), rows(5208, GW), rows(3416, R_IN),
             rows(2576, 128), rows(2704, 128), rows(3344, HD), rows(2048, NH), rows(2056, NH), rows(3408, NH),
             jnp.zeros((w.shape[0], NP - C_SM - HD - 3 * NH, w.shape[1]), w.dtype)]
    return jnp.swapaxes(jnp.concatenate(parts, axis=1), 1, 2).astype(BF16)


def _layer(x2, nb, t, l, W, st, tables, cache):
    n = nb * t
    c0, n0, m0, rs0, rsh0, sre0, sim0, conv0 = st
    proj = _in_proj(x2, W['norm_mix'], W['w_in'], l)
    aq_r, iq_r, k_r, v_r, ik_r = _rope_call(proj, tables[0], tables[1], *tables[2])

    if cache is None:
        ym, c1, n1, m1 = _mlstm_prompt(proj, W['m_b_i'], W['m_b_f'], W['m_norm'], nb, t)
        ya = _dsa_prompt(iq_r, proj, ik_r, aq_r, k_r, v_r, nb, t)
    else:
        ym, c1, n1, m1 = _mlstm_step(proj, W['m_b_i'], W['m_b_f'], W['m_norm'], c0, l, n0, m0)
        ym = ym.reshape(n, GW)
        cki, ck, cv, page_table, n_pool = cache
        ya = _dsa_sample(l, page_table, iq_r, proj[:, C_SM + SM_IW:C_SM + SM_IW + NH], ik_r, aq_r, k_r, v_r,
                         cki, ck, cv, n_pool)

    prev = rsh0.reshape(nb, 1, R_IN) if t > 1 else rsh0
    rs = _rwkv_prep(proj, prev, W['r_mu'], W['r_w0'], W['r_w_w2'], W['r_a0'], W['r_w_a2'], W['r_w_g2'],
                    W['r_k_k'], W['r_k_a'], nb, t)
    if t > 1:
        yr, rs1 = _rwkv_scan(rs, W['r_r_k'], W['r_ln_w'], W['r_ln_b'], nb, t)
    else:
        yr, rs1 = _rwkv_step(rs, W['r_r_k'], W['r_ln_w'], W['r_ln_b'], rs0, l)
        yr = yr.reshape(n, GW)
    rsh1 = proj.reshape(nb, t, NP)[:, t - 1, C_RIN:C_RIN + R_IN]

    su = proj[:, C_SU:C_SU + GW]
    u_tm = su.reshape(nb, t, GW).transpose(1, 0, 2).reshape(t * nb, GW)
    ys_tm, sre1, sim1 = _s5(u_tm, W['s5'], sre0.reshape(nb, S5_W), sim0.reshape(nb, S5_W), nb, t)
    ys = ys_tm.reshape(t, nb, GW).transpose(1, 0, 2).reshape(n, GW)

    x2 = _res_matmul(x2, [ym, ya, yr, ys], W['w_out'], l)
    h2 = _rmsnorm(x2, W['norm_ffn'], BF16)
    y, conv1 = _ffn_up(h2, W['ffn_w_up'], l, W['ffn_conv_w'], W['ffn_conv_b'], conv0, nb, t)
    x2 = _res_matmul(x2, [y], W['ffn_w_down'], l)
    outs = (k_r.reshape(nb, t, A_KV, HD), v_r.reshape(nb, t, A_KV, HD), ik_r.reshape(nb, t, HD),
            c1, n1, m1.reshape(nb, NH), rs1, rsh1, sre1.reshape(nb, S5_G, S5_P), sim1.reshape(nb, S5_G, S5_P), conv1)
    return x2, outs


def kernel(x_prompt, x_sample, cache_k, cache_v, cache_kidx, page_table, state_mlstm_c, state_mlstm_n,
           state_mlstm_m, state_rwkv_s, state_rwkv_shift, state_s5_re, state_s5_im, state_ffn_conv,
           norm_mix, w_in, w_out, m_b_i, m_b_f, m_norm, r_mu, r_w0, r_w_w2, r_a0, r_w_a2, r_w_g2,
           r_k_k, r_k_a, r_r_k, r_ln_w, r_ln_b, s5_a_re, s5_a_im, s5_b_re, s5_b_im, s5_c_re, s5_c_im,
           s5_d, s5_log_dt, s5_w_glu, s5_b_glu, norm_ffn, ffn_w_up, ffn_conv_w, ffn_conv_b, ffn_w_down,
           norm_final):
    bp, tp, d = x_prompt.shape
    bs, ts, _ = x_sample.shape
    assert ts == 1 and tp % CHUNK == 0
    depth = w_in.shape[0]
    n_pool = cache_k.shape[1]
    past = page_table.shape[1] * PAGE
    dff = ffn_conv_b.shape[-1]

    row = lambda z: z.reshape(1, -1)
    layers = []
    w_in_all, w_out_all = _permute_w_in(w_in), w_out.astype(BF16)
    w_up_all, w_down_all = ffn_w_up.astype(BF16), ffn_w_down.astype(BF16)
    for l in range(depth):
        layers.append(dict(
            norm_mix=row(norm_mix[l]), w_in=w_in_all, w_out=w_out_all,
            m_b_i=row(m_b_i[l]), m_b_f=row(m_b_f[l]), m_norm=row(m_norm[l]),
            r_mu=row(r_mu[l]), r_w0=row(r_w0[l]), r_w_w2=r_w_w2[l], r_a0=row(r_a0[l]), r_w_a2=r_w_a2[l],
            r_w_g2=r_w_g2[l], r_k_k=row(r_k_k[l]), r_k_a=row(r_k_a[l]), r_r_k=row(r_r_k[l]),
            r_ln_w=row(r_ln_w[l]), r_ln_b=row(r_ln_b[l]),
            s5=_s5_mats(s5_a_re[l], s5_a_im[l], s5_b_re[l], s5_b_im[l], s5_c_re[l], s5_c_im[l], s5_d[l],
                        s5_log_dt[l], s5_w_glu[l], s5_b_glu[l]),
            norm_ffn=row(norm_ffn[l]), ffn_w_up=w_up_all, ffn_conv_w=ffn_conv_w[l],
            ffn_conv_b=row(ffn_conv_b[l]), ffn_w_down=w_down_all))

    cos_p, sin_p = _rope_tables(jnp.arange(tp))
    cos_s, sin_s = _rope_tables(jnp.full((bs,), past))
    tab_p = (cos_p, sin_p, (bp, tp))
    tab_s = (cos_s, sin_s, (1, bs))

    zeros = lambda *s: jnp.zeros(s, F32)
    st_p = (zeros(bp, NH, HD, HD), zeros(bp, NH, HD), zeros(bp, NH), zeros(bp, NH, HD, HD), zeros(bp, R_IN),
            zeros(bp, S5_G, S5_P), zeros(bp, S5_G, S5_P), zeros(bp, 2, dff))
    cki = cache_kidx.reshape(depth * n_pool, PAGE, HD).transpose(0, 2, 1)
    ck = cache_k.reshape(depth * n_pool, PAGE, A_KV * HD).transpose(0, 2, 1)
    cv = cache_v.reshape(depth * n_pool, PAGE, A_KV * HD).transpose(0, 2, 1)

    xp = x_prompt.reshape(bp * tp, d)
    xs = x_sample.reshape(bs, d)
    new_p, new_s = [], []
    for l in range(depth):
        xp, sp = _layer(xp, bp, tp, l, layers[l], st_p, tab_p, None)
        st_s = (state_mlstm_c, state_mlstm_n[l], state_mlstm_m[l], state_rwkv_s, state_rwkv_shift[l],
                state_s5_re[l], state_s5_im[l], state_ffn_conv[l])
        xs, ss = _layer(xs, bs, 1, l, layers[l], st_s, tab_s, (cki, ck, cv, page_table, n_pool))
        new_p.append(sp)
        new_s.append(ss)
    (k_p, v_p, ki_p, mc_p, mn_p, mm_p, rs_p, rsh_p, sre_p, sim_p, fc_p) = [jnp.stack(z) for z in zip(*new_p)]
    (k_s, v_s, ki_s, mc_s, mn_s, mm_s, rs_s, rsh_s, sre_s, sim_s, fc_s) = [jnp.stack(z) for z in zip(*new_s)]
    y_prompt = _rmsnorm(xp, row(norm_final), F32).reshape(bp, tp, d)
    y_sample = _rmsnorm(xs, row(norm_final), F32).reshape(bs, ts, d)
    return (y_prompt, y_sample, k_p, k_s, v_p, v_s, ki_p, ki_s, mc_p, mc_s, mn_p, mn_s, mm_p, mm_s,
            rs_p, rs_s, rsh_p, rsh_s, sre_p, sre_s, sim_p, sim_s, fc_p, fc_s)
```
